```python
import jax, jax.numpy as jnp
from jax import lax
import numpy as np

D_MODEL = 2048
BATCH = 8
SEQ = 4096
DEPTH = 4

N_MIXERS = 4
RMS_EPS = 1e-6
HGRN_EXPAND = 128
HGRN_HEADS = D_MODEL // HGRN_EXPAND
HGRN_DK = HGRN_EXPAND
HGRN_DV = D_MODEL // HGRN_HEADS
HGRN_WIDTH = HGRN_HEADS * HGRN_DK
HGRN_CHUNK = 64
SWA_HEAD_DIM = 64
SWA_Q_HEADS = D_MODEL // SWA_HEAD_DIM
SWA_KV_HEADS = SWA_Q_HEADS // 8
SWA_WINDOW = 128
SCONV_WIDTH = 3
FOX_HEAD_DIM = 64
FOX_HEADS = D_MODEL // FOX_HEAD_DIM
FOX_BLOCK = 128
ROPE_THETA = 500000.0
ROT_DIM = SWA_HEAD_DIM // 4
D_FF = 5632
FFN_CONV_WIDTH = 3

kernel_name = "hybrid_interleaved_hgrn2_swa_sconv_fox"


def rmsnorm(x, g):
    xf = x.astype(jnp.float32)
    y = xf * lax.rsqrt(jnp.mean(xf * xf, axis=-1, keepdims=True) + RMS_EPS)
    return (y * g.astype(jnp.float32)).astype(x.dtype)


def causal_dwconv(x, w):
    K, C = w.shape
    return lax.conv_general_dilated(
        x, w[:, None, :].astype(x.dtype), window_strides=(1,), padding=[(K - 1, 0)],
        dimension_numbers=("NWC", "WIO", "NWC"), feature_group_count=C)


def partial_rope(x, positions):
    half = ROT_DIM // 2
    inv_freq = ROPE_THETA ** (-jnp.arange(half, dtype=jnp.float32) / half)
    ang = positions.astype(jnp.float32)[:, None] * inv_freq[None, :]
    cos = jnp.cos(ang)[None, :, None, :]
    sin = jnp.sin(ang)[None, :, None, :]
    xf = x.astype(jnp.float32)
    x1, x2 = xf[..., :half], xf[..., half:ROT_DIM]
    out = jnp.concatenate([x1 * cos - x2 * sin, x2 * cos + x1 * sin, xf[..., ROT_DIM:]], axis=-1)
    return out.astype(x.dtype)


def hgrn2_mixer(h, w_in, w_out, norm_g, lb):
    Bsz, T, _ = h.shape
    C = HGRN_CHUNK
    nC = T // C
    q, f, i, g = jnp.split(h @ w_in, 4, axis=-1)
    q = jax.nn.silu(q.astype(jnp.float32))
    f = lb + (1.0 - lb) * jax.nn.sigmoid(f.astype(jnp.float32))
    log_f = jnp.log(f)
    k = 1.0 - f
    v = i.astype(jnp.float32)

    def to_chunks(a, d):
        return a.reshape(Bsz, nC, C, HGRN_HEADS, d).transpose(1, 0, 3, 2, 4)

    qc, kc, vc = to_chunks(q, HGRN_DK), to_chunks(k, HGRN_DK), to_chunks(v, HGRN_DV)
    bc = jnp.cumsum(to_chunks(log_f, HGRN_DK), axis=3)
    causal = jnp.tril(jnp.ones((C, C), dtype=bool))

    def chunk_step(S, inp):
        qb, kb, vb, bb = inp
        inter = jnp.einsum("bhtk,bhkv->bhtv", qb * jnp.exp(bb), S)
        rel = jnp.where(causal[:, :, None], bb[:, :, :, None, :] - bb[:, :, None, :, :], -jnp.inf)
        A = jnp.einsum("bhtk,bhsk,bhtsk->bhts", qb, kb, jnp.exp(rel))
        intra = jnp.einsum("bhts,bhsv->bhtv", A, vb)
        b_last = bb[:, :, -1:, :]
        S_new = jnp.exp(b_last[:, :, 0, :])[..., None] * S + jnp.einsum(
            "bhsk,bhsv->bhkv", kb * jnp.exp(b_last - bb), vb)
        return S_new, inter + intra

    S0 = jnp.zeros((Bsz, HGRN_HEADS, HGRN_DK, HGRN_DV), jnp.float32)
    _, o = lax.scan(chunk_step, S0, (qc, kc, vc, bc))
    o = o.transpose(1, 0, 3, 2, 4).reshape(Bsz, T, HGRN_HEADS, HGRN_DV)
    o = rmsnorm(o, norm_g).reshape(Bsz, T, HGRN_HEADS * HGRN_DV)
    o = o * jax.nn.silu(g.astype(jnp.float32))
    return o.astype(h.dtype) @ w_out


def swa_sink_mixer(h, positions, w_in, w_out, sinks):
    Bsz, T, _ = h.shape
    W, d = SWA_WINDOW, SWA_HEAD_DIM
    KV, G = SWA_KV_HEADS, SWA_Q_HEADS // SWA_KV_HEADS
    nblk = T // W
    q, k, v = jnp.split(h @ w_in, [SWA_Q_HEADS * d, SWA_Q_HEADS * d + KV * d], axis=-1)
    q = partial_rope(q.reshape(Bsz, T, SWA_Q_HEADS, d), positions)
    k = partial_rope(k.reshape(Bsz, T, KV, d), positions)
    v = v.reshape(Bsz, T, KV, d)
    qb = q.reshape(Bsz, nblk, W, KV, G, d)

    def band(a):
        cur = a.reshape(Bsz, nblk, W, KV, d)
        prev = jnp.concatenate([jnp.zeros_like(cur[:, :1]), cur[:, :-1]], axis=1)
        return jnp.concatenate([prev, cur], axis=2)

    kb, vb = band(k), band(v)
    s = jnp.einsum("bnqhgd,bnkhd->bnhgqk", qb, kb).astype(jnp.float32) * (d ** -0.5)
    qi = jnp.arange(W)[:, None]
    kj = jnp.arange(2 * W)[None, :]
    diff = qi + W - kj
    blk = jnp.arange(nblk)[:, None, None]
    allowed = (diff >= 0) & (diff < W) & (blk * W + kj - W >= 0)
    s = jnp.where(allowed[None, :, None, None], s, -jnp.inf)
    sink = sinks.astype(jnp.float32).reshape(KV, G)[None, None, :, :, None, None]
    m = jnp.maximum(jnp.max(s, axis=-1, keepdims=True), sink)
    e = jnp.exp(s - m)
    p = e / (jnp.sum(e, axis=-1, keepdims=True) + jnp.exp(sink - m))
    o = jnp.einsum("bnhgqk,bnkhd->bnqhgd", p.astype(h.dtype), vb)
    return o.reshape(Bsz, T, SWA_Q_HEADS * d) @ w_out


def short_conv_mixer(h, w_in, conv_w, w_out):
    b_gate, c_gate, xv = jnp.split(h @ w_in, 3, axis=-1)
    return (b_gate * causal_dwconv(c_gate * xv, conv_w)) @ w_out


def fox_mixer(h, w_in, b_f, w_out):
    Bsz, T, _ = h.shape
    H, d, W = FOX_HEADS, FOX_HEAD_DIM, FOX_BLOCK
    width = H * d
    nblk = T // W
    q, k, v, f_logit, g = jnp.split(h @ w_in, [width, 2 * width, 3 * width, 3 * width + H], axis=-1)
    q = q.reshape(Bsz, T, H, d)
    k = k.reshape(Bsz, T, H, d)
    v = v.reshape(Bsz, T, H, d)
    log_f = jax.nn.log_sigmoid(f_logit.astype(jnp.float32) + b_f.astype(jnp.float32))
    c = jnp.cumsum(log_f, axis=1).transpose(0, 2, 1)
    key_pos = jnp.arange(T)

    def q_block(n):
        start = n * W
        qs = lax.dynamic_slice_in_dim(q, start, W, axis=1)
        cq = lax.dynamic_slice_in_dim(c, start, W, axis=2)
        s = jnp.einsum("bqhd,bkhd->bhqk", qs, k).astype(jnp.float32) * (d ** -0.5)
        s = s + cq[..., None] - c[:, :, None, :]
        q_pos = start + jnp.arange(W)
        s = jnp.where((key_pos[None, :] <= q_pos[:, None])[None, None], s, -jnp.inf)
        p = jax.nn.softmax(s, axis=-1)
        return jnp.einsum("bhqk,bkhd->bqhd", p.astype(v.dtype), v)

    o = lax.map(q_block, jnp.arange(nblk))
    o = o.transpose(1, 0, 2, 3, 4).reshape(Bsz, T, width)
    o = o * jax.nn.sigmoid(g.astype(jnp.float32)).astype(o.dtype)
    return o @ w_out


def conv_glu_ffn(h, w_up, conv_w, conv_b, w_down):
    u = causal_dwconv(h @ w_up, conv_w) + conv_b.astype(h.dtype)
    gate, up = jnp.split(u, 2, axis=-1)
    return (jax.nn.silu(gate) * up) @ w_down


def _fwd_setup_inputs(seed: int = 0) -> dict:
    key = jax.random.key(seed)
    ks = iter(jax.random.split(key, 32))

    def nrm(shape, scale):
        return scale * jax.random.normal(next(ks), shape, jnp.float32)

    n_of = [len(range(m, DEPTH, N_MIXERS)) for m in range(N_MIXERS)]
    nA, nB, nC, nD = n_of
    D = D_MODEL
    sd = D ** -0.5
    return {
        "x": nrm((BATCH, SEQ, D), 1.0),
        "positions": jnp.arange(SEQ, dtype=jnp.int32),
        "mix_pre_g": 1.0 + nrm((DEPTH, D), 0.05),
        "mix_post_g": 1.0 + nrm((DEPTH, D), 0.05),
        "ffn_pre_g": 1.0 + nrm((DEPTH, D), 0.05),
        "ffn_post_g": 1.0 + nrm((DEPTH, D), 0.05),
        "hgrn_w_in": nrm((nA, D, 3 * HGRN_WIDTH + HGRN_HEADS * HGRN_DV), sd),
        "hgrn_w_out": nrm((nA, HGRN_HEADS * HGRN_DV, D), (HGRN_HEADS * HGRN_DV) ** -0.5),
        "hgrn_norm_g": 1.0 + nrm((nA, HGRN_DV), 0.05),
        "hgrn_lb_param": nrm((DEPTH + 1, HGRN_WIDTH), 0.5),
        "swa_w_in": nrm((nB, D, (SWA_Q_HEADS + 2 * SWA_KV_HEADS) * SWA_HEAD_DIM), sd),
        "swa_w_out": nrm((nB, SWA_Q_HEADS * SWA_HEAD_DIM, D), (SWA_Q_HEADS * SWA_HEAD_DIM) ** -0.5),
        "swa_sinks": nrm((nB, SWA_Q_HEADS), 0.5),
        "sc_w_in": nrm((nC, D, 3 * D), sd),
        "sc_conv_w": nrm((nC, SCONV_WIDTH, D), SCONV_WIDTH ** -0.5),
        "sc_w_out": nrm((nC, D, D), sd),
        "fox_w_in": nrm((nD, D, 4 * FOX_HEADS * FOX_HEAD_DIM + FOX_HEADS), sd),
        "fox_b_f": 3.0 + nrm((nD, FOX_HEADS), 0.5),
        "fox_w_out": nrm((nD, FOX_HEADS * FOX_HEAD_DIM, D), (FOX_HEADS * FOX_HEAD_DIM) ** -0.5),
        "ffn_w_up": nrm((DEPTH, D, 2 * D_FF), sd),
        "ffn_conv_w": nrm((DEPTH, FFN_CONV_WIDTH, 2 * D_FF), FFN_CONV_WIDTH ** -0.5),
        "ffn_conv_b": nrm((DEPTH, 2 * D_FF), 0.02),
        "ffn_w_down": nrm((DEPTH, D_FF, D), D_FF ** -0.5),
    }


def _fwd_reference(x, positions, mix_pre_g, mix_post_g, ffn_pre_g, ffn_post_g,
              hgrn_w_in, hgrn_w_out, hgrn_norm_g, hgrn_lb_param,
              swa_w_in, swa_w_out, swa_sinks,
              sc_w_in, sc_conv_w, sc_w_out,
              fox_w_in, fox_b_f, fox_w_out,
              ffn_w_up, ffn_conv_w, ffn_conv_b, ffn_w_down):
    lb_table = jnp.cumsum(jax.nn.softmax(hgrn_lb_param.astype(jnp.float32), axis=0), axis=0)
    for i in range(DEPTH):
        m, j = i % N_MIXERS, i // N_MIXERS
        hn = rmsnorm(x, mix_pre_g[i])
        if m == 0:
            y = hgrn2_mixer(hn, hgrn_w_in[j], hgrn_w_out[j], hgrn_norm_g[j], lb_table[i])
        elif m == 1:
            y = swa_sink_mixer(hn, positions, swa_w_in[j], swa_w_out[j], swa_sinks[j])
        elif m == 2:
            y = short_conv_mixer(hn, sc_w_in[j], sc_conv_w[j], sc_w_out[j])
        else:
            y = fox_mixer(hn, fox_w_in[j], fox_b_f[j], fox_w_out[j])
        x = x + rmsnorm(y.astype(x.dtype), mix_post_g[i])
        hn = rmsnorm(x, ffn_pre_g[i])
        y = conv_glu_ffn(hn, ffn_w_up[i], ffn_conv_w[i], ffn_conv_b[i], ffn_w_down[i])
        x = x + rmsnorm(y.astype(x.dtype), ffn_post_g[i])
    return x


import jax as _jax
import jax.numpy as _jnp

TWIN_FORMAT = 'train_step'
FWD_PARAMS = ['x', 'positions', 'mix_pre_g', 'mix_post_g', 'ffn_pre_g', 'ffn_post_g', 'hgrn_w_in', 'hgrn_w_out', 'hgrn_norm_g', 'hgrn_lb_param', 'swa_w_in', 'swa_w_out', 'swa_sinks', 'sc_w_in', 'sc_conv_w', 'sc_w_out', 'fox_w_in', 'fox_b_f', 'fox_w_out', 'ffn_w_up', 'ffn_conv_w', 'ffn_conv_b', 'ffn_w_down']
TWIN_WEIGHTS = ['mix_pre_g', 'mix_post_g', 'ffn_pre_g', 'ffn_post_g', 'hgrn_w_in', 'hgrn_w_out', 'hgrn_norm_g', 'hgrn_lb_param', 'swa_w_in', 'swa_w_out', 'swa_sinks', 'sc_w_in', 'sc_conv_w', 'sc_w_out', 'fox_w_in', 'fox_b_f', 'fox_w_out', 'ffn_w_up', 'ffn_conv_w', 'ffn_conv_b', 'ffn_w_down']
TWIN_DIFF_INPUT = 'x'
TWIN_INPUTS = ['x', 'positions', 'mix_pre_g', 'mix_post_g', 'ffn_pre_g', 'ffn_post_g', 'hgrn_w_in', 'hgrn_w_out', 'hgrn_norm_g', 'hgrn_lb_param', 'swa_w_in', 'swa_w_out', 'swa_sinks', 'sc_w_in', 'sc_conv_w', 'sc_w_out', 'fox_w_in', 'fox_b_f', 'fox_w_out', 'ffn_w_up', 'ffn_conv_w', 'ffn_conv_b', 'ffn_w_down', 'loss_target', 'm_mix_pre_g', 'm_mix_post_g', 'm_ffn_pre_g', 'm_ffn_post_g', 'm_hgrn_w_in', 'm_hgrn_w_out', 'm_hgrn_norm_g', 'm_hgrn_lb_param', 'm_swa_w_in', 'm_swa_w_out', 'm_swa_sinks', 'm_sc_w_in', 'm_sc_conv_w', 'm_sc_w_out', 'm_fox_w_in', 'm_fox_b_f', 'm_fox_w_out', 'm_ffn_w_up', 'm_ffn_conv_w', 'm_ffn_conv_b', 'm_ffn_w_down', 'v_mix_pre_g', 'v_mix_post_g', 'v_ffn_pre_g', 'v_ffn_post_g', 'v_hgrn_w_in', 'v_hgrn_w_out', 'v_hgrn_norm_g', 'v_hgrn_lb_param', 'v_swa_w_in', 'v_swa_w_out', 'v_swa_sinks', 'v_sc_w_in', 'v_sc_conv_w', 'v_sc_w_out', 'v_fox_w_in', 'v_fox_b_f', 'v_fox_w_out', 'v_ffn_w_up', 'v_ffn_conv_w', 'v_ffn_conv_b', 'v_ffn_w_down']
TWIN_OUTPUTS = ['loss', 'grad_x', 'grad_mix_pre_g', 'grad_mix_post_g', 'grad_ffn_pre_g', 'grad_ffn_post_g', 'grad_hgrn_w_in', 'grad_hgrn_w_out', 'grad_hgrn_norm_g', 'grad_hgrn_lb_param', 'grad_swa_w_in', 'grad_swa_w_out', 'grad_swa_sinks', 'grad_sc_w_in', 'grad_sc_conv_w', 'grad_sc_w_out', 'grad_fox_w_in', 'grad_fox_b_f', 'grad_fox_w_out', 'grad_ffn_w_up', 'grad_ffn_conv_w', 'grad_ffn_conv_b', 'grad_ffn_w_down', 'delta_mix_pre_g', 'delta_mix_post_g', 'delta_ffn_pre_g', 'delta_ffn_post_g', 'delta_hgrn_w_in', 'delta_hgrn_w_out', 'delta_hgrn_norm_g', 'delta_hgrn_lb_param', 'delta_swa_w_in', 'delta_swa_w_out', 'delta_swa_sinks', 'delta_sc_w_in', 'delta_sc_conv_w', 'delta_sc_w_out', 'delta_fox_w_in', 'delta_fox_b_f', 'delta_fox_w_out', 'delta_ffn_w_up', 'delta_ffn_conv_w', 'delta_ffn_conv_b', 'delta_ffn_w_down', 'new_m_mix_pre_g', 'new_m_mix_post_g', 'new_m_ffn_pre_g', 'new_m_ffn_post_g', 'new_m_hgrn_w_in', 'new_m_hgrn_w_out', 'new_m_hgrn_norm_g', 'new_m_hgrn_lb_param', 'new_m_swa_w_in', 'new_m_swa_w_out', 'new_m_swa_sinks', 'new_m_sc_w_in', 'new_m_sc_conv_w', 'new_m_sc_w_out', 'new_m_fox_w_in', 'new_m_fox_b_f', 'new_m_fox_w_out', 'new_m_ffn_w_up', 'new_m_ffn_conv_w', 'new_m_ffn_conv_b', 'new_m_ffn_w_down', 'new_v_mix_pre_g', 'new_v_mix_post_g', 'new_v_ffn_pre_g', 'new_v_ffn_post_g', 'new_v_hgrn_w_in', 'new_v_hgrn_w_out', 'new_v_hgrn_norm_g', 'new_v_hgrn_lb_param', 'new_v_swa_w_in', 'new_v_swa_w_out', 'new_v_swa_sinks', 'new_v_sc_w_in', 'new_v_sc_conv_w', 'new_v_sc_w_out', 'new_v_fox_w_in', 'new_v_fox_b_f', 'new_v_fox_w_out', 'new_v_ffn_w_up', 'new_v_ffn_conv_w', 'new_v_ffn_conv_b', 'new_v_ffn_w_down']
TWIN_LEAF_KINDS = {'loss': 'loss', 'grad_x': 'grad_x', 'grad_mix_pre_g': 'grad_w', 'grad_mix_post_g': 'grad_w', 'grad_ffn_pre_g': 'grad_w', 'grad_ffn_post_g': 'grad_w', 'grad_hgrn_w_in': 'grad_w', 'grad_hgrn_w_out': 'grad_w', 'grad_hgrn_norm_g': 'grad_w', 'grad_hgrn_lb_param': 'grad_w', 'grad_swa_w_in': 'grad_w', 'grad_swa_w_out': 'grad_w', 'grad_swa_sinks': 'grad_w', 'grad_sc_w_in': 'grad_w', 'grad_sc_conv_w': 'grad_w', 'grad_sc_w_out': 'grad_w', 'grad_fox_w_in': 'grad_w', 'grad_fox_b_f': 'grad_w', 'grad_fox_w_out': 'grad_w', 'grad_ffn_w_up': 'grad_w', 'grad_ffn_conv_w': 'grad_w', 'grad_ffn_conv_b': 'grad_w', 'grad_ffn_w_down': 'grad_w', 'delta_mix_pre_g': 'delta_w', 'delta_mix_post_g': 'delta_w', 'delta_ffn_pre_g': 'delta_w', 'delta_ffn_post_g': 'delta_w', 'delta_hgrn_w_in': 'delta_w', 'delta_hgrn_w_out': 'delta_w', 'delta_hgrn_norm_g': 'delta_w', 'delta_hgrn_lb_param': 'delta_w', 'delta_swa_w_in': 'delta_w', 'delta_swa_w_out': 'delta_w', 'delta_swa_sinks': 'delta_w', 'delta_sc_w_in': 'delta_w', 'delta_sc_conv_w': 'delta_w', 'delta_sc_w_out': 'delta_w', 'delta_fox_w_in': 'delta_w', 'delta_fox_b_f': 'delta_w', 'delta_fox_w_out': 'delta_w', 'delta_ffn_w_up': 'delta_w', 'delta_ffn_conv_w': 'delta_w', 'delta_ffn_conv_b': 'delta_w', 'delta_ffn_w_down': 'delta_w', 'new_m_mix_pre_g': 'new_m', 'new_m_mix_post_g': 'new_m', 'new_m_ffn_pre_g': 'new_m', 'new_m_ffn_post_g': 'new_m', 'new_m_hgrn_w_in': 'new_m', 'new_m_hgrn_w_out': 'new_m', 'new_m_hgrn_norm_g': 'new_m', 'new_m_hgrn_lb_param': 'new_m', 'new_m_swa_w_in': 'new_m', 'new_m_swa_w_out': 'new_m', 'new_m_swa_sinks': 'new_m', 'new_m_sc_w_in': 'new_m', 'new_m_sc_conv_w': 'new_m', 'new_m_sc_w_out': 'new_m', 'new_m_fox_w_in': 'new_m', 'new_m_fox_b_f': 'new_m', 'new_m_fox_w_out': 'new_m', 'new_m_ffn_w_up': 'new_m', 'new_m_ffn_conv_w': 'new_m', 'new_m_ffn_conv_b': 'new_m', 'new_m_ffn_w_down': 'new_m', 'new_v_mix_pre_g': 'new_v', 'new_v_mix_post_g': 'new_v', 'new_v_ffn_pre_g': 'new_v', 'new_v_ffn_post_g': 'new_v', 'new_v_hgrn_w_in': 'new_v', 'new_v_hgrn_w_out': 'new_v', 'new_v_hgrn_norm_g': 'new_v', 'new_v_hgrn_lb_param': 'new_v', 'new_v_swa_w_in': 'new_v', 'new_v_swa_w_out': 'new_v', 'new_v_swa_sinks': 'new_v', 'new_v_sc_w_in': 'new_v', 'new_v_sc_conv_w': 'new_v', 'new_v_sc_w_out': 'new_v', 'new_v_fox_w_in': 'new_v', 'new_v_fox_b_f': 'new_v', 'new_v_fox_w_out': 'new_v', 'new_v_ffn_w_up': 'new_v', 'new_v_ffn_conv_w': 'new_v', 'new_v_ffn_conv_b': 'new_v', 'new_v_ffn_w_down': 'new_v'}


def _forward(args):
    return _fwd_reference(*[args[k] for k in FWD_PARAMS])


def _output_shape():
    out = _jax.eval_shape(lambda: _forward(_fwd_setup_inputs(0)))
    return out.shape, out.dtype

N_MICROBATCH = 1
ADAM_LR = 0.001
ADAM_B1 = 0.9
ADAM_B2 = 0.999
ADAM_EPS = 1e-08
ADAM_WD = 0.01
ADAM_STEP = 10
PER_EXAMPLE_BATCH_AXIS = {'x': 0, 'loss_target': 0}
SHARED_INPUTS = ['positions']
_WEIGHT_DTYPES = {'mix_pre_g': _jnp.float32, 'mix_post_g': _jnp.float32, 'ffn_pre_g': _jnp.float32, 'ffn_post_g': _jnp.float32, 'hgrn_w_in': _jnp.float32, 'hgrn_w_out': _jnp.float32, 'hgrn_norm_g': _jnp.float32, 'hgrn_lb_param': _jnp.float32, 'swa_w_in': _jnp.float32, 'swa_w_out': _jnp.float32, 'swa_sinks': _jnp.float32, 'sc_w_in': _jnp.float32, 'sc_conv_w': _jnp.float32, 'sc_w_out': _jnp.float32, 'fox_w_in': _jnp.float32, 'fox_b_f': _jnp.float32, 'fox_w_out': _jnp.float32, 'ffn_w_up': _jnp.float32, 'ffn_conv_w': _jnp.float32, 'ffn_conv_b': _jnp.float32, 'ffn_w_down': _jnp.float32}
MOMENT_SCALE = {'mix_pre_g': 1.211237e+00, 'mix_post_g': 1.535289e+01, 'ffn_pre_g': 8.180239e-01, 'ffn_post_g': 1.578505e+01, 'hgrn_w_in': 9.573238e-01, 'hgrn_w_out': 1.335949e+00, 'hgrn_norm_g': 5.569883e+00, 'hgrn_lb_param': 2.795665e-02, 'swa_w_in': 1.092261e+00, 'swa_w_out': 9.117918e-01, 'swa_sinks': 2.634081e-01, 'sc_w_in': 3.914111e-01, 'sc_conv_w': 3.927315e-01, 'sc_w_out': 3.931336e-01, 'fox_w_in': 3.232883e-01, 'fox_b_f': 1.698268e+00, 'fox_w_out': 5.838278e-01, 'ffn_w_up': 3.548724e-01, 'ffn_conv_w': 3.568093e-01, 'ffn_conv_b': 8.364308e-01, 'ffn_w_down': 5.911217e-01}


def _to_microbatches(a, axis):
    t = _jnp.moveaxis(a, axis, 0)
    t = t.reshape((N_MICROBATCH, t.shape[0] // N_MICROBATCH) + t.shape[1:])
    return _jnp.moveaxis(t, 1, axis + 1)


def setup_inputs(seed: int = 0) -> dict:
    inp = _fwd_setup_inputs(seed)
    key = _jax.random.fold_in(_jax.random.key(seed), 7919)
    shape, _ = _output_shape()
    out = dict(inp)
    out["loss_target"] = _jax.random.normal(_jax.random.fold_in(key, 0), shape, _jnp.float32)
    for i, name in enumerate(TWIN_WEIGHTS):
        w = inp[name].astype(_jnp.float32)
        if MOMENT_SCALE is None:
            s = _jnp.sqrt(_jnp.mean(_jnp.square(w)) + 1e-30)
        else:
            s = MOMENT_SCALE[name]
        km, kv = _jax.random.split(_jax.random.fold_in(key, i + 1))
        out[name] = w
        out["m_" + name] = s * _jax.random.normal(km, w.shape, _jnp.float32)
        out["v_" + name] = (s * s) * _jax.random.uniform(kv, w.shape, _jnp.float32, 0.5, 1.5)
    if N_MICROBATCH > 1:
        for name, axis in PER_EXAMPLE_BATCH_AXIS.items():
            out[name] = _to_microbatches(out[name], axis)
    return {'x': out['x'], 'positions': out['positions'], 'mix_pre_g': out['mix_pre_g'], 'mix_post_g': out['mix_post_g'], 'ffn_pre_g': out['ffn_pre_g'], 'ffn_post_g': out['ffn_post_g'], 'hgrn_w_in': out['hgrn_w_in'], 'hgrn_w_out': out['hgrn_w_out'], 'hgrn_norm_g': out['hgrn_norm_g'], 'hgrn_lb_param': out['hgrn_lb_param'], 'swa_w_in': out['swa_w_in'], 'swa_w_out': out['swa_w_out'], 'swa_sinks': out['swa_sinks'], 'sc_w_in': out['sc_w_in'], 'sc_conv_w': out['sc_conv_w'], 'sc_w_out': out['sc_w_out'], 'fox_w_in': out['fox_w_in'], 'fox_b_f': out['fox_b_f'], 'fox_w_out': out['fox_w_out'], 'ffn_w_up': out['ffn_w_up'], 'ffn_conv_w': out['ffn_conv_w'], 'ffn_conv_b': out['ffn_conv_b'], 'ffn_w_down': out['ffn_w_down'], 'loss_target': out['loss_target'], 'm_mix_pre_g': out['m_mix_pre_g'], 'm_mix_post_g': out['m_mix_post_g'], 'm_ffn_pre_g': out['m_ffn_pre_g'], 'm_ffn_post_g': out['m_ffn_post_g'], 'm_hgrn_w_in': out['m_hgrn_w_in'], 'm_hgrn_w_out': out['m_hgrn_w_out'], 'm_hgrn_norm_g': out['m_hgrn_norm_g'], 'm_hgrn_lb_param': out['m_hgrn_lb_param'], 'm_swa_w_in': out['m_swa_w_in'], 'm_swa_w_out': out['m_swa_w_out'], 'm_swa_sinks': out['m_swa_sinks'], 'm_sc_w_in': out['m_sc_w_in'], 'm_sc_conv_w': out['m_sc_conv_w'], 'm_sc_w_out': out['m_sc_w_out'], 'm_fox_w_in': out['m_fox_w_in'], 'm_fox_b_f': out['m_fox_b_f'], 'm_fox_w_out': out['m_fox_w_out'], 'm_ffn_w_up': out['m_ffn_w_up'], 'm_ffn_conv_w': out['m_ffn_conv_w'], 'm_ffn_conv_b': out['m_ffn_conv_b'], 'm_ffn_w_down': out['m_ffn_w_down'], 'v_mix_pre_g': out['v_mix_pre_g'], 'v_mix_post_g': out['v_mix_post_g'], 'v_ffn_pre_g': out['v_ffn_pre_g'], 'v_ffn_post_g': out['v_ffn_post_g'], 'v_hgrn_w_in': out['v_hgrn_w_in'], 'v_hgrn_w_out': out['v_hgrn_w_out'], 'v_hgrn_norm_g': out['v_hgrn_norm_g'], 'v_hgrn_lb_param': out['v_hgrn_lb_param'], 'v_swa_w_in': out['v_swa_w_in'], 'v_swa_w_out': out['v_swa_w_out'], 'v_swa_sinks': out['v_swa_sinks'], 'v_sc_w_in': out['v_sc_w_in'], 'v_sc_conv_w': out['v_sc_conv_w'], 'v_sc_w_out': out['v_sc_w_out'], 'v_fox_w_in': out['v_fox_w_in'], 'v_fox_b_f': out['v_fox_b_f'], 'v_fox_w_out': out['v_fox_w_out'], 'v_ffn_w_up': out['v_ffn_w_up'], 'v_ffn_conv_w': out['v_ffn_conv_w'], 'v_ffn_conv_b': out['v_ffn_conv_b'], 'v_ffn_w_down': out['v_ffn_w_down']}


def _loss(weights, diff, rest, loss_target):
    with _jax.named_scope("forward"):
        args = {**rest, TWIN_DIFF_INPUT: diff, **{k: w.astype(_WEIGHT_DTYPES[k]) for k, w in weights.items()}}
        y = _forward(args)
    with _jax.named_scope("loss_head"):
        err = _jnp.square(y.astype(_jnp.float32) - loss_target)
        return 0.5 * _jnp.sum(_jnp.mean(err, axis=-1)) if err.ndim else 0.5 * err


def _adamw(w, g, m, v):
    m = ADAM_B1 * m + (1.0 - ADAM_B1) * g
    v = ADAM_B2 * v + (1.0 - ADAM_B2) * _jnp.square(g)
    m_hat = m / (1.0 - ADAM_B1 ** ADAM_STEP)
    v_hat = v / (1.0 - ADAM_B2 ** ADAM_STEP)
    delta = -ADAM_LR * (m_hat / (_jnp.sqrt(v_hat) + ADAM_EPS) + ADAM_WD * w)
    return delta, m, v


def reference(x, positions, mix_pre_g, mix_post_g, ffn_pre_g, ffn_post_g, hgrn_w_in, hgrn_w_out, hgrn_norm_g, hgrn_lb_param, swa_w_in, swa_w_out, swa_sinks, sc_w_in, sc_conv_w, sc_w_out, fox_w_in, fox_b_f, fox_w_out, ffn_w_up, ffn_conv_w, ffn_conv_b, ffn_w_down, loss_target, m_mix_pre_g, m_mix_post_g, m_ffn_pre_g, m_ffn_post_g, m_hgrn_w_in, m_hgrn_w_out, m_hgrn_norm_g, m_hgrn_lb_param, m_swa_w_in, m_swa_w_out, m_swa_sinks, m_sc_w_in, m_sc_conv_w, m_sc_w_out, m_fox_w_in, m_fox_b_f, m_fox_w_out, m_ffn_w_up, m_ffn_conv_w, m_ffn_conv_b, m_ffn_w_down, v_mix_pre_g, v_mix_post_g, v_ffn_pre_g, v_ffn_post_g, v_hgrn_w_in, v_hgrn_w_out, v_hgrn_norm_g, v_hgrn_lb_param, v_swa_w_in, v_swa_w_out, v_swa_sinks, v_sc_w_in, v_sc_conv_w, v_sc_w_out, v_fox_w_in, v_fox_b_f, v_fox_w_out, v_ffn_w_up, v_ffn_conv_w, v_ffn_conv_b, v_ffn_w_down):
    given = dict(x=x, positions=positions, mix_pre_g=mix_pre_g, mix_post_g=mix_post_g, ffn_pre_g=ffn_pre_g, ffn_post_g=ffn_post_g, hgrn_w_in=hgrn_w_in, hgrn_w_out=hgrn_w_out, hgrn_norm_g=hgrn_norm_g, hgrn_lb_param=hgrn_lb_param, swa_w_in=swa_w_in, swa_w_out=swa_w_out, swa_sinks=swa_sinks, sc_w_in=sc_w_in, sc_conv_w=sc_conv_w, sc_w_out=sc_w_out, fox_w_in=fox_w_in, fox_b_f=fox_b_f, fox_w_out=fox_w_out, ffn_w_up=ffn_w_up, ffn_conv_w=ffn_conv_w, ffn_conv_b=ffn_conv_b, ffn_w_down=ffn_w_down, loss_target=loss_target, m_mix_pre_g=m_mix_pre_g, m_mix_post_g=m_mix_post_g, m_ffn_pre_g=m_ffn_pre_g, m_ffn_post_g=m_ffn_post_g, m_hgrn_w_in=m_hgrn_w_in, m_hgrn_w_out=m_hgrn_w_out, m_hgrn_norm_g=m_hgrn_norm_g, m_hgrn_lb_param=m_hgrn_lb_param, m_swa_w_in=m_swa_w_in, m_swa_w_out=m_swa_w_out, m_swa_sinks=m_swa_sinks, m_sc_w_in=m_sc_w_in, m_sc_conv_w=m_sc_conv_w, m_sc_w_out=m_sc_w_out, m_fox_w_in=m_fox_w_in, m_fox_b_f=m_fox_b_f, m_fox_w_out=m_fox_w_out, m_ffn_w_up=m_ffn_w_up, m_ffn_conv_w=m_ffn_conv_w, m_ffn_conv_b=m_ffn_conv_b, m_ffn_w_down=m_ffn_w_down, v_mix_pre_g=v_mix_pre_g, v_mix_post_g=v_mix_post_g, v_ffn_pre_g=v_ffn_pre_g, v_ffn_post_g=v_ffn_post_g, v_hgrn_w_in=v_hgrn_w_in, v_hgrn_w_out=v_hgrn_w_out, v_hgrn_norm_g=v_hgrn_norm_g, v_hgrn_lb_param=v_hgrn_lb_param, v_swa_w_in=v_swa_w_in, v_swa_w_out=v_swa_w_out, v_swa_sinks=v_swa_sinks, v_sc_w_in=v_sc_w_in, v_sc_conv_w=v_sc_conv_w, v_sc_w_out=v_sc_w_out, v_fox_w_in=v_fox_w_in, v_fox_b_f=v_fox_b_f, v_fox_w_out=v_fox_w_out, v_ffn_w_up=v_ffn_w_up, v_ffn_conv_w=v_ffn_conv_w, v_ffn_conv_b=v_ffn_conv_b, v_ffn_w_down=v_ffn_w_down)
    weights = {n: given[n] for n in TWIN_WEIGHTS}
    shared = {n: given[n] for n in SHARED_INPUTS}
    per_example = {n: given[n] for n in ['x']}
    grad_fn = _jax.value_and_grad(_loss, argnums=(0, 1))

    def one_microbatch(ex, loss_target):
        ex = dict(ex)
        diff = ex.pop(TWIN_DIFF_INPUT)
        return grad_fn(weights, diff, {**shared, **ex}, loss_target)

    if N_MICROBATCH == 1:
        loss, (grad_w, grad_x) = one_microbatch(per_example, given["loss_target"])
    else:
        def body(carry, xs):
            loss_sum, grad_sum = carry
            l_k, (gw_k, gx_k) = one_microbatch(xs[0], xs[1])
            with _jax.named_scope("update"):
                return (loss_sum + l_k, _jax.tree.map(_jnp.add, grad_sum, gw_k)), gx_k

        init = (_jnp.zeros((), _jnp.float32), _jax.tree.map(_jnp.zeros_like, weights))
        (loss, grad_w), grad_x = _jax.lax.scan(body, init, (per_example, given["loss_target"]))
    with _jax.named_scope("update"):
        delta_w, new_m, new_v = {}, {}, {}
        for n in TWIN_WEIGHTS:
            delta_w[n], new_m[n], new_v[n] = _adamw(weights[n], grad_w[n], given["m_" + n], given["v_" + n])
    return (loss, grad_x, *[grad_w[n] for n in TWIN_WEIGHTS], *[delta_w[n] for n in TWIN_WEIGHTS],
            *[new_m[n] for n in TWIN_WEIGHTS], *[new_v[n] for n in TWIN_WEIGHTS])
```

```python
import functools
import math

import numpy as np
import jax
import jax.numpy as jnp
from jax import lax
from jax.experimental import pallas as pl
from jax.experimental.pallas import tpu as pltpu

F32 = jnp.float32
BF16 = jnp.bfloat16

RMS_EPS = 1e-6
HGRN_HEAD = 128
HGRN_CHUNK = 32
ATT_HEAD = 64
SWA_WINDOW = 128
SWA_GROUP = 8
ROT_DIM = 16
ROPE_THETA = 500000.0
CONV_WIDTH = 3
ADAM_LR = 0.001
ADAM_B1 = 0.9
ADAM_B2 = 0.999
ADAM_EPS = 1e-08
ADAM_WD = 0.01
ADAM_STEP = 10
N_CHIPS = 4
LANES = 128
BF16_ROWS = 16
VMEM_LIMIT = 48 * 1024 * 1024

_ARB = "arbitrary"
_PAR = "parallel"


def _params(sem, **kw):
    return pltpu.CompilerParams(dimension_semantics=sem, vmem_limit_bytes=VMEM_LIMIT, **kw)


def _tile(n, prefs):
    for p in prefs:
        if n % p == 0:
            return p
    return n


def _sigmoid(x):
    return 1.0 / (1.0 + jnp.exp(-x))


def _dot(a, b, dims):
    return lax.dot_general(a, b, (dims, ((), ())), preferred_element_type=F32)


def _dot_nn(a, b):
    return _dot(a, b, ((1,), (0,)))


def _dot_nt(a, b):
    return _dot(a, b, ((1,), (1,)))


def _dot_tn(a, b):
    return _dot(a, b, ((0,), (0,)))


def mm(a, b, mode, out_dtype, name="mm", b_idx=None, stack=1):
    bshape = b.shape if b_idx is None else b.shape[1:]
    if mode == "nn":
        (m, k), (k2, n) = a.shape, bshape
    elif mode == "nt":
        (m, k), (n, k2) = a.shape, bshape
    else:
        (k, m), (k2, n) = a.shape, bshape
    assert k == k2, (a.shape, b.shape, mode)
    tm = _tile(m, (512, 256, 128))
    tn = _tile(n // stack, (512, 256, 128))
    tk = _tile(k, (2048, 1408, 1024, 512, 256, 128))
    nk = k // tk
    nbs = n // stack // tn

    def body(a_ref, b_ref, o_ref, acc_ref):
        kk = pl.program_id(2)

        @pl.when(kk == 0)
        def _():
            acc_ref[...] = jnp.zeros_like(acc_ref)

        av = a_ref[...].astype(BF16)
        bv = b_ref[...].astype(BF16)
        if mode == "nn":
            acc_ref[...] += _dot_nn(av, bv)
        elif mode == "nt":
            acc_ref[...] += _dot_nt(av, bv)
        else:
            acc_ref[...] += _dot_tn(av, bv)

        @pl.when(kk == nk - 1)
        def _():
            o_ref[...] = acc_ref[...].astype(out_dtype)

    def b_block(rows, cols, imap):
        if b_idx is None:
            return pl.BlockSpec((rows, cols), imap)
        return pl.BlockSpec((None, rows, cols), lambda i, j, kk: (b_idx,) + imap(i, j, kk))

    if mode == "nn":
        a_spec = pl.BlockSpec((tm, tk), lambda i, j, kk: (i, kk))
        b_spec = b_block(tk, tn, lambda i, j, kk: (kk, j))
    elif mode == "nt":
        a_spec = pl.BlockSpec((tm, tk), lambda i, j, kk: (i, kk))
        b_spec = b_block(tn, tk, lambda i, j, kk: (j, kk))
    else:
        a_spec = pl.BlockSpec((tk, tm), lambda i, j, kk: (kk, i))
        b_spec = b_block(tk, tn, lambda i, j, kk: (kk, j))
    if stack == 1:
        out_spec = pl.BlockSpec((tm, tn), lambda i, j, kk: (i, j))
        out_shape = jax.ShapeDtypeStruct((m, n), out_dtype)
    else:
        out_spec = pl.BlockSpec((None, tm, tn), lambda i, j, kk: (j // nbs, i, j % nbs))
        out_shape = jax.ShapeDtypeStruct((stack, m, n // stack), out_dtype)
    return pl.pallas_call(
        body,
        name=name,
        grid=(m // tm, n // tn, nk),
        in_specs=[a_spec, b_spec],
        out_specs=out_spec,
        out_shape=out_shape,
        scratch_shapes=[pltpu.VMEM((tm, tn), F32)],
        compiler_params=_params((_PAR, _PAR, _ARB)),
    )(a, b)


def _rstd(xv):
    return lax.rsqrt(jnp.mean(xv * xv, axis=1, keepdims=True) + RMS_EPS)


def _row_spec(tr, w):
    return pl.BlockSpec((tr, w), lambda i: (i, 0))


def _full_spec(shape):
    nd = len(shape)
    return pl.BlockSpec(shape, lambda *_: (0,) * nd)


def rms_fwd(x, g, name):
    t, d = x.shape
    tr = _tile(t, (256, 128, 64, 32, 16))

    def body(x_ref, g_ref, o_ref):
        xv = x_ref[...]
        o_ref[...] = (xv * _rstd(xv) * g_ref[...]).astype(BF16)

    return pl.pallas_call(
        body, name=name, grid=(t // tr,),
        in_specs=[_row_spec(tr, d), _full_spec((1, d))],
        out_specs=_row_spec(tr, d),
        out_shape=jax.ShapeDtypeStruct((t, d), BF16),
        compiler_params=_params((_PAR,)),
    )(x, g)


def resid_norm(x, y, g_post, g_next, name):
    t, d = x.shape
    tr = _tile(t, (256, 128, 64, 32, 16))

    def body(x_ref, y_ref, gp_ref, gn_ref, x1_ref, hn_ref):
        yv = y_ref[...]
        x1 = x_ref[...] + yv * _rstd(yv) * gp_ref[...]
        x1_ref[...] = x1
        hn_ref[...] = (x1 * _rstd(x1) * gn_ref[...]).astype(BF16)

    return pl.pallas_call(
        body, name=name, grid=(t // tr,),
        in_specs=[_row_spec(tr, d), _row_spec(tr, d), _full_spec((1, d)), _full_spec((1, d))],
        out_specs=[_row_spec(tr, d), _row_spec(tr, d)],
        out_shape=[jax.ShapeDtypeStruct((t, d), F32), jax.ShapeDtypeStruct((t, d), BF16)],
        compiler_params=_params((_PAR,)),
    )(x, y, g_post, g_next)


def resid_loss(x, y, g_post, target, name):
    t, d = x.shape
    tr = _tile(t, (256, 128, 64, 32, 16))

    def body(x_ref, y_ref, gp_ref, t_ref, dx_ref, loss_ref):
        @pl.when(pl.program_id(0) == 0)
        def _():
            loss_ref[...] = jnp.zeros_like(loss_ref)

        yv = y_ref[...]
        err = x_ref[...] + yv * _rstd(yv) * gp_ref[...] - t_ref[...]
        dx_ref[...] = err * (1.0 / d)
        loss_ref[...] += 0.5 * jnp.sum(jnp.mean(err * err, axis=1, keepdims=True), axis=0, keepdims=True)

    dx, loss = pl.pallas_call(
        body, name=name, grid=(t // tr,),
        in_specs=[_row_spec(tr, d), _row_spec(tr, d), _full_spec((1, d)), _row_spec(tr, d)],
        out_specs=[_row_spec(tr, d), _full_spec((8, LANES))],
        out_shape=[jax.ShapeDtypeStruct((t, d), F32), jax.ShapeDtypeStruct((8, LANES), F32)],
        compiler_params=_params((_ARB,)),
    )(x, y, g_post, target)
    return dx, loss[0:1, 0:1]


def norm_bwd(yin, g, dout, res, out_dtype, name):
    t, d = yin.shape
    tr = _tile(t, (256, 128, 64, 32, 16))
    has_res = res is not None

    def body(*refs):
        if has_res:
            y_ref, g_ref, d_ref, r_ref, o_ref, dg_ref = refs
        else:
            y_ref, g_ref, d_ref, o_ref, dg_ref = refs

        @pl.when(pl.program_id(0) == 0)
        def _():
            dg_ref[...] = jnp.zeros_like(dg_ref)

        yv = y_ref[...]
        dv = d_ref[...].astype(F32)
        r = _rstd(yv)
        yh = yv * r
        dyh = dv * g_ref[...]
        dy = r * (dyh - yh * jnp.mean(dyh * yh, axis=1, keepdims=True))
        if has_res:
            dy = dy + r_ref[...]
        o_ref[...] = dy.astype(out_dtype)
        dg_ref[...] += jnp.sum(dv * yh, axis=0, keepdims=True)

    ins = [yin, g, dout] + ([res] if has_res else [])
    in_specs = [_row_spec(tr, d), _full_spec((1, d)), _row_spec(tr, d)] + ([_row_spec(tr, d)] if has_res else [])
    return pl.pallas_call(
        body, name=name, grid=(t // tr,),
        in_specs=in_specs,
        out_specs=[_row_spec(tr, d), _full_spec((1, d))],
        out_shape=[jax.ShapeDtypeStruct((t, d), out_dtype), jax.ShapeDtypeStruct((1, d), F32)],
        compiler_params=_params((_ARB,)),
    )(*ins)


def _shift_down(x, halo):
    tr = x.shape[0]
    row = lax.broadcasted_iota(jnp.int32, x.shape, 0)
    h1 = halo[BF16_ROWS - 1:BF16_ROWS, :]
    h2 = halo[BF16_ROWS - 2:BF16_ROWS - 1, :]
    x1 = jnp.where(row == 0, h1, pltpu.roll(x, 1, 0))
    x2 = jnp.where(row == 0, h2, jnp.where(row == 1, h1, pltpu.roll(x, 2, 0)))
    return x1, x2


def _shift_up(x, halo):
    tr = x.shape[0]
    row = lax.broadcasted_iota(jnp.int32, x.shape, 0)
    h0 = halo[0:1, :]
    h1 = halo[1:2, :]
    x1 = jnp.where(row == tr - 1, h0, pltpu.roll(x, tr - 1, 0))
    x2 = jnp.where(row == tr - 1, h1, jnp.where(row == tr - 2, h0, pltpu.roll(x, tr - 2, 0)))
    return x1, x2


def _prev_halo_spec(tr, w, nt):
    return pl.BlockSpec((BF16_ROWS, w), lambda i: (jnp.maximum(i * (tr // BF16_ROWS) - 1, 0), 0))


def _next_halo_spec(tr, w, nt):
    last = nt * (tr // BF16_ROWS) - 1
    return pl.BlockSpec((BF16_ROWS, w), lambda i: (jnp.minimum((i + 1) * (tr // BF16_ROWS), last), 0))


def _silu_and_grad(u):
    s = _sigmoid(u)
    return u * s, s * (1.0 + u * (1.0 - s))


def ffn_act(z, conv_w, conv_b, name):
    t, f2 = z.shape
    f = f2 // 2
    tr = _tile(t, (128, 64, 32, 16))
    nt = t // tr
    cw = _tile(f, (512, 256, 128))

    def body(z_ref, zp_ref, w_ref, b_ref, a_ref):
        first = pl.program_id(0) == 0
        for j in range(f // cw):
            us = []
            for off in (j * cw, f + j * cw):
                cols = slice(off, off + cw)
                zc = z_ref[:, cols].astype(F32)
                hp = jnp.where(first, 0.0, zp_ref[:, cols].astype(F32))
                z1, z2 = _shift_down(zc, hp)
                us.append(w_ref[2:3, cols] * zc + w_ref[1:2, cols] * z1 + w_ref[0:1, cols] * z2 + b_ref[:, cols])
            sil, _ = _silu_and_grad(us[0])
            a_ref[:, j * cw:(j + 1) * cw] = (sil * us[1]).astype(BF16)

    return pl.pallas_call(
        body, name=name, grid=(nt,),
        in_specs=[_row_spec(tr, f2), _prev_halo_spec(tr, f2, nt), _full_spec((CONV_WIDTH, f2)), _full_spec((1, f2))],
        out_specs=_row_spec(tr, f),
        out_shape=jax.ShapeDtypeStruct((t, f), BF16),
        compiler_params=_params((_PAR,)),
    )(z, z, conv_w, conv_b)


def ffn_act_bwd(z, da, conv_w, conv_b, name):
    t, f2 = z.shape
    f = f2 // 2
    tr = _tile(t, (128, 64, 32, 16))
    nt = t // tr
    cw = _tile(f, (512, 256, 128))

    def body(z_ref, zp_ref, da_ref, w_ref, b_ref, du_ref, acc_ref):
        first = pl.program_id(0) == 0

        @pl.when(first)
        def _():
            acc_ref[...] = jnp.zeros_like(acc_ref)

        for j in range(f // cw):
            us, zs = [], []
            for off in (j * cw, f + j * cw):
                cols = slice(off, off + cw)
                zc = z_ref[:, cols].astype(F32)
                hp = jnp.where(first, 0.0, zp_ref[:, cols].astype(F32))
                z1, z2 = _shift_down(zc, hp)
                zs.append((z2, z1, zc))
                us.append(w_ref[2:3, cols] * zc + w_ref[1:2, cols] * z1 + w_ref[0:1, cols] * z2 + b_ref[:, cols])
            dav = da_ref[:, j * cw:(j + 1) * cw].astype(F32)
            sil, dsil = _silu_and_grad(us[0])
            dus = (dav * us[1] * dsil, dav * sil)
            for off, du, zsh in zip((j * cw, f + j * cw), dus, zs):
                cols = slice(off, off + cw)
                du_ref[:, cols] = du.astype(BF16)
                for k in range(CONV_WIDTH):
                    acc_ref[k:k + 1, cols] += jnp.sum(du * zsh[k], axis=0, keepdims=True)
                acc_ref[3:4, cols] += jnp.sum(du, axis=0, keepdims=True)

    return pl.pallas_call(
        body, name=name, grid=(nt,),
        in_specs=[_row_spec(tr, f2), _prev_halo_spec(tr, f2, nt), _row_spec(tr, f),
                  _full_spec((CONV_WIDTH, f2)), _full_spec((1, f2))],
        out_specs=[_row_spec(tr, f2), _full_spec((8, f2))],
        out_shape=[jax.ShapeDtypeStruct((t, f2), BF16), jax.ShapeDtypeStruct((8, f2), F32)],
        compiler_params=_params((_ARB,)),
    )(z, z, da, conv_w, conv_b)


def conv_transpose(du, conv_w, name):
    t, w = du.shape
    tr = _tile(t, (128, 64, 32, 16))
    nt = t // tr
    cw = _tile(w, (512, 256, 128))

    def body(d_ref, dn_ref, w_ref, o_ref):
        last = pl.program_id(0) == nt - 1
        for j in range(w // cw):
            cols = slice(j * cw, (j + 1) * cw)
            dc = d_ref[:, cols].astype(F32)
            hn = jnp.where(last, 0.0, dn_ref[:, cols].astype(F32))
            d1, d2 = _shift_up(dc, hn)
            o_ref[:, cols] = (w_ref[2:3, cols] * dc + w_ref[1:2, cols] * d1 + w_ref[0:1, cols] * d2).astype(BF16)

    return pl.pallas_call(
        body, name=name, grid=(nt,),
        in_specs=[_row_spec(tr, w), _next_halo_spec(tr, w, nt), _full_spec((CONV_WIDTH, w))],
        out_specs=_row_spec(tr, w),
        out_shape=jax.ShapeDtypeStruct((t, w), BF16),
        compiler_params=_params((_PAR,)),
    )(du, du, conv_w)


def sconv_fwd(proj, conv_w, name):
    t, w3 = proj.shape
    d = w3 // 3
    tr = _tile(t, (128, 64, 32, 16))
    nt = t // tr
    cw = _tile(d, (512, 256, 128))

    def body(p_ref, pp_ref, w_ref, o_ref):
        first = pl.program_id(0) == 0
        for j in range(d // cw):
            cb, cc, cx = (slice(k * d + j * cw, k * d + (j + 1) * cw) for k in range(3))
            zc = p_ref[:, cc].astype(F32) * p_ref[:, cx].astype(F32)
            hp = jnp.where(first, 0.0, pp_ref[:, cc].astype(F32) * pp_ref[:, cx].astype(F32))
            z1, z2 = _shift_down(zc, hp)
            wc = slice(j * cw, (j + 1) * cw)
            cz = w_ref[2:3, wc] * zc + w_ref[1:2, wc] * z1 + w_ref[0:1, wc] * z2
            o_ref[:, wc] = (p_ref[:, cb].astype(F32) * cz).astype(BF16)

    return pl.pallas_call(
        body, name=name, grid=(nt,),
        in_specs=[_row_spec(tr, w3), _prev_halo_spec(tr, w3, nt), _full_spec((CONV_WIDTH, d))],
        out_specs=_row_spec(tr, d),
        out_shape=jax.ShapeDtypeStruct((t, d), BF16),
        compiler_params=_params((_PAR,)),
    )(proj, proj, conv_w)


def sconv_bwd(proj, dyb, conv_w, name):
    t, w3 = proj.shape
    d = w3 // 3
    tr = _tile(t, (128, 64, 32, 16))
    nt = t // tr
    cw = _tile(d, (512, 256, 128))

    def body(p_ref, pp_ref, pn_ref, dy_ref, dyn_ref, w_ref, o_ref, acc_ref):
        first = pl.program_id(0) == 0
        last = pl.program_id(0) == nt - 1

        @pl.when(first)
        def _():
            acc_ref[...] = jnp.zeros_like(acc_ref)

        for j in range(d // cw):
            cb, cc, cx = (slice(k * d + j * cw, k * d + (j + 1) * cw) for k in range(3))
            wc = slice(j * cw, (j + 1) * cw)
            bv, cv, xv = p_ref[:, cb].astype(F32), p_ref[:, cc].astype(F32), p_ref[:, cx].astype(F32)
            zc = cv * xv
            hp = jnp.where(first, 0.0, pp_ref[:, cc].astype(F32) * pp_ref[:, cx].astype(F32))
            z1, z2 = _shift_down(zc, hp)
            w0, w1, w2 = w_ref[0:1, wc], w_ref[1:2, wc], w_ref[2:3, wc]
            cz = w2 * zc + w1 * z1 + w0 * z2
            dyv = dy_ref[:, wc].astype(F32)
            dcz = dyv * bv
            hn = jnp.where(last, 0.0, dyn_ref[:, wc].astype(F32) * pn_ref[:, cb].astype(F32))
            n1, n2 = _shift_up(dcz, hn)
            dz = w2 * dcz + w1 * n1 + w0 * n2
            o_ref[:, cb] = (dyv * cz).astype(BF16)
            o_ref[:, cc] = (dz * xv).astype(BF16)
            o_ref[:, cx] = (dz * cv).astype(BF16)
            for k, zsh in enumerate((z2, z1, zc)):
                acc_ref[k:k + 1, wc] += jnp.sum(dcz * zsh, axis=0, keepdims=True)

    return pl.pallas_call(
        body, name=name, grid=(nt,),
        in_specs=[_row_spec(tr, w3), _prev_halo_spec(tr, w3, nt), _next_halo_spec(tr, w3, nt),
                  _row_spec(tr, d), _next_halo_spec(tr, d, nt), _full_spec((CONV_WIDTH, d))],
        out_specs=[_row_spec(tr, w3), _full_spec((8, d))],
        out_shape=[jax.ShapeDtypeStruct((t, w3), BF16), jax.ShapeDtypeStruct((8, d), F32)],
        compiler_params=_params((_ARB,)),
    )(proj, proj, proj, dyb, dyb, conv_w)


def rope_tables(positions):
    half = ROT_DIM // 2
    inv_freq = ROPE_THETA ** (-jnp.arange(half, dtype=F32) / half)
    ang = positions.astype(F32)[:, None] * inv_freq[None, :]
    cos, sin = jnp.cos(ang), jnp.sin(ang)
    ones = jnp.ones((positions.shape[0], ATT_HEAD - ROT_DIM), F32)
    c64 = jnp.concatenate([cos, cos, ones], axis=1)
    s64 = jnp.concatenate([-sin, sin, 0.0 * ones], axis=1)
    perm = np.zeros((LANES, LANES), np.float32)
    for lane in range(LANES):
        dim = lane % ATT_HEAD
        if dim < half:
            perm[lane + half, lane] = 1.0
        elif dim < ROT_DIM:
            perm[lane - half, lane] = 1.0
    return jnp.tile(c64, (1, 2)), jnp.tile(s64, (1, 2)), jnp.asarray(perm, BF16)


def rope(xin, ctab, stab, perm, n_rot, sign, name):
    t, w = xin.shape
    tr = _tile(t, (256, 128, 64, 32, 16))

    def body(x_ref, c_ref, s_ref, p_ref, o_ref):
        cv, sv = c_ref[...], s_ref[...] * sign
        for j in range(n_rot // LANES):
            cols = slice(j * LANES, (j + 1) * LANES)
            xb = x_ref[:, cols]
            o_ref[:, cols] = (xb.astype(F32) * cv + _dot_nn(xb, p_ref[...]) * sv).astype(BF16)
        if n_rot < w:
            o_ref[:, n_rot:] = x_ref[:, n_rot:]

    return pl.pallas_call(
        body, name=name, grid=(t // tr,),
        in_specs=[_row_spec(tr, w), _row_spec(tr, LANES), _row_spec(tr, LANES), _full_spec((LANES, LANES))],
        out_specs=_row_spec(tr, w),
        out_shape=jax.ShapeDtypeStruct((t, w), BF16),
        compiler_params=_params((_PAR,)),
    )(xin, ctab, stab, perm)


NEG = -1e30


def _half(shape, h):
    return (lax.broadcasted_iota(jnp.int32, shape, 1) // ATT_HEAD) == h


def _dup_head(xb, kvh):
    xf = jnp.where(_half(xb.shape, kvh), xb.astype(F32), 0.0)
    return (xf + pltpu.roll(xf, ATT_HEAD, 1)).astype(BF16)


def _swa_mask(n, rows, cur_only):
    w = SWA_WINDOW
    shape = (w, w) if cur_only else (w, 2 * w)
    qi = lax.broadcasted_iota(jnp.int32, shape, 0)
    kj = lax.broadcasted_iota(jnp.int32, shape, 1) + (w if cur_only else 0)
    diff = qi + w - kj
    ok = (diff >= 0) & (diff < w)
    return ok & ((kj >= w) | (n > 0))


def swa_fwd(qkv, sinks, hq, name):
    t = qkv.shape[0]
    w = SWA_WINDOW
    nb = t // w
    hkv = hq // SWA_GROUP
    npair = hkv // 2
    qw = 2 * SWA_GROUP * ATT_HEAD
    kcol = hq * ATT_HEAD // LANES
    vcol = kcol + npair
    scale = ATT_HEAD ** -0.5

    def body(sink_ref, q_ref, kp_ref, kc_ref, vp_ref, vc_ref, o_ref, lse_ref):
        m, n = pl.program_id(0), pl.program_id(1)
        kb = jnp.concatenate([kp_ref[...], kc_ref[...]], axis=0)
        vb = jnp.concatenate([vp_ref[...], vc_ref[...]], axis=0)
        ok = _swa_mask(n, w, False)
        for kvh in range(2):
            kd, vd = _dup_head(kb, kvh), _dup_head(vb, kvh)
            for jj in range(SWA_GROUP // 2):
                jp = kvh * (SWA_GROUP // 2) + jj
                q2 = q_ref[:, jp * LANES:(jp + 1) * LANES]
                outs = []
                for a in range(2):
                    qa = jnp.where(_half(q2.shape, a), q2, jnp.zeros_like(q2))
                    s = jnp.where(ok, _dot_nt(qa, kd) * scale, NEG)
                    sink = sink_ref[m * 2 * SWA_GROUP + jp * 2 + a]
                    mx = jnp.maximum(jnp.max(s, axis=1, keepdims=True), sink)
                    e = jnp.exp(s - mx)
                    den = jnp.sum(e, axis=1, keepdims=True) + jnp.exp(sink - mx)
                    p = (e / den).astype(BF16)
                    outs.append(_dot_nn(p, vd))
                    lse_ref[jp * 2 + a] = jnp.broadcast_to(mx + jnp.log(den), (w, LANES))
                o_ref[:, jp * LANES:(jp + 1) * LANES] = jnp.where(_half(outs[0].shape, 0), outs[0], outs[1]).astype(BF16)

    prev = lambda m, n: jnp.maximum(n - 1, 0)
    grid_spec = pltpu.PrefetchScalarGridSpec(
        num_scalar_prefetch=1, grid=(npair, nb),
        in_specs=[
            pl.BlockSpec((w, qw), lambda m, n, s: (n, m)),
            pl.BlockSpec((w, LANES), lambda m, n, s: (prev(m, n), kcol + m)),
            pl.BlockSpec((w, LANES), lambda m, n, s: (n, kcol + m)),
            pl.BlockSpec((w, LANES), lambda m, n, s: (prev(m, n), vcol + m)),
            pl.BlockSpec((w, LANES), lambda m, n, s: (n, vcol + m)),
        ],
        out_specs=[
            pl.BlockSpec((w, qw), lambda m, n, s: (n, m)),
            pl.BlockSpec((2 * SWA_GROUP, w, LANES), lambda m, n, s: (m, n, 0)),
        ],
    )
    return pl.pallas_call(
        body, name=name, grid_spec=grid_spec,
        out_shape=[jax.ShapeDtypeStruct((t, hq * ATT_HEAD), BF16), jax.ShapeDtypeStruct((hq, t, LANES), F32)],
        compiler_params=_params((_PAR, _PAR)),
    )(sinks, qkv, qkv, qkv, qkv, qkv)


def swa_bwd(qkv, o, lse, do, sinks, hq, name):
    t = qkv.shape[0]
    w = SWA_WINDOW
    nb = t // w
    hkv = hq // SWA_GROUP
    npair = hkv // 2
    qw = 2 * SWA_GROUP * ATT_HEAD
    kcol = hq * ATT_HEAD // LANES
    vcol = kcol + npair
    scale = ATT_HEAD ** -0.5
    gh = 2 * SWA_GROUP

    def body(sink_ref, qc_ref, qn_ref, kp_ref, kc_ref, vp_ref, vc_ref, oc_ref, on_ref, dc_ref, dn_ref,
             lc_ref, ln_ref, dq_ref, dk_ref, dv_ref, ds_ref):
        m, n = pl.program_id(0), pl.program_id(1)
        kb = jnp.concatenate([kp_ref[...], kc_ref[...]], axis=0)
        vb = jnp.concatenate([vp_ref[...], vc_ref[...]], axis=0)
        ok_band = _swa_mask(n, w, False)
        ok_cur = _swa_mask(n, w, True)
        qi = lax.broadcasted_iota(jnp.int32, (w, w), 0)
        kj = lax.broadcasted_iota(jnp.int32, (w, w), 1)
        ok_next = (kj > qi) & (n < nb - 1)
        row16 = lax.broadcasted_iota(jnp.int32, (gh, LANES), 0)
        dsink = jnp.zeros((gh, LANES), F32)
        dk_tot = jnp.zeros((w, LANES), F32)
        dv_tot = jnp.zeros((w, LANES), F32)
        for kvh in range(2):
            kd, vd = _dup_head(kb, kvh), _dup_head(vb, kvh)
            kdc, vdc = kd[w:, :], vd[w:, :]
            acc_k = [jnp.zeros((w, LANES), F32), jnp.zeros((w, LANES), F32)]
            acc_v = [jnp.zeros((w, LANES), F32), jnp.zeros((w, LANES), F32)]
            for jj in range(SWA_GROUP // 2):
                jp = kvh * (SWA_GROUP // 2) + jj
                cols = slice(jp * LANES, (jp + 1) * LANES)
                dqs = []
                for a in range(2):
                    hd = jp * 2 + a
                    sink = sink_ref[m * gh + hd]
                    half = _half((w, LANES), a)
                    q2 = jnp.where(half, qc_ref[:, cols], jnp.zeros((w, LANES), BF16))
                    d2 = jnp.where(half, dc_ref[:, cols], jnp.zeros((w, LANES), BF16))
                    delta = jnp.sum(d2.astype(F32) * oc_ref[:, cols].astype(F32), axis=1, keepdims=True)
                    lse_c = lc_ref[hd][:, 0:1]
                    p = jnp.exp(jnp.where(ok_band, _dot_nt(q2, kd) * scale, NEG) - lse_c)
                    dsv = p * (_dot_nt(d2, vd) - delta)
                    dqs.append(_dot_nn(dsv.astype(BF16), kd) * scale)
                    psink = jnp.exp(sink - lse_c)
                    dsink = jnp.where(row16 == hd, dsink - jnp.sum(psink * delta, axis=0, keepdims=True), dsink)
                    for q_ref, d_ref, o_ref, l_ref, okm in ((qc_ref, dc_ref, oc_ref, lc_ref, ok_cur),
                                                           (qn_ref, dn_ref, on_ref, ln_ref, ok_next)):
                        q2 = jnp.where(half, q_ref[:, cols], jnp.zeros((w, LANES), BF16))
                        d2 = jnp.where(half, d_ref[:, cols], jnp.zeros((w, LANES), BF16))
                        delta = jnp.sum(d2.astype(F32) * o_ref[:, cols].astype(F32), axis=1, keepdims=True)
                        p = jnp.exp(jnp.where(okm, _dot_nt(q2, kdc) * scale, NEG) - l_ref[hd][:, 0:1])
                        dsv = p * (_dot_nt(d2, vdc) - delta)
                        acc_v[a] = acc_v[a] + _dot_tn(p.astype(BF16), d2)
                        acc_k[a] = acc_k[a] + _dot_tn(dsv.astype(BF16), q2) * scale
                dq_ref[:, cols] = jnp.where(_half((w, LANES), 0), dqs[0], dqs[1]).astype(BF16)
            dk_tot = dk_tot + acc_k[kvh] + pltpu.roll(acc_k[1 - kvh], ATT_HEAD, 1)
            dv_tot = dv_tot + acc_v[kvh] + pltpu.roll(acc_v[1 - kvh], ATT_HEAD, 1)
        dk_ref[...] = dk_tot.astype(BF16)
        dv_ref[...] = dv_tot.astype(BF16)
        ds_ref[0, 0] = dsink

    prev = lambda n: jnp.maximum(n - 1, 0)
    nxt = lambda n: jnp.minimum(n + 1, nb - 1)
    qspec = lambda f: pl.BlockSpec((w, qw), lambda m, n, s: (f(n), m))
    lspec = lambda f: pl.BlockSpec((gh, w, LANES), lambda m, n, s: (m, f(n), 0))
    same = lambda n: n
    grid_spec = pltpu.PrefetchScalarGridSpec(
        num_scalar_prefetch=1, grid=(npair, nb),
        in_specs=[
            qspec(same), qspec(nxt),
            pl.BlockSpec((w, LANES), lambda m, n, s: (prev(n), kcol + m)),
            pl.BlockSpec((w, LANES), lambda m, n, s: (n, kcol + m)),
            pl.BlockSpec((w, LANES), lambda m, n, s: (prev(n), vcol + m)),
            pl.BlockSpec((w, LANES), lambda m, n, s: (n, vcol + m)),
            qspec(same), qspec(nxt), qspec(same), qspec(nxt),
            lspec(same), lspec(nxt),
        ],
        out_specs=[
            pl.BlockSpec((w, qw), lambda m, n, s: (n, m)),
            pl.BlockSpec((w, LANES), lambda m, n, s: (n, m)),
            pl.BlockSpec((w, LANES), lambda m, n, s: (n, m)),
            pl.BlockSpec((1, 1, gh, LANES), lambda m, n, s: (m, n, 0, 0)),
        ],
    )
    return pl.pallas_call(
        body, name=name, grid_spec=grid_spec,
        out_shape=[jax.ShapeDtypeStruct((t, hq * ATT_HEAD), BF16),
                   jax.ShapeDtypeStruct((t, hkv * ATT_HEAD), BF16),
                   jax.ShapeDtypeStruct((t, hkv * ATT_HEAD), BF16),
                   jax.ShapeDtypeStruct((npair, nb, gh, LANES), F32)],
        compiler_params=_params((_PAR, _PAR)),
    )(sinks, qkv, qkv, qkv, qkv, qkv, qkv, o, o, do, do, lse, lse)


def swa_mixer_fwd(hn, w_in, w_out, sinks, positions, name):
    hq = sinks.shape[0]
    n_rot = (hq + hq // SWA_GROUP) * ATT_HEAD
    tabs = rope_tables(positions)
    proj = mm(hn, w_in, "nn", BF16, name + "_in")
    qkv = rope(proj, *tabs, n_rot, 1.0, name + "_rope")
    o, lse = swa_fwd(qkv, sinks, hq, name + "_att")
    y = mm(o, w_out, "nn", F32, name + "_out")
    return y, (qkv, o, lse)


def swa_mixer_bwd(dy, hn, w_in, w_out, sinks, positions, saved, name, ns=1):
    qkv, o, lse = saved
    hq = sinks.shape[0]
    n_rot = (hq + hq // SWA_GROUP) * ATT_HEAD
    tabs = rope_tables(positions)
    dwo = mm(o, dy, "tn", BF16, name + "_dwo")
    do = mm(dy, w_out, "nt", BF16, name + "_do")
    dq, dk, dv, dsp = swa_bwd(qkv, o, lse, do, sinks, hq, name + "_attb")
    dproj = rope(jnp.concatenate([dq, dk, dv], axis=1), *tabs, n_rot, -1.0, name + "_ropeb")
    dwi = mm(hn, dproj, "tn", BF16, name + "_dwi", stack=ns)
    dhn = mm(dproj, w_in, "nt", BF16, name + "_dhn")
    dsinks = jnp.sum(dsp[:, :, :, 0], axis=1).reshape(hq)
    return dhn, dwi, dwo, dsinks


FOX_FPAD = 512


def _log_sigmoid(x):
    return jnp.minimum(x, 0.0) - jnp.log(1.0 + jnp.exp(-jnp.abs(x)))


def _tri(n, upper):
    r = lax.broadcasted_iota(jnp.int32, (n, n), 0)
    c = lax.broadcasted_iota(jnp.int32, (n, n), 1)
    return jnp.where((c >= r) if upper else (c <= r), 1.0, 0.0).astype(F32)


def _dot_exact(a, b):
    return jnp.dot(a, b, precision=lax.Precision.HIGHEST, preferred_element_type=F32)


def fox_cumsum(fl, b_pad, name):
    t = fl.shape[0]
    tr = _tile(t, (256, 128, 64, 32, 16, 8))

    def body(f_ref, b_ref, c_ref, carry_ref):
        @pl.when(pl.program_id(0) == 0)
        def _():
            carry_ref[...] = jnp.zeros_like(carry_ref)

        c = _dot_exact(_tri(tr, False), _log_sigmoid(f_ref[...] + b_ref[...])) + carry_ref[...]
        c_ref[...] = c
        carry_ref[...] = c[tr - 1:tr, :]

    return pl.pallas_call(
        body, name=name, grid=(t // tr,),
        in_specs=[_row_spec(tr, LANES), _full_spec((1, LANES))],
        out_specs=_row_spec(tr, LANES),
        out_shape=jax.ShapeDtypeStruct((t, LANES), F32),
        scratch_shapes=[pltpu.VMEM((1, LANES), F32)],
        compiler_params=_params((_ARB,)),
    )(fl, b_pad)


def fox_cumsum_bwd(dc, fl, b_pad, name):
    t = fl.shape[0]
    tr = _tile(t, (256, 128, 64, 32, 16, 8))
    nt = t // tr

    def body(d_ref, f_ref, b_ref, o_ref, db_ref, carry_ref):
        @pl.when(pl.program_id(0) == 0)
        def _():
            carry_ref[...] = jnp.zeros_like(carry_ref)
            db_ref[...] = jnp.zeros_like(db_ref)

        dlf = _dot_exact(_tri(tr, True), d_ref[...]) + carry_ref[...]
        carry_ref[...] = dlf[0:1, :]
        dfl = dlf * _sigmoid(-(f_ref[...] + b_ref[...]))
        o_ref[...] = dfl.astype(BF16)
        db_ref[...] += jnp.sum(dfl, axis=0, keepdims=True)

    rev = pl.BlockSpec((tr, LANES), lambda i: (nt - 1 - i, 0))
    return pl.pallas_call(
        body, name=name, grid=(nt,),
        in_specs=[rev, rev, _full_spec((1, LANES))],
        out_specs=[rev, _full_spec((1, LANES))],
        out_shape=[jax.ShapeDtypeStruct((t, LANES), BF16), jax.ShapeDtypeStruct((1, LANES), F32)],
        scratch_shapes=[pltpu.VMEM((1, LANES), F32)],
        compiler_params=_params((_ARB,)),
    )(dc, fl, b_pad)


def _fox_tile(t):
    return _tile(t, (256, 128))


def _causal(i, j, tq, tk):
    r = lax.broadcasted_iota(jnp.int32, (tq, tk), 0) + i * tq
    c = lax.broadcasted_iota(jnp.int32, (tq, tk), 1) + j * tk
    return c <= r


def _lanes_to(x, tk):
    return x if tk == LANES else jnp.tile(x, (1, tk // LANES))


def fox_fwd(proj, ccol, crow, nh, name):
    t = proj.shape[0]
    hd = nh * ATT_HEAD
    npair = nh // 2
    tq = tk = _fox_tile(t)
    nt = t // tq
    gcol = (3 * hd + FOX_FPAD) // LANES
    scale = ATT_HEAD ** -0.5

    def body(q_ref, k_ref, v_ref, g_ref, cc_ref, cr_ref, o_ref, og_ref, lse_ref):
        i = pl.program_id(1)
        q2 = q_ref[...]
        outs = []
        for h in range(2):
            qa = jnp.where(_half(q2.shape, h), q2, jnp.zeros_like(q2))
            cq = _lanes_to(cc_ref[h], tk)

            def step(j, carry, qa=qa, cq=cq, h=h):
                mx, l, acc = carry
                rows = pl.ds(pl.multiple_of(j * tk, tk), tk)
                s = _dot_nt(qa, k_ref[rows, :]) * scale + cq - cr_ref[0, h, pl.ds(j, 1), :]
                s = jnp.where(_causal(i, j, tq, tk), s, NEG)
                mnew = jnp.maximum(mx, jnp.max(s, axis=1, keepdims=True))
                alpha = jnp.exp(mx - mnew)
                p = jnp.exp(s - mnew)
                l = alpha * l + jnp.sum(p, axis=1, keepdims=True)
                acc = alpha * acc + _dot_nn(p.astype(BF16), v_ref[rows, :])
                return mnew, l, acc

            init = (jnp.full((tq, 1), NEG, F32), jnp.zeros((tq, 1), F32), jnp.zeros((tq, LANES), F32))
            mx, l, acc = lax.fori_loop(0, i + 1, step, init)
            outs.append(acc / l)
            lse_ref[h] = jnp.broadcast_to(mx + jnp.log(l), (tq, LANES))
        o = jnp.where(_half(q2.shape, 0), outs[0], outs[1])
        o_ref[...] = o.astype(BF16)
        og_ref[...] = (o * _sigmoid(g_ref[...].astype(F32))).astype(BF16)

    tile = lambda col: pl.BlockSpec((tq, LANES), lambda p, i: (i, col + p))
    resident = lambda col: pl.BlockSpec((t, LANES), lambda p, i: (0, col + p))
    return pl.pallas_call(
        body, name=name, grid=(npair, nt),
        in_specs=[tile(0), resident(npair), resident(2 * npair), tile(gcol),
                  pl.BlockSpec((2, tq, LANES), lambda p, i: (p, i, 0)),
                  pl.BlockSpec((1, 2, nt, tk), lambda p, i: (p, 0, 0, 0))],
        out_specs=[pl.BlockSpec((tq, LANES), lambda p, i: (i, p)),
                   pl.BlockSpec((tq, LANES), lambda p, i: (i, p)),
                   pl.BlockSpec((2, tq, LANES), lambda p, i: (p, i, 0))],
        out_shape=[jax.ShapeDtypeStruct((t, hd), BF16), jax.ShapeDtypeStruct((t, hd), BF16),
                   jax.ShapeDtypeStruct((nh, t, LANES), F32)],
        compiler_params=_params((_PAR, _PAR)),
    )(proj, proj, proj, proj, ccol, crow)


def fox_gate_bwd(dog, o, proj, nh, name):
    t, hd = o.shape
    npair = nh // 2
    tr = _tile(t, (256, 128))
    gcol = (3 * hd + FOX_FPAD) // LANES

    def body(d_ref, o_ref, g_ref, do_ref, dg_ref, dl_ref):
        dv, ov = d_ref[...].astype(F32), o_ref[...].astype(F32)
        sg = _sigmoid(g_ref[...].astype(F32))
        do = (dv * sg).astype(BF16)
        do_ref[...] = do
        dg_ref[...] = (dv * ov * sg * (1.0 - sg)).astype(BF16)
        prod = do.astype(F32) * ov
        for h in range(2):
            dl = jnp.sum(jnp.where(_half(prod.shape, h), prod, 0.0), axis=1, keepdims=True)
            dl_ref[h] = jnp.broadcast_to(dl, (tr, LANES))

    blk = pl.BlockSpec((tr, LANES), lambda p, i: (i, p))
    return pl.pallas_call(
        body, name=name, grid=(npair, t // tr),
        in_specs=[blk, blk, pl.BlockSpec((tr, LANES), lambda p, i: (i, gcol + p))],
        out_specs=[blk, blk, pl.BlockSpec((2, tr, LANES), lambda p, i: (p, i, 0))],
        out_shape=[jax.ShapeDtypeStruct((t, hd), BF16), jax.ShapeDtypeStruct((t, hd), BF16),
                   jax.ShapeDtypeStruct((nh, t, LANES), F32)],
        compiler_params=_params((_PAR, _PAR)),
    )(dog, o, proj)


def fox_dq(proj, do, lse, delta, ccol, crow, nh, name):
    t = proj.shape[0]
    hd = nh * ATT_HEAD
    npair = nh // 2
    tq = tk = _fox_tile(t)
    nt = t // tq
    scale = ATT_HEAD ** -0.5

    def body(q_ref, k_ref, v_ref, do_ref, l_ref, dl_ref, cc_ref, cr_ref, dq_ref, rs_ref):
        i = pl.program_id(1)
        q2, d2 = q_ref[...], do_ref[...]
        acc = jnp.zeros((tq, LANES), F32)
        for h in range(2):
            half = _half(q2.shape, h)
            qa = jnp.where(half, q2, jnp.zeros_like(q2))
            da = jnp.where(half, d2, jnp.zeros_like(d2))
            cq = _lanes_to(cc_ref[h], tk)
            lse = _lanes_to(l_ref[h], tk)
            dl = _lanes_to(dl_ref[h], tk)

            def step(j, carry, qa=qa, da=da, cq=cq, lse=lse, dl=dl, h=h):
                acc, rsum = carry
                rows = pl.ds(pl.multiple_of(j * tk, tk), tk)
                kj = k_ref[rows, :]
                s = _dot_nt(qa, kj) * scale + cq - cr_ref[0, h, pl.ds(j, 1), :]
                p = jnp.exp(jnp.where(_causal(i, j, tq, tk), s, NEG) - lse)
                dsv = p * (_dot_nt(da, v_ref[rows, :]) - dl)
                kh = jnp.where(_half(kj.shape, h), kj, jnp.zeros_like(kj))
                return acc + _dot_nn(dsv.astype(BF16), kh), rsum + jnp.sum(dsv, axis=1, keepdims=True)

            acc, rsum = lax.fori_loop(0, i + 1, step, (acc, jnp.zeros((tq, 1), F32)))
            rs_ref[h] = jnp.broadcast_to(rsum, (tq, LANES))
        dq_ref[...] = (acc * scale).astype(BF16)

    tile = lambda col: pl.BlockSpec((tq, LANES), lambda p, i: (i, col + p))
    resident = lambda col: pl.BlockSpec((t, LANES), lambda p, i: (0, col + p))
    stat = pl.BlockSpec((2, tq, LANES), lambda p, i: (p, i, 0))
    return pl.pallas_call(
        body, name=name, grid=(npair, nt),
        in_specs=[tile(0), resident(npair), resident(2 * npair), tile(0), stat, stat, stat,
                  pl.BlockSpec((1, 2, nt, tk), lambda p, i: (p, 0, 0, 0))],
        out_specs=[pl.BlockSpec((tq, LANES), lambda p, i: (i, p)), stat],
        out_shape=[jax.ShapeDtypeStruct((t, hd), BF16), jax.ShapeDtypeStruct((nh, t, LANES), F32)],
        compiler_params=_params((_PAR, _PAR)),
    )(proj, proj, proj, do, lse, delta, ccol, crow)


def fox_dkv(proj, do, lse, delta, ccol, crow, nh, name):
    t = proj.shape[0]
    hd = nh * ATT_HEAD
    npair = nh // 2
    tq = tk = _fox_tile(t)
    nt = t // tq
    scale = ATT_HEAD ** -0.5

    def body(q_ref, k_ref, v_ref, do_ref, l_ref, dl_ref, cc_ref, cr_ref, dk_ref, dv_ref, dc_ref):
        j = pl.program_id(1)
        k2, v2 = k_ref[...], v_ref[...]
        dks, dvs = [], []
        for h in range(2):
            half = _half(k2.shape, h)
            kh = jnp.where(half, k2, jnp.zeros_like(k2))
            vh = jnp.where(half, v2, jnp.zeros_like(v2))
            ck = cr_ref[0, h, pl.ds(j, 1), :]

            def step(i, carry, kh=kh, vh=vh, ck=ck, h=h):
                dk, dv, dc = carry
                rows = pl.ds(pl.multiple_of(i * tq, tq), tq)
                qi, di = q_ref[rows, :], do_ref[rows, :]
                s = _dot_nt(qi, kh) * scale + _lanes_to(cc_ref[h, rows, :], tk) - ck
                p = jnp.exp(jnp.where(_causal(i, j, tq, tk), s, NEG) - _lanes_to(l_ref[h, rows, :], tk))
                dsv = p * (_dot_nt(di, vh) - _lanes_to(dl_ref[h, rows, :], tk))
                dv = dv + _dot_tn(p.astype(BF16), di)
                dk = dk + _dot_tn(dsv.astype(BF16), qi)
                dc = dc - jnp.sum(dsv, axis=0, keepdims=True)
                return dk, dv, dc

            init = (jnp.zeros((tk, LANES), F32), jnp.zeros((tk, LANES), F32), jnp.zeros((1, tk), F32))
            dk, dv, dc = lax.fori_loop(j, nt, step, init)
            dks.append(dk)
            dvs.append(dv)
            dc_ref[0, 0, pl.ds(h, 1), :] = dc
        first = _half(k2.shape, 0)
        dk_ref[...] = (jnp.where(first, dks[0], dks[1]) * scale).astype(BF16)
        dv_ref[...] = jnp.where(first, dvs[0], dvs[1]).astype(BF16)

    tile = lambda col: pl.BlockSpec((tk, LANES), lambda p, j: (j, col + p))
    resident = lambda col: pl.BlockSpec((t, LANES), lambda p, j: (0, col + p))
    stat = pl.BlockSpec((2, t, LANES), lambda p, j: (p, 0, 0))
    return pl.pallas_call(
        body, name=name, grid=(npair, nt),
        in_specs=[resident(0), tile(npair), tile(2 * npair), resident(0), stat, stat, stat,
                  pl.BlockSpec((1, 2, nt, tk), lambda p, j: (p, 0, 0, 0))],
        out_specs=[pl.BlockSpec((tk, LANES), lambda p, j: (j, p)),
                   pl.BlockSpec((tk, LANES), lambda p, j: (j, p)),
                   pl.BlockSpec((1, 1, 2, tk), lambda p, j: (p, j, 0, 0))],
        out_shape=[jax.ShapeDtypeStruct((t, hd), BF16), jax.ShapeDtypeStruct((t, hd), BF16),
                   jax.ShapeDtypeStruct((npair, nt, 2, tk), F32)],
        compiler_params=_params((_PAR, _PAR)),
    )(proj, proj, proj, do, lse, delta, ccol, crow)


def fox_pad_w_in(w_in, nh):
    hd = nh * ATT_HEAD
    pad = jnp.zeros((w_in.shape[0], FOX_FPAD - nh), w_in.dtype)
    return jnp.concatenate([w_in[:, :3 * hd + nh], pad, w_in[:, 3 * hd + nh:]], axis=1)


def fox_unpad_dw(dw, nh):
    hd = nh * ATT_HEAD
    return jnp.concatenate([dw[:, :3 * hd + nh], dw[:, 3 * hd + FOX_FPAD:]], axis=1)


def _pad_lanes(v):
    return jnp.pad(v.reshape(1, -1).astype(F32), ((0, 0), (0, LANES - v.size)))


def fox_mixer_fwd(hn, w_pad, w_out, b_f, name):
    nh = b_f.shape[0]
    hd = nh * ATT_HEAD
    t = hn.shape[0]
    tk = _fox_tile(t)
    proj = mm(hn, w_pad, "nn", BF16, name + "_in")
    fl = mm(hn, w_pad[:, 3 * hd:3 * hd + LANES], "nn", F32, name + "_fl")
    c = fox_cumsum(fl, _pad_lanes(b_f), name + "_cum")
    ch = c[:, :nh].T
    ccol = jnp.broadcast_to(ch[:, :, None], (nh, t, LANES))
    crow = ch.reshape(nh // 2, 2, t // tk, tk)
    o, og, lse = fox_fwd(proj, ccol, crow, nh, name + "_att")
    y = mm(og, w_out, "nn", F32, name + "_out")
    return y, (proj, fl, ccol, crow, o, og, lse)


def fox_mixer_bwd(dy, hn, w_pad, w_out, b_f, saved, name):
    proj, fl, ccol, crow, o, og, lse = saved
    nh = b_f.shape[0]
    t = hn.shape[0]
    dwo = mm(og, dy, "tn", BF16, name + "_dwo")
    dog = mm(dy, w_out, "nt", BF16, name + "_dog")
    do, dg, delta = fox_gate_bwd(dog, o, proj, nh, name + "_gateb")
    dq, drow = fox_dq(proj, do, lse, delta, ccol, crow, nh, name + "_dq")
    dk, dv, dcr = fox_dkv(proj, do, lse, delta, ccol, crow, nh, name + "_dkv")
    dc = jnp.pad(drow[:, :, 0].T + dcr.transpose(1, 3, 0, 2).reshape(t, nh), ((0, 0), (0, LANES - nh)))
    dfl, db = fox_cumsum_bwd(dc, fl, _pad_lanes(b_f), name + "_cumb")
    dfl = jnp.pad(dfl, ((0, 0), (0, FOX_FPAD - LANES)))
    dproj = jnp.concatenate([dq, dk, dv, dfl, dg], axis=1)
    dwi = mm(hn, dproj, "tn", BF16, name + "_dwi")
    dhn = mm(dproj, w_pad, "nt", BF16, name + "_dhn")
    return dhn, dwi, dwo, db[0, :nh]


HGRN_ROWS = 256


def lb_table_row(lb_param, idx, name):
    nrow, w = lb_param.shape

    def body(p_ref, o_ref):
        rows = [p_ref[r:r + 1, :] for r in range(nrow)]
        mx = functools.reduce(jnp.maximum, rows)
        es = [jnp.exp(r - mx) for r in rows]
        o_ref[...] = sum(es[:idx + 1]) / sum(es)

    return pl.pallas_call(
        body, name=name, in_specs=[_full_spec((nrow, w))], out_specs=_full_spec((1, w)), grid=(1,),
        out_shape=jax.ShapeDtypeStruct((1, w), F32),
    )(lb_param)


def lb_table_row_bwd(lb_param, dlb, idx, name):
    nrow, w = lb_param.shape

    def body(p_ref, d_ref, o_ref):
        rows = [p_ref[r:r + 1, :] for r in range(nrow)]
        mx = functools.reduce(jnp.maximum, rows)
        es = [jnp.exp(r - mx) for r in rows]
        tot = sum(es)
        ps = [e / tot for e in es]
        dv = d_ref[...]
        inner = sum(ps[:idx + 1]) * dv
        for r in range(nrow):
            o_ref[r:r + 1, :] = ps[r] * ((dv if r <= idx else 0.0) - inner)

    return pl.pallas_call(
        body, name=name, in_specs=[_full_spec((nrow, w)), _full_spec((1, w))], out_specs=_full_spec((nrow, w)),
        grid=(1,), out_shape=jax.ShapeDtypeStruct((nrow, w), F32),
    )(lb_param, dlb)


def _hgrn_gates(qraw, fraw, lb):
    sq = _sigmoid(qraw)
    sf = _sigmoid(fraw)
    f = lb + (1.0 - lb) * sf
    return qraw * sq, sq, sf, f, 1.0 - f


def _hgrn_chunk(q, k, f):
    c = HGRN_CHUNK
    b = _dot_exact(_tri(c, False), jnp.log(f))
    bl = b[c - 1:c, :]
    bm = b[c // 2 - 1:c // 2, :]
    eq, ek = jnp.exp(b - bm), jnp.exp(bm - b)
    eb, el = jnp.exp(b), jnp.exp(bl - b)
    qt, kt = (q * eq).astype(BF16), (k * ek).astype(BF16)
    causal = _tri(c, False) > 0.5
    amat = jnp.where(causal, _dot_nt(qt, kt), 0.0).astype(BF16)
    return amat, qt, kt, (q * eb).astype(BF16), (k * el).astype(BF16), eq, ek, eb, el, jnp.exp(bl), causal


def hgrn_fwd(proj, lb, norm_g, nh, name):
    t = proj.shape[0]
    w = nh * HGRN_HEAD
    c = HGRN_CHUNK
    rows = _tile(t, (HGRN_ROWS, 128, 64, 32))
    nr, nc = t // rows, rows // c

    def body(q_ref, f_ref, i_ref, g_ref, lb_ref, ng_ref, og_ref, o_ref, st_ref, state):
        @pl.when(pl.program_id(1) == 0)
        def _():
            state[...] = jnp.zeros_like(state)

        def step(cc, carry):
            sl = pl.ds(pl.multiple_of(cc * c, c), c)
            q, _, _, f, k = _hgrn_gates(q_ref[sl, :].astype(F32), f_ref[sl, :].astype(F32), lb_ref[...])
            v = i_ref[sl, :]
            amat, _, _, qd, kd, _, _, _, _, ebl, _ = _hgrn_chunk(q, k, f)
            st = state[...]
            st_ref[0, cc] = st.astype(BF16)
            o = _dot_nt(qd, st.astype(BF16)) + _dot_nn(amat, v)
            state[...] = st * ebl + _dot_tn(v, kd)
            o_ref[sl, :] = o
            graw = g_ref[sl, :].astype(F32)
            og_ref[sl, :] = (o * _rstd(o) * ng_ref[...] * (graw * _sigmoid(graw))).astype(BF16)
            return carry

        lax.fori_loop(0, nc, step, 0)

    col = lambda off: pl.BlockSpec((rows, HGRN_HEAD), lambda h, r: (r, off + h))
    return pl.pallas_call(
        body, name=name, grid=(nh, nr),
        in_specs=[col(0), col(nh), col(2 * nh), col(3 * nh),
                  pl.BlockSpec((1, HGRN_HEAD), lambda h, r: (0, h)), _full_spec((1, HGRN_HEAD))],
        out_specs=[col(0), col(0), pl.BlockSpec((1, nc, HGRN_HEAD, HGRN_HEAD), lambda h, r: (h, r, 0, 0))],
        out_shape=[jax.ShapeDtypeStruct((t, w), BF16), jax.ShapeDtypeStruct((t, w), F32),
                   jax.ShapeDtypeStruct((nh, t // c, HGRN_HEAD, HGRN_HEAD), BF16)],
        scratch_shapes=[pltpu.VMEM((HGRN_HEAD, HGRN_HEAD), F32)],
        compiler_params=_params((_PAR, _ARB)),
    )(proj, proj, proj, proj, lb, norm_g)


def hgrn_bwd(proj, lb, norm_g, o, states, dog, nh, name):
    t = proj.shape[0]
    w = nh * HGRN_HEAD
    c = HGRN_CHUNK
    rows = _tile(t, (HGRN_ROWS, 128, 64, 32))
    nr, nc = t // rows, rows // c

    def body(q_ref, f_ref, i_ref, g_ref, lb_ref, ng_ref, o_ref, st_ref, dog_ref,
             dq_ref, df_ref, di_ref, dg_ref, dlb_ref, dng_ref, dstate):
        @pl.when(pl.program_id(1) == 0)
        def _():
            dstate[...] = jnp.zeros_like(dstate)
            dlb_ref[...] = jnp.zeros_like(dlb_ref)
            dng_ref[...] = jnp.zeros_like(dng_ref)

        lb = lb_ref[...]
        ng = ng_ref[...]

        def step(idx, carry):
            cc = nc - 1 - idx
            sl = pl.ds(pl.multiple_of(cc * c, c), c)
            qraw, fraw = q_ref[sl, :].astype(F32), f_ref[sl, :].astype(F32)
            q, sq, sf, f, k = _hgrn_gates(qraw, fraw, lb)
            v = i_ref[sl, :]
            amat, qt, kt, qd, kd, eq, ek, eb, el, ebl, causal = _hgrn_chunk(q, k, f)
            ov = o_ref[sl, :]
            graw = g_ref[sl, :].astype(F32)
            dogv = dog_ref[sl, :].astype(F32)
            sil, dsil = _silu_and_grad(graw)
            r = _rstd(ov)
            oh = ov * r
            don = dogv * sil
            dg_ref[sl, :] = (dogv * oh * ng * dsil).astype(BF16)
            dng_ref[0] += jnp.sum(don * oh, axis=0, keepdims=True)
            doh = don * ng
            do = (r * (doh - oh * jnp.mean(doh * oh, axis=1, keepdims=True))).astype(BF16)
            dst = dstate[...]
            dstb = dst.astype(BF16)
            da = jnp.where(causal, _dot_nt(do, v), 0.0).astype(BF16)
            dv = _dot_tn(amat, do) + _dot_nt(kd, dstb)
            st0 = st_ref[0, cc]
            dq = _dot_nn(da, kt) * eq + _dot_nn(do, st0) * eb
            dk_inter = _dot_nn(v, dstb) * el
            dk = _dot_tn(da, qt) * ek + dk_inter
            dstate[...] = dst * ebl + _dot_tn(do, qd)
            through = jnp.sum(dst * st0.astype(F32), axis=0, keepdims=True) * ebl
            later = jnp.sum(k * dk_inter, axis=0, keepdims=True) + through
            dlf = _dot_exact(_tri(c, True), q * dq - k * dk) + later
            df = dlf / f - dk
            dq_ref[sl, :] = (dq * (sq * (1.0 + qraw * (1.0 - sq)))).astype(BF16)
            df_ref[sl, :] = (df * (1.0 - lb) * sf * (1.0 - sf)).astype(BF16)
            di_ref[sl, :] = dv.astype(BF16)
            dlb_ref[...] += jnp.sum(df * (1.0 - sf), axis=0, keepdims=True)
            return carry

        lax.fori_loop(0, nc, step, 0)

    col = lambda off: pl.BlockSpec((rows, HGRN_HEAD), lambda h, r: (nr - 1 - r, off + h))
    out = col(0)
    return pl.pallas_call(
        body, name=name, grid=(nh, nr),
        in_specs=[col(0), col(nh), col(2 * nh), col(3 * nh),
                  pl.BlockSpec((1, HGRN_HEAD), lambda h, r: (0, h)), _full_spec((1, HGRN_HEAD)),
                  out, pl.BlockSpec((1, nc, HGRN_HEAD, HGRN_HEAD), lambda h, r: (h, nr - 1 - r, 0, 0)), out],
        out_specs=[out, out, out, out, pl.BlockSpec((1, HGRN_HEAD), lambda h, r: (0, h)),
                   pl.BlockSpec((1, 1, HGRN_HEAD), lambda h, r: (h, 0, 0))],
        out_shape=[jax.ShapeDtypeStruct((t, w), BF16)] * 4 + [jax.ShapeDtypeStruct((1, w), F32),
                                                             jax.ShapeDtypeStruct((nh, 1, HGRN_HEAD), F32)],
        scratch_shapes=[pltpu.VMEM((HGRN_HEAD, HGRN_HEAD), F32)],
        compiler_params=_params((_PAR, _ARB)),
    )(proj, proj, proj, proj, lb, norm_g, o, states, dog)


def hgrn_mixer_fwd(hn, w_in, w_out, norm_g, lb_param, idx, name):
    nh = w_out.shape[0] // HGRN_HEAD
    lb = lb_table_row(lb_param, idx, name + "_lb")
    proj = mm(hn, w_in, "nn", BF16, name + "_in")
    og, o, states = hgrn_fwd(proj, lb, norm_g, nh, name + "_rec")
    y = mm(og, w_out, "nn", F32, name + "_out")
    return y, (proj, lb, og, o, states)


def hgrn_mixer_bwd(dy, hn, w_in, w_out, norm_g, lb_param, idx, saved, name, ns=1):
    proj, lb, og, o, states = saved
    nh = w_out.shape[0] // HGRN_HEAD
    dwo = mm(og, dy, "tn", BF16, name + "_dwo")
    dog = mm(dy, w_out, "nt", BF16, name + "_dog")
    dq, df, di, dg, dlb, dng = hgrn_bwd(proj, lb, norm_g, o, states, dog, nh, name + "_recb")
    dproj = jnp.concatenate([dq, df, di, dg], axis=1)
    dwi = mm(hn, dproj, "tn", BF16, name + "_dwi", stack=ns)
    dhn = mm(dproj, w_in, "nt", BF16, name + "_dhn")
    dlbp = lb_table_row_bwd(lb_param, dlb, idx, name + "_lbb")
    return dhn, dwi, dwo, jnp.sum(dng, axis=0), dlbp


_ANY = pl.BlockSpec(memory_space=pl.ANY)
_MESH = pl.DeviceIdType.MESH


def _place():
    x, y, c = lax.axis_index("x"), lax.axis_index("y"), lax.axis_index("c")
    chips = [(1 - x, y), (x, 1 - y), (1 - x, 1 - y)]
    return x, y, c, N_CHIPS // 2 * x + y, chips


def _chip_index(chip):
    return N_CHIPS // 2 * chip[0] + chip[1]


def _window(ref, axis, start, size):
    idx = [slice(None)] * len(ref.shape)
    idx[axis] = pl.ds(start, size)
    return ref.at[tuple(idx)]


def gather_shards(shards, axes, name):
    n = len(shards)

    def body(*refs):
        ins, outs = refs[:n], refs[n:2 * n]
        send, recv, local = refs[2 * n:]
        x, y, c, me, chips = _place()
        started = []
        for k in range(n):
            size = ins[k].shape[axes[k]]
            mine = pltpu.make_async_copy(ins[k], _window(outs[k], axes[k], me * size, size), local.at[k])
            mine.start()
            started.append(mine)
            for j, chip in enumerate(chips):
                pltpu.make_async_remote_copy(
                    src_ref=ins[k], dst_ref=_window(outs[k], axes[k], me * size, size),
                    send_sem=send.at[k, j], recv_sem=recv.at[k, j],
                    device_id=(chip[0], chip[1], c), device_id_type=_MESH).start()
        for k in range(n):
            size = ins[k].shape[axes[k]]
            for j, chip in enumerate(chips):
                landed = pltpu.make_async_remote_copy(
                    src_ref=ins[k], dst_ref=_window(outs[k], axes[k], _chip_index(chip) * size, size),
                    send_sem=send.at[k, j], recv_sem=recv.at[k, j],
                    device_id=(chip[0], chip[1], c), device_id_type=_MESH)
                landed.wait_recv()
                landed.wait_send()
            started[k].wait()

    out_shape = []
    for s, ax in zip(shards, axes):
        shape = list(s.shape)
        shape[ax] *= N_CHIPS
        out_shape.append(jax.ShapeDtypeStruct(tuple(shape), s.dtype))
    return pl.pallas_call(
        body, name=name, in_specs=[_ANY] * n, out_specs=[_ANY] * n, out_shape=out_shape,
        scratch_shapes=[pltpu.SemaphoreType.DMA((n, N_CHIPS - 1)), pltpu.SemaphoreType.DMA((n, N_CHIPS - 1)),
                        pltpu.SemaphoreType.DMA((n,))],
    )(*shards)


def pair_halves(gs, name):
    n = len(gs)

    def body(*refs):
        ins, outs = refs[:n], refs[n:2 * n]
        send, recv, local = refs[2 * n:]
        x, y, c, me, chips = _place()
        pending = []
        for k in range(n):
            half = ins[k].shape[1] // 2
            mine = pltpu.make_async_copy(_window(ins[k], 1, c * half, half), outs[k].at[0], local.at[k])
            mine.start()
            other = pltpu.make_async_remote_copy(
                src_ref=_window(ins[k], 1, (1 - c) * half, half), dst_ref=outs[k].at[1],
                send_sem=send.at[k], recv_sem=recv.at[k], device_id=(x, y, 1 - c), device_id_type=_MESH)
            other.start()
            pending.append((mine, other))
        for mine, other in pending:
            other.wait_recv()
            other.wait_send()
            mine.wait()

    out_shape = [jax.ShapeDtypeStruct((2, g.shape[0], g.shape[1] // 2, g.shape[2]), g.dtype) for g in gs]
    return pl.pallas_call(
        body, name=name, in_specs=[_ANY] * n, out_specs=[_ANY] * n, out_shape=out_shape,
        scratch_shapes=[pltpu.SemaphoreType.DMA((n,)), pltpu.SemaphoreType.DMA((n,)), pltpu.SemaphoreType.DMA((n,))],
    )(*gs)


def chip_exchange(ps, name):
    n = len(ps)

    def body(*refs):
        ins, outs = refs[:n], refs[n:2 * n]
        send, recv, local = refs[2 * n:]
        x, y, c, me, chips = _place()
        started = []
        for k in range(n):
            mine = pltpu.make_async_copy(ins[k].at[me], outs[k].at[me], local.at[k])
            mine.start()
            started.append(mine)
            for j, chip in enumerate(chips):
                pltpu.make_async_remote_copy(
                    src_ref=ins[k].at[_chip_index(chip)], dst_ref=outs[k].at[me],
                    send_sem=send.at[k, j], recv_sem=recv.at[k, j],
                    device_id=(chip[0], chip[1], c), device_id_type=_MESH).start()
        for k in range(n):
            for j, chip in enumerate(chips):
                landed = pltpu.make_async_remote_copy(
                    src_ref=ins[k].at[_chip_index(chip)], dst_ref=outs[k].at[_chip_index(chip)],
                    send_sem=send.at[k, j], recv_sem=recv.at[k, j],
                    device_id=(chip[0], chip[1], c), device_id_type=_MESH)
                landed.wait_recv()
                landed.wait_send()
            started[k].wait()

    return pl.pallas_call(
        body, name=name, in_specs=[_ANY] * n, out_specs=[_ANY] * n,
        out_shape=[jax.ShapeDtypeStruct(p.shape, p.dtype) for p in ps],
        scratch_shapes=[pltpu.SemaphoreType.DMA((n, N_CHIPS - 1)), pltpu.SemaphoreType.DMA((n, N_CHIPS - 1)),
                        pltpu.SemaphoreType.DMA((n,))],
    )(*ps)


def pair_share(halves, groups, name):
    n = len(halves)
    n_out = 1 + max(g for g, _ in groups)
    layers = [1 + max(l for g, l in groups if g == o) for o in range(n_out)]
    first = [next(k for k, (g, _) in enumerate(groups) if g == o) for o in range(n_out)]

    def body(*refs):
        ins, outs = refs[:n], refs[n:n + n_out]
        send, recv, local = refs[n + n_out:]
        x, y, c, me, chips = _place()
        pending = []
        for k, (g, l) in enumerate(groups):
            half = ins[k].shape[0]
            mine = pltpu.make_async_copy(ins[k], outs[g].at[l, pl.ds(c * half, half), :], local.at[k])
            mine.start()
            other = pltpu.make_async_remote_copy(
                src_ref=ins[k], dst_ref=outs[g].at[l, pl.ds(c * half, half), :],
                send_sem=send.at[k], recv_sem=recv.at[k], device_id=(x, y, 1 - c), device_id_type=_MESH)
            other.start()
            pending.append((k, g, l, half, mine, other))
        for k, g, l, half, mine, other in pending:
            landed = pltpu.make_async_remote_copy(
                src_ref=ins[k], dst_ref=outs[g].at[l, pl.ds((1 - c) * half, half), :],
                send_sem=send.at[k], recv_sem=recv.at[k], device_id=(x, y, 1 - c), device_id_type=_MESH)
            landed.wait_recv()
            landed.wait_send()
            mine.wait()

    out_shape = [jax.ShapeDtypeStruct((layers[o], 2 * halves[first[o]].shape[0], halves[first[o]].shape[1]), F32)
                 for o in range(n_out)]
    return pl.pallas_call(
        body, name=name, in_specs=[_ANY] * n, out_specs=[_ANY] * n_out, out_shape=out_shape,
        scratch_shapes=[pltpu.SemaphoreType.DMA((n,)), pltpu.SemaphoreType.DMA((n,)), pltpu.SemaphoreType.DMA((n,))],
    )(*halves)


def allreduce_small(buf, name):
    rows = buf.shape[0]
    n_dev = 2 * N_CHIPS

    def body(in_ref, out_ref, slots, send, recv):
        x, y, c, me, chips = _place()
        my_id = 2 * me + c
        slots[my_id] = in_ref[...]
        for j in range(1, n_dev):
            fx, fy, fc = (j >> 2) & 1, (j >> 1) & 1, j & 1
            peer = ((1 - x) if fx else x, (1 - y) if fy else y, (1 - c) if fc else c)
            pltpu.make_async_remote_copy(
                src_ref=in_ref, dst_ref=slots.at[my_id], send_sem=send.at[j], recv_sem=recv.at[j],
                device_id=peer, device_id_type=_MESH).start()
        for j in range(1, n_dev):
            fx, fy, fc = (j >> 2) & 1, (j >> 1) & 1, j & 1
            peer = ((1 - x) if fx else x, (1 - y) if fy else y, (1 - c) if fc else c)
            peer_id = 2 * _chip_index(peer) + peer[2]
            landed = pltpu.make_async_remote_copy(
                src_ref=in_ref, dst_ref=slots.at[peer_id], send_sem=send.at[j], recv_sem=recv.at[j],
                device_id=peer, device_id_type=_MESH)
            landed.wait_recv()
            landed.wait_send()
        tot = slots[0]
        for d in range(1, n_dev):
            tot = tot + slots[d]
        out_ref[...] = tot

    return pl.pallas_call(
        body, name=name,
        in_specs=[pl.BlockSpec(memory_space=pltpu.VMEM)], out_specs=pl.BlockSpec(memory_space=pltpu.VMEM),
        out_shape=jax.ShapeDtypeStruct(buf.shape, F32),
        scratch_shapes=[pltpu.VMEM((n_dev, rows, LANES), F32), pltpu.SemaphoreType.DMA((n_dev,)),
                        pltpu.SemaphoreType.DMA((n_dev,))],
        compiler_params=pltpu.CompilerParams(vmem_limit_bytes=VMEM_LIMIT),
    )(buf)


def _row_tile(rows, cols, budget):
    for tr in (1024, 512, 256, 128, 64, 32, 16, 8):
        if rows % tr == 0 and tr * cols <= budget:
            return tr
    return rows


def sum_slots(xs, out_dtype, name):
    s, n, c = xs.shape
    tr = _row_tile(n, c * s, 1 << 19)

    def body(x_ref, o_ref):
        tot = x_ref[0].astype(F32)
        for k in range(1, s):
            tot = tot + x_ref[k].astype(F32)
        o_ref[...] = tot.astype(out_dtype)

    return pl.pallas_call(
        body, name=name, grid=(n // tr,),
        in_specs=[pl.BlockSpec((s, tr, c), lambda i: (0, i, 0))],
        out_specs=pl.BlockSpec((tr, c), lambda i: (i, 0)),
        out_shape=jax.ShapeDtypeStruct((n, c), out_dtype),
        compiler_params=_params((_PAR,)),
    )(xs)


def adamw(w, g, m, v, name):
    rows, cols = w.shape
    tr = _row_tile(rows, cols, 1 << 18)
    c1 = 1.0 - ADAM_B1 ** ADAM_STEP
    c2 = 1.0 - ADAM_B2 ** ADAM_STEP

    def body(w_ref, g_ref, m_ref, v_ref, d_ref, nm_ref, nv_ref):
        gv = g_ref[...]
        nm = ADAM_B1 * m_ref[...] + (1.0 - ADAM_B1) * gv
        nv = ADAM_B2 * v_ref[...] + (1.0 - ADAM_B2) * (gv * gv)
        nm_ref[...] = nm
        nv_ref[...] = nv
        d_ref[...] = -ADAM_LR * ((nm / c1) / (jnp.sqrt(nv / c2) + ADAM_EPS) + ADAM_WD * w_ref[...])

    spec = pl.BlockSpec((tr, cols), lambda i: (i, 0))
    return pl.pallas_call(
        body, name=name, grid=(rows // tr,),
        in_specs=[spec] * 4, out_specs=[spec] * 3,
        out_shape=[jax.ShapeDtypeStruct((rows, cols), F32)] * 3,
        compiler_params=_params((_PAR,)),
    )(w, g, m, v)


WEIGHTS = ("mix_pre_g", "mix_post_g", "ffn_pre_g", "ffn_post_g", "hgrn_w_in", "hgrn_w_out", "hgrn_norm_g",
           "hgrn_lb_param", "swa_w_in", "swa_w_out", "swa_sinks", "sc_w_in", "sc_conv_w", "sc_w_out", "fox_w_in",
           "fox_b_f", "fox_w_out", "ffn_w_up", "ffn_conv_w", "ffn_conv_b", "ffn_w_down")
N_MIXERS = 4


def _pack_small(parts):
    flat = jnp.concatenate([p.reshape(-1).astype(F32) for p in parts])
    rows = -(-flat.shape[0] // (8 * LANES)) * 8
    return jnp.pad(flat, (0, rows * LANES - flat.shape[0])).reshape(rows, LANES)


def _unpack_small(buf, shapes):
    flat, out, off = buf.reshape(-1), [], 0
    for s in shapes:
        n = math.prod(s)
        out.append(flat[off:off + n].reshape(s))
        off += n
    return out


def _stack_rows(dw):
    return dw.reshape(N_CHIPS, dw.shape[0] // N_CHIPS, dw.shape[1])


def kernel(x, positions, mix_pre_g, mix_post_g, ffn_pre_g, ffn_post_g, hgrn_w_in, hgrn_w_out, hgrn_norm_g, hgrn_lb_param, swa_w_in, swa_w_out, swa_sinks, sc_w_in, sc_conv_w, sc_w_out, fox_w_in, fox_b_f, fox_w_out, ffn_w_up, ffn_conv_w, ffn_conv_b, ffn_w_down, loss_target, m_mix_pre_g, m_mix_post_g, m_ffn_pre_g, m_ffn_post_g, m_hgrn_w_in, m_hgrn_w_out, m_hgrn_norm_g, m_hgrn_lb_param, m_swa_w_in, m_swa_w_out, m_swa_sinks, m_sc_w_in, m_sc_conv_w, m_sc_w_out, m_fox_w_in, m_fox_b_f, m_fox_w_out, m_ffn_w_up, m_ffn_conv_w, m_ffn_conv_b, m_ffn_w_down, v_mix_pre_g, v_mix_post_g, v_ffn_pre_g, v_ffn_post_g, v_hgrn_w_in, v_hgrn_w_out, v_hgrn_norm_g, v_hgrn_lb_param, v_swa_w_in, v_swa_w_out, v_swa_sinks, v_sc_w_in, v_sc_conv_w, v_sc_w_out, v_fox_w_in, v_fox_b_f, v_fox_w_out, v_ffn_w_up, v_ffn_conv_w, v_ffn_conv_b, v_ffn_w_down):
    given = dict(locals())
    depth = mix_pre_g.shape[0]
    assert depth == N_MIXERS and x.shape[0] == 1, "one batch element per device, one layer of each mixer"
    xi, target = x[0], loss_target[0]
    chip = N_CHIPS // 2 * lax.axis_index("x") + lax.axis_index("y")
    nh_fox = fox_b_f.shape[1]
    row = lambda a, i: a[i:i + 1]

    bf = lambda a: a.astype(BF16)
    shards = [bf(hgrn_w_in[0]), bf(hgrn_w_out[0]), bf(swa_w_in[0]), bf(swa_w_out[0]), bf(sc_w_in[0]),
              bf(sc_w_out[0]), bf(fox_w_in), bf(fox_w_out[0]), bf(ffn_w_up), bf(ffn_w_down), sc_conv_w[0], ffn_conv_w]
    axes = [1, 0, 1, 0, 1, 0, 0, 0, 2, 1, 1, 2]
    (hg_in, hg_out, sw_in, sw_out, sc_in, sc_out, fx_in4, fx_out, w_up, w_down, sc_cw, f_cw) = gather_shards(
        shards, axes, "gather_weights")
    fx_in = fox_pad_w_in(jnp.concatenate([fx_in4[s] for s in range(N_CHIPS)], axis=1), nh_fox)

    saved = []
    xs = xi
    hn = rms_fwd(xs, row(mix_pre_g, 0), "pre_norm0")
    dx = loss = None
    for i in range(depth):
        nm = f"l{i}"
        if i == 0:
            y, sv = hgrn_mixer_fwd(hn, hg_in, hg_out, hgrn_norm_g, hgrn_lb_param, i, nm + "_hgrn")
        elif i == 1:
            y, sv = swa_mixer_fwd(hn, sw_in, sw_out, swa_sinks[0], positions, nm + "_swa")
        elif i == 2:
            proj = mm(hn, sc_in, "nn", BF16, nm + "_sc_in")
            yb = sconv_fwd(proj, sc_cw, nm + "_sc_conv")
            y, sv = mm(yb, sc_out, "nn", F32, nm + "_sc_out"), (proj, yb)
        else:
            y, sv = fox_mixer_fwd(hn, fx_in, fx_out, fox_b_f[0], nm + "_fox")
        x1, hn2 = resid_norm(xs, y, row(mix_post_g, i), row(ffn_pre_g, i), nm + "_mix_resid")
        z = mm(hn2, w_up, "nn", BF16, nm + "_ffn_up", b_idx=i)
        a = ffn_act(z, f_cw[i], row(ffn_conv_b, i), nm + "_ffn_act")
        y2 = mm(a, w_down, "nn", F32, nm + "_ffn_down", b_idx=i)
        saved.append((xs, hn, y, sv, x1, hn2, z, a, y2))
        if i < depth - 1:
            xs, hn = resid_norm(x1, y2, row(ffn_post_g, i), row(mix_pre_g, i + 1), nm + "_ffn_resid")
        else:
            dx, loss = resid_loss(x1, y2, row(ffn_post_g, i), target, nm + "_loss")

    big = {}
    d_pre, d_post, d_fpre, d_fpost = [None] * depth, [None] * depth, [None] * depth, [None] * depth
    d_up, d_down, d_fcw, d_fcb = [None] * depth, [None] * depth, [None] * depth, [None] * depth
    small = {}
    for i in reversed(range(depth)):
        nm = f"l{i}b"
        xs, hn, y, sv, x1, hn2, z, a, y2 = saved[i]
        dy2, d_fpost[i] = norm_bwd(y2, row(ffn_post_g, i), dx, None, BF16, nm + "_ffn_post")
        d_down[i] = _stack_rows(mm(a, dy2, "tn", BF16, nm + "_dw_down"))
        da = mm(dy2, w_down, "nt", BF16, nm + "_da", b_idx=i)
        du, acc = ffn_act_bwd(z, da, f_cw[i], row(ffn_conv_b, i), nm + "_ffn_actb")
        d_fcw[i], d_fcb[i] = acc[0:CONV_WIDTH], acc[CONV_WIDTH]
        dz = conv_transpose(du, f_cw[i], nm + "_ffn_convT")
        d_up[i] = mm(hn2, dz, "tn", BF16, nm + "_dw_up", stack=N_CHIPS)
        dhn2 = mm(dz, w_up, "nt", BF16, nm + "_dhn2", b_idx=i)
        dx1, d_fpre[i] = norm_bwd(x1, row(ffn_pre_g, i), dhn2, dx, F32, nm + "_ffn_pre")
        dy, d_post[i] = norm_bwd(y, row(mix_post_g, i), dx1, None, BF16, nm + "_mix_post")
        if i == 0:
            dhn, dwi, dwo, small["hgrn_norm_g"], small["hgrn_lb_param"] = hgrn_mixer_bwd(
                dy, hn, hg_in, hg_out, hgrn_norm_g, hgrn_lb_param, i, sv, nm + "_hgrn", ns=N_CHIPS)
            big["hgrn_w_in"], big["hgrn_w_out"] = [dwi], [_stack_rows(dwo)]
        elif i == 1:
            dhn, dwi, dwo, dsink = swa_mixer_bwd(dy, hn, sw_in, sw_out, swa_sinks[0], positions, sv, nm + "_swa",
                                                 ns=N_CHIPS)
            big["swa_w_in"], big["swa_w_out"], small["swa_sinks"] = [dwi], [_stack_rows(dwo)], dsink
        elif i == 2:
            proj, yb = sv
            dwo = mm(yb, dy, "tn", BF16, nm + "_sc_dwo")
            dyb = mm(dy, sc_out, "nt", BF16, nm + "_sc_dyb")
            dproj, acc = sconv_bwd(proj, dyb, sc_cw, nm + "_sc_convb")
            dwi = mm(hn, dproj, "tn", BF16, nm + "_sc_dwi", stack=N_CHIPS)
            dhn = mm(dproj, sc_in, "nt", BF16, nm + "_sc_dhn")
            big["sc_w_in"], big["sc_w_out"], small["sc_conv_w"] = [dwi], [_stack_rows(dwo)], acc[0:CONV_WIDTH]
        else:
            dhn, dwi, dwo, small["fox_b_f"] = fox_mixer_bwd(dy, hn, fx_in, fx_out, fox_b_f[0], sv, nm + "_fox")
            dwi = fox_unpad_dw(dwi, nh_fox)
            cols = dwi.shape[1] // N_CHIPS
            big["fox_w_in"] = [jnp.stack([dwi[:, s * cols:(s + 1) * cols] for s in range(N_CHIPS)])]
            big["fox_w_out"] = [_stack_rows(dwo)]
        dx, d_pre[i] = norm_bwd(xs, row(mix_pre_g, i), dhn, dx1, F32, nm + "_mix_pre")
    big["ffn_w_up"], big["ffn_w_down"] = d_up, d_down
    small.update(mix_pre_g=jnp.concatenate(d_pre), mix_post_g=jnp.concatenate(d_post),
                 ffn_pre_g=jnp.concatenate(d_fpre), ffn_post_g=jnp.concatenate(d_fpost),
                 ffn_conv_w=jnp.stack(d_fcw), ffn_conv_b=jnp.stack(d_fcb))

    big_names = [n for n in WEIGHTS if n in big]
    groups = [(g, l) for g, n in enumerate(big_names) for l in range(len(big[n]))]
    gs = [big[n][l] for n in big_names for l in range(len(big[n]))]
    pairs = pair_halves(gs, "grads_pair")
    ps = [sum_slots(p.reshape(2, -1, p.shape[-1]), BF16, f"grads_pair_sum{k}").reshape(p.shape[1:])
          for k, p in enumerate(pairs)]
    qs = chip_exchange(ps, "grads_chips")
    rs = [sum_slots(q, F32, f"grads_chip_sum{k}") for k, q in enumerate(qs)]
    grads = dict(zip(big_names, pair_share(rs, groups, "grads_share")))

    small_names = [n for n in WEIGHTS if n in small]
    full_shape = {n: tuple(given[n].shape) for n in small_names}
    full_shape["sc_conv_w"] = (1, CONV_WIDTH, sc_cw.shape[1])
    full_shape["ffn_conv_w"] = tuple(f_cw.shape)
    summed = _unpack_small(allreduce_small(_pack_small([small[n] for n in small_names] + [loss]), "small_sum"),
                           [full_shape[n] for n in small_names] + [()])
    loss = summed[-1]
    for n, g in zip(small_names, summed):
        if g.shape != given[n].shape:
            width = given[n].shape[-1]
            g = lax.dynamic_slice_in_dim(g, chip * width, width, axis=g.ndim - 1)
        grads[n] = g

    deltas, new_m, new_v = {}, {}, {}
    for n in WEIGHTS:
        w = given[n]
        flat = lambda a: a.reshape(-1, w.shape[-1])
        dl, nm_, nv_ = adamw(flat(w), flat(grads[n]), flat(given["m_" + n]), flat(given["v_" + n]), "adamw_" + n)
        deltas[n], new_m[n], new_v[n] = dl.reshape(w.shape), nm_.reshape(w.shape), nv_.reshape(w.shape)
    return (loss, dx[None], *[grads[n].reshape(given[n].shape) for n in WEIGHTS], *[deltas[n] for n in WEIGHTS],
            *[new_m[n] for n in WEIGHTS], *[new_v[n] for n in WEIGHTS])
```

```python
import functools
import math

import numpy as np
import jax
import jax.numpy as jnp
from jax import lax
from jax.experimental import pallas as pl
from jax.experimental.pallas import tpu as pltpu

F32 = jnp.float32
BF16 = jnp.bfloat16

RMS_EPS = 1e-6
HGRN_HEAD = 128
HGRN_CHUNK = 32
ATT_HEAD = 64
SWA_WINDOW = 128
SWA_GROUP = 8
ROT_DIM = 16
ROPE_THETA = 500000.0
CONV_WIDTH = 3
ADAM_LR = 0.001
ADAM_B1 = 0.9
ADAM_B2 = 0.999
ADAM_EPS = 1e-08
ADAM_WD = 0.01
ADAM_STEP = 10
N_CHIPS = 4
LANES = 128
BF16_ROWS = 16
VMEM_LIMIT = 48 * 1024 * 1024

_ARB = "arbitrary"
_PAR = "parallel"


def _params(sem, **kw):
    return pltpu.CompilerParams(dimension_semantics=sem, vmem_limit_bytes=VMEM_LIMIT, **kw)


def _tile(n, prefs):
    for p in prefs:
        if n % p == 0:
            return p
    return n


def _sigmoid(x):
    return 1.0 / (1.0 + jnp.exp(-x))


def _dot(a, b, dims):
    return lax.dot_general(a, b, (dims, ((), ())), preferred_element_type=F32)


def _dot_nn(a, b):
    return _dot(a, b, ((1,), (0,)))


def _dot_nt(a, b):
    return _dot(a, b, ((1,), (1,)))


def _dot_tn(a, b):
    return _dot(a, b, ((0,), (0,)))


def mm(a, b, mode, out_dtype, name="mm", b_idx=None, stack=1):
    bshape = b.shape if b_idx is None else b.shape[1:]
    if mode == "nn":
        (m, k), (k2, n) = a.shape, bshape
    elif mode == "nt":
        (m, k), (n, k2) = a.shape, bshape
    else:
        (k, m), (k2, n) = a.shape, bshape
    assert k == k2, (a.shape, b.shape, mode)
    tm = _tile(m, (512, 256, 128))
    tn = _tile(n // stack, (512, 256, 128))
    tk = _tile(k, (2048, 1408, 1024, 512, 256, 128))
    nk = k // tk
    nbs = n // stack // tn

    def body(a_ref, b_ref, o_ref, acc_ref):
        kk = pl.program_id(2)

        @pl.when(kk == 0)
        def _():
            acc_ref[...] = jnp.zeros_like(acc_ref)

        av = a_ref[...].astype(BF16)
        bv = b_ref[...].astype(BF16)
        if mode == "nn":
            acc_ref[...] += _dot_nn(av, bv)
        elif mode == "nt":
            acc_ref[...] += _dot_nt(av, bv)
        else:
            acc_ref[...] += _dot_tn(av, bv)

        @pl.when(kk == nk - 1)
        def _():
            o_ref[...] = acc_ref[...].astype(out_dtype)

    def b_block(rows, cols, imap):
        if b_idx is None:
            return pl.BlockSpec((rows, cols), imap)
        return pl.BlockSpec((None, rows, cols), lambda i, j, kk: (b_idx,) + imap(i, j, kk))

    if mode == "nn":
        a_spec = pl.BlockSpec((tm, tk), lambda i, j, kk: (i, kk))
        b_spec = b_block(tk, tn, lambda i, j, kk: (kk, j))
    elif mode == "nt":
        a_spec = pl.BlockSpec((tm, tk), lambda i, j, kk: (i, kk))
        b_spec = b_block(tn, tk, lambda i, j, kk: (j, kk))
    else:
        a_spec = pl.BlockSpec((tk, tm), lambda i, j, kk: (kk, i))
        b_spec = b_block(tk, tn, lambda i, j, kk: (kk, j))
    if stack == 1:
        out_spec = pl.BlockSpec((tm, tn), lambda i, j, kk: (i, j))
        out_shape = jax.ShapeDtypeStruct((m, n), out_dtype)
    else:
        out_spec = pl.BlockSpec((None, tm, tn), lambda i, j, kk: (j // nbs, i, j % nbs))
        out_shape = jax.ShapeDtypeStruct((stack, m, n // stack), out_dtype)
    return pl.pallas_call(
        body,
        name=name,
        grid=(m // tm, n // tn, nk),
        in_specs=[a_spec, b_spec],
        out_specs=out_spec,
        out_shape=out_shape,
        scratch_shapes=[pltpu.VMEM((tm, tn), F32)],
        compiler_params=_params((_PAR, _PAR, _ARB)),
    )(a, b)


def _rstd(xv):
    return lax.rsqrt(jnp.mean(xv * xv, axis=1, keepdims=True) + RMS_EPS)


def _row_spec(tr, w):
    return pl.BlockSpec((tr, w), lambda i: (i, 0))


def _full_spec(shape):
    nd = len(shape)
    return pl.BlockSpec(shape, lambda *_: (0,) * nd)


def rms_fwd(x, g, name):
    t, d = x.shape
    tr = _tile(t, (256, 128, 64, 32, 16))

    def body(x_ref, g_ref, o_ref):
        xv = x_ref[...]
        o_ref[...] = (xv * _rstd(xv) * g_ref[...]).astype(BF16)

    return pl.pallas_call(
        body, name=name, grid=(t // tr,),
        in_specs=[_row_spec(tr, d), _full_spec((1, d))],
        out_specs=_row_spec(tr, d),
        out_shape=jax.ShapeDtypeStruct((t, d), BF16),
        compiler_params=_params((_PAR,)),
    )(x, g)


def resid_norm(x, y, g_post, g_next, name):
    t, d = x.shape
    tr = _tile(t, (256, 128, 64, 32, 16))

    def body(x_ref, y_ref, gp_ref, gn_ref, x1_ref, hn_ref):
        yv = y_ref[...]
        x1 = x_ref[...] + yv * _rstd(yv) * gp_ref[...]
        x1_ref[...] = x1
        hn_ref[...] = (x1 * _rstd(x1) * gn_ref[...]).astype(BF16)

    return pl.pallas_call(
        body, name=name, grid=(t // tr,),
        in_specs=[_row_spec(tr, d), _row_spec(tr, d), _full_spec((1, d)), _full_spec((1, d))],
        out_specs=[_row_spec(tr, d), _row_spec(tr, d)],
        out_shape=[jax.ShapeDtypeStruct((t, d), F32), jax.ShapeDtypeStruct((t, d), BF16)],
        compiler_params=_params((_PAR,)),
    )(x, y, g_post, g_next)


def resid_loss(x, y, g_post, target, name):
    t, d = x.shape
    tr = _tile(t, (256, 128, 64, 32, 16))

    def body(x_ref, y_ref, gp_ref, t_ref, dx_ref, loss_ref):
        @pl.when(pl.program_id(0) == 0)
        def _():
            loss_ref[...] = jnp.zeros_like(loss_ref)

        yv = y_ref[...]
        err = x_ref[...] + yv * _rstd(yv) * gp_ref[...] - t_ref[...]
        dx_ref[...] = err * (1.0 / d)
        loss_ref[...] += 0.5 * jnp.sum(jnp.mean(err * err, axis=1, keepdims=True), axis=0, keepdims=True)

    dx, loss = pl.pallas_call(
        body, name=name, grid=(t // tr,),
        in_specs=[_row_spec(tr, d), _row_spec(tr, d), _full_spec((1, d)), _row_spec(tr, d)],
        out_specs=[_row_spec(tr, d), _full_spec((8, LANES))],
        out_shape=[jax.ShapeDtypeStruct((t, d), F32), jax.ShapeDtypeStruct((8, LANES), F32)],
        compiler_params=_params((_ARB,)),
    )(x, y, g_post, target)
    return dx, loss[0:1, 0:1]


def norm_bwd(yin, g, dout, res, out_dtype, name):
    t, d = yin.shape
    tr = _tile(t, (256, 128, 64, 32, 16))
    has_res = res is not None

    def body(*refs):
        if has_res:
            y_ref, g_ref, d_ref, r_ref, o_ref, dg_ref = refs
        else:
            y_ref, g_ref, d_ref, o_ref, dg_ref = refs

        @pl.when(pl.program_id(0) == 0)
        def _():
            dg_ref[...] = jnp.zeros_like(dg_ref)

        yv = y_ref[...]
        dv = d_ref[...].astype(F32)
        r = _rstd(yv)
        yh = yv * r
        dyh = dv * g_ref[...]
        dy = r * (dyh - yh * jnp.mean(dyh * yh, axis=1, keepdims=True))
        if has_res:
            dy = dy + r_ref[...]
        o_ref[...] = dy.astype(out_dtype)
        dg_ref[...] += jnp.sum(dv * yh, axis=0, keepdims=True)

    ins = [yin, g, dout] + ([res] if has_res else [])
    in_specs = [_row_spec(tr, d), _full_spec((1, d)), _row_spec(tr, d)] + ([_row_spec(tr, d)] if has_res else [])
    return pl.pallas_call(
        body, name=name, grid=(t // tr,),
        in_specs=in_specs,
        out_specs=[_row_spec(tr, d), _full_spec((1, d))],
        out_shape=[jax.ShapeDtypeStruct((t, d), out_dtype), jax.ShapeDtypeStruct((1, d), F32)],
        compiler_params=_params((_ARB,)),
    )(*ins)


def _shift_down(x, halo):
    tr = x.shape[0]
    row = lax.broadcasted_iota(jnp.int32, x.shape, 0)
    h1 = halo[BF16_ROWS - 1:BF16_ROWS, :]
    h2 = halo[BF16_ROWS - 2:BF16_ROWS - 1, :]
    x1 = jnp.where(row == 0, h1, pltpu.roll(x, 1, 0))
    x2 = jnp.where(row == 0, h2, jnp.where(row == 1, h1, pltpu.roll(x, 2, 0)))
    return x1, x2


def _shift_up(x, halo):
    tr = x.shape[0]
    row = lax.broadcasted_iota(jnp.int32, x.shape, 0)
    h0 = halo[0:1, :]
    h1 = halo[1:2, :]
    x1 = jnp.where(row == tr - 1, h0, pltpu.roll(x, tr - 1, 0))
    x2 = jnp.where(row == tr - 1, h1, jnp.where(row == tr - 2, h0, pltpu.roll(x, tr - 2, 0)))
    return x1, x2


def _prev_halo_spec(tr, w, nt):
    return pl.BlockSpec((BF16_ROWS, w), lambda i: (jnp.maximum(i * (tr // BF16_ROWS) - 1, 0), 0))


def _next_halo_spec(tr, w, nt):
    last = nt * (tr // BF16_ROWS) - 1
    return pl.BlockSpec((BF16_ROWS, w), lambda i: (jnp.minimum((i + 1) * (tr // BF16_ROWS), last), 0))


def _silu_and_grad(u):
    s = _sigmoid(u)
    return u * s, s * (1.0 + u * (1.0 - s))


def ffn_act(z, conv_w, conv_b, name):
    t, f2 = z.shape
    f = f2 // 2
    tr = _tile(t, (128, 64, 32, 16))
    nt = t // tr
    cw = _tile(f, (512, 256, 128))

    def body(z_ref, zp_ref, w_ref, b_ref, a_ref):
        first = pl.program_id(0) == 0
        for j in range(f // cw):
            us = []
            for off in (j * cw, f + j * cw):
                cols = slice(off, off + cw)
                zc = z_ref[:, cols].astype(F32)
                hp = jnp.where(first, 0.0, zp_ref[:, cols].astype(F32))
                z1, z2 = _shift_down(zc, hp)
                us.append(w_ref[2:3, cols] * zc + w_ref[1:2, cols] * z1 + w_ref[0:1, cols] * z2 + b_ref[:, cols])
            sil, _ = _silu_and_grad(us[0])
            a_ref[:, j * cw:(j + 1) * cw] = (sil * us[1]).astype(BF16)

    return pl.pallas_call(
        body, name=name, grid=(nt,),
        in_specs=[_row_spec(tr, f2), _prev_halo_spec(tr, f2, nt), _full_spec((CONV_WIDTH, f2)), _full_spec((1, f2))],
        out_specs=_row_spec(tr, f),
        out_shape=jax.ShapeDtypeStruct((t, f), BF16),
        compiler_params=_params((_PAR,)),
    )(z, z, conv_w, conv_b)


def ffn_act_bwd(z, da, conv_w, conv_b, name):
    t, f2 = z.shape
    f = f2 // 2
    tr = _tile(t, (128, 64, 32, 16))
    nt = t // tr
    cw = _tile(f, (512, 256, 128))

    def body(z_ref, zp_ref, da_ref, w_ref, b_ref, du_ref, acc_ref):
        first = pl.program_id(0) == 0

        @pl.when(first)
        def _():
            acc_ref[...] = jnp.zeros_like(acc_ref)

        for j in range(f // cw):
            us, zs = [], []
            for off in (j * cw, f + j * cw):
                cols = slice(off, off + cw)
                zc = z_ref[:, cols].astype(F32)
                hp = jnp.where(first, 0.0, zp_ref[:, cols].astype(F32))
                z1, z2 = _shift_down(zc, hp)
                zs.append((z2, z1, zc))
                us.append(w_ref[2:3, cols] * zc + w_ref[1:2, cols] * z1 + w_ref[0:1, cols] * z2 + b_ref[:, cols])
            dav = da_ref[:, j * cw:(j + 1) * cw].astype(F32)
            sil, dsil = _silu_and_grad(us[0])
            dus = (dav * us[1] * dsil, dav * sil)
            for off, du, zsh in zip((j * cw, f + j * cw), dus, zs):
                cols = slice(off, off + cw)
                du_ref[:, cols] = du.astype(BF16)
                for k in range(CONV_WIDTH):
                    acc_ref[k:k + 1, cols] += jnp.sum(du * zsh[k], axis=0, keepdims=True)
                acc_ref[3:4, cols] += jnp.sum(du, axis=0, keepdims=True)

    return pl.pallas_call(
        body, name=name, grid=(nt,),
        in_specs=[_row_spec(tr, f2), _prev_halo_spec(tr, f2, nt), _row_spec(tr, f),
                  _full_spec((CONV_WIDTH, f2)), _full_spec((1, f2))],
        out_specs=[_row_spec(tr, f2), _full_spec((8, f2))],
        out_shape=[jax.ShapeDtypeStruct((t, f2), BF16), jax.ShapeDtypeStruct((8, f2), F32)],
        compiler_params=_params((_ARB,)),
    )(z, z, da, conv_w, conv_b)


def conv_transpose(du, conv_w, name):
    t, w = du.shape
    tr = _tile(t, (128, 64, 32, 16))
    nt = t // tr
    cw = _tile(w, (512, 256, 128))

    def body(d_ref, dn_ref, w_ref, o_ref):
        last = pl.program_id(0) == nt - 1
        for j in range(w // cw):
            cols = slice(j * cw, (j + 1) * cw)
            dc = d_ref[:, cols].astype(F32)
            hn = jnp.where(last, 0.0, dn_ref[:, cols].astype(F32))
            d1, d2 = _shift_up(dc, hn)
            o_ref[:, cols] = (w_ref[2:3, cols] * dc + w_ref[1:2, cols] * d1 + w_ref[0:1, cols] * d2).astype(BF16)

    return pl.pallas_call(
        body, name=name, grid=(nt,),
        in_specs=[_row_spec(tr, w), _next_halo_spec(tr, w, nt), _full_spec((CONV_WIDTH, w))],
        out_specs=_row_spec(tr, w),
        out_shape=jax.ShapeDtypeStruct((t, w), BF16),
        compiler_params=_params((_PAR,)),
    )(du, du, conv_w)


def sconv_fwd(proj, conv_w, name):
    t, w3 = proj.shape
    d = w3 // 3
    tr = _tile(t, (128, 64, 32, 16))
    nt = t // tr
    cw = _tile(d, (512, 256, 128))

    def body(p_ref, pp_ref, w_ref, o_ref):
        first = pl.program_id(0) == 0
        for j in range(d // cw):
            cb, cc, cx = (slice(k * d + j * cw, k * d + (j + 1) * cw) for k in range(3))
            zc = p_ref[:, cc].astype(F32) * p_ref[:, cx].astype(F32)
            hp = jnp.where(first, 0.0, pp_ref[:, cc].astype(F32) * pp_ref[:, cx].astype(F32))
            z1, z2 = _shift_down(zc, hp)
            wc = slice(j * cw, (j + 1) * cw)
            cz = w_ref[2:3, wc] * zc + w_ref[1:2, wc] * z1 + w_ref[0:1, wc] * z2
            o_ref[:, wc] = (p_ref[:, cb].astype(F32) * cz).astype(BF16)

    return pl.pallas_call(
        body, name=name, grid=(nt,),
        in_specs=[_row_spec(tr, w3), _prev_halo_spec(tr, w3, nt), _full_spec((CONV_WIDTH, d))],
        out_specs=_row_spec(tr, d),
        out_shape=jax.ShapeDtypeStruct((t, d), BF16),
        compiler_params=_params((_PAR,)),
    )(proj, proj, conv_w)


def sconv_bwd(proj, dyb, conv_w, name):
    t, w3 = proj.shape
    d = w3 // 3
    tr = _tile(t, (128, 64, 32, 16))
    nt = t // tr
    cw = _tile(d, (512, 256, 128))

    def body(p_ref, pp_ref, pn_ref, dy_ref, dyn_ref, w_ref, o_ref, acc_ref):
        first = pl.program_id(0) == 0
        last = pl.program_id(0) == nt - 1

        @pl.when(first)
        def _():
            acc_ref[...] = jnp.zeros_like(acc_ref)

        for j in range(d // cw):
            cb, cc, cx = (slice(k * d + j * cw, k * d + (j + 1) * cw) for k in range(3))
            wc = slice(j * cw, (j + 1) * cw)
            bv, cv, xv = p_ref[:, cb].astype(F32), p_ref[:, cc].astype(F32), p_ref[:, cx].astype(F32)
            zc = cv * xv
            hp = jnp.where(first, 0.0, pp_ref[:, cc].astype(F32) * pp_ref[:, cx].astype(F32))
            z1, z2 = _shift_down(zc, hp)
            w0, w1, w2 = w_ref[0:1, wc], w_ref[1:2, wc], w_ref[2:3, wc]
            cz = w2 * zc + w1 * z1 + w0 * z2
            dyv = dy_ref[:, wc].astype(F32)
            dcz = dyv * bv
            hn = jnp.where(last, 0.0, dyn_ref[:, wc].astype(F32) * pn_ref[:, cb].astype(F32))
            n1, n2 = _shift_up(dcz, hn)
            dz = w2 * dcz + w1 * n1 + w0 * n2
            o_ref[:, cb] = (dyv * cz).astype(BF16)
            o_ref[:, cc] = (dz * xv).astype(BF16)
            o_ref[:, cx] = (dz * cv).astype(BF16)
            for k, zsh in enumerate((z2, z1, zc)):
                acc_ref[k:k + 1, wc] += jnp.sum(dcz * zsh, axis=0, keepdims=True)

    return pl.pallas_call(
        body, name=name, grid=(nt,),
        in_specs=[_row_spec(tr, w3), _prev_halo_spec(tr, w3, nt), _next_halo_spec(tr, w3, nt),
                  _row_spec(tr, d), _next_halo_spec(tr, d, nt), _full_spec((CONV_WIDTH, d))],
        out_specs=[_row_spec(tr, w3), _full_spec((8, d))],
        out_shape=[jax.ShapeDtypeStruct((t, w3), BF16), jax.ShapeDtypeStruct((8, d), F32)],
        compiler_params=_params((_ARB,)),
    )(proj, proj, proj, dyb, dyb, conv_w)


def rope_tables(positions):
    half = ROT_DIM // 2
    inv_freq = ROPE_THETA ** (-jnp.arange(half, dtype=F32) / half)
    ang = positions.astype(F32)[:, None] * inv_freq[None, :]
    cos, sin = jnp.cos(ang), jnp.sin(ang)
    ones = jnp.ones((positions.shape[0], ATT_HEAD - ROT_DIM), F32)
    c64 = jnp.concatenate([cos, cos, ones], axis=1)
    s64 = jnp.concatenate([-sin, sin, 0.0 * ones], axis=1)
    perm = np.zeros((LANES, LANES), np.float32)
    for lane in range(LANES):
        dim = lane % ATT_HEAD
        if dim < half:
            perm[lane + half, lane] = 1.0
        elif dim < ROT_DIM:
            perm[lane - half, lane] = 1.0
    return jnp.tile(c64, (1, 2)), jnp.tile(s64, (1, 2)), jnp.asarray(perm, BF16)


def rope(xin, ctab, stab, perm, n_rot, sign, name):
    t, w = xin.shape
    tr = _tile(t, (256, 128, 64, 32, 16))

    def body(x_ref, c_ref, s_ref, p_ref, o_ref):
        cv, sv = c_ref[...], s_ref[...] * sign
        for j in range(n_rot // LANES):
            cols = slice(j * LANES, (j + 1) * LANES)
            xb = x_ref[:, cols]
            o_ref[:, cols] = (xb.astype(F32) * cv + _dot_nn(xb, p_ref[...]) * sv).astype(BF16)
        if n_rot < w:
            o_ref[:, n_rot:] = x_ref[:, n_rot:]

    return pl.pallas_call(
        body, name=name, grid=(t // tr,),
        in_specs=[_row_spec(tr, w), _row_spec(tr, LANES), _row_spec(tr, LANES), _full_spec((LANES, LANES))],
        out_specs=_row_spec(tr, w),
        out_shape=jax.ShapeDtypeStruct((t, w), BF16),
        compiler_params=_params((_PAR,)),
    )(xin, ctab, stab, perm)


NEG = -1e30


def _half(shape, h):
    return (lax.broadcasted_iota(jnp.int32, shape, 1) // ATT_HEAD) == h


def _dup_head(xb, kvh):
    xf = jnp.where(_half(xb.shape, kvh), xb.astype(F32), 0.0)
    return (xf + pltpu.roll(xf, ATT_HEAD, 1)).astype(BF16)


def _swa_mask(n, rows, cur_only):
    w = SWA_WINDOW
    shape = (w, w) if cur_only else (w, 2 * w)
    qi = lax.broadcasted_iota(jnp.int32, shape, 0)
    kj = lax.broadcasted_iota(jnp.int32, shape, 1) + (w if cur_only else 0)
    diff = qi + w - kj
    ok = (diff >= 0) & (diff < w)
    return ok & ((kj >= w) | (n > 0))


def swa_fwd(qkv, sinks, hq, name):
    t = qkv.shape[0]
    w = SWA_WINDOW
    nb = t // w
    hkv = hq // SWA_GROUP
    npair = hkv // 2
    qw = 2 * SWA_GROUP * ATT_HEAD
    kcol = hq * ATT_HEAD // LANES
    vcol = kcol + npair
    scale = ATT_HEAD ** -0.5

    def body(sink_ref, q_ref, kp_ref, kc_ref, vp_ref, vc_ref, o_ref, lse_ref):
        m, n = pl.program_id(0), pl.program_id(1)
        kb = jnp.concatenate([kp_ref[...], kc_ref[...]], axis=0)
        vb = jnp.concatenate([vp_ref[...], vc_ref[...]], axis=0)
        ok = _swa_mask(n, w, False)
        for kvh in range(2):
            kd, vd = _dup_head(kb, kvh), _dup_head(vb, kvh)
            for jj in range(SWA_GROUP // 2):
                jp = kvh * (SWA_GROUP // 2) + jj
                q2 = q_ref[:, jp * LANES:(jp + 1) * LANES]
                outs = []
                for a in range(2):
                    qa = jnp.where(_half(q2.shape, a), q2, jnp.zeros_like(q2))
                    s = jnp.where(ok, _dot_nt(qa, kd) * scale, NEG)
                    sink = sink_ref[m * 2 * SWA_GROUP + jp * 2 + a]
                    mx = jnp.maximum(jnp.max(s, axis=1, keepdims=True), sink)
                    e = jnp.exp(s - mx)
                    den = jnp.sum(e, axis=1, keepdims=True) + jnp.exp(sink - mx)
                    p = (e / den).astype(BF16)
                    outs.append(_dot_nn(p, vd))
                    lse_ref[jp * 2 + a] = jnp.broadcast_to(mx + jnp.log(den), (w, LANES))
                o_ref[:, jp * LANES:(jp + 1) * LANES] = jnp.where(_half(outs[0].shape, 0), outs[0], outs[1]).astype(BF16)

    prev = lambda m, n: jnp.maximum(n - 1, 0)
    grid_spec = pltpu.PrefetchScalarGridSpec(
        num_scalar_prefetch=1, grid=(npair, nb),
        in_specs=[
            pl.BlockSpec((w, qw), lambda m, n, s: (n, m)),
            pl.BlockSpec((w, LANES), lambda m, n, s: (prev(m, n), kcol + m)),
            pl.BlockSpec((w, LANES), lambda m, n, s: (n, kcol + m)),
            pl.BlockSpec((w, LANES), lambda m, n, s: (prev(m, n), vcol + m)),
            pl.BlockSpec((w, LANES), lambda m, n, s: (n, vcol + m)),
        ],
        out_specs=[
            pl.BlockSpec((w, qw), lambda m, n, s: (n, m)),
            pl.BlockSpec((2 * SWA_GROUP, w, LANES), lambda m, n, s: (m, n, 0)),
        ],
    )
    return pl.pallas_call(
        body, name=name, grid_spec=grid_spec,
        out_shape=[jax.ShapeDtypeStruct((t, hq * ATT_HEAD), BF16), jax.ShapeDtypeStruct((hq, t, LANES), F32)],
        compiler_params=_params((_PAR, _PAR)),
    )(sinks, qkv, qkv, qkv, qkv, qkv)


def swa_bwd(qkv, o, lse, do, sinks, hq, name):
    t = qkv.shape[0]
    w = SWA_WINDOW
    nb = t // w
    hkv = hq // SWA_GROUP
    npair = hkv // 2
    qw = 2 * SWA_GROUP * ATT_HEAD
    kcol = hq * ATT_HEAD // LANES
    vcol = kcol + npair
    scale = ATT_HEAD ** -0.5
    gh = 2 * SWA_GROUP

    def body(sink_ref, qc_ref, qn_ref, kp_ref, kc_ref, vp_ref, vc_ref, oc_ref, on_ref, dc_ref, dn_ref,
             lc_ref, ln_ref, dq_ref, dk_ref, dv_ref, ds_ref):
        m, n = pl.program_id(0), pl.program_id(1)
        kb = jnp.concatenate([kp_ref[...], kc_ref[...]], axis=0)
        vb = jnp.concatenate([vp_ref[...], vc_ref[...]], axis=0)
        ok_band = _swa_mask(n, w, False)
        ok_cur = _swa_mask(n, w, True)
        qi = lax.broadcasted_iota(jnp.int32, (w, w), 0)
        kj = lax.broadcasted_iota(jnp.int32, (w, w), 1)
        ok_next = (kj > qi) & (n < nb - 1)
        row16 = lax.broadcasted_iota(jnp.int32, (gh, LANES), 0)
        dsink = jnp.zeros((gh, LANES), F32)
        dk_tot = jnp.zeros((w, LANES), F32)
        dv_tot = jnp.zeros((w, LANES), F32)
        for kvh in range(2):
            kd, vd = _dup_head(kb, kvh), _dup_head(vb, kvh)
            kdc, vdc = kd[w:, :], vd[w:, :]
            acc_k = [jnp.zeros((w, LANES), F32), jnp.zeros((w, LANES), F32)]
            acc_v = [jnp.zeros((w, LANES), F32), jnp.zeros((w, LANES), F32)]
            for jj in range(SWA_GROUP // 2):
                jp = kvh * (SWA_GROUP // 2) + jj
                cols = slice(jp * LANES, (jp + 1) * LANES)
                dqs = []
                for a in range(2):
                    hd = jp * 2 + a
                    sink = sink_ref[m * gh + hd]
                    half = _half((w, LANES), a)
                    q2 = jnp.where(half, qc_ref[:, cols], jnp.zeros((w, LANES), BF16))
                    d2 = jnp.where(half, dc_ref[:, cols], jnp.zeros((w, LANES), BF16))
                    delta = jnp.sum(d2.astype(F32) * oc_ref[:, cols].astype(F32), axis=1, keepdims=True)
                    lse_c = lc_ref[hd][:, 0:1]
                    p = jnp.exp(jnp.where(ok_band, _dot_nt(q2, kd) * scale, NEG) - lse_c)
                    dsv = p * (_dot_nt(d2, vd) - delta)
                    dqs.append(_dot_nn(dsv.astype(BF16), kd) * scale)
                    psink = jnp.exp(sink - lse_c)
                    dsink = jnp.where(row16 == hd, dsink - jnp.sum(psink * delta, axis=0, keepdims=True), dsink)
                    for q_ref, d_ref, o_ref, l_ref, okm in ((qc_ref, dc_ref, oc_ref, lc_ref, ok_cur),
                                                           (qn_ref, dn_ref, on_ref, ln_ref, ok_next)):
                        q2 = jnp.where(half, q_ref[:, cols], jnp.zeros((w, LANES), BF16))
                        d2 = jnp.where(half, d_ref[:, cols], jnp.zeros((w, LANES), BF16))
                        delta = jnp.sum(d2.astype(F32) * o_ref[:, cols].astype(F32), axis=1, keepdims=True)
                        p = jnp.exp(jnp.where(okm, _dot_nt(q2, kdc) * scale, NEG) - l_ref[hd][:, 0:1])
                        dsv = p * (_dot_nt(d2, vdc) - delta)
                        acc_v[a] = acc_v[a] + _dot_tn(p.astype(BF16), d2)
                        acc_k[a] = acc_k[a] + _dot_tn(dsv.astype(BF16), q2) * scale
                dq_ref[:, cols] = jnp.where(_half((w, LANES), 0), dqs[0], dqs[1]).astype(BF16)
            dk_tot = dk_tot + acc_k[kvh] + pltpu.roll(acc_k[1 - kvh], ATT_HEAD, 1)
            dv_tot = dv_tot + acc_v[kvh] + pltpu.roll(acc_v[1 - kvh], ATT_HEAD, 1)
        dk_ref[...] = dk_tot.astype(BF16)
        dv_ref[...] = dv_tot.astype(BF16)
        ds_ref[0, 0] = dsink

    prev = lambda n: jnp.maximum(n - 1, 0)
    nxt = lambda n: jnp.minimum(n + 1, nb - 1)
    qspec = lambda f: pl.BlockSpec((w, qw), lambda m, n, s: (f(n), m))
    lspec = lambda f: pl.BlockSpec((gh, w, LANES), lambda m, n, s: (m, f(n), 0))
    same = lambda n: n
    grid_spec = pltpu.PrefetchScalarGridSpec(
        num_scalar_prefetch=1, grid=(npair, nb),
        in_specs=[
            qspec(same), qspec(nxt),
            pl.BlockSpec((w, LANES), lambda m, n, s: (prev(n), kcol + m)),
            pl.BlockSpec((w, LANES), lambda m, n, s: (n, kcol + m)),
            pl.BlockSpec((w, LANES), lambda m, n, s: (prev(n), vcol + m)),
            pl.BlockSpec((w, LANES), lambda m, n, s: (n, vcol + m)),
            qspec(same), qspec(nxt), qspec(same), qspec(nxt),
            lspec(same), lspec(nxt),
        ],
        out_specs=[
            pl.BlockSpec((w, qw), lambda m, n, s: (n, m)),
            pl.BlockSpec((w, LANES), lambda m, n, s: (n, m)),
            pl.BlockSpec((w, LANES), lambda m, n, s: (n, m)),
            pl.BlockSpec((1, 1, gh, LANES), lambda m, n, s: (m, n, 0, 0)),
        ],
    )
    return pl.pallas_call(
        body, name=name, grid_spec=grid_spec,
        out_shape=[jax.ShapeDtypeStruct((t, hq * ATT_HEAD), BF16),
                   jax.ShapeDtypeStruct((t, hkv * ATT_HEAD), BF16),
                   jax.ShapeDtypeStruct((t, hkv * ATT_HEAD), BF16),
                   jax.ShapeDtypeStruct((npair, nb, gh, LANES), F32)],
        compiler_params=_params((_PAR, _PAR)),
    )(sinks, qkv, qkv, qkv, qkv, qkv, qkv, o, o, do, do, lse, lse)


def swa_mixer_fwd(hn, w_in, w_out, sinks, positions, name):
    hq = sinks.shape[0]
    n_rot = (hq + hq // SWA_GROUP) * ATT_HEAD
    tabs = rope_tables(positions)
    proj = mm(hn, w_in, "nn", BF16, name + "_in")
    qkv = rope(proj, *tabs, n_rot, 1.0, name + "_rope")
    o, lse = swa_fwd(qkv, sinks, hq, name + "_att")
    y = mm(o, w_out, "nn", F32, name + "_out")
    return y, (qkv, o, lse)


def swa_mixer_bwd(dy, hn, w_in, w_out, sinks, positions, saved, name, ns=1):
    qkv, o, lse = saved
    hq = sinks.shape[0]
    n_rot = (hq + hq // SWA_GROUP) * ATT_HEAD
    tabs = rope_tables(positions)
    dwo = mm(o, dy, "tn", BF16, name + "_dwo")
    do = mm(dy, w_out, "nt", BF16, name + "_do")
    dq, dk, dv, dsp = swa_bwd(qkv, o, lse, do, sinks, hq, name + "_attb")
    dproj = rope(jnp.concatenate([dq, dk, dv], axis=1), *tabs, n_rot, -1.0, name + "_ropeb")
    dwi = mm(hn, dproj, "tn", BF16, name + "_dwi", stack=ns)
    dhn = mm(dproj, w_in, "nt", BF16, name + "_dhn")
    dsinks = jnp.sum(dsp[:, :, :, 0], axis=1).reshape(hq)
    return dhn, dwi, dwo, dsinks


FOX_FPAD = 512


def _log_sigmoid(x):
    return jnp.minimum(x, 0.0) - jnp.log(1.0 + jnp.exp(-jnp.abs(x)))


def _tri(n, upper):
    r = lax.broadcasted_iota(jnp.int32, (n, n), 0)
    c = lax.broadcasted_iota(jnp.int32, (n, n), 1)
    return jnp.where((c >= r) if upper else (c <= r), 1.0, 0.0).astype(F32)


def _dot_exact(a, b):
    return jnp.dot(a, b, precision=lax.Precision.HIGHEST, preferred_element_type=F32)


def fox_cumsum(fl, b_pad, name):
    t = fl.shape[0]
    tr = _tile(t, (256, 128, 64, 32, 16, 8))

    def body(f_ref, b_ref, c_ref, carry_ref):
        @pl.when(pl.program_id(0) == 0)
        def _():
            carry_ref[...] = jnp.zeros_like(carry_ref)

        c = _dot_exact(_tri(tr, False), _log_sigmoid(f_ref[...] + b_ref[...])) + carry_ref[...]
        c_ref[...] = c
        carry_ref[...] = c[tr - 1:tr, :]

    return pl.pallas_call(
        body, name=name, grid=(t // tr,),
        in_specs=[_row_spec(tr, LANES), _full_spec((1, LANES))],
        out_specs=_row_spec(tr, LANES),
        out_shape=jax.ShapeDtypeStruct((t, LANES), F32),
        scratch_shapes=[pltpu.VMEM((1, LANES), F32)],
        compiler_params=_params((_ARB,)),
    )(fl, b_pad)


def fox_cumsum_bwd(dc, fl, b_pad, name):
    t = fl.shape[0]
    tr = _tile(t, (256, 128, 64, 32, 16, 8))
    nt = t // tr

    def body(d_ref, f_ref, b_ref, o_ref, db_ref, carry_ref):
        @pl.when(pl.program_id(0) == 0)
        def _():
            carry_ref[...] = jnp.zeros_like(carry_ref)
            db_ref[...] = jnp.zeros_like(db_ref)

        dlf = _dot_exact(_tri(tr, True), d_ref[...]) + carry_ref[...]
        carry_ref[...] = dlf[0:1, :]
        dfl = dlf * _sigmoid(-(f_ref[...] + b_ref[...]))
        o_ref[...] = dfl.astype(BF16)
        db_ref[...] += jnp.sum(dfl, axis=0, keepdims=True)

    rev = pl.BlockSpec((tr, LANES), lambda i: (nt - 1 - i, 0))
    return pl.pallas_call(
        body, name=name, grid=(nt,),
        in_specs=[rev, rev, _full_spec((1, LANES))],
        out_specs=[rev, _full_spec((1, LANES))],
        out_shape=[jax.ShapeDtypeStruct((t, LANES), BF16), jax.ShapeDtypeStruct((1, LANES), F32)],
        scratch_shapes=[pltpu.VMEM((1, LANES), F32)],
        compiler_params=_params((_ARB,)),
    )(dc, fl, b_pad)


def _fox_tile(t):
    return _tile(t, (256, 128))


def _causal(i, j, tq, tk):
    r = lax.broadcasted_iota(jnp.int32, (tq, tk), 0) + i * tq
    c = lax.broadcasted_iota(jnp.int32, (tq, tk), 1) + j * tk
    return c <= r


def _lanes_to(x, tk):
    return x if tk == LANES else jnp.tile(x, (1, tk // LANES))


def fox_fwd(proj, ccol, crow, nh, name):
    t = proj.shape[0]
    hd = nh * ATT_HEAD
    npair = nh // 2
    tq = tk = _fox_tile(t)
    nt = t // tq
    gcol = (3 * hd + FOX_FPAD) // LANES
    scale = ATT_HEAD ** -0.5

    def body(q_ref, k_ref, v_ref, g_ref, cc_ref, cr_ref, o_ref, og_ref, lse_ref):
        i = pl.program_id(1)
        q2 = q_ref[...]
        outs = []
        for h in range(2):
            qa = jnp.where(_half(q2.shape, h), q2, jnp.zeros_like(q2))
            cq = _lanes_to(cc_ref[h], tk)

            def step(j, carry, qa=qa, cq=cq, h=h):
                mx, l, acc = carry
                rows = pl.ds(pl.multiple_of(j * tk, tk), tk)
                s = _dot_nt(qa, k_ref[rows, :]) * scale + cq - cr_ref[0, h, pl.ds(j, 1), :]
                s = jnp.where(_causal(i, j, tq, tk), s, NEG)
                mnew = jnp.maximum(mx, jnp.max(s, axis=1, keepdims=True))
                alpha = jnp.exp(mx - mnew)
                p = jnp.exp(s - mnew)
                l = alpha * l + jnp.sum(p, axis=1, keepdims=True)
                acc = alpha * acc + _dot_nn(p.astype(BF16), v_ref[rows, :])
                return mnew, l, acc

            init = (jnp.full((tq, 1), NEG, F32), jnp.zeros((tq, 1), F32), jnp.zeros((tq, LANES), F32))
            mx, l, acc = lax.fori_loop(0, i + 1, step, init)
            outs.append(acc / l)
            lse_ref[h] = jnp.broadcast_to(mx + jnp.log(l), (tq, LANES))
        o = jnp.where(_half(q2.shape, 0), outs[0], outs[1])
        o_ref[...] = o.astype(BF16)
        og_ref[...] = (o * _sigmoid(g_ref[...].astype(F32))).astype(BF16)

    tile = lambda col: pl.BlockSpec((tq, LANES), lambda p, i: (i, col + p))
    resident = lambda col: pl.BlockSpec((t, LANES), lambda p, i: (0, col + p))
    return pl.pallas_call(
        body, name=name, grid=(npair, nt),
        in_specs=[tile(0), resident(npair), resident(2 * npair), tile(gcol),
                  pl.BlockSpec((2, tq, LANES), lambda p, i: (p, i, 0)),
                  pl.BlockSpec((1, 2, nt, tk), lambda p, i: (p, 0, 0, 0))],
        out_specs=[pl.BlockSpec((tq, LANES), lambda p, i: (i, p)),
                   pl.BlockSpec((tq, LANES), lambda p, i: (i, p)),
                   pl.BlockSpec((2, tq, LANES), lambda p, i: (p, i, 0))],
        out_shape=[jax.ShapeDtypeStruct((t, hd), BF16), jax.ShapeDtypeStruct((t, hd), BF16),
                   jax.ShapeDtypeStruct((nh, t, LANES), F32)],
        compiler_params=_params((_PAR, _PAR)),
    )(proj, proj, proj, proj, ccol, crow)


def fox_gate_bwd(dog, o, proj, nh, name):
    t, hd = o.shape
    npair = nh // 2
    tr = _tile(t, (256, 128))
    gcol = (3 * hd + FOX_FPAD) // LANES

    def body(d_ref, o_ref, g_ref, do_ref, dg_ref, dl_ref):
        dv, ov = d_ref[...].astype(F32), o_ref[...].astype(F32)
        sg = _sigmoid(g_ref[...].astype(F32))
        do = (dv * sg).astype(BF16)
        do_ref[...] = do
        dg_ref[...] = (dv * ov * sg * (1.0 - sg)).astype(BF16)
        prod = do.astype(F32) * ov
        for h in range(2):
            dl = jnp.sum(jnp.where(_half(prod.shape, h), prod, 0.0), axis=1, keepdims=True)
            dl_ref[h] = jnp.broadcast_to(dl, (tr, LANES))

    blk = pl.BlockSpec((tr, LANES), lambda p, i: (i, p))
    return pl.pallas_call(
        body, name=name, grid=(npair, t // tr),
        in_specs=[blk, blk, pl.BlockSpec((tr, LANES), lambda p, i: (i, gcol + p))],
        out_specs=[blk, blk, pl.BlockSpec((2, tr, LANES), lambda p, i: (p, i, 0))],
        out_shape=[jax.ShapeDtypeStruct((t, hd), BF16), jax.ShapeDtypeStruct((t, hd), BF16),
                   jax.ShapeDtypeStruct((nh, t, LANES), F32)],
        compiler_params=_params((_PAR, _PAR)),
    )(dog, o, proj)


def fox_dq(proj, do, lse, delta, ccol, crow, nh, name):
    t = proj.shape[0]
    hd = nh * ATT_HEAD
    npair = nh // 2
    tq = tk = _fox_tile(t)
    nt = t // tq
    scale = ATT_HEAD ** -0.5

    def body(q_ref, k_ref, v_ref, do_ref, l_ref, dl_ref, cc_ref, cr_ref, dq_ref, rs_ref):
        i = pl.program_id(1)
        q2, d2 = q_ref[...], do_ref[...]
        acc = jnp.zeros((tq, LANES), F32)
        for h in range(2):
            half = _half(q2.shape, h)
            qa = jnp.where(half, q2, jnp.zeros_like(q2))
            da = jnp.where(half, d2, jnp.zeros_like(d2))
            cq = _lanes_to(cc_ref[h], tk)
            lse = _lanes_to(l_ref[h], tk)
            dl = _lanes_to(dl_ref[h], tk)

            def step(j, carry, qa=qa, da=da, cq=cq, lse=lse, dl=dl, h=h):
                acc, rsum = carry
                rows = pl.ds(pl.multiple_of(j * tk, tk), tk)
                kj = k_ref[rows, :]
                s = _dot_nt(qa, kj) * scale + cq - cr_ref[0, h, pl.ds(j, 1), :]
                p = jnp.exp(jnp.where(_causal(i, j, tq, tk), s, NEG) - lse)
                dsv = p * (_dot_nt(da, v_ref[rows, :]) - dl)
                kh = jnp.where(_half(kj.shape, h), kj, jnp.zeros_like(kj))
                return acc + _dot_nn(dsv.astype(BF16), kh), rsum + jnp.sum(dsv, axis=1, keepdims=True)

            acc, rsum = lax.fori_loop(0, i + 1, step, (acc, jnp.zeros((tq, 1), F32)))
            rs_ref[h] = jnp.broadcast_to(rsum, (tq, LANES))
        dq_ref[...] = (acc * scale).astype(BF16)

    tile = lambda col: pl.BlockSpec((tq, LANES), lambda p, i: (i, col + p))
    resident = lambda col: pl.BlockSpec((t, LANES), lambda p, i: (0, col + p))
    stat = pl.BlockSpec((2, tq, LANES), lambda p, i: (p, i, 0))
    return pl.pallas_call(
        body, name=name, grid=(npair, nt),
        in_specs=[tile(0), resident(npair), resident(2 * npair), tile(0), stat, stat, stat,
                  pl.BlockSpec((1, 2, nt, tk), lambda p, i: (p, 0, 0, 0))],
        out_specs=[pl.BlockSpec((tq, LANES), lambda p, i: (i, p)), stat],
        out_shape=[jax.ShapeDtypeStruct((t, hd), BF16), jax.ShapeDtypeStruct((nh, t, LANES), F32)],
        compiler_params=_params((_PAR, _PAR)),
    )(proj, proj, proj, do, lse, delta, ccol, crow)


def fox_dkv(proj, do, lse, delta, ccol, crow, nh, name):
    t = proj.shape[0]
    hd = nh * ATT_HEAD
    npair = nh // 2
    tq = tk = _fox_tile(t)
    nt = t // tq
    scale = ATT_HEAD ** -0.5

    def body(q_ref, k_ref, v_ref, do_ref, l_ref, dl_ref, cc_ref, cr_ref, dk_ref, dv_ref, dc_ref):
        j = pl.program_id(1)
        k2, v2 = k_ref[...], v_ref[...]
        dks, dvs = [], []
        for h in range(2):
            half = _half(k2.shape, h)
            kh = jnp.where(half, k2, jnp.zeros_like(k2))
            vh = jnp.where(half, v2, jnp.zeros_like(v2))
            ck = cr_ref[0, h, pl.ds(j, 1), :]

            def step(i, carry, kh=kh, vh=vh, ck=ck, h=h):
                dk, dv, dc = carry
                rows = pl.ds(pl.multiple_of(i * tq, tq), tq)
                qi, di = q_ref[rows, :], do_ref[rows, :]
                s = _dot_nt(qi, kh) * scale + _lanes_to(cc_ref[h, rows, :], tk) - ck
                p = jnp.exp(jnp.where(_causal(i, j, tq, tk), s, NEG) - _lanes_to(l_ref[h, rows, :], tk))
                dsv = p * (_dot_nt(di, vh) - _lanes_to(dl_ref[h, rows, :], tk))
                dv = dv + _dot_tn(p.astype(BF16), di)
                dk = dk + _dot_tn(dsv.astype(BF16), qi)
                dc = dc - jnp.sum(dsv, axis=0, keepdims=True)
                return dk, dv, dc

            init = (jnp.zeros((tk, LANES), F32), jnp.zeros((tk, LANES), F32), jnp.zeros((1, tk), F32))
            dk, dv, dc = lax.fori_loop(j, nt, step, init)
            dks.append(dk)
            dvs.append(dv)
            dc_ref[0, 0, pl.ds(h, 1), :] = dc
        first = _half(k2.shape, 0)
        dk_ref[...] = (jnp.where(first, dks[0], dks[1]) * scale).astype(BF16)
        dv_ref[...] = jnp.where(first, dvs[0], dvs[1]).astype(BF16)

    tile = lambda col: pl.BlockSpec((tk, LANES), lambda p, j: (j, col + p))
    resident = lambda col: pl.BlockSpec((t, LANES), lambda p, j: (0, col + p))
    stat = pl.BlockSpec((2, t, LANES), lambda p, j: (p, 0, 0))
    return pl.pallas_call(
        body, name=name, grid=(npair, nt),
        in_specs=[resident(0), tile(npair), tile(2 * npair), resident(0), stat, stat, stat,
                  pl.BlockSpec((1, 2, nt, tk), lambda p, j: (p, 0, 0, 0))],
        out_specs=[pl.BlockSpec((tk, LANES), lambda p, j: (j, p)),
                   pl.BlockSpec((tk, LANES), lambda p, j: (j, p)),
                   pl.BlockSpec((1, 1, 2, tk), lambda p, j: (p, j, 0, 0))],
        out_shape=[jax.ShapeDtypeStruct((t, hd), BF16), jax.ShapeDtypeStruct((t, hd), BF16),
                   jax.ShapeDtypeStruct((npair, nt, 2, tk), F32)],
        compiler_params=_params((_PAR, _PAR)),
    )(proj, proj, proj, do, lse, delta, ccol, crow)


def fox_pad_w_in(w_in, nh):
    hd = nh * ATT_HEAD
    pad = jnp.zeros((w_in.shape[0], FOX_FPAD - nh), w_in.dtype)
    return jnp.concatenate([w_in[:, :3 * hd + nh], pad, w_in[:, 3 * hd + nh:]], axis=1)


def fox_unpad_dw(dw, nh):
    hd = nh * ATT_HEAD
    return jnp.concatenate([dw[:, :3 * hd + nh], dw[:, 3 * hd + FOX_FPAD:]], axis=1)


def _pad_lanes(v):
    return jnp.pad(v.reshape(1, -1).astype(F32), ((0, 0), (0, LANES - v.size)))


def fox_mixer_fwd(hn, w_pad, w_out, b_f, name):
    nh = b_f.shape[0]
    hd = nh * ATT_HEAD
    t = hn.shape[0]
    tk = _fox_tile(t)
    proj = mm(hn, w_pad, "nn", BF16, name + "_in")
    fl = mm(hn, w_pad[:, 3 * hd:3 * hd + LANES], "nn", F32, name + "_fl")
    c = fox_cumsum(fl, _pad_lanes(b_f), name + "_cum")
    ch = c[:, :nh].T
    ccol = jnp.broadcast_to(ch[:, :, None], (nh, t, LANES))
    crow = ch.reshape(nh // 2, 2, t // tk, tk)
    o, og, lse = fox_fwd(proj, ccol, crow, nh, name + "_att")
    y = mm(og, w_out, "nn", F32, name + "_out")
    return y, (proj, fl, ccol, crow, o, og, lse)


def fox_mixer_bwd(dy, hn, w_pad, w_out, b_f, saved, name):
    proj, fl, ccol, crow, o, og, lse = saved
    nh = b_f.shape[0]
    t = hn.shape[0]
    dwo = mm(og, dy, "tn", BF16, name + "_dwo")
    dog = mm(dy, w_out, "nt", BF16, name + "_dog")
    do, dg, delta = fox_gate_bwd(dog, o, proj, nh, name + "_gateb")
    dq, drow = fox_dq(proj, do, lse, delta, ccol, crow, nh, name + "_dq")
    dk, dv, dcr = fox_dkv(proj, do, lse, delta, ccol, crow, nh, name + "_dkv")
    dc = jnp.pad(drow[:, :, 0].T + dcr.transpose(1, 3, 0, 2).reshape(t, nh), ((0, 0), (0, LANES - nh)))
    dfl, db = fox_cumsum_bwd(dc, fl, _pad_lanes(b_f), name + "_cumb")
    dfl = jnp.pad(dfl, ((0, 0), (0, FOX_FPAD - LANES)))
    dproj = jnp.concatenate([dq, dk, dv, dfl, dg], axis=1)
    dwi = mm(hn, dproj, "tn", BF16, name + "_dwi")
    dhn = mm(dproj, w_pad, "nt", BF16, name + "_dhn")
    return dhn, dwi, dwo, db[0, :nh]


HGRN_ROWS = 256


def lb_table_row(lb_param, idx, name):
    nrow, w = lb_param.shape

    def body(p_ref, o_ref):
        rows = [p_ref[r:r + 1, :] for r in range(nrow)]
        mx = functools.reduce(jnp.maximum, rows)
        es = [jnp.exp(r - mx) for r in rows]
        o_ref[...] = sum(es[:idx + 1]) / sum(es)

    return pl.pallas_call(
        body, name=name, in_specs=[_full_spec((nrow, w))], out_specs=_full_spec((1, w)), grid=(1,),
        out_shape=jax.ShapeDtypeStruct((1, w), F32),
    )(lb_param)


def lb_table_row_bwd(lb_param, dlb, idx, name):
    nrow, w = lb_param.shape

    def body(p_ref, d_ref, o_ref):
        rows = [p_ref[r:r + 1, :] for r in range(nrow)]
        mx = functools.reduce(jnp.maximum, rows)
        es = [jnp.exp(r - mx) for r in rows]
        tot = sum(es)
        ps = [e / tot for e in es]
        dv = d_ref[...]
        inner = sum(ps[:idx + 1]) * dv
        for r in range(nrow):
            o_ref[r:r + 1, :] = ps[r] * ((dv if r <= idx else 0.0) - inner)

    return pl.pallas_call(
        body, name=name, in_specs=[_full_spec((nrow, w)), _full_spec((1, w))], out_specs=_full_spec((nrow, w)),
        grid=(1,), out_shape=jax.ShapeDtypeStruct((nrow, w), F32),
    )(lb_param, dlb)


def _hgrn_gates(qraw, fraw, lb):
    sq = _sigmoid(qraw)
    sf = _sigmoid(fraw)
    f = lb + (1.0 - lb) * sf
    return qraw * sq, sq, sf, f, 1.0 - f


def _hgrn_chunk(q, k, f):
    c = HGRN_CHUNK
    b = _dot_exact(_tri(c, False), jnp.log(f))
    bl = b[c - 1:c, :]
    bm = b[c // 2 - 1:c // 2, :]
    eq, ek = jnp.exp(b - bm), jnp.exp(bm - b)
    eb, el = jnp.exp(b), jnp.exp(bl - b)
    qt, kt = (q * eq).astype(BF16), (k * ek).astype(BF16)
    causal = _tri(c, False) > 0.5
    amat = jnp.where(causal, _dot_nt(qt, kt), 0.0).astype(BF16)
    return amat, qt, kt, (q * eb).astype(BF16), (k * el).astype(BF16), eq, ek, eb, el, jnp.exp(bl), causal


def hgrn_fwd(proj, lb, norm_g, nh, name):
    t = proj.shape[0]
    w = nh * HGRN_HEAD
    c = HGRN_CHUNK
    rows = _tile(t, (HGRN_ROWS, 128, 64, 32))
    nr, nc = t // rows, rows // c

    def body(q_ref, f_ref, i_ref, g_ref, lb_ref, ng_ref, og_ref, o_ref, st_ref, state):
        @pl.when(pl.program_id(1) == 0)
        def _():
            state[...] = jnp.zeros_like(state)

        def step(cc, carry):
            sl = pl.ds(pl.multiple_of(cc * c, c), c)
            q, _, _, f, k = _hgrn_gates(q_ref[sl, :].astype(F32), f_ref[sl, :].astype(F32), lb_ref[...])
            v = i_ref[sl, :]
            amat, _, _, qd, kd, _, _, _, _, ebl, _ = _hgrn_chunk(q, k, f)
            st = state[...]
            st_ref[0, cc] = st.astype(BF16)
            o = _dot_nt(qd, st.astype(BF16)) + _dot_nn(amat, v)
            state[...] = st * ebl + _dot_tn(v, kd)
            o_ref[sl, :] = o
            graw = g_ref[sl, :].astype(F32)
            og_ref[sl, :] = (o * _rstd(o) * ng_ref[...] * (graw * _sigmoid(graw))).astype(BF16)
            return carry

        lax.fori_loop(0, nc, step, 0)

    col = lambda off: pl.BlockSpec((rows, HGRN_HEAD), lambda h, r: (r, off + h))
    return pl.pallas_call(
        body, name=name, grid=(nh, nr),
        in_specs=[col(0), col(nh), col(2 * nh), col(3 * nh),
                  pl.BlockSpec((1, HGRN_HEAD), lambda h, r: (0, h)), _full_spec((1, HGRN_HEAD))],
        out_specs=[col(0), col(0), pl.BlockSpec((1, nc, HGRN_HEAD, HGRN_HEAD), lambda h, r: (h, r, 0, 0))],
        out_shape=[jax.ShapeDtypeStruct((t, w), BF16), jax.ShapeDtypeStruct((t, w), F32),
                   jax.ShapeDtypeStruct((nh, t // c, HGRN_HEAD, HGRN_HEAD), BF16)],
        scratch_shapes=[pltpu.VMEM((HGRN_HEAD, HGRN_HEAD), F32)],
        compiler_params=_params((_PAR, _ARB)),
    )(proj, proj, proj, proj, lb, norm_g)


def hgrn_bwd(proj, lb, norm_g, o, states, dog, nh, name):
    t = proj.shape[0]
    w = nh * HGRN_HEAD
    c = HGRN_CHUNK
    rows = _tile(t, (HGRN_ROWS, 128, 64, 32))
    nr, nc = t // rows, rows // c

    def body(q_ref, f_ref, i_ref, g_ref, lb_ref, ng_ref, o_ref, st_ref, dog_ref,
             dq_ref, df_ref, di_ref, dg_ref, dlb_ref, dng_ref, dstate):
        @pl.when(pl.program_id(1) == 0)
        def _():
            dstate[...] = jnp.zeros_like(dstate)
            dlb_ref[...] = jnp.zeros_like(dlb_ref)
            dng_ref[...] = jnp.zeros_like(dng_ref)

        lb = lb_ref[...]
        ng = ng_ref[...]

        def step(idx, carry):
            cc = nc - 1 - idx
            sl = pl.ds(pl.multiple_of(cc * c, c), c)
            qraw, fraw = q_ref[sl, :].astype(F32), f_ref[sl, :].astype(F32)
            q, sq, sf, f, k = _hgrn_gates(qraw, fraw, lb)
            v = i_ref[sl, :]
            amat, qt, kt, qd, kd, eq, ek, eb, el, ebl, causal = _hgrn_chunk(q, k, f)
            ov = o_ref[sl, :]
            graw = g_ref[sl, :].astype(F32)
            dogv = dog_ref[sl, :].astype(F32)
            sil, dsil = _silu_and_grad(graw)
            r = _rstd(ov)
            oh = ov * r
            don = dogv * sil
            dg_ref[sl, :] = (dogv * oh * ng * dsil).astype(BF16)
            dng_ref[0] += jnp.sum(don * oh, axis=0, keepdims=True)
            doh = don * ng
            do = (r * (doh - oh * jnp.mean(doh * oh, axis=1, keepdims=True))).astype(BF16)
            dst = dstate[...]
            dstb = dst.astype(BF16)
            da = jnp.where(causal, _dot_nt(do, v), 0.0).astype(BF16)
            dv = _dot_tn(amat, do) + _dot_nt(kd, dstb)
            st0 = st_ref[0, cc]
            dq = _dot_nn(da, kt) * eq + _dot_nn(do, st0) * eb
            dk_inter = _dot_nn(v, dstb) * el
            dk = _dot_tn(da, qt) * ek + dk_inter
            dstate[...] = dst * ebl + _dot_tn(do, qd)
            through = jnp.sum(dst * st0.astype(F32), axis=0, keepdims=True) * ebl
            later = jnp.sum(k * dk_inter, axis=0, keepdims=True) + through
            dlf = _dot_exact(_tri(c, True), q * dq - k * dk) + later
            df = dlf / f - dk
            dq_ref[sl, :] = (dq * (sq * (1.0 + qraw * (1.0 - sq)))).astype(BF16)
            df_ref[sl, :] = (df * (1.0 - lb) * sf * (1.0 - sf)).astype(BF16)
            di_ref[sl, :] = dv.astype(BF16)
            dlb_ref[...] += jnp.sum(df * (1.0 - sf), axis=0, keepdims=True)
            return carry

        lax.fori_loop(0, nc, step, 0)

    col = lambda off: pl.BlockSpec((rows, HGRN_HEAD), lambda h, r: (nr - 1 - r, off + h))
    out = col(0)
    return pl.pallas_call(
        body, name=name, grid=(nh, nr),
        in_specs=[col(0), col(nh), col(2 * nh), col(3 * nh),
                  pl.BlockSpec((1, HGRN_HEAD), lambda h, r: (0, h)), _full_spec((1, HGRN_HEAD)),
                  out, pl.BlockSpec((1, nc, HGRN_HEAD, HGRN_HEAD), lambda h, r: (h, nr - 1 - r, 0, 0)), out],
        out_specs=[out, out, out, out, pl.BlockSpec((1, HGRN_HEAD), lambda h, r: (0, h)),
                   pl.BlockSpec((1, 1, HGRN_HEAD), lambda h, r: (h, 0, 0))],
        out_shape=[jax.ShapeDtypeStruct((t, w), BF16)] * 4 + [jax.ShapeDtypeStruct((1, w), F32),
                                                             jax.ShapeDtypeStruct((nh, 1, HGRN_HEAD), F32)],
        scratch_shapes=[pltpu.VMEM((HGRN_HEAD, HGRN_HEAD), F32)],
        compiler_params=_params((_PAR, _ARB)),
    )(proj, proj, proj, proj, lb, norm_g, o, states, dog)


def hgrn_mixer_fwd(hn, w_in, w_out, norm_g, lb_param, idx, name):
    nh = w_out.shape[0] // HGRN_HEAD
    lb = lb_table_row(lb_param, idx, name + "_lb")
    proj = mm(hn, w_in, "nn", BF16, name + "_in")
    og, o, states = hgrn_fwd(proj, lb, norm_g, nh, name + "_rec")
    y = mm(og, w_out, "nn", F32, name + "_out")
    return y, (proj, lb, og, o, states)


def hgrn_mixer_bwd(dy, hn, w_in, w_out, norm_g, lb_param, idx, saved, name, ns=1):
    proj, lb, og, o, states = saved
    nh = w_out.shape[0] // HGRN_HEAD
    dwo = mm(og, dy, "tn", BF16, name + "_dwo")
    dog = mm(dy, w_out, "nt", BF16, name + "_dog")
    dq, df, di, dg, dlb, dng = hgrn_bwd(proj, lb, norm_g, o, states, dog, nh, name + "_recb")
    dproj = jnp.concatenate([dq, df, di, dg], axis=1)
    dwi = mm(hn, dproj, "tn", BF16, name + "_dwi", stack=ns)
    dhn = mm(dproj, w_in, "nt", BF16, name + "_dhn")
    dlbp = lb_table_row_bwd(lb_param, dlb, idx, name + "_lbb")
    return dhn, dwi, dwo, jnp.sum(dng, axis=0), dlbp


_ANY = pl.BlockSpec(memory_space=pl.ANY)
_MESH = pl.DeviceIdType.MESH


def _place():
    x, y, c = lax.axis_index("x"), lax.axis_index("y"), lax.axis_index("c")
    chips = [(1 - x, y), (x, 1 - y), (1 - x, 1 - y)]
    return x, y, c, N_CHIPS // 2 * x + y, chips


def _chip_index(chip):
    return N_CHIPS // 2 * chip[0] + chip[1]


def _window(ref, axis, start, size):
    idx = [slice(None)] * len(ref.shape)
    idx[axis] = pl.ds(start, size)
    return ref.at[tuple(idx)]


def gather_shards(shards, axes, name):
    n = len(shards)

    def body(*refs):
        ins, outs = refs[:n], refs[n:2 * n]
        send, recv, local = refs[2 * n:]
        x, y, c, me, chips = _place()
        started = []
        for k in range(n):
            size = ins[k].shape[axes[k]]
            mine = pltpu.make_async_copy(ins[k], _window(outs[k], axes[k], me * size, size), local.at[k])
            mine.start()
            started.append(mine)
            for j, chip in enumerate(chips):
                pltpu.make_async_remote_copy(
                    src_ref=ins[k], dst_ref=_window(outs[k], axes[k], me * size, size),
                    send_sem=send.at[k, j], recv_sem=recv.at[k, j],
                    device_id=(chip[0], chip[1], c), device_id_type=_MESH).start()
        for k in range(n):
            size = ins[k].shape[axes[k]]
            for j, chip in enumerate(chips):
                landed = pltpu.make_async_remote_copy(
                    src_ref=ins[k], dst_ref=_window(outs[k], axes[k], _chip_index(chip) * size, size),
                    send_sem=send.at[k, j], recv_sem=recv.at[k, j],
                    device_id=(chip[0], chip[1], c), device_id_type=_MESH)
                landed.wait_recv()
                landed.wait_send()
            started[k].wait()

    out_shape = []
    for s, ax in zip(shards, axes):
        shape = list(s.shape)
        shape[ax] *= N_CHIPS
        out_shape.append(jax.ShapeDtypeStruct(tuple(shape), s.dtype))
    return pl.pallas_call(
        body, name=name, in_specs=[_ANY] * n, out_specs=[_ANY] * n, out_shape=out_shape,
        scratch_shapes=[pltpu.SemaphoreType.DMA((n, N_CHIPS - 1)), pltpu.SemaphoreType.DMA((n, N_CHIPS - 1)),
                        pltpu.SemaphoreType.DMA((n,))],
    )(*shards)


def chip_exchange(ps, name):
    n = len(ps)

    def body(*refs):
        ins, outs = refs[:n], refs[n:2 * n]
        send, recv, local = refs[2 * n:]
        x, y, c, me, chips = _place()
        started = []
        for k in range(n):
            mine = pltpu.make_async_copy(ins[k].at[me], outs[k].at[me], local.at[k])
            mine.start()
            started.append(mine)
            for j, chip in enumerate(chips):
                pltpu.make_async_remote_copy(
                    src_ref=ins[k].at[_chip_index(chip)], dst_ref=outs[k].at[me],
                    send_sem=send.at[k, j], recv_sem=recv.at[k, j],
                    device_id=(chip[0], chip[1], c), device_id_type=_MESH).start()
        for k in range(n):
            for j, chip in enumerate(chips):
                landed = pltpu.make_async_remote_copy(
                    src_ref=ins[k].at[_chip_index(chip)], dst_ref=outs[k].at[_chip_index(chip)],
                    send_sem=send.at[k, j], recv_sem=recv.at[k, j],
                    device_id=(chip[0], chip[1], c), device_id_type=_MESH)
                landed.wait_recv()
                landed.wait_send()
            started[k].wait()

    return pl.pallas_call(
        body, name=name, in_specs=[_ANY] * n, out_specs=[_ANY] * n,
        out_shape=[jax.ShapeDtypeStruct(p.shape, p.dtype) for p in ps],
        scratch_shapes=[pltpu.SemaphoreType.DMA((n, N_CHIPS - 1)), pltpu.SemaphoreType.DMA((n, N_CHIPS - 1)),
                        pltpu.SemaphoreType.DMA((n,))],
    )(*ps)


def allreduce_small(buf, name):
    rows = buf.shape[0]
    n_dev = 2 * N_CHIPS

    def body(in_ref, out_ref, slots, send, recv):
        x, y, c, me, chips = _place()
        my_id = 2 * me + c
        slots[my_id] = in_ref[...]
        for j in range(1, n_dev):
            fx, fy, fc = (j >> 2) & 1, (j >> 1) & 1, j & 1
            peer = ((1 - x) if fx else x, (1 - y) if fy else y, (1 - c) if fc else c)
            pltpu.make_async_remote_copy(
                src_ref=in_ref, dst_ref=slots.at[my_id], send_sem=send.at[j], recv_sem=recv.at[j],
                device_id=peer, device_id_type=_MESH).start()
        for j in range(1, n_dev):
            fx, fy, fc = (j >> 2) & 1, (j >> 1) & 1, j & 1
            peer = ((1 - x) if fx else x, (1 - y) if fy else y, (1 - c) if fc else c)
            peer_id = 2 * _chip_index(peer) + peer[2]
            landed = pltpu.make_async_remote_copy(
                src_ref=in_ref, dst_ref=slots.at[peer_id], send_sem=send.at[j], recv_sem=recv.at[j],
                device_id=peer, device_id_type=_MESH)
            landed.wait_recv()
            landed.wait_send()
        tot = slots[0]
        for d in range(1, n_dev):
            tot = tot + slots[d]
        out_ref[...] = tot

    return pl.pallas_call(
        body, name=name,
        in_specs=[pl.BlockSpec(memory_space=pltpu.VMEM)], out_specs=pl.BlockSpec(memory_space=pltpu.VMEM),
        out_shape=jax.ShapeDtypeStruct(buf.shape, F32),
        scratch_shapes=[pltpu.VMEM((n_dev, rows, LANES), F32), pltpu.SemaphoreType.DMA((n_dev,)),
                        pltpu.SemaphoreType.DMA((n_dev,))],
        compiler_params=pltpu.CompilerParams(vmem_limit_bytes=VMEM_LIMIT),
    )(buf)


def _sibling_step(src_ref, slots, send, recv, credit, step, n_steps):
    x, y, c = lax.axis_index("x"), lax.axis_index("y"), lax.axis_index("c")
    slot = step % 2

    @pl.when(step >= 2)
    def _():
        pl.semaphore_wait(credit, 1)

    cp = pltpu.make_async_remote_copy(src_ref=src_ref, dst_ref=slots.at[slot], send_sem=send.at[slot],
                                      recv_sem=recv.at[slot], device_id=(x, y, 1 - c), device_id_type=_MESH)
    cp.start()
    cp.wait_recv()
    return cp, slot


def _sibling_done(cp, credit, step, n_steps):
    x, y, c = lax.axis_index("x"), lax.axis_index("y"), lax.axis_index("c")
    cp.wait_send()

    @pl.when(step < n_steps - 2)
    def _():
        pl.semaphore_signal(credit, 1, device_id=(x, y, 1 - c), device_id_type=_MESH)


def pair_sum(g, name):
    s, r, cols = g.shape
    half = r // 2
    tr = _row_tile(half, cols, 1 << 19)
    nt = half // tr
    n_steps = s * nt

    def body(g_ref, o_ref, slots, send, recv, credit):
        c = lax.axis_index("c")
        step = pl.program_id(0) * nt + pl.program_id(1)
        cp, slot = _sibling_step(g_ref.at[0, 1 - c], slots, send, recv, credit, step, n_steps)
        o_ref[0] = (g_ref[0, c].astype(F32) + slots[slot].astype(F32)).astype(BF16)
        _sibling_done(cp, credit, step, n_steps)

    return pl.pallas_call(
        body, name=name, grid=(s, nt),
        in_specs=[pl.BlockSpec((1, 2, tr, cols), lambda k, i: (k, 0, i, 0))],
        out_specs=pl.BlockSpec((1, tr, cols), lambda k, i: (k, i, 0)),
        out_shape=jax.ShapeDtypeStruct((s, half, cols), BF16),
        scratch_shapes=[pltpu.VMEM((2, tr, cols), BF16), pltpu.SemaphoreType.DMA((2,)), pltpu.SemaphoreType.DMA((2,)),
                        pltpu.SemaphoreType.REGULAR],
        compiler_params=_params((_ARB, _ARB)),
    )(g.reshape(s, 2, half, cols))


def chip_sum_share(q, acc, layer, name):
    s, r2, cols = q.shape
    tr = _row_tile(r2, cols, 1 << 18)
    nt = r2 // tr

    def body(q_ref, acc_ref, o_ref, slots, send, recv, credit):
        c = lax.axis_index("c")
        step = pl.program_id(0)
        tot = q_ref[0].astype(F32)
        for k in range(1, s):
            tot = tot + q_ref[k].astype(F32)
        o_ref[0, c] = tot
        cp, slot = _sibling_step(o_ref.at[0, c], slots, send, recv, credit, step, nt)
        o_ref[0, 1 - c] = slots[slot]
        _sibling_done(cp, credit, step, nt)

    return pl.pallas_call(
        body, name=name, grid=(nt,),
        in_specs=[pl.BlockSpec((s, tr, cols), lambda i: (0, i, 0)), _ANY],
        out_specs=pl.BlockSpec((1, 2, tr, cols), lambda i: (layer, 0, i, 0)),
        out_shape=jax.ShapeDtypeStruct(acc.shape, F32),
        input_output_aliases={1: 0},
        scratch_shapes=[pltpu.VMEM((2, tr, cols), F32), pltpu.SemaphoreType.DMA((2,)), pltpu.SemaphoreType.DMA((2,)),
                        pltpu.SemaphoreType.REGULAR],
        compiler_params=_params((_ARB,)),
    )(q, acc)


def _row_tile(rows, cols, budget):
    for tr in (1024, 512, 256, 128, 64, 32, 16, 8):
        if rows % tr == 0 and tr * cols <= budget:
            return tr
    return rows


def adamw(w, g, m, v, name):
    rows, cols = w.shape
    tr = _row_tile(rows, cols, 1 << 18)
    c1 = 1.0 - ADAM_B1 ** ADAM_STEP
    c2 = 1.0 - ADAM_B2 ** ADAM_STEP

    def body(w_ref, g_ref, m_ref, v_ref, d_ref, nm_ref, nv_ref):
        gv = g_ref[...]
        nm = ADAM_B1 * m_ref[...] + (1.0 - ADAM_B1) * gv
        nv = ADAM_B2 * v_ref[...] + (1.0 - ADAM_B2) * (gv * gv)
        nm_ref[...] = nm
        nv_ref[...] = nv
        d_ref[...] = -ADAM_LR * ((nm / c1) / (jnp.sqrt(nv / c2) + ADAM_EPS) + ADAM_WD * w_ref[...])

    spec = pl.BlockSpec((tr, cols), lambda i: (i, 0))
    return pl.pallas_call(
        body, name=name, grid=(rows // tr,),
        in_specs=[spec] * 4, out_specs=[spec] * 3,
        out_shape=[jax.ShapeDtypeStruct((rows, cols), F32)] * 3,
        compiler_params=_params((_PAR,)),
    )(w, g, m, v)


WEIGHTS = ("mix_pre_g", "mix_post_g", "ffn_pre_g", "ffn_post_g", "hgrn_w_in", "hgrn_w_out", "hgrn_norm_g",
           "hgrn_lb_param", "swa_w_in", "swa_w_out", "swa_sinks", "sc_w_in", "sc_conv_w", "sc_w_out", "fox_w_in",
           "fox_b_f", "fox_w_out", "ffn_w_up", "ffn_conv_w", "ffn_conv_b", "ffn_w_down")
N_MIXERS = 4


def _pack_small(parts):
    flat = jnp.concatenate([p.reshape(-1).astype(F32) for p in parts])
    rows = -(-flat.shape[0] // (8 * LANES)) * 8
    return jnp.pad(flat, (0, rows * LANES - flat.shape[0])).reshape(rows, LANES)


def _unpack_small(buf, shapes):
    flat, out, off = buf.reshape(-1), [], 0
    for s in shapes:
        n = math.prod(s)
        out.append(flat[off:off + n].reshape(s))
        off += n
    return out


def _stack_rows(dw):
    return dw.reshape(N_CHIPS, dw.shape[0] // N_CHIPS, dw.shape[1])


def kernel(x, positions, mix_pre_g, mix_post_g, ffn_pre_g, ffn_post_g, hgrn_w_in, hgrn_w_out, hgrn_norm_g, hgrn_lb_param, swa_w_in, swa_w_out, swa_sinks, sc_w_in, sc_conv_w, sc_w_out, fox_w_in, fox_b_f, fox_w_out, ffn_w_up, ffn_conv_w, ffn_conv_b, ffn_w_down, loss_target, m_mix_pre_g, m_mix_post_g, m_ffn_pre_g, m_ffn_post_g, m_hgrn_w_in, m_hgrn_w_out, m_hgrn_norm_g, m_hgrn_lb_param, m_swa_w_in, m_swa_w_out, m_swa_sinks, m_sc_w_in, m_sc_conv_w, m_sc_w_out, m_fox_w_in, m_fox_b_f, m_fox_w_out, m_ffn_w_up, m_ffn_conv_w, m_ffn_conv_b, m_ffn_w_down, v_mix_pre_g, v_mix_post_g, v_ffn_pre_g, v_ffn_post_g, v_hgrn_w_in, v_hgrn_w_out, v_hgrn_norm_g, v_hgrn_lb_param, v_swa_w_in, v_swa_w_out, v_swa_sinks, v_sc_w_in, v_sc_conv_w, v_sc_w_out, v_fox_w_in, v_fox_b_f, v_fox_w_out, v_ffn_w_up, v_ffn_conv_w, v_ffn_conv_b, v_ffn_w_down):
    given = dict(locals())
    depth = mix_pre_g.shape[0]
    assert depth == N_MIXERS and x.shape[0] == 1, "one batch element per device, one layer of each mixer"
    xi, target = x[0], loss_target[0]
    chip = N_CHIPS // 2 * lax.axis_index("x") + lax.axis_index("y")
    nh_fox = fox_b_f.shape[1]
    row = lambda a, i: a[i:i + 1]

    bf = lambda a: a.astype(BF16)
    shards = [bf(hgrn_w_in[0]), bf(hgrn_w_out[0]), bf(swa_w_in[0]), bf(swa_w_out[0]), bf(sc_w_in[0]),
              bf(sc_w_out[0]), bf(fox_w_in), bf(fox_w_out[0]), bf(ffn_w_up), bf(ffn_w_down), sc_conv_w[0], ffn_conv_w]
    axes = [1, 0, 1, 0, 1, 0, 0, 0, 2, 1, 1, 2]
    (hg_in, hg_out, sw_in, sw_out, sc_in, sc_out, fx_in4, fx_out, w_up, w_down, sc_cw, f_cw) = gather_shards(
        shards, axes, "gather_weights")
    fx_in = fox_pad_w_in(jnp.concatenate([fx_in4[s] for s in range(N_CHIPS)], axis=1), nh_fox)

    saved = []
    xs = xi
    hn = rms_fwd(xs, row(mix_pre_g, 0), "pre_norm0")
    dx = loss = None
    for i in range(depth):
        nm = f"l{i}"
        if i == 0:
            y, sv = hgrn_mixer_fwd(hn, hg_in, hg_out, hgrn_norm_g, hgrn_lb_param, i, nm + "_hgrn")
        elif i == 1:
            y, sv = swa_mixer_fwd(hn, sw_in, sw_out, swa_sinks[0], positions, nm + "_swa")
        elif i == 2:
            proj = mm(hn, sc_in, "nn", BF16, nm + "_sc_in")
            yb = sconv_fwd(proj, sc_cw, nm + "_sc_conv")
            y, sv = mm(yb, sc_out, "nn", F32, nm + "_sc_out"), (proj, yb)
        else:
            y, sv = fox_mixer_fwd(hn, fx_in, fx_out, fox_b_f[0], nm + "_fox")
        x1, hn2 = resid_norm(xs, y, row(mix_post_g, i), row(ffn_pre_g, i), nm + "_mix_resid")
        z = mm(hn2, w_up, "nn", BF16, nm + "_ffn_up", b_idx=i)
        a = ffn_act(z, f_cw[i], row(ffn_conv_b, i), nm + "_ffn_act")
        y2 = mm(a, w_down, "nn", F32, nm + "_ffn_down", b_idx=i)
        saved.append((xs, hn, y, sv, x1, hn2, z, a, y2))
        if i < depth - 1:
            xs, hn = resid_norm(x1, y2, row(ffn_post_g, i), row(mix_pre_g, i + 1), nm + "_ffn_resid")
        else:
            dx, loss = resid_loss(x1, y2, row(ffn_post_g, i), target, nm + "_loss")

    big = {}
    d_pre, d_post, d_fpre, d_fpost = [None] * depth, [None] * depth, [None] * depth, [None] * depth
    d_up, d_down, d_fcw, d_fcb = [None] * depth, [None] * depth, [None] * depth, [None] * depth
    small = {}
    for i in reversed(range(depth)):
        nm = f"l{i}b"
        xs, hn, y, sv, x1, hn2, z, a, y2 = saved[i]
        dy2, d_fpost[i] = norm_bwd(y2, row(ffn_post_g, i), dx, None, BF16, nm + "_ffn_post")
        d_down[i] = _stack_rows(mm(a, dy2, "tn", BF16, nm + "_dw_down"))
        da = mm(dy2, w_down, "nt", BF16, nm + "_da", b_idx=i)
        du, acc = ffn_act_bwd(z, da, f_cw[i], row(ffn_conv_b, i), nm + "_ffn_actb")
        d_fcw[i], d_fcb[i] = acc[0:CONV_WIDTH], acc[CONV_WIDTH]
        dz = conv_transpose(du, f_cw[i], nm + "_ffn_convT")
        d_up[i] = mm(hn2, dz, "tn", BF16, nm + "_dw_up", stack=N_CHIPS)
        dhn2 = mm(dz, w_up, "nt", BF16, nm + "_dhn2", b_idx=i)
        dx1, d_fpre[i] = norm_bwd(x1, row(ffn_pre_g, i), dhn2, dx, F32, nm + "_ffn_pre")
        dy, d_post[i] = norm_bwd(y, row(mix_post_g, i), dx1, None, BF16, nm + "_mix_post")
        if i == 0:
            dhn, dwi, dwo, small["hgrn_norm_g"], small["hgrn_lb_param"] = hgrn_mixer_bwd(
                dy, hn, hg_in, hg_out, hgrn_norm_g, hgrn_lb_param, i, sv, nm + "_hgrn", ns=N_CHIPS)
            big["hgrn_w_in"], big["hgrn_w_out"] = [dwi], [_stack_rows(dwo)]
        elif i == 1:
            dhn, dwi, dwo, dsink = swa_mixer_bwd(dy, hn, sw_in, sw_out, swa_sinks[0], positions, sv, nm + "_swa",
                                                 ns=N_CHIPS)
            big["swa_w_in"], big["swa_w_out"], small["swa_sinks"] = [dwi], [_stack_rows(dwo)], dsink
        elif i == 2:
            proj, yb = sv
            dwo = mm(yb, dy, "tn", BF16, nm + "_sc_dwo")
            dyb = mm(dy, sc_out, "nt", BF16, nm + "_sc_dyb")
            dproj, acc = sconv_bwd(proj, dyb, sc_cw, nm + "_sc_convb")
            dwi = mm(hn, dproj, "tn", BF16, nm + "_sc_dwi", stack=N_CHIPS)
            dhn = mm(dproj, sc_in, "nt", BF16, nm + "_sc_dhn")
            big["sc_w_in"], big["sc_w_out"], small["sc_conv_w"] = [dwi], [_stack_rows(dwo)], acc[0:CONV_WIDTH]
        else:
            dhn, dwi, dwo, small["fox_b_f"] = fox_mixer_bwd(dy, hn, fx_in, fx_out, fox_b_f[0], sv, nm + "_fox")
            dwi = fox_unpad_dw(dwi, nh_fox)
            cols = dwi.shape[1] // N_CHIPS
            big["fox_w_in"] = [jnp.stack([dwi[:, s * cols:(s + 1) * cols] for s in range(N_CHIPS)])]
            big["fox_w_out"] = [_stack_rows(dwo)]
        dx, d_pre[i] = norm_bwd(xs, row(mix_pre_g, i), dhn, dx1, F32, nm + "_mix_pre")
    big["ffn_w_up"], big["ffn_w_down"] = d_up, d_down
    small.update(mix_pre_g=jnp.concatenate(d_pre), mix_post_g=jnp.concatenate(d_post),
                 ffn_pre_g=jnp.concatenate(d_fpre), ffn_post_g=jnp.concatenate(d_fpost),
                 ffn_conv_w=jnp.stack(d_fcw), ffn_conv_b=jnp.stack(d_fcb))

    big_names = [n for n in WEIGHTS if n in big]
    groups = [(n, l) for n in big_names for l in range(len(big[n]))]
    ps = [pair_sum(big[n][l], f"grads_pair_{n}{l}") for n, l in groups]
    qs = chip_exchange(ps, "grads_chips")
    grads = {}
    for (n, l), q in zip(groups, qs):
        if l == 0:
            grads[n] = lax.empty((len(big[n]), 2) + q.shape[1:], F32)
        grads[n] = chip_sum_share(q, grads[n], l, f"grads_share_{n}{l}")

    small_names = [n for n in WEIGHTS if n in small]
    full_shape = {n: tuple(given[n].shape) for n in small_names}
    full_shape["sc_conv_w"] = (1, CONV_WIDTH, sc_cw.shape[1])
    full_shape["ffn_conv_w"] = tuple(f_cw.shape)
    summed = _unpack_small(allreduce_small(_pack_small([small[n] for n in small_names] + [loss]), "small_sum"),
                           [full_shape[n] for n in small_names] + [()])
    loss = summed[-1]
    for n, g in zip(small_names, summed):
        if g.shape != given[n].shape:
            width = given[n].shape[-1]
            g = lax.dynamic_slice_in_dim(g, chip * width, width, axis=g.ndim - 1)
        grads[n] = g

    deltas, new_m, new_v = {}, {}, {}
    for n in WEIGHTS:
        w = given[n]
        flat = lambda a: a.reshape(-1, w.shape[-1])
        dl, nm_, nv_ = adamw(flat(w), flat(grads[n]), flat(given["m_" + n]), flat(given["v_" + n]), "adamw_" + n)
        deltas[n], new_m[n], new_v[n] = dl.reshape(w.shape), nm_.reshape(w.shape), nv_.reshape(w.shape)
    return (loss, dx[None], *[grads[n].reshape(given[n].shape) for n in WEIGHTS], *[deltas[n] for n in WEIGHTS],
            *[new_m[n] for n in WEIGHTS], *[new_v[n] for n in WEIGHTS])
```

```python
import functools
import math

import numpy as np
import jax
import jax.numpy as jnp
from jax import lax
from jax.experimental import pallas as pl
from jax.experimental.pallas import tpu as pltpu

F32 = jnp.float32
BF16 = jnp.bfloat16

RMS_EPS = 1e-6
HGRN_HEAD = 128
HGRN_CHUNK = 32
ATT_HEAD = 64
SWA_WINDOW = 128
SWA_GROUP = 8
ROT_DIM = 16
ROPE_THETA = 500000.0
CONV_WIDTH = 3
ADAM_LR = 0.001
ADAM_B1 = 0.9
ADAM_B2 = 0.999
ADAM_EPS = 1e-08
ADAM_WD = 0.01
ADAM_STEP = 10
N_CHIPS = 4
LANES = 128
BF16_ROWS = 16
VMEM_LIMIT = 48 * 1024 * 1024

_ARB = "arbitrary"
_PAR = "parallel"


def _params(sem, **kw):
    return pltpu.CompilerParams(dimension_semantics=sem, vmem_limit_bytes=VMEM_LIMIT, **kw)


def _tile(n, prefs):
    for p in prefs:
        if n % p == 0:
            return p
    return n


def _sigmoid(x):
    return 1.0 / (1.0 + jnp.exp(-x))


def _dot(a, b, dims):
    return lax.dot_general(a, b, (dims, ((), ())), preferred_element_type=F32)


def _dot_nn(a, b):
    return _dot(a, b, ((1,), (0,)))


def _dot_nt(a, b):
    return _dot(a, b, ((1,), (1,)))


def _dot_tn(a, b):
    return _dot(a, b, ((0,), (0,)))


MM_VMEM_BUDGET = 36 * 1024 * 1024
MM_HBM_RATE = 3.0e12
MM_MXU_RATE = 6.5e14
MM_STEP_S = 0.35e-6
MM_ACC_RATE = 3.0e12


def _mm_tiles(m, n, k, out_bytes):
    best = None
    for tm in (2048, 1024, 512, 256, 128):
        for tn in (2048, 1024, 512, 256, 128):
            for tk in sorted({k, 4096, 2816, 2048, 1408, 1024, 512, 256, 128}, reverse=True):
                if m % tm or n % tn or tk > k or k % tk:
                    continue
                nk = k // tk
                vmem = 4 * (tm * tk + tk * tn) + (4 * tm * tn if nk > 1 else 0) + 2 * tm * tn * out_bytes
                if vmem > MM_VMEM_BUDGET:
                    continue
                steps = (m // tm) * (n // tn) * nk
                traffic = 2 * m * k * (1 if nk == 1 else n // tn) + 2 * k * n * (m // tm) + m * n * out_bytes
                cost = max(traffic / MM_HBM_RATE, 2 * m * n * k / MM_MXU_RATE) + steps * MM_STEP_S
                if nk > 1:
                    cost += steps * 8 * tm * tn / MM_ACC_RATE
                if best is None or cost < best[0]:
                    best = (cost, tm, tn, tk)
    assert best is not None, (m, n, k)
    return best[1:]


def mm(a, b, mode, out_dtype, name="mm", b_idx=None):
    bshape = b.shape if b_idx is None else b.shape[1:]
    if mode == "nn":
        (m, k), (k2, n) = a.shape, bshape
    elif mode == "nt":
        (m, k), (n, k2) = a.shape, bshape
    else:
        (k, m), (k2, n) = a.shape, bshape
    assert k == k2, (a.shape, b.shape, mode)
    tm, tn, tk = _mm_tiles(m, n, k, jnp.dtype(out_dtype).itemsize)
    nk = k // tk

    def product(a_ref, b_ref):
        av = a_ref[...].astype(BF16)
        bv = b_ref[...].astype(BF16)
        return {"nn": _dot_nn, "nt": _dot_nt, "tn": _dot_tn}[mode](av, bv)

    def body_one(a_ref, b_ref, o_ref):
        o_ref[...] = product(a_ref, b_ref).astype(out_dtype)

    def body_acc(a_ref, b_ref, o_ref, acc_ref):
        kk = pl.program_id(2)

        @pl.when(kk == 0)
        def _():
            acc_ref[...] = jnp.zeros_like(acc_ref)

        acc_ref[...] += product(a_ref, b_ref)

        @pl.when(kk == nk - 1)
        def _():
            o_ref[...] = acc_ref[...].astype(out_dtype)

    def b_block(rows, cols, imap):
        if b_idx is None:
            return pl.BlockSpec((rows, cols), imap)
        return pl.BlockSpec((None, rows, cols), lambda i, j, kk: (b_idx,) + imap(i, j, kk))

    if mode == "nn":
        a_spec = pl.BlockSpec((tm, tk), lambda i, j, kk: (i, kk))
        b_spec = b_block(tk, tn, lambda i, j, kk: (kk, j))
    elif mode == "nt":
        a_spec = pl.BlockSpec((tm, tk), lambda i, j, kk: (i, kk))
        b_spec = b_block(tn, tk, lambda i, j, kk: (j, kk))
    else:
        a_spec = pl.BlockSpec((tk, tm), lambda i, j, kk: (kk, i))
        b_spec = b_block(tk, tn, lambda i, j, kk: (kk, j))
    return pl.pallas_call(
        body_one if nk == 1 else body_acc,
        name=name,
        grid=(m // tm, n // tn, nk),
        in_specs=[a_spec, b_spec],
        out_specs=pl.BlockSpec((tm, tn), lambda i, j, kk: (i, j)),
        out_shape=jax.ShapeDtypeStruct((m, n), out_dtype),
        scratch_shapes=[] if nk == 1 else [pltpu.VMEM((tm, tn), F32)],
        compiler_params=_params((_PAR, _PAR, _ARB)),
    )(a, b)


def wgrad(a, b, name):
    return mm(a.T, b, "nn", BF16, name)


def _rstd(xv):
    return lax.rsqrt(jnp.mean(xv * xv, axis=1, keepdims=True) + RMS_EPS)


def _row_spec(tr, w):
    return pl.BlockSpec((tr, w), lambda i: (i, 0))


def _full_spec(shape):
    nd = len(shape)
    return pl.BlockSpec(shape, lambda *_: (0,) * nd)


def rms_fwd(x, g, name):
    t, d = x.shape
    tr = _tile(t, (256, 128, 64, 32, 16))

    def body(x_ref, g_ref, o_ref):
        xv = x_ref[...]
        o_ref[...] = (xv * _rstd(xv) * g_ref[...]).astype(BF16)

    return pl.pallas_call(
        body, name=name, grid=(t // tr,),
        in_specs=[_row_spec(tr, d), _full_spec((1, d))],
        out_specs=_row_spec(tr, d),
        out_shape=jax.ShapeDtypeStruct((t, d), BF16),
        compiler_params=_params((_PAR,)),
    )(x, g)


def resid_norm(x, y, g_post, g_next, name):
    t, d = x.shape
    tr = _tile(t, (256, 128, 64, 32, 16))

    def body(x_ref, y_ref, gp_ref, gn_ref, x1_ref, hn_ref):
        yv = y_ref[...]
        x1 = x_ref[...] + yv * _rstd(yv) * gp_ref[...]
        x1_ref[...] = x1
        hn_ref[...] = (x1 * _rstd(x1) * gn_ref[...]).astype(BF16)

    return pl.pallas_call(
        body, name=name, grid=(t // tr,),
        in_specs=[_row_spec(tr, d), _row_spec(tr, d), _full_spec((1, d)), _full_spec((1, d))],
        out_specs=[_row_spec(tr, d), _row_spec(tr, d)],
        out_shape=[jax.ShapeDtypeStruct((t, d), F32), jax.ShapeDtypeStruct((t, d), BF16)],
        compiler_params=_params((_PAR,)),
    )(x, y, g_post, g_next)


def resid_loss(x, y, g_post, target, name):
    t, d = x.shape
    tr = _tile(t, (256, 128, 64, 32, 16))

    def body(x_ref, y_ref, gp_ref, t_ref, dx_ref, loss_ref):
        @pl.when(pl.program_id(0) == 0)
        def _():
            loss_ref[...] = jnp.zeros_like(loss_ref)

        yv = y_ref[...]
        err = x_ref[...] + yv * _rstd(yv) * gp_ref[...] - t_ref[...]
        dx_ref[...] = err * (1.0 / d)
        loss_ref[...] += 0.5 * jnp.sum(jnp.mean(err * err, axis=1, keepdims=True), axis=0, keepdims=True)

    dx, loss = pl.pallas_call(
        body, name=name, grid=(t // tr,),
        in_specs=[_row_spec(tr, d), _row_spec(tr, d), _full_spec((1, d)), _row_spec(tr, d)],
        out_specs=[_row_spec(tr, d), _full_spec((8, LANES))],
        out_shape=[jax.ShapeDtypeStruct((t, d), F32), jax.ShapeDtypeStruct((8, LANES), F32)],
        compiler_params=_params((_ARB,)),
    )(x, y, g_post, target)
    return dx, loss[0:1, 0:1]


def norm_bwd(yin, g, dout, res, out_dtype, name):
    t, d = yin.shape
    tr = _tile(t, (256, 128, 64, 32, 16))
    has_res = res is not None

    def body(*refs):
        if has_res:
            y_ref, g_ref, d_ref, r_ref, o_ref, dg_ref = refs
        else:
            y_ref, g_ref, d_ref, o_ref, dg_ref = refs

        @pl.when(pl.program_id(0) == 0)
        def _():
            dg_ref[...] = jnp.zeros_like(dg_ref)

        yv = y_ref[...]
        dv = d_ref[...].astype(F32)
        r = _rstd(yv)
        yh = yv * r
        dyh = dv * g_ref[...]
        dy = r * (dyh - yh * jnp.mean(dyh * yh, axis=1, keepdims=True))
        if has_res:
            dy = dy + r_ref[...]
        o_ref[...] = dy.astype(out_dtype)
        dg_ref[...] += jnp.sum(dv * yh, axis=0, keepdims=True)

    ins = [yin, g, dout] + ([res] if has_res else [])
    in_specs = [_row_spec(tr, d), _full_spec((1, d)), _row_spec(tr, d)] + ([_row_spec(tr, d)] if has_res else [])
    return pl.pallas_call(
        body, name=name, grid=(t // tr,),
        in_specs=in_specs,
        out_specs=[_row_spec(tr, d), _full_spec((1, d))],
        out_shape=[jax.ShapeDtypeStruct((t, d), out_dtype), jax.ShapeDtypeStruct((1, d), F32)],
        compiler_params=_params((_ARB,)),
    )(*ins)


def _shift_down(x, halo):
    tr = x.shape[0]
    row = lax.broadcasted_iota(jnp.int32, x.shape, 0)
    h1 = halo[BF16_ROWS - 1:BF16_ROWS, :]
    h2 = halo[BF16_ROWS - 2:BF16_ROWS - 1, :]
    x1 = jnp.where(row == 0, h1, pltpu.roll(x, 1, 0))
    x2 = jnp.where(row == 0, h2, jnp.where(row == 1, h1, pltpu.roll(x, 2, 0)))
    return x1, x2


def _shift_up(x, halo):
    tr = x.shape[0]
    row = lax.broadcasted_iota(jnp.int32, x.shape, 0)
    h0 = halo[0:1, :]
    h1 = halo[1:2, :]
    x1 = jnp.where(row == tr - 1, h0, pltpu.roll(x, tr - 1, 0))
    x2 = jnp.where(row == tr - 1, h1, jnp.where(row == tr - 2, h0, pltpu.roll(x, tr - 2, 0)))
    return x1, x2


def _prev_halo_spec(tr, w, nt):
    return pl.BlockSpec((BF16_ROWS, w), lambda i: (jnp.maximum(i * (tr // BF16_ROWS) - 1, 0), 0))


def _next_halo_spec(tr, w, nt):
    last = nt * (tr // BF16_ROWS) - 1
    return pl.BlockSpec((BF16_ROWS, w), lambda i: (jnp.minimum((i + 1) * (tr // BF16_ROWS), last), 0))


def _silu_and_grad(u):
    s = _sigmoid(u)
    return u * s, s * (1.0 + u * (1.0 - s))


def ffn_act(z, conv_w, conv_b, name):
    t, f2 = z.shape
    f = f2 // 2
    tr = _tile(t, (128, 64, 32, 16))
    nt = t // tr
    cw = _tile(f, (512, 256, 128))

    def body(z_ref, zp_ref, w_ref, b_ref, a_ref):
        first = pl.program_id(0) == 0
        for j in range(f // cw):
            us = []
            for off in (j * cw, f + j * cw):
                cols = slice(off, off + cw)
                zc = z_ref[:, cols].astype(F32)
                hp = jnp.where(first, 0.0, zp_ref[:, cols].astype(F32))
                z1, z2 = _shift_down(zc, hp)
                us.append(w_ref[2:3, cols] * zc + w_ref[1:2, cols] * z1 + w_ref[0:1, cols] * z2 + b_ref[:, cols])
            sil, _ = _silu_and_grad(us[0])
            a_ref[:, j * cw:(j + 1) * cw] = (sil * us[1]).astype(BF16)

    return pl.pallas_call(
        body, name=name, grid=(nt,),
        in_specs=[_row_spec(tr, f2), _prev_halo_spec(tr, f2, nt), _full_spec((CONV_WIDTH, f2)), _full_spec((1, f2))],
        out_specs=_row_spec(tr, f),
        out_shape=jax.ShapeDtypeStruct((t, f), BF16),
        compiler_params=_params((_PAR,)),
    )(z, z, conv_w, conv_b)


def ffn_act_bwd(z, da, conv_w, conv_b, name):
    t, f2 = z.shape
    f = f2 // 2
    tr = _tile(t, (128, 64, 32, 16))
    nt = t // tr
    cw = _tile(f, (512, 256, 128))

    def body(z_ref, zp_ref, da_ref, w_ref, b_ref, du_ref, acc_ref):
        first = pl.program_id(0) == 0

        @pl.when(first)
        def _():
            acc_ref[...] = jnp.zeros_like(acc_ref)

        for j in range(f // cw):
            us, zs = [], []
            for off in (j * cw, f + j * cw):
                cols = slice(off, off + cw)
                zc = z_ref[:, cols].astype(F32)
                hp = jnp.where(first, 0.0, zp_ref[:, cols].astype(F32))
                z1, z2 = _shift_down(zc, hp)
                zs.append((z2, z1, zc))
                us.append(w_ref[2:3, cols] * zc + w_ref[1:2, cols] * z1 + w_ref[0:1, cols] * z2 + b_ref[:, cols])
            dav = da_ref[:, j * cw:(j + 1) * cw].astype(F32)
            sil, dsil = _silu_and_grad(us[0])
            dus = (dav * us[1] * dsil, dav * sil)
            for off, du, zsh in zip((j * cw, f + j * cw), dus, zs):
                cols = slice(off, off + cw)
                du_ref[:, cols] = du.astype(BF16)
                for k in range(CONV_WIDTH):
                    acc_ref[k:k + 1, cols] += jnp.sum(du * zsh[k], axis=0, keepdims=True)
                acc_ref[3:4, cols] += jnp.sum(du, axis=0, keepdims=True)

    return pl.pallas_call(
        body, name=name, grid=(nt,),
        in_specs=[_row_spec(tr, f2), _prev_halo_spec(tr, f2, nt), _row_spec(tr, f),
                  _full_spec((CONV_WIDTH, f2)), _full_spec((1, f2))],
        out_specs=[_row_spec(tr, f2), _full_spec((8, f2))],
        out_shape=[jax.ShapeDtypeStruct((t, f2), BF16), jax.ShapeDtypeStruct((8, f2), F32)],
        compiler_params=_params((_ARB,)),
    )(z, z, da, conv_w, conv_b)


def conv_transpose(du, conv_w, name):
    t, w = du.shape
    tr = _tile(t, (128, 64, 32, 16))
    nt = t // tr
    cw = _tile(w, (512, 256, 128))

    def body(d_ref, dn_ref, w_ref, o_ref):
        last = pl.program_id(0) == nt - 1
        for j in range(w // cw):
            cols = slice(j * cw, (j + 1) * cw)
            dc = d_ref[:, cols].astype(F32)
            hn = jnp.where(last, 0.0, dn_ref[:, cols].astype(F32))
            d1, d2 = _shift_up(dc, hn)
            o_ref[:, cols] = (w_ref[2:3, cols] * dc + w_ref[1:2, cols] * d1 + w_ref[0:1, cols] * d2).astype(BF16)

    return pl.pallas_call(
        body, name=name, grid=(nt,),
        in_specs=[_row_spec(tr, w), _next_halo_spec(tr, w, nt), _full_spec((CONV_WIDTH, w))],
        out_specs=_row_spec(tr, w),
        out_shape=jax.ShapeDtypeStruct((t, w), BF16),
        compiler_params=_params((_PAR,)),
    )(du, du, conv_w)


def sconv_fwd(proj, conv_w, name):
    t, w3 = proj.shape
    d = w3 // 3
    tr = _tile(t, (128, 64, 32, 16))
    nt = t // tr
    cw = _tile(d, (512, 256, 128))

    def body(p_ref, pp_ref, w_ref, o_ref):
        first = pl.program_id(0) == 0
        for j in range(d // cw):
            cb, cc, cx = (slice(k * d + j * cw, k * d + (j + 1) * cw) for k in range(3))
            zc = p_ref[:, cc].astype(F32) * p_ref[:, cx].astype(F32)
            hp = jnp.where(first, 0.0, pp_ref[:, cc].astype(F32) * pp_ref[:, cx].astype(F32))
            z1, z2 = _shift_down(zc, hp)
            wc = slice(j * cw, (j + 1) * cw)
            cz = w_ref[2:3, wc] * zc + w_ref[1:2, wc] * z1 + w_ref[0:1, wc] * z2
            o_ref[:, wc] = (p_ref[:, cb].astype(F32) * cz).astype(BF16)

    return pl.pallas_call(
        body, name=name, grid=(nt,),
        in_specs=[_row_spec(tr, w3), _prev_halo_spec(tr, w3, nt), _full_spec((CONV_WIDTH, d))],
        out_specs=_row_spec(tr, d),
        out_shape=jax.ShapeDtypeStruct((t, d), BF16),
        compiler_params=_params((_PAR,)),
    )(proj, proj, conv_w)


def sconv_bwd(proj, dyb, conv_w, name):
    t, w3 = proj.shape
    d = w3 // 3
    tr = _tile(t, (128, 64, 32, 16))
    nt = t // tr
    cw = _tile(d, (512, 256, 128))

    def body(p_ref, pp_ref, pn_ref, dy_ref, dyn_ref, w_ref, o_ref, acc_ref):
        first = pl.program_id(0) == 0
        last = pl.program_id(0) == nt - 1

        @pl.when(first)
        def _():
            acc_ref[...] = jnp.zeros_like(acc_ref)

        for j in range(d // cw):
            cb, cc, cx = (slice(k * d + j * cw, k * d + (j + 1) * cw) for k in range(3))
            wc = slice(j * cw, (j + 1) * cw)
            bv, cv, xv = p_ref[:, cb].astype(F32), p_ref[:, cc].astype(F32), p_ref[:, cx].astype(F32)
            zc = cv * xv
            hp = jnp.where(first, 0.0, pp_ref[:, cc].astype(F32) * pp_ref[:, cx].astype(F32))
            z1, z2 = _shift_down(zc, hp)
            w0, w1, w2 = w_ref[0:1, wc], w_ref[1:2, wc], w_ref[2:3, wc]
            cz = w2 * zc + w1 * z1 + w0 * z2
            dyv = dy_ref[:, wc].astype(F32)
            dcz = dyv * bv
            hn = jnp.where(last, 0.0, dyn_ref[:, wc].astype(F32) * pn_ref[:, cb].astype(F32))
            n1, n2 = _shift_up(dcz, hn)
            dz = w2 * dcz + w1 * n1 + w0 * n2
            o_ref[:, cb] = (dyv * cz).astype(BF16)
            o_ref[:, cc] = (dz * xv).astype(BF16)
            o_ref[:, cx] = (dz * cv).astype(BF16)
            for k, zsh in enumerate((z2, z1, zc)):
                acc_ref[k:k + 1, wc] += jnp.sum(dcz * zsh, axis=0, keepdims=True)

    return pl.pallas_call(
        body, name=name, grid=(nt,),
        in_specs=[_row_spec(tr, w3), _prev_halo_spec(tr, w3, nt), _next_halo_spec(tr, w3, nt),
                  _row_spec(tr, d), _next_halo_spec(tr, d, nt), _full_spec((CONV_WIDTH, d))],
        out_specs=[_row_spec(tr, w3), _full_spec((8, d))],
        out_shape=[jax.ShapeDtypeStruct((t, w3), BF16), jax.ShapeDtypeStruct((8, d), F32)],
        compiler_params=_params((_ARB,)),
    )(proj, proj, proj, dyb, dyb, conv_w)


def rope_tables(positions):
    half = ROT_DIM // 2
    inv_freq = ROPE_THETA ** (-jnp.arange(half, dtype=F32) / half)
    ang = positions.astype(F32)[:, None] * inv_freq[None, :]
    cos, sin = jnp.cos(ang), jnp.sin(ang)
    ones = jnp.ones((positions.shape[0], ATT_HEAD - ROT_DIM), F32)
    c64 = jnp.concatenate([cos, cos, ones], axis=1)
    s64 = jnp.concatenate([-sin, sin, 0.0 * ones], axis=1)
    perm = np.zeros((LANES, LANES), np.float32)
    for lane in range(LANES):
        dim = lane % ATT_HEAD
        if dim < half:
            perm[lane + half, lane] = 1.0
        elif dim < ROT_DIM:
            perm[lane - half, lane] = 1.0
    return jnp.tile(c64, (1, 2)), jnp.tile(s64, (1, 2)), jnp.asarray(perm, BF16)


def rope(xin, ctab, stab, perm, n_rot, sign, name):
    t, w = xin.shape
    tr = _tile(t, (256, 128, 64, 32, 16))

    def body(x_ref, c_ref, s_ref, p_ref, o_ref):
        cv, sv = c_ref[...], s_ref[...] * sign
        for j in range(n_rot // LANES):
            cols = slice(j * LANES, (j + 1) * LANES)
            xb = x_ref[:, cols]
            o_ref[:, cols] = (xb.astype(F32) * cv + _dot_nn(xb, p_ref[...]) * sv).astype(BF16)
        if n_rot < w:
            o_ref[:, n_rot:] = x_ref[:, n_rot:]

    return pl.pallas_call(
        body, name=name, grid=(t // tr,),
        in_specs=[_row_spec(tr, w), _row_spec(tr, LANES), _row_spec(tr, LANES), _full_spec((LANES, LANES))],
        out_specs=_row_spec(tr, w),
        out_shape=jax.ShapeDtypeStruct((t, w), BF16),
        compiler_params=_params((_PAR,)),
    )(xin, ctab, stab, perm)


NEG = -1e30


def _half(shape, h):
    return (lax.broadcasted_iota(jnp.int32, shape, 1) // ATT_HEAD) == h


def _dup_head(xb, kvh):
    xf = jnp.where(_half(xb.shape, kvh), xb.astype(F32), 0.0)
    return (xf + pltpu.roll(xf, ATT_HEAD, 1)).astype(BF16)


def _swa_mask(n, rows, cur_only):
    w = SWA_WINDOW
    shape = (w, w) if cur_only else (w, 2 * w)
    qi = lax.broadcasted_iota(jnp.int32, shape, 0)
    kj = lax.broadcasted_iota(jnp.int32, shape, 1) + (w if cur_only else 0)
    diff = qi + w - kj
    ok = (diff >= 0) & (diff < w)
    return ok & ((kj >= w) | (n > 0))


def swa_fwd(qkv, sinks, hq, name):
    t = qkv.shape[0]
    w = SWA_WINDOW
    nb = t // w
    hkv = hq // SWA_GROUP
    npair = hkv // 2
    qw = 2 * SWA_GROUP * ATT_HEAD
    kcol = hq * ATT_HEAD // LANES
    vcol = kcol + npair
    scale = ATT_HEAD ** -0.5

    def body(sink_ref, q_ref, kp_ref, kc_ref, vp_ref, vc_ref, o_ref, lse_ref):
        m, n = pl.program_id(0), pl.program_id(1)
        kb = jnp.concatenate([kp_ref[...], kc_ref[...]], axis=0)
        vb = jnp.concatenate([vp_ref[...], vc_ref[...]], axis=0)
        ok = _swa_mask(n, w, False)
        for kvh in range(2):
            kd, vd = _dup_head(kb, kvh), _dup_head(vb, kvh)
            for jj in range(SWA_GROUP // 2):
                jp = kvh * (SWA_GROUP // 2) + jj
                q2 = q_ref[:, jp * LANES:(jp + 1) * LANES]
                outs = []
                for a in range(2):
                    qa = jnp.where(_half(q2.shape, a), q2, jnp.zeros_like(q2))
                    s = jnp.where(ok, _dot_nt(qa, kd) * scale, NEG)
                    sink = sink_ref[m * 2 * SWA_GROUP + jp * 2 + a]
                    mx = jnp.maximum(jnp.max(s, axis=1, keepdims=True), sink)
                    e = jnp.exp(s - mx)
                    den = jnp.sum(e, axis=1, keepdims=True) + jnp.exp(sink - mx)
                    p = (e / den).astype(BF16)
                    outs.append(_dot_nn(p, vd))
                    lse_ref[jp * 2 + a] = jnp.broadcast_to(mx + jnp.log(den), (w, LANES))
                o_ref[:, jp * LANES:(jp + 1) * LANES] = jnp.where(_half(outs[0].shape, 0), outs[0], outs[1]).astype(BF16)

    prev = lambda m, n: jnp.maximum(n - 1, 0)
    grid_spec = pltpu.PrefetchScalarGridSpec(
        num_scalar_prefetch=1, grid=(npair, nb),
        in_specs=[
            pl.BlockSpec((w, qw), lambda m, n, s: (n, m)),
            pl.BlockSpec((w, LANES), lambda m, n, s: (prev(m, n), kcol + m)),
            pl.BlockSpec((w, LANES), lambda m, n, s: (n, kcol + m)),
            pl.BlockSpec((w, LANES), lambda m, n, s: (prev(m, n), vcol + m)),
            pl.BlockSpec((w, LANES), lambda m, n, s: (n, vcol + m)),
        ],
        out_specs=[
            pl.BlockSpec((w, qw), lambda m, n, s: (n, m)),
            pl.BlockSpec((2 * SWA_GROUP, w, LANES), lambda m, n, s: (m, n, 0)),
        ],
    )
    return pl.pallas_call(
        body, name=name, grid_spec=grid_spec,
        out_shape=[jax.ShapeDtypeStruct((t, hq * ATT_HEAD), BF16), jax.ShapeDtypeStruct((hq, t, LANES), F32)],
        compiler_params=_params((_PAR, _PAR)),
    )(sinks, qkv, qkv, qkv, qkv, qkv)


def swa_bwd(qkv, o, lse, do, sinks, hq, name):
    t = qkv.shape[0]
    w = SWA_WINDOW
    nb = t // w
    hkv = hq // SWA_GROUP
    npair = hkv // 2
    qw = 2 * SWA_GROUP * ATT_HEAD
    kcol = hq * ATT_HEAD // LANES
    vcol = kcol + npair
    scale = ATT_HEAD ** -0.5
    gh = 2 * SWA_GROUP

    def body(sink_ref, qc_ref, qn_ref, kp_ref, kc_ref, vp_ref, vc_ref, oc_ref, on_ref, dc_ref, dn_ref,
             lc_ref, ln_ref, dq_ref, dk_ref, dv_ref, ds_ref):
        m, n = pl.program_id(0), pl.program_id(1)
        kb = jnp.concatenate([kp_ref[...], kc_ref[...]], axis=0)
        vb = jnp.concatenate([vp_ref[...], vc_ref[...]], axis=0)
        ok_band = _swa_mask(n, w, False)
        ok_cur = _swa_mask(n, w, True)
        qi = lax.broadcasted_iota(jnp.int32, (w, w), 0)
        kj = lax.broadcasted_iota(jnp.int32, (w, w), 1)
        ok_next = (kj > qi) & (n < nb - 1)
        row16 = lax.broadcasted_iota(jnp.int32, (gh, LANES), 0)
        dsink = jnp.zeros((gh, LANES), F32)
        dk_tot = jnp.zeros((w, LANES), F32)
        dv_tot = jnp.zeros((w, LANES), F32)
        for kvh in range(2):
            kd, vd = _dup_head(kb, kvh), _dup_head(vb, kvh)
            kdc, vdc = kd[w:, :], vd[w:, :]
            acc_k = [jnp.zeros((w, LANES), F32), jnp.zeros((w, LANES), F32)]
            acc_v = [jnp.zeros((w, LANES), F32), jnp.zeros((w, LANES), F32)]
            for jj in range(SWA_GROUP // 2):
                jp = kvh * (SWA_GROUP // 2) + jj
                cols = slice(jp * LANES, (jp + 1) * LANES)
                dqs = []
                for a in range(2):
                    hd = jp * 2 + a
                    sink = sink_ref[m * gh + hd]
                    half = _half((w, LANES), a)
                    q2 = jnp.where(half, qc_ref[:, cols], jnp.zeros((w, LANES), BF16))
                    d2 = jnp.where(half, dc_ref[:, cols], jnp.zeros((w, LANES), BF16))
                    delta = jnp.sum(d2.astype(F32) * oc_ref[:, cols].astype(F32), axis=1, keepdims=True)
                    lse_c = lc_ref[hd][:, 0:1]
                    p = jnp.exp(jnp.where(ok_band, _dot_nt(q2, kd) * scale, NEG) - lse_c)
                    dsv = p * (_dot_nt(d2, vd) - delta)
                    dqs.append(_dot_nn(dsv.astype(BF16), kd) * scale)
                    psink = jnp.exp(sink - lse_c)
                    dsink = jnp.where(row16 == hd, dsink - jnp.sum(psink * delta, axis=0, keepdims=True), dsink)
                    for q_ref, d_ref, o_ref, l_ref, okm in ((qc_ref, dc_ref, oc_ref, lc_ref, ok_cur),
                                                           (qn_ref, dn_ref, on_ref, ln_ref, ok_next)):
                        q2 = jnp.where(half, q_ref[:, cols], jnp.zeros((w, LANES), BF16))
                        d2 = jnp.where(half, d_ref[:, cols], jnp.zeros((w, LANES), BF16))
                        delta = jnp.sum(d2.astype(F32) * o_ref[:, cols].astype(F32), axis=1, keepdims=True)
                        p = jnp.exp(jnp.where(okm, _dot_nt(q2, kdc) * scale, NEG) - l_ref[hd][:, 0:1])
                        dsv = p * (_dot_nt(d2, vdc) - delta)
                        acc_v[a] = acc_v[a] + _dot_tn(p.astype(BF16), d2)
                        acc_k[a] = acc_k[a] + _dot_tn(dsv.astype(BF16), q2) * scale
                dq_ref[:, cols] = jnp.where(_half((w, LANES), 0), dqs[0], dqs[1]).astype(BF16)
            dk_tot = dk_tot + acc_k[kvh] + pltpu.roll(acc_k[1 - kvh], ATT_HEAD, 1)
            dv_tot = dv_tot + acc_v[kvh] + pltpu.roll(acc_v[1 - kvh], ATT_HEAD, 1)
        dk_ref[...] = dk_tot.astype(BF16)
        dv_ref[...] = dv_tot.astype(BF16)
        ds_ref[0, 0] = dsink

    prev = lambda n: jnp.maximum(n - 1, 0)
    nxt = lambda n: jnp.minimum(n + 1, nb - 1)
    qspec = lambda f: pl.BlockSpec((w, qw), lambda m, n, s: (f(n), m))
    lspec = lambda f: pl.BlockSpec((gh, w, LANES), lambda m, n, s: (m, f(n), 0))
    same = lambda n: n
    grid_spec = pltpu.PrefetchScalarGridSpec(
        num_scalar_prefetch=1, grid=(npair, nb),
        in_specs=[
            qspec(same), qspec(nxt),
            pl.BlockSpec((w, LANES), lambda m, n, s: (prev(n), kcol + m)),
            pl.BlockSpec((w, LANES), lambda m, n, s: (n, kcol + m)),
            pl.BlockSpec((w, LANES), lambda m, n, s: (prev(n), vcol + m)),
            pl.BlockSpec((w, LANES), lambda m, n, s: (n, vcol + m)),
            qspec(same), qspec(nxt), qspec(same), qspec(nxt),
            lspec(same), lspec(nxt),
        ],
        out_specs=[
            pl.BlockSpec((w, qw), lambda m, n, s: (n, m)),
            pl.BlockSpec((w, LANES), lambda m, n, s: (n, m)),
            pl.BlockSpec((w, LANES), lambda m, n, s: (n, m)),
            pl.BlockSpec((1, 1, gh, LANES), lambda m, n, s: (m, n, 0, 0)),
        ],
    )
    return pl.pallas_call(
        body, name=name, grid_spec=grid_spec,
        out_shape=[jax.ShapeDtypeStruct((t, hq * ATT_HEAD), BF16),
                   jax.ShapeDtypeStruct((t, hkv * ATT_HEAD), BF16),
                   jax.ShapeDtypeStruct((t, hkv * ATT_HEAD), BF16),
                   jax.ShapeDtypeStruct((npair, nb, gh, LANES), F32)],
        compiler_params=_params((_PAR, _PAR)),
    )(sinks, qkv, qkv, qkv, qkv, qkv, qkv, o, o, do, do, lse, lse)


def swa_mixer_fwd(hn, w_in, w_out, sinks, positions, name):
    hq = sinks.shape[0]
    n_rot = (hq + hq // SWA_GROUP) * ATT_HEAD
    tabs = rope_tables(positions)
    proj = mm(hn, w_in, "nn", BF16, name + "_in")
    qkv = rope(proj, *tabs, n_rot, 1.0, name + "_rope")
    o, lse = swa_fwd(qkv, sinks, hq, name + "_att")
    y = mm(o, w_out, "nn", F32, name + "_out")
    return y, (qkv, o, lse)


def swa_mixer_bwd(dy, hn, w_in, w_out, sinks, positions, saved, name):
    qkv, o, lse = saved
    hq = sinks.shape[0]
    n_rot = (hq + hq // SWA_GROUP) * ATT_HEAD
    tabs = rope_tables(positions)
    dwo = wgrad(o, dy, name + "_dwo")
    do = mm(dy, w_out, "nt", BF16, name + "_do")
    dq, dk, dv, dsp = swa_bwd(qkv, o, lse, do, sinks, hq, name + "_attb")
    dproj = rope(jnp.concatenate([dq, dk, dv], axis=1), *tabs, n_rot, -1.0, name + "_ropeb")
    dwi = wgrad(hn, dproj, name + "_dwi")
    dhn = mm(dproj, w_in, "nt", BF16, name + "_dhn")
    dsinks = jnp.sum(dsp[:, :, :, 0], axis=1).reshape(hq)
    return dhn, dwi, dwo, dsinks


FOX_FPAD = 512


def _log_sigmoid(x):
    return jnp.minimum(x, 0.0) - jnp.log(1.0 + jnp.exp(-jnp.abs(x)))


def _tri(n, upper):
    r = lax.broadcasted_iota(jnp.int32, (n, n), 0)
    c = lax.broadcasted_iota(jnp.int32, (n, n), 1)
    return jnp.where((c >= r) if upper else (c <= r), 1.0, 0.0).astype(F32)


def _dot_exact(a, b):
    return jnp.dot(a, b, precision=lax.Precision.HIGHEST, preferred_element_type=F32)


def fox_cumsum(fl, b_pad, name):
    t = fl.shape[0]
    tr = _tile(t, (256, 128, 64, 32, 16, 8))

    def body(f_ref, b_ref, c_ref, carry_ref):
        @pl.when(pl.program_id(0) == 0)
        def _():
            carry_ref[...] = jnp.zeros_like(carry_ref)

        c = _dot_exact(_tri(tr, False), _log_sigmoid(f_ref[...] + b_ref[...])) + carry_ref[...]
        c_ref[...] = c
        carry_ref[...] = c[tr - 1:tr, :]

    return pl.pallas_call(
        body, name=name, grid=(t // tr,),
        in_specs=[_row_spec(tr, LANES), _full_spec((1, LANES))],
        out_specs=_row_spec(tr, LANES),
        out_shape=jax.ShapeDtypeStruct((t, LANES), F32),
        scratch_shapes=[pltpu.VMEM((1, LANES), F32)],
        compiler_params=_params((_ARB,)),
    )(fl, b_pad)


def fox_cumsum_bwd(dc, fl, b_pad, name):
    t = fl.shape[0]
    tr = _tile(t, (256, 128, 64, 32, 16, 8))
    nt = t // tr

    def body(d_ref, f_ref, b_ref, o_ref, db_ref, carry_ref):
        @pl.when(pl.program_id(0) == 0)
        def _():
            carry_ref[...] = jnp.zeros_like(carry_ref)
            db_ref[...] = jnp.zeros_like(db_ref)

        dlf = _dot_exact(_tri(tr, True), d_ref[...]) + carry_ref[...]
        carry_ref[...] = dlf[0:1, :]
        dfl = dlf * _sigmoid(-(f_ref[...] + b_ref[...]))
        o_ref[...] = dfl.astype(BF16)
        db_ref[...] += jnp.sum(dfl, axis=0, keepdims=True)

    rev = pl.BlockSpec((tr, LANES), lambda i: (nt - 1 - i, 0))
    return pl.pallas_call(
        body, name=name, grid=(nt,),
        in_specs=[rev, rev, _full_spec((1, LANES))],
        out_specs=[rev, _full_spec((1, LANES))],
        out_shape=[jax.ShapeDtypeStruct((t, LANES), BF16), jax.ShapeDtypeStruct((1, LANES), F32)],
        scratch_shapes=[pltpu.VMEM((1, LANES), F32)],
        compiler_params=_params((_ARB,)),
    )(dc, fl, b_pad)


def _fox_tile(t):
    return _tile(t, (256, 128))


AUG_C, AUG_ONE, AUG_LSE = ATT_HEAD, ATT_HEAD + 3, ATT_HEAD + 6


def _split3(x):
    hi = x.astype(BF16).astype(F32)
    mid = (x - hi).astype(BF16).astype(F32)
    return hi, mid, (x - hi - mid).astype(BF16).astype(F32)


def _aug(base, lane, entries):
    out = jnp.where(lane < ATT_HEAD, base, 0.0)
    for first, parts in entries:
        if parts is None:
            out = jnp.where((lane >= first) & (lane < first + 3), 1.0, out)
        else:
            for k, part in enumerate(parts):
                out = jnp.where(lane == first + k, part, out)
    return out.astype(BF16)


def _head_of_pair(x2, a):
    xf = x2.astype(F32)
    return xf if a == 0 else pltpu.roll(xf, ATT_HEAD, 1)


def fa_prep(proj, c, nh, name):
    t = proj.shape[0]
    npair = nh // 2
    tr = _tile(t, (256, 128))
    scale = ATT_HEAD ** -0.5

    def body(q_ref, k_ref, v_ref, c_ref, qa_ref, ka_ref, va_ref):
        lane = lax.broadcasted_iota(jnp.int32, (tr, LANES), 1)
        for p in range(npair):
            pc = slice(p * LANES, (p + 1) * LANES)
            for a in range(2):
                h = 2 * p + a
                hc = slice(h * LANES, (h + 1) * LANES)
                ch = c_ref[:, h:h + 1]
                qa_ref[:, hc] = _aug(_head_of_pair(q_ref[:, pc], a) * scale, lane,
                                     [(AUG_C, _split3(ch)), (AUG_ONE, None)])
                ka_ref[:, hc] = _aug(_head_of_pair(k_ref[:, pc], a), lane,
                                     [(AUG_C, None), (AUG_ONE, _split3(-ch)), (AUG_LSE, None)])
                va_ref[:, hc] = _aug(_head_of_pair(v_ref[:, pc], a), lane, [(AUG_C, None)])

    hd = nh * ATT_HEAD
    part = lambda k: pl.BlockSpec((tr, hd), lambda i: (i, k))
    out = pl.BlockSpec((tr, nh * LANES), lambda i: (i, 0))
    return pl.pallas_call(
        body, name=name, grid=(t // tr,),
        in_specs=[part(0), part(1), part(2), _row_spec(tr, LANES)],
        out_specs=[out, out, out],
        out_shape=[jax.ShapeDtypeStruct((t, nh * LANES), BF16)] * 3,
        compiler_params=_params((_PAR,)),
    )(proj, proj, proj, c)


def _diag_mask(n, transposed=False):
    r = lax.broadcasted_iota(jnp.int32, (n, n), 0)
    c = lax.broadcasted_iota(jnp.int32, (n, n), 1)
    return (r <= c) if transposed else (c <= r)


def fa_fwd(qa, ka, va, proj, nh, name):
    t = qa.shape[0]
    hd = nh * ATT_HEAD
    npair = nh // 2
    tq = _fox_tile(t)
    nt = t // tq
    gcol = (3 * hd + FOX_FPAD) // LANES

    def body(q_ref, k_ref, v_ref, g_ref, o_ref, og_ref, lse_ref):
        i = pl.program_id(1)
        heads = [slice(a * LANES, (a + 1) * LANES) for a in range(2)]
        qs = [q_ref[:, cols] for cols in heads]

        def tile(j, carry, masked):
            rows = pl.ds(pl.multiple_of(j * tq, tq), tq)
            out = []
            for (mx, acc), q, cols in zip(carry, qs, heads):
                s = _dot_nt(q, k_ref[rows, cols])
                if masked:
                    s = jnp.where(_diag_mask(tq), s, NEG)
                mnew = jnp.maximum(mx, jnp.max(s, axis=1, keepdims=True))
                p = jnp.exp(s - mnew).astype(BF16)
                out.append((mnew, jnp.exp(mx - mnew) * acc + _dot_nn(p, v_ref[rows, cols])))
            return tuple(out)

        init = ((jnp.full((tq, 1), NEG, F32), jnp.zeros((tq, LANES), F32)),) * 2
        carry = lax.fori_loop(0, i, functools.partial(tile, masked=False), init)
        outs = []
        for a, (mx, acc) in enumerate(tile(i, carry, True)):
            l = acc[:, AUG_C:AUG_C + 1]
            outs.append(acc / l)
            lse_ref[a] = jnp.broadcast_to(mx + jnp.log(l), (tq, LANES))
        o = jnp.where(_half((tq, LANES), 0), outs[0], pltpu.roll(outs[1], ATT_HEAD, 1))
        o_ref[...] = o.astype(BF16)
        og_ref[...] = (o * _sigmoid(g_ref[...].astype(F32))).astype(BF16)

    pair = pl.BlockSpec((tq, LANES), lambda p, i: (i, p))
    return pl.pallas_call(
        body, name=name, grid=(npair, nt),
        in_specs=[pl.BlockSpec((tq, 2 * LANES), lambda p, i: (i, p)),
                  pl.BlockSpec((t, 2 * LANES), lambda p, i: (0, p)),
                  pl.BlockSpec((t, 2 * LANES), lambda p, i: (0, p)),
                  pl.BlockSpec((tq, LANES), lambda p, i: (i, gcol + p))],
        out_specs=[pair, pair, pl.BlockSpec((2, tq, LANES), lambda p, i: (p, i, 0))],
        out_shape=[jax.ShapeDtypeStruct((t, hd), BF16), jax.ShapeDtypeStruct((t, hd), BF16),
                   jax.ShapeDtypeStruct((nh, t, LANES), F32)],
        compiler_params=_params((_PAR, _PAR)),
    )(qa, ka, va, proj)


def fa_prep_bwd(dog, o, proj, qa, lse, nh, name):
    t, hd = o.shape
    npair = nh // 2
    tr = _tile(t, (256, 128))
    gcol = (3 * hd + FOX_FPAD) // LANES

    def body(d_ref, o_ref, g_ref, q_ref, l_ref, dg_ref, qb_ref, da_ref):
        lane = lax.broadcasted_iota(jnp.int32, (tr, LANES), 1)
        dv, ov = d_ref[...].astype(F32), o_ref[...].astype(F32)
        sg = _sigmoid(g_ref[...].astype(F32))
        do = (dv * sg).astype(BF16).astype(F32)
        dg_ref[...] = (dv * ov * sg * (1.0 - sg)).astype(BF16)
        prod = do * ov
        for a in range(2):
            cols = slice(a * LANES, (a + 1) * LANES)
            delta = jnp.sum(jnp.where(_half(prod.shape, a), prod, 0.0), axis=1, keepdims=True)
            da_ref[:, cols] = _aug(_head_of_pair(do, a), lane, [(AUG_C, _split3(-delta))])
            nl = _split3(-l_ref[a][:, 0:1])
            qb = q_ref[:, cols].astype(F32)
            for k in range(3):
                qb = jnp.where(lane == AUG_LSE + k, nl[k], qb)
            qb_ref[:, cols] = qb.astype(BF16)

    pair = pl.BlockSpec((tr, LANES), lambda p, i: (i, p))
    wide = pl.BlockSpec((tr, 2 * LANES), lambda p, i: (i, p))
    return pl.pallas_call(
        body, name=name, grid=(npair, t // tr),
        in_specs=[pair, pair, pl.BlockSpec((tr, LANES), lambda p, i: (i, gcol + p)), wide,
                  pl.BlockSpec((2, tr, LANES), lambda p, i: (p, i, 0))],
        out_specs=[pair, wide, wide],
        out_shape=[jax.ShapeDtypeStruct((t, hd), BF16), jax.ShapeDtypeStruct((t, nh * LANES), BF16),
                   jax.ShapeDtypeStruct((t, nh * LANES), BF16)],
        compiler_params=_params((_PAR, _PAR)),
    )(dog, o, proj, qa, lse)


def fa_dq(qb, ka, va, da, nh, name):
    t = qb.shape[0]
    hd = nh * ATT_HEAD
    npair = nh // 2
    tq = _fox_tile(t)
    nt = t // tq
    scale = ATT_HEAD ** -0.5

    def body(q_ref, k_ref, v_ref, d_ref, dq_ref, rs_ref):
        i = pl.program_id(1)
        heads = [slice(a * LANES, (a + 1) * LANES) for a in range(2)]
        qs = [q_ref[:, cols] for cols in heads]
        ds = [d_ref[:, cols] for cols in heads]

        def tile(j, carry, masked):
            rows = pl.ds(pl.multiple_of(j * tq, tq), tq)
            out = []
            for acc, q, d, cols in zip(carry, qs, ds, heads):
                kj = k_ref[rows, cols]
                s = _dot_nt(q, kj)
                if masked:
                    s = jnp.where(_diag_mask(tq), s, NEG)
                dsv = jnp.exp(s) * _dot_nt(d, v_ref[rows, cols])
                out.append(acc + _dot_nn(dsv.astype(BF16), kj))
            return tuple(out)

        init = (jnp.zeros((tq, LANES), F32),) * 2
        accs = tile(i, lax.fori_loop(0, i, functools.partial(tile, masked=False), init), True)
        dq_ref[...] = (jnp.where(_half((tq, LANES), 0), accs[0], pltpu.roll(accs[1], ATT_HEAD, 1)) * scale).astype(BF16)
        lane = lax.broadcasted_iota(jnp.int32, (tq, LANES), 1)
        rs_ref[...] = jnp.where(lane == 0, accs[0][:, AUG_C:AUG_C + 1],
                                jnp.where(lane == 1, accs[1][:, AUG_C:AUG_C + 1], 0.0))

    wide = pl.BlockSpec((tq, 2 * LANES), lambda p, i: (i, p))
    resident = pl.BlockSpec((t, 2 * LANES), lambda p, i: (0, p))
    pair = pl.BlockSpec((tq, LANES), lambda p, i: (i, p))
    return pl.pallas_call(
        body, name=name, grid=(npair, nt),
        in_specs=[wide, resident, resident, wide],
        out_specs=[pair, pair],
        out_shape=[jax.ShapeDtypeStruct((t, hd), BF16), jax.ShapeDtypeStruct((t, npair * LANES), F32)],
        compiler_params=_params((_PAR, _PAR)),
    )(qb, ka, va, da)


def fa_dkv(qb, ka, va, da, nh, name):
    t = qb.shape[0]
    hd = nh * ATT_HEAD
    npair = nh // 2
    tk = _fox_tile(t)
    nt = t // tk

    def body(q_ref, k_ref, v_ref, d_ref, dk_ref, dv_ref, cs_ref):
        j = pl.program_id(1)
        heads = [slice(a * LANES, (a + 1) * LANES) for a in range(2)]
        ks = [k_ref[:, cols] for cols in heads]
        vs = [v_ref[:, cols] for cols in heads]

        def tile(i, carry, masked):
            rows = pl.ds(pl.multiple_of(i * tk, tk), tk)
            out = []
            for (dk, dv), k, v, cols in zip(carry, ks, vs, heads):
                qi, di = q_ref[rows, cols], d_ref[rows, cols]
                st = _dot_nt(k, qi)
                if masked:
                    st = jnp.where(_diag_mask(tk, True), st, NEG)
                pt = jnp.exp(st)
                dst = pt * _dot_nt(v, di)
                out.append((dk + _dot_nn(dst.astype(BF16), qi), dv + _dot_nn(pt.astype(BF16), di)))
            return tuple(out)

        zero = jnp.zeros((tk, LANES), F32)
        carry = tile(j, ((zero, zero),) * 2, True)
        carry = lax.fori_loop(j + 1, nt, functools.partial(tile, masked=False), carry)
        dks, dvs = [c[0] for c in carry], [c[1] for c in carry]
        first = _half((tk, LANES), 0)
        dk_ref[...] = jnp.where(first, dks[0], pltpu.roll(dks[1], ATT_HEAD, 1)).astype(BF16)
        dv_ref[...] = jnp.where(first, dvs[0], pltpu.roll(dvs[1], ATT_HEAD, 1)).astype(BF16)
        lane = lax.broadcasted_iota(jnp.int32, (tk, LANES), 1)
        cs_ref[...] = jnp.where(lane == 0, dks[0][:, AUG_ONE:AUG_ONE + 1],
                                jnp.where(lane == 1, dks[1][:, AUG_ONE:AUG_ONE + 1], 0.0))

    wide = pl.BlockSpec((tk, 2 * LANES), lambda p, j: (j, p))
    resident = pl.BlockSpec((t, 2 * LANES), lambda p, j: (0, p))
    pair = pl.BlockSpec((tk, LANES), lambda p, j: (j, p))
    return pl.pallas_call(
        body, name=name, grid=(npair, nt),
        in_specs=[resident, wide, wide, resident],
        out_specs=[pair, pair, pair],
        out_shape=[jax.ShapeDtypeStruct((t, hd), BF16), jax.ShapeDtypeStruct((t, hd), BF16),
                   jax.ShapeDtypeStruct((t, npair * LANES), F32)],
        compiler_params=_params((_PAR, _PAR)),
    )(qb, ka, va, da)


def fox_pad_w_in(w_in, nh):
    hd = nh * ATT_HEAD
    pad = jnp.zeros((w_in.shape[0], FOX_FPAD - nh), w_in.dtype)
    return jnp.concatenate([w_in[:, :3 * hd + nh], pad, w_in[:, 3 * hd + nh:]], axis=1)


def fox_unpad_dw(dw, nh):
    hd = nh * ATT_HEAD
    return jnp.concatenate([dw[:, :3 * hd + nh], dw[:, 3 * hd + FOX_FPAD:]], axis=1)


def _pad_lanes(v):
    return jnp.pad(v.reshape(1, -1).astype(F32), ((0, 0), (0, LANES - v.size)))


def fox_mixer_fwd(hn, w_pad, w_out, b_f, name):
    nh = b_f.shape[0]
    hd = nh * ATT_HEAD
    proj = mm(hn, w_pad, "nn", BF16, name + "_in")
    fl = mm(hn, w_pad[:, 3 * hd:3 * hd + LANES], "nn", F32, name + "_fl")
    c = fox_cumsum(fl, _pad_lanes(b_f), name + "_cum")
    qa, ka, va = fa_prep(proj, c, nh, name + "_prep")
    o, og, lse = fa_fwd(qa, ka, va, proj, nh, name + "_att")
    y = mm(og, w_out, "nn", F32, name + "_out")
    return y, (proj, fl, qa, ka, va, o, og, lse)


def fox_mixer_bwd(dy, hn, w_pad, w_out, b_f, saved, name):
    proj, fl, qa, ka, va, o, og, lse = saved
    nh = b_f.shape[0]
    t = hn.shape[0]
    dwo = wgrad(og, dy, name + "_dwo")
    dog = mm(dy, w_out, "nt", BF16, name + "_dog")
    dg, qb, da = fa_prep_bwd(dog, o, proj, qa, lse, nh, name + "_prepb")
    dq, rsum = fa_dq(qb, ka, va, da, nh, name + "_dq")
    dk, dv, csum = fa_dkv(qb, ka, va, da, nh, name + "_dkv")
    dc = (rsum - csum).reshape(t, nh // 2, LANES)[:, :, :2].reshape(t, nh)
    dc = jnp.pad(dc, ((0, 0), (0, LANES - nh)))
    dfl, db = fox_cumsum_bwd(dc, fl, _pad_lanes(b_f), name + "_cumb")
    dfl = jnp.pad(dfl, ((0, 0), (0, FOX_FPAD - LANES)))
    dproj = jnp.concatenate([dq, dk, dv, dfl, dg], axis=1)
    dwi = wgrad(hn, dproj, name + "_dwi")
    dhn = mm(dproj, w_pad, "nt", BF16, name + "_dhn")
    return dhn, dwi, dwo, db[0, :nh]


HGRN_ROWS = 256


def lb_table_row(lb_param, idx, name):
    nrow, w = lb_param.shape

    def body(p_ref, o_ref):
        rows = [p_ref[r:r + 1, :] for r in range(nrow)]
        mx = functools.reduce(jnp.maximum, rows)
        es = [jnp.exp(r - mx) for r in rows]
        o_ref[...] = sum(es[:idx + 1]) / sum(es)

    return pl.pallas_call(
        body, name=name, in_specs=[_full_spec((nrow, w))], out_specs=_full_spec((1, w)), grid=(1,),
        out_shape=jax.ShapeDtypeStruct((1, w), F32),
    )(lb_param)


def lb_table_row_bwd(lb_param, dlb, idx, name):
    nrow, w = lb_param.shape

    def body(p_ref, d_ref, o_ref):
        rows = [p_ref[r:r + 1, :] for r in range(nrow)]
        mx = functools.reduce(jnp.maximum, rows)
        es = [jnp.exp(r - mx) for r in rows]
        tot = sum(es)
        ps = [e / tot for e in es]
        dv = d_ref[...]
        inner = sum(ps[:idx + 1]) * dv
        for r in range(nrow):
            o_ref[r:r + 1, :] = ps[r] * ((dv if r <= idx else 0.0) - inner)

    return pl.pallas_call(
        body, name=name, in_specs=[_full_spec((nrow, w)), _full_spec((1, w))], out_specs=_full_spec((nrow, w)),
        grid=(1,), out_shape=jax.ShapeDtypeStruct((nrow, w), F32),
    )(lb_param, dlb)


def _hgrn_gates(qraw, fraw, lb):
    sq = _sigmoid(qraw)
    sf = _sigmoid(fraw)
    f = lb + (1.0 - lb) * sf
    return qraw * sq, sq, sf, f, 1.0 - f


def _hgrn_chunk(q, k, f):
    c = HGRN_CHUNK
    b = _dot_exact(_tri(c, False), jnp.log(f))
    bl = b[c - 1:c, :]
    bm = b[c // 2 - 1:c // 2, :]
    eq, ek = jnp.exp(b - bm), jnp.exp(bm - b)
    eb, el = jnp.exp(b), jnp.exp(bl - b)
    qt, kt = (q * eq).astype(BF16), (k * ek).astype(BF16)
    causal = _tri(c, False) > 0.5
    amat = jnp.where(causal, _dot_nt(qt, kt), 0.0).astype(BF16)
    return amat, qt, kt, (q * eb).astype(BF16), (k * el).astype(BF16), eq, ek, eb, el, jnp.exp(bl), causal


def hgrn_fwd(proj, lb, norm_g, nh, name):
    t = proj.shape[0]
    w = nh * HGRN_HEAD
    c = HGRN_CHUNK
    rows = _tile(t, (HGRN_ROWS, 128, 64, 32))
    nr, nc = t // rows, rows // c

    def body(q_ref, f_ref, i_ref, g_ref, lb_ref, ng_ref, og_ref, o_ref, st_ref, state):
        @pl.when(pl.program_id(1) == 0)
        def _():
            state[...] = jnp.zeros_like(state)

        def step(cc, carry):
            sl = pl.ds(pl.multiple_of(cc * c, c), c)
            q, _, _, f, k = _hgrn_gates(q_ref[sl, :].astype(F32), f_ref[sl, :].astype(F32), lb_ref[...])
            v = i_ref[sl, :]
            amat, _, _, qd, kd, _, _, _, _, ebl, _ = _hgrn_chunk(q, k, f)
            st = state[...]
            st_ref[0, cc] = st.astype(BF16)
            o = _dot_nt(qd, st.astype(BF16)) + _dot_nn(amat, v)
            state[...] = st * ebl + _dot_tn(v, kd)
            o_ref[sl, :] = o
            graw = g_ref[sl, :].astype(F32)
            og_ref[sl, :] = (o * _rstd(o) * ng_ref[...] * (graw * _sigmoid(graw))).astype(BF16)
            return carry

        lax.fori_loop(0, nc, step, 0)

    col = lambda off: pl.BlockSpec((rows, HGRN_HEAD), lambda h, r: (r, off + h))
    return pl.pallas_call(
        body, name=name, grid=(nh, nr),
        in_specs=[col(0), col(nh), col(2 * nh), col(3 * nh),
                  pl.BlockSpec((1, HGRN_HEAD), lambda h, r: (0, h)), _full_spec((1, HGRN_HEAD))],
        out_specs=[col(0), col(0), pl.BlockSpec((1, nc, HGRN_HEAD, HGRN_HEAD), lambda h, r: (h, r, 0, 0))],
        out_shape=[jax.ShapeDtypeStruct((t, w), BF16), jax.ShapeDtypeStruct((t, w), F32),
                   jax.ShapeDtypeStruct((nh, t // c, HGRN_HEAD, HGRN_HEAD), BF16)],
        scratch_shapes=[pltpu.VMEM((HGRN_HEAD, HGRN_HEAD), F32)],
        compiler_params=_params((_PAR, _ARB)),
    )(proj, proj, proj, proj, lb, norm_g)


def hgrn_bwd(proj, lb, norm_g, o, states, dog, nh, name):
    t = proj.shape[0]
    w = nh * HGRN_HEAD
    c = HGRN_CHUNK
    rows = _tile(t, (HGRN_ROWS, 128, 64, 32))
    nr, nc = t // rows, rows // c

    def body(q_ref, f_ref, i_ref, g_ref, lb_ref, ng_ref, o_ref, st_ref, dog_ref,
             dq_ref, df_ref, di_ref, dg_ref, dlb_ref, dng_ref, dstate):
        @pl.when(pl.program_id(1) == 0)
        def _():
            dstate[...] = jnp.zeros_like(dstate)
            dlb_ref[...] = jnp.zeros_like(dlb_ref)
            dng_ref[...] = jnp.zeros_like(dng_ref)

        lb = lb_ref[...]
        ng = ng_ref[...]

        def step(idx, carry):
            cc = nc - 1 - idx
            sl = pl.ds(pl.multiple_of(cc * c, c), c)
            qraw, fraw = q_ref[sl, :].astype(F32), f_ref[sl, :].astype(F32)
            q, sq, sf, f, k = _hgrn_gates(qraw, fraw, lb)
            v = i_ref[sl, :]
            amat, qt, kt, qd, kd, eq, ek, eb, el, ebl, causal = _hgrn_chunk(q, k, f)
            ov = o_ref[sl, :]
            graw = g_ref[sl, :].astype(F32)
            dogv = dog_ref[sl, :].astype(F32)
            sil, dsil = _silu_and_grad(graw)
            r = _rstd(ov)
            oh = ov * r
            don = dogv * sil
            dg_ref[sl, :] = (dogv * oh * ng * dsil).astype(BF16)
            dng_ref[0] += jnp.sum(don * oh, axis=0, keepdims=True)
            doh = don * ng
            do = (r * (doh - oh * jnp.mean(doh * oh, axis=1, keepdims=True))).astype(BF16)
            dst = dstate[...]
            dstb = dst.astype(BF16)
            da = jnp.where(causal, _dot_nt(do, v), 0.0).astype(BF16)
            dv = _dot_tn(amat, do) + _dot_nt(kd, dstb)
            st0 = st_ref[0, cc]
            dq = _dot_nn(da, kt) * eq + _dot_nn(do, st0) * eb
            dk_inter = _dot_nn(v, dstb) * el
            dk = _dot_tn(da, qt) * ek + dk_inter
            dstate[...] = dst * ebl + _dot_tn(do, qd)
            through = jnp.sum(dst * st0.astype(F32), axis=0, keepdims=True) * ebl
            later = jnp.sum(k * dk_inter, axis=0, keepdims=True) + through
            dlf = _dot_exact(_tri(c, True), q * dq - k * dk) + later
            df = dlf / f - dk
            dq_ref[sl, :] = (dq * (sq * (1.0 + qraw * (1.0 - sq)))).astype(BF16)
            df_ref[sl, :] = (df * (1.0 - lb) * sf * (1.0 - sf)).astype(BF16)
            di_ref[sl, :] = dv.astype(BF16)
            dlb_ref[...] += jnp.sum(df * (1.0 - sf), axis=0, keepdims=True)
            return carry

        lax.fori_loop(0, nc, step, 0)

    col = lambda off: pl.BlockSpec((rows, HGRN_HEAD), lambda h, r: (nr - 1 - r, off + h))
    out = col(0)
    return pl.pallas_call(
        body, name=name, grid=(nh, nr),
        in_specs=[col(0), col(nh), col(2 * nh), col(3 * nh),
                  pl.BlockSpec((1, HGRN_HEAD), lambda h, r: (0, h)), _full_spec((1, HGRN_HEAD)),
                  out, pl.BlockSpec((1, nc, HGRN_HEAD, HGRN_HEAD), lambda h, r: (h, nr - 1 - r, 0, 0)), out],
        out_specs=[out, out, out, out, pl.BlockSpec((1, HGRN_HEAD), lambda h, r: (0, h)),
                   pl.BlockSpec((1, 1, HGRN_HEAD), lambda h, r: (h, 0, 0))],
        out_shape=[jax.ShapeDtypeStruct((t, w), BF16)] * 4 + [jax.ShapeDtypeStruct((1, w), F32),
                                                             jax.ShapeDtypeStruct((nh, 1, HGRN_HEAD), F32)],
        scratch_shapes=[pltpu.VMEM((HGRN_HEAD, HGRN_HEAD), F32)],
        compiler_params=_params((_PAR, _ARB)),
    )(proj, proj, proj, proj, lb, norm_g, o, states, dog)


def hgrn_mixer_fwd(hn, w_in, w_out, norm_g, lb_param, idx, name):
    nh = w_out.shape[0] // HGRN_HEAD
    lb = lb_table_row(lb_param, idx, name + "_lb")
    proj = mm(hn, w_in, "nn", BF16, name + "_in")
    og, o, states = hgrn_fwd(proj, lb, norm_g, nh, name + "_rec")
    y = mm(og, w_out, "nn", F32, name + "_out")
    return y, (proj, lb, og, o, states)


def hgrn_mixer_bwd(dy, hn, w_in, w_out, norm_g, lb_param, idx, saved, name):
    proj, lb, og, o, states = saved
    nh = w_out.shape[0] // HGRN_HEAD
    dwo = wgrad(og, dy, name + "_dwo")
    dog = mm(dy, w_out, "nt", BF16, name + "_dog")
    dq, df, di, dg, dlb, dng = hgrn_bwd(proj, lb, norm_g, o, states, dog, nh, name + "_recb")
    dproj = jnp.concatenate([dq, df, di, dg], axis=1)
    dwi = wgrad(hn, dproj, name + "_dwi")
    dhn = mm(dproj, w_in, "nt", BF16, name + "_dhn")
    dlbp = lb_table_row_bwd(lb_param, dlb, idx, name + "_lbb")
    return dhn, dwi, dwo, jnp.sum(dng, axis=0), dlbp


_ANY = pl.BlockSpec(memory_space=pl.ANY)
_MESH = pl.DeviceIdType.MESH


def _place():
    x, y, c = lax.axis_index("x"), lax.axis_index("y"), lax.axis_index("c")
    chips = [(1 - x, y), (x, 1 - y), (1 - x, 1 - y)]
    return x, y, c, N_CHIPS // 2 * x + y, chips


def _chip_index(chip):
    return N_CHIPS // 2 * chip[0] + chip[1]


def _window(ref, axis, start, size):
    idx = [slice(None)] * len(ref.shape)
    idx[axis] = pl.ds(start, size)
    return ref.at[tuple(idx)]


def gather_shards(shards, axes, name):
    n = len(shards)

    def body(*refs):
        ins, outs = refs[:n], refs[n:2 * n]
        send, recv, local = refs[2 * n:]
        x, y, c, me, chips = _place()
        started = []
        for k in range(n):
            size = ins[k].shape[axes[k]]
            mine = pltpu.make_async_copy(ins[k], _window(outs[k], axes[k], me * size, size), local.at[k])
            mine.start()
            started.append(mine)
            for j, chip in enumerate(chips):
                pltpu.make_async_remote_copy(
                    src_ref=ins[k], dst_ref=_window(outs[k], axes[k], me * size, size),
                    send_sem=send.at[k, j], recv_sem=recv.at[k, j],
                    device_id=(chip[0], chip[1], c), device_id_type=_MESH).start()
        for k in range(n):
            size = ins[k].shape[axes[k]]
            for j, chip in enumerate(chips):
                landed = pltpu.make_async_remote_copy(
                    src_ref=ins[k], dst_ref=_window(outs[k], axes[k], _chip_index(chip) * size, size),
                    send_sem=send.at[k, j], recv_sem=recv.at[k, j],
                    device_id=(chip[0], chip[1], c), device_id_type=_MESH)
                landed.wait_recv()
                landed.wait_send()
            started[k].wait()

    out_shape = []
    for s, ax in zip(shards, axes):
        shape = list(s.shape)
        shape[ax] *= N_CHIPS
        out_shape.append(jax.ShapeDtypeStruct(tuple(shape), s.dtype))
    return pl.pallas_call(
        body, name=name, in_specs=[_ANY] * n, out_specs=[_ANY] * n, out_shape=out_shape,
        scratch_shapes=[pltpu.SemaphoreType.DMA((n, N_CHIPS - 1)), pltpu.SemaphoreType.DMA((n, N_CHIPS - 1)),
                        pltpu.SemaphoreType.DMA((n,))],
    )(*shards)


def chip_exchange(ps, name):
    n = len(ps)

    def body(*refs):
        ins, outs = refs[:n], refs[n:2 * n]
        send, recv, local = refs[2 * n:]
        x, y, c, me, chips = _place()
        started = []
        for k in range(n):
            mine = pltpu.make_async_copy(ins[k].at[me], outs[k].at[me], local.at[k])
            mine.start()
            started.append(mine)
            for j, chip in enumerate(chips):
                pltpu.make_async_remote_copy(
                    src_ref=ins[k].at[_chip_index(chip)], dst_ref=outs[k].at[me],
                    send_sem=send.at[k, j], recv_sem=recv.at[k, j],
                    device_id=(chip[0], chip[1], c), device_id_type=_MESH).start()
        for k in range(n):
            for j, chip in enumerate(chips):
                landed = pltpu.make_async_remote_copy(
                    src_ref=ins[k].at[_chip_index(chip)], dst_ref=outs[k].at[_chip_index(chip)],
                    send_sem=send.at[k, j], recv_sem=recv.at[k, j],
                    device_id=(chip[0], chip[1], c), device_id_type=_MESH)
                landed.wait_recv()
                landed.wait_send()
            started[k].wait()

    return pl.pallas_call(
        body, name=name, in_specs=[_ANY] * n, out_specs=[_ANY] * n,
        out_shape=[jax.ShapeDtypeStruct(p.shape, p.dtype) for p in ps],
        scratch_shapes=[pltpu.SemaphoreType.DMA((n, N_CHIPS - 1)), pltpu.SemaphoreType.DMA((n, N_CHIPS - 1)),
                        pltpu.SemaphoreType.DMA((n,))],
    )(*ps)


def allreduce_small(buf, name):
    rows = buf.shape[0]
    n_dev = 2 * N_CHIPS

    def body(in_ref, out_ref, slots, send, recv):
        x, y, c, me, chips = _place()
        my_id = 2 * me + c
        slots[my_id] = in_ref[...]
        for j in range(1, n_dev):
            fx, fy, fc = (j >> 2) & 1, (j >> 1) & 1, j & 1
            peer = ((1 - x) if fx else x, (1 - y) if fy else y, (1 - c) if fc else c)
            pltpu.make_async_remote_copy(
                src_ref=in_ref, dst_ref=slots.at[my_id], send_sem=send.at[j], recv_sem=recv.at[j],
                device_id=peer, device_id_type=_MESH).start()
        for j in range(1, n_dev):
            fx, fy, fc = (j >> 2) & 1, (j >> 1) & 1, j & 1
            peer = ((1 - x) if fx else x, (1 - y) if fy else y, (1 - c) if fc else c)
            peer_id = 2 * _chip_index(peer) + peer[2]
            landed = pltpu.make_async_remote_copy(
                src_ref=in_ref, dst_ref=slots.at[peer_id], send_sem=send.at[j], recv_sem=recv.at[j],
                device_id=peer, device_id_type=_MESH)
            landed.wait_recv()
            landed.wait_send()
        tot = slots[0]
        for d in range(1, n_dev):
            tot = tot + slots[d]
        out_ref[...] = tot

    return pl.pallas_call(
        body, name=name,
        in_specs=[pl.BlockSpec(memory_space=pltpu.VMEM)], out_specs=pl.BlockSpec(memory_space=pltpu.VMEM),
        out_shape=jax.ShapeDtypeStruct(buf.shape, F32),
        scratch_shapes=[pltpu.VMEM((n_dev, rows, LANES), F32), pltpu.SemaphoreType.DMA((n_dev,)),
                        pltpu.SemaphoreType.DMA((n_dev,))],
        compiler_params=pltpu.CompilerParams(vmem_limit_bytes=VMEM_LIMIT),
    )(buf)


def _sibling_step(src_ref, slots, send, recv, credit, step, n_steps):
    x, y, c = lax.axis_index("x"), lax.axis_index("y"), lax.axis_index("c")
    slot = step % 2

    @pl.when(step >= 2)
    def _():
        pl.semaphore_wait(credit, 1)

    cp = pltpu.make_async_remote_copy(src_ref=src_ref, dst_ref=slots.at[slot], send_sem=send.at[slot],
                                      recv_sem=recv.at[slot], device_id=(x, y, 1 - c), device_id_type=_MESH)
    cp.start()
    cp.wait_recv()
    return cp, slot


def _sibling_done(cp, credit, step, n_steps):
    x, y, c = lax.axis_index("x"), lax.axis_index("y"), lax.axis_index("c")
    cp.wait_send()

    @pl.when(step < n_steps - 2)
    def _():
        pl.semaphore_signal(credit, 1, device_id=(x, y, 1 - c), device_id_type=_MESH)


def pair_sum(g, name):
    by_cols = g.ndim == 2
    s = N_CHIPS if by_cols else g.shape[0]
    r = g.shape[-2]
    cols = g.shape[-1] // s if by_cols else g.shape[-1]
    half = r // 2
    tr = _row_tile(half, cols, 1 << 19)
    nt = half // tr
    n_steps = s * nt

    def body(g_ref, o_ref, slots, send, recv, credit):
        c = lax.axis_index("c")
        step = pl.program_id(0) * nt + pl.program_id(1)
        cp, slot = _sibling_step(g_ref.at[0, 1 - c], slots, send, recv, credit, step, n_steps)
        o_ref[0] = (g_ref[0, c].astype(F32) + slots[slot].astype(F32)).astype(BF16)
        _sibling_done(cp, credit, step, n_steps)

    if by_cols:
        in_spec = pl.BlockSpec((1, 2, tr, cols), lambda k, i: (0, 0, i, k))
        g4 = g.reshape(1, 2, half, s * cols)
    else:
        in_spec = pl.BlockSpec((1, 2, tr, cols), lambda k, i: (k, 0, i, 0))
        g4 = g.reshape(s, 2, half, cols)
    return pl.pallas_call(
        body, name=name, grid=(s, nt),
        in_specs=[in_spec],
        out_specs=pl.BlockSpec((1, tr, cols), lambda k, i: (k, i, 0)),
        out_shape=jax.ShapeDtypeStruct((s, half, cols), BF16),
        scratch_shapes=[pltpu.VMEM((2, tr, cols), BF16), pltpu.SemaphoreType.DMA((2,)), pltpu.SemaphoreType.DMA((2,)),
                        pltpu.SemaphoreType.REGULAR],
        compiler_params=_params((_ARB, _ARB)),
    )(g4)


def chip_sum_share(q, acc, layer, name):
    s, r2, cols = q.shape
    tr = _row_tile(r2, cols, 1 << 18)
    nt = r2 // tr

    def body(q_ref, acc_ref, o_ref, slots, send, recv, credit):
        c = lax.axis_index("c")
        step = pl.program_id(0)
        tot = q_ref[0].astype(F32)
        for k in range(1, s):
            tot = tot + q_ref[k].astype(F32)
        o_ref[0, c] = tot
        cp, slot = _sibling_step(o_ref.at[0, c], slots, send, recv, credit, step, nt)
        o_ref[0, 1 - c] = slots[slot]
        _sibling_done(cp, credit, step, nt)

    return pl.pallas_call(
        body, name=name, grid=(nt,),
        in_specs=[pl.BlockSpec((s, tr, cols), lambda i: (0, i, 0)), _ANY],
        out_specs=pl.BlockSpec((1, 2, tr, cols), lambda i: (layer, 0, i, 0)),
        out_shape=jax.ShapeDtypeStruct(acc.shape, F32),
        input_output_aliases={1: 0},
        scratch_shapes=[pltpu.VMEM((2, tr, cols), F32), pltpu.SemaphoreType.DMA((2,)), pltpu.SemaphoreType.DMA((2,)),
                        pltpu.SemaphoreType.REGULAR],
        compiler_params=_params((_ARB,)),
    )(q, acc)


def _row_tile(rows, cols, budget):
    for tr in (1024, 512, 256, 128, 64, 32, 16, 8):
        if rows % tr == 0 and tr * cols <= budget:
            return tr
    return rows


def adamw(w, g, m, v, name):
    rows, cols = w.shape
    tr = _row_tile(rows, cols, 1 << 18)
    c1 = 1.0 - ADAM_B1 ** ADAM_STEP
    c2 = 1.0 - ADAM_B2 ** ADAM_STEP

    def body(w_ref, g_ref, m_ref, v_ref, d_ref, nm_ref, nv_ref):
        gv = g_ref[...]
        nm = ADAM_B1 * m_ref[...] + (1.0 - ADAM_B1) * gv
        nv = ADAM_B2 * v_ref[...] + (1.0 - ADAM_B2) * (gv * gv)
        nm_ref[...] = nm
        nv_ref[...] = nv
        d_ref[...] = -ADAM_LR * ((nm / c1) / (jnp.sqrt(nv / c2) + ADAM_EPS) + ADAM_WD * w_ref[...])

    spec = pl.BlockSpec((tr, cols), lambda i: (i, 0))
    return pl.pallas_call(
        body, name=name, grid=(rows // tr,),
        in_specs=[spec] * 4, out_specs=[spec] * 3,
        out_shape=[jax.ShapeDtypeStruct((rows, cols), F32)] * 3,
        compiler_params=_params((_PAR,)),
    )(w, g, m, v)


WEIGHTS = ("mix_pre_g", "mix_post_g", "ffn_pre_g", "ffn_post_g", "hgrn_w_in", "hgrn_w_out", "hgrn_norm_g",
           "hgrn_lb_param", "swa_w_in", "swa_w_out", "swa_sinks", "sc_w_in", "sc_conv_w", "sc_w_out", "fox_w_in",
           "fox_b_f", "fox_w_out", "ffn_w_up", "ffn_conv_w", "ffn_conv_b", "ffn_w_down")
N_MIXERS = 4


def _pack_small(parts):
    flat = jnp.concatenate([p.reshape(-1).astype(F32) for p in parts])
    rows = -(-flat.shape[0] // (8 * LANES)) * 8
    return jnp.pad(flat, (0, rows * LANES - flat.shape[0])).reshape(rows, LANES)


def _unpack_small(buf, shapes):
    flat, out, off = buf.reshape(-1), [], 0
    for s in shapes:
        n = math.prod(s)
        out.append(flat[off:off + n].reshape(s))
        off += n
    return out


def _stack_rows(dw):
    return dw.reshape(N_CHIPS, dw.shape[0] // N_CHIPS, dw.shape[1])


def kernel(x, positions, mix_pre_g, mix_post_g, ffn_pre_g, ffn_post_g, hgrn_w_in, hgrn_w_out, hgrn_norm_g, hgrn_lb_param, swa_w_in, swa_w_out, swa_sinks, sc_w_in, sc_conv_w, sc_w_out, fox_w_in, fox_b_f, fox_w_out, ffn_w_up, ffn_conv_w, ffn_conv_b, ffn_w_down, loss_target, m_mix_pre_g, m_mix_post_g, m_ffn_pre_g, m_ffn_post_g, m_hgrn_w_in, m_hgrn_w_out, m_hgrn_norm_g, m_hgrn_lb_param, m_swa_w_in, m_swa_w_out, m_swa_sinks, m_sc_w_in, m_sc_conv_w, m_sc_w_out, m_fox_w_in, m_fox_b_f, m_fox_w_out, m_ffn_w_up, m_ffn_conv_w, m_ffn_conv_b, m_ffn_w_down, v_mix_pre_g, v_mix_post_g, v_ffn_pre_g, v_ffn_post_g, v_hgrn_w_in, v_hgrn_w_out, v_hgrn_norm_g, v_hgrn_lb_param, v_swa_w_in, v_swa_w_out, v_swa_sinks, v_sc_w_in, v_sc_conv_w, v_sc_w_out, v_fox_w_in, v_fox_b_f, v_fox_w_out, v_ffn_w_up, v_ffn_conv_w, v_ffn_conv_b, v_ffn_w_down):
    given = dict(locals())
    depth = mix_pre_g.shape[0]
    assert depth == N_MIXERS and x.shape[0] == 1, "one batch element per device, one layer of each mixer"
    xi, target = x[0], loss_target[0]
    chip = N_CHIPS // 2 * lax.axis_index("x") + lax.axis_index("y")
    nh_fox = fox_b_f.shape[1]
    row = lambda a, i: a[i:i + 1]

    bf = lambda a: a.astype(BF16)
    shards = [bf(hgrn_w_in[0]), bf(hgrn_w_out[0]), bf(swa_w_in[0]), bf(swa_w_out[0]), bf(sc_w_in[0]),
              bf(sc_w_out[0]), bf(fox_w_in), bf(fox_w_out[0]), bf(ffn_w_up), bf(ffn_w_down), sc_conv_w[0], ffn_conv_w]
    axes = [1, 0, 1, 0, 1, 0, 0, 0, 2, 1, 1, 2]
    (hg_in, hg_out, sw_in, sw_out, sc_in, sc_out, fx_in4, fx_out, w_up, w_down, sc_cw, f_cw) = gather_shards(
        shards, axes, "gather_weights")
    fx_in = fox_pad_w_in(jnp.concatenate([fx_in4[s] for s in range(N_CHIPS)], axis=1), nh_fox)

    saved = []
    xs = xi
    hn = rms_fwd(xs, row(mix_pre_g, 0), "pre_norm0")
    dx = loss = None
    for i in range(depth):
        nm = f"l{i}"
        if i == 0:
            y, sv = hgrn_mixer_fwd(hn, hg_in, hg_out, hgrn_norm_g, hgrn_lb_param, i, nm + "_hgrn")
        elif i == 1:
            y, sv = swa_mixer_fwd(hn, sw_in, sw_out, swa_sinks[0], positions, nm + "_swa")
        elif i == 2:
            proj = mm(hn, sc_in, "nn", BF16, nm + "_sc_in")
            yb = sconv_fwd(proj, sc_cw, nm + "_sc_conv")
            y, sv = mm(yb, sc_out, "nn", F32, nm + "_sc_out"), (proj, yb)
        else:
            y, sv = fox_mixer_fwd(hn, fx_in, fx_out, fox_b_f[0], nm + "_fox")
        x1, hn2 = resid_norm(xs, y, row(mix_post_g, i), row(ffn_pre_g, i), nm + "_mix_resid")
        z = mm(hn2, w_up, "nn", BF16, nm + "_ffn_up", b_idx=i)
        a = ffn_act(z, f_cw[i], row(ffn_conv_b, i), nm + "_ffn_act")
        y2 = mm(a, w_down, "nn", F32, nm + "_ffn_down", b_idx=i)
        saved.append((xs, hn, y, sv, x1, hn2, z, a, y2))
        if i < depth - 1:
            xs, hn = resid_norm(x1, y2, row(ffn_post_g, i), row(mix_pre_g, i + 1), nm + "_ffn_resid")
        else:
            dx, loss = resid_loss(x1, y2, row(ffn_post_g, i), target, nm + "_loss")

    big = {}
    d_pre, d_post, d_fpre, d_fpost = [None] * depth, [None] * depth, [None] * depth, [None] * depth
    d_up, d_down, d_fcw, d_fcb = [None] * depth, [None] * depth, [None] * depth, [None] * depth
    small = {}
    for i in reversed(range(depth)):
        nm = f"l{i}b"
        xs, hn, y, sv, x1, hn2, z, a, y2 = saved[i]
        dy2, d_fpost[i] = norm_bwd(y2, row(ffn_post_g, i), dx, None, BF16, nm + "_ffn_post")
        d_down[i] = _stack_rows(wgrad(a, dy2, nm + "_dw_down"))
        da = mm(dy2, w_down, "nt", BF16, nm + "_da", b_idx=i)
        du, acc = ffn_act_bwd(z, da, f_cw[i], row(ffn_conv_b, i), nm + "_ffn_actb")
        d_fcw[i], d_fcb[i] = acc[0:CONV_WIDTH], acc[CONV_WIDTH]
        dz = conv_transpose(du, f_cw[i], nm + "_ffn_convT")
        d_up[i] = wgrad(hn2, dz, nm + "_dw_up")
        dhn2 = mm(dz, w_up, "nt", BF16, nm + "_dhn2", b_idx=i)
        dx1, d_fpre[i] = norm_bwd(x1, row(ffn_pre_g, i), dhn2, dx, F32, nm + "_ffn_pre")
        dy, d_post[i] = norm_bwd(y, row(mix_post_g, i), dx1, None, BF16, nm + "_mix_post")
        if i == 0:
            dhn, dwi, dwo, small["hgrn_norm_g"], small["hgrn_lb_param"] = hgrn_mixer_bwd(
                dy, hn, hg_in, hg_out, hgrn_norm_g, hgrn_lb_param, i, sv, nm + "_hgrn")
            big["hgrn_w_in"], big["hgrn_w_out"] = [dwi], [_stack_rows(dwo)]
        elif i == 1:
            dhn, dwi, dwo, dsink = swa_mixer_bwd(dy, hn, sw_in, sw_out, swa_sinks[0], positions, sv, nm + "_swa")
            big["swa_w_in"], big["swa_w_out"], small["swa_sinks"] = [dwi], [_stack_rows(dwo)], dsink
        elif i == 2:
            proj, yb = sv
            dwo = wgrad(yb, dy, nm + "_sc_dwo")
            dyb = mm(dy, sc_out, "nt", BF16, nm + "_sc_dyb")
            dproj, acc = sconv_bwd(proj, dyb, sc_cw, nm + "_sc_convb")
            dwi = wgrad(hn, dproj, nm + "_sc_dwi")
            dhn = mm(dproj, sc_in, "nt", BF16, nm + "_sc_dhn")
            big["sc_w_in"], big["sc_w_out"], small["sc_conv_w"] = [dwi], [_stack_rows(dwo)], acc[0:CONV_WIDTH]
        else:
            dhn, dwi, dwo, small["fox_b_f"] = fox_mixer_bwd(dy, hn, fx_in, fx_out, fox_b_f[0], sv, nm + "_fox")
            dwi = fox_unpad_dw(dwi, nh_fox)
            cols = dwi.shape[1] // N_CHIPS
            big["fox_w_in"] = [jnp.stack([dwi[:, s * cols:(s + 1) * cols] for s in range(N_CHIPS)])]
            big["fox_w_out"] = [_stack_rows(dwo)]
        dx, d_pre[i] = norm_bwd(xs, row(mix_pre_g, i), dhn, dx1, F32, nm + "_mix_pre")
    big["ffn_w_up"], big["ffn_w_down"] = d_up, d_down
    small.update(mix_pre_g=jnp.concatenate(d_pre), mix_post_g=jnp.concatenate(d_post),
                 ffn_pre_g=jnp.concatenate(d_fpre), ffn_post_g=jnp.concatenate(d_fpost),
                 ffn_conv_w=jnp.stack(d_fcw), ffn_conv_b=jnp.stack(d_fcb))

    big_names = [n for n in WEIGHTS if n in big]
    groups = [(n, l) for n in big_names for l in range(len(big[n]))]
    ps = [pair_sum(big[n][l], f"grads_pair_{n}{l}") for n, l in groups]
    qs = chip_exchange(ps, "grads_chips")
    grads = {}
    for (n, l), q in zip(groups, qs):
        if l == 0:
            grads[n] = lax.empty((len(big[n]), 2) + q.shape[1:], F32)
        grads[n] = chip_sum_share(q, grads[n], l, f"grads_share_{n}{l}")

    small_names = [n for n in WEIGHTS if n in small]
    full_shape = {n: tuple(given[n].shape) for n in small_names}
    full_shape["sc_conv_w"] = (1, CONV_WIDTH, sc_cw.shape[1])
    full_shape["ffn_conv_w"] = tuple(f_cw.shape)
    summed = _unpack_small(allreduce_small(_pack_small([small[n] for n in small_names] + [loss]), "small_sum"),
                           [full_shape[n] for n in small_names] + [()])
    loss = summed[-1]
    for n, g in zip(small_names, summed):
        if g.shape != given[n].shape:
            width = given[n].shape[-1]
            g = lax.dynamic_slice_in_dim(g, chip * width, width, axis=g.ndim - 1)
        grads[n] = g

    deltas, new_m, new_v = {}, {}, {}
    for n in WEIGHTS:
        w = given[n]
        flat = lambda a: a.reshape(-1, w.shape[-1])
        dl, nm_, nv_ = adamw(flat(w), flat(grads[n]), flat(given["m_" + n]), flat(given["v_" + n]), "adamw_" + n)
        deltas[n], new_m[n], new_v[n] = dl.reshape(w.shape), nm_.reshape(w.shape), nv_.reshape(w.shape)
    return (loss, dx[None], *[grads[n].reshape(given[n].shape) for n in WEIGHTS], *[deltas[n] for n in WEIGHTS],
            *[new_m[n] for n in WEIGHTS], *[new_v[n] for n in WEIGHTS])
```

```python
import functools
import math

import numpy as np
import jax
import jax.numpy as jnp
from jax import lax
from jax.experimental import pallas as pl
from jax.experimental.pallas import tpu as pltpu

F32 = jnp.float32
BF16 = jnp.bfloat16

RMS_EPS = 1e-6
HGRN_HEAD = 128
HGRN_CHUNK = 32
ATT_HEAD = 64
SWA_WINDOW = 128
SWA_GROUP = 8
ROT_DIM = 16
ROPE_THETA = 500000.0
CONV_WIDTH = 3
ADAM_LR = 0.001
ADAM_B1 = 0.9
ADAM_B2 = 0.999
ADAM_EPS = 1e-08
ADAM_WD = 0.01
ADAM_STEP = 10
N_CHIPS = 4
LANES = 128
BF16_ROWS = 16
VMEM_LIMIT = 48 * 1024 * 1024

_ARB = "arbitrary"
_PAR = "parallel"


def _params(sem, **kw):
    return pltpu.CompilerParams(dimension_semantics=sem, vmem_limit_bytes=VMEM_LIMIT, **kw)


def _tile(n, prefs):
    for p in prefs:
        if n % p == 0:
            return p
    return n


def _sigmoid(x):
    return 1.0 / (1.0 + jnp.exp(-x))


def _dot(a, b, dims):
    return lax.dot_general(a, b, (dims, ((), ())), preferred_element_type=F32)


def _dot_nn(a, b):
    return _dot(a, b, ((1,), (0,)))


def _dot_nt(a, b):
    return _dot(a, b, ((1,), (1,)))


def _dot_tn(a, b):
    return _dot(a, b, ((0,), (0,)))


MM_VMEM_BUDGET = 36 * 1024 * 1024
MM_HBM_RATE = 3.0e12
MM_MXU_RATE = 6.5e14
MM_STEP_S = 0.35e-6
MM_ACC_RATE = 3.0e12


def _mm_tiles(m, n, k, out_bytes):
    best = None
    for tm in (2048, 1024, 512, 256, 128):
        for tn in (2048, 1024, 512, 256, 128):
            for tk in sorted({k, 4096, 2816, 2048, 1408, 1024, 512, 256, 128}, reverse=True):
                if m % tm or n % tn or tk > k or k % tk:
                    continue
                nk = k // tk
                vmem = 4 * (tm * tk + tk * tn) + (4 * tm * tn if nk > 1 else 0) + 2 * tm * tn * out_bytes
                if vmem > MM_VMEM_BUDGET:
                    continue
                steps = (m // tm) * (n // tn) * nk
                traffic = 2 * m * k * (1 if nk == 1 else n // tn) + 2 * k * n * (m // tm) + m * n * out_bytes
                cost = max(traffic / MM_HBM_RATE, 2 * m * n * k / MM_MXU_RATE) + steps * MM_STEP_S
                if nk > 1:
                    cost += steps * 8 * tm * tn / MM_ACC_RATE
                if best is None or cost < best[0]:
                    best = (cost, tm, tn, tk)
    assert best is not None, (m, n, k)
    return best[1:]


def mm(a, b, mode, out_dtype, name="mm"):
    if mode == "nn":
        (m, k), (k2, n) = a.shape, b.shape
    elif mode == "nt":
        (m, k), (n, k2) = a.shape, b.shape
    else:
        (k, m), (k2, n) = a.shape, b.shape
    assert k == k2, (a.shape, b.shape, mode)
    tm, tn, tk = _mm_tiles(m, n, k, jnp.dtype(out_dtype).itemsize)
    nk = k // tk

    def product(a_ref, b_ref):
        av = a_ref[...].astype(BF16)
        bv = b_ref[...].astype(BF16)
        return {"nn": _dot_nn, "nt": _dot_nt, "tn": _dot_tn}[mode](av, bv)

    def body_one(a_ref, b_ref, o_ref):
        o_ref[...] = product(a_ref, b_ref).astype(out_dtype)

    def body_acc(a_ref, b_ref, o_ref, acc_ref):
        kk = pl.program_id(2)

        @pl.when(kk == 0)
        def _():
            acc_ref[...] = jnp.zeros_like(acc_ref)

        acc_ref[...] += product(a_ref, b_ref)

        @pl.when(kk == nk - 1)
        def _():
            o_ref[...] = acc_ref[...].astype(out_dtype)

    if mode == "nn":
        a_spec = pl.BlockSpec((tm, tk), lambda i, j, kk: (i, kk))
        b_spec = pl.BlockSpec((tk, tn), lambda i, j, kk: (kk, j))
    elif mode == "nt":
        a_spec = pl.BlockSpec((tm, tk), lambda i, j, kk: (i, kk))
        b_spec = pl.BlockSpec((tn, tk), lambda i, j, kk: (j, kk))
    else:
        a_spec = pl.BlockSpec((tk, tm), lambda i, j, kk: (kk, i))
        b_spec = pl.BlockSpec((tk, tn), lambda i, j, kk: (kk, j))
    return pl.pallas_call(
        body_one if nk == 1 else body_acc,
        name=name,
        grid=(m // tm, n // tn, nk),
        in_specs=[a_spec, b_spec],
        out_specs=pl.BlockSpec((tm, tn), lambda i, j, kk: (i, j)),
        out_shape=jax.ShapeDtypeStruct((m, n), out_dtype),
        scratch_shapes=[] if nk == 1 else [pltpu.VMEM((tm, tn), F32)],
        compiler_params=_params((_PAR, _PAR, _ARB)),
    )(a, b)


def wgrad(a, b, name):
    return mm(a.T, b, "nn", BF16, name)


def _rstd(xv):
    return lax.rsqrt(jnp.mean(xv * xv, axis=1, keepdims=True) + RMS_EPS)


def _row_spec(tr, w):
    return pl.BlockSpec((tr, w), lambda i: (i, 0))


def _full_spec(shape):
    nd = len(shape)
    return pl.BlockSpec(shape, lambda *_: (0,) * nd)


def rms_fwd(x, g, name):
    t, d = x.shape
    tr = _tile(t, (256, 128, 64, 32, 16))

    def body(x_ref, g_ref, o_ref):
        xv = x_ref[...]
        o_ref[...] = (xv * _rstd(xv) * g_ref[...]).astype(BF16)

    return pl.pallas_call(
        body, name=name, grid=(t // tr,),
        in_specs=[_row_spec(tr, d), _full_spec((1, d))],
        out_specs=_row_spec(tr, d),
        out_shape=jax.ShapeDtypeStruct((t, d), BF16),
        compiler_params=_params((_PAR,)),
    )(x, g)


def resid_norm(x, y, g_post, g_next, name):
    t, d = x.shape
    tr = _tile(t, (256, 128, 64, 32, 16))

    def body(x_ref, y_ref, gp_ref, gn_ref, x1_ref, hn_ref):
        yv = y_ref[...]
        x1 = x_ref[...] + yv * _rstd(yv) * gp_ref[...]
        x1_ref[...] = x1
        hn_ref[...] = (x1 * _rstd(x1) * gn_ref[...]).astype(BF16)

    return pl.pallas_call(
        body, name=name, grid=(t // tr,),
        in_specs=[_row_spec(tr, d), _row_spec(tr, d), _full_spec((1, d)), _full_spec((1, d))],
        out_specs=[_row_spec(tr, d), _row_spec(tr, d)],
        out_shape=[jax.ShapeDtypeStruct((t, d), F32), jax.ShapeDtypeStruct((t, d), BF16)],
        compiler_params=_params((_PAR,)),
    )(x, y, g_post, g_next)


def resid_loss(x, y, g_post, target, name):
    t, d = x.shape
    tr = _tile(t, (256, 128, 64, 32, 16))

    def body(x_ref, y_ref, gp_ref, t_ref, dx_ref, loss_ref):
        @pl.when(pl.program_id(0) == 0)
        def _():
            loss_ref[...] = jnp.zeros_like(loss_ref)

        yv = y_ref[...]
        err = x_ref[...] + yv * _rstd(yv) * gp_ref[...] - t_ref[...]
        dx_ref[...] = err * (1.0 / d)
        loss_ref[...] += 0.5 * jnp.sum(jnp.mean(err * err, axis=1, keepdims=True), axis=0, keepdims=True)

    dx, loss = pl.pallas_call(
        body, name=name, grid=(t // tr,),
        in_specs=[_row_spec(tr, d), _row_spec(tr, d), _full_spec((1, d)), _row_spec(tr, d)],
        out_specs=[_row_spec(tr, d), _full_spec((8, LANES))],
        out_shape=[jax.ShapeDtypeStruct((t, d), F32), jax.ShapeDtypeStruct((8, LANES), F32)],
        compiler_params=_params((_ARB,)),
    )(x, y, g_post, target)
    return dx, loss[0:1, 0:1]


def norm_bwd(yin, g, dout, res, out_dtype, name, after=None):
    t, d = yin.shape
    tr = _tile(t, (256, 128, 64, 32, 16))
    has_res = res is not None

    def body(*refs):
        refs = refs[:3 + has_res] + refs[-2:]
        if has_res:
            y_ref, g_ref, d_ref, r_ref, o_ref, dg_ref = refs
        else:
            y_ref, g_ref, d_ref, o_ref, dg_ref = refs

        @pl.when(pl.program_id(0) == 0)
        def _():
            dg_ref[...] = jnp.zeros_like(dg_ref)

        yv = y_ref[...]
        dv = d_ref[...].astype(F32)
        r = _rstd(yv)
        yh = yv * r
        dyh = dv * g_ref[...]
        dy = r * (dyh - yh * jnp.mean(dyh * yh, axis=1, keepdims=True))
        if has_res:
            dy = dy + r_ref[...]
        o_ref[...] = dy.astype(out_dtype)
        dg_ref[...] += jnp.sum(dv * yh, axis=0, keepdims=True)

    ins = [yin, g, dout] + ([res] if has_res else []) + ([] if after is None else [after])
    in_specs = ([_row_spec(tr, d), _full_spec((1, d)), _row_spec(tr, d)] + ([_row_spec(tr, d)] if has_res else [])
                + ([] if after is None else [pl.BlockSpec(memory_space=pl.ANY)]))
    return pl.pallas_call(
        body, name=name, grid=(t // tr,),
        in_specs=in_specs,
        out_specs=[_row_spec(tr, d), _full_spec((1, d))],
        out_shape=[jax.ShapeDtypeStruct((t, d), out_dtype), jax.ShapeDtypeStruct((1, d), F32)],
        compiler_params=_params((_ARB,)),
    )(*ins)


def _shift_down(x, halo):
    tr = x.shape[0]
    row = lax.broadcasted_iota(jnp.int32, x.shape, 0)
    h1 = halo[BF16_ROWS - 1:BF16_ROWS, :]
    h2 = halo[BF16_ROWS - 2:BF16_ROWS - 1, :]
    x1 = jnp.where(row == 0, h1, pltpu.roll(x, 1, 0))
    x2 = jnp.where(row == 0, h2, jnp.where(row == 1, h1, pltpu.roll(x, 2, 0)))
    return x1, x2


def _shift_up(x, halo):
    tr = x.shape[0]
    row = lax.broadcasted_iota(jnp.int32, x.shape, 0)
    h0 = halo[0:1, :]
    h1 = halo[1:2, :]
    x1 = jnp.where(row == tr - 1, h0, pltpu.roll(x, tr - 1, 0))
    x2 = jnp.where(row == tr - 1, h1, jnp.where(row == tr - 2, h0, pltpu.roll(x, tr - 2, 0)))
    return x1, x2


def _prev_halo_spec(tr, w, nt):
    return pl.BlockSpec((BF16_ROWS, w), lambda i: (jnp.maximum(i * (tr // BF16_ROWS) - 1, 0), 0))


def _next_halo_spec(tr, w, nt):
    last = nt * (tr // BF16_ROWS) - 1
    return pl.BlockSpec((BF16_ROWS, w), lambda i: (jnp.minimum((i + 1) * (tr // BF16_ROWS), last), 0))


def _silu_and_grad(u):
    s = _sigmoid(u)
    return u * s, s * (1.0 + u * (1.0 - s))


def ffn_act(z, conv_w, conv_b, name):
    t, f2 = z.shape
    f = f2 // 2
    tr = _tile(t, (128, 64, 32, 16))
    nt = t // tr
    cw = _tile(f, (512, 256, 128))

    def body(z_ref, zp_ref, w_ref, b_ref, a_ref):
        first = pl.program_id(0) == 0
        for j in range(f // cw):
            us = []
            for off in (j * cw, f + j * cw):
                cols = slice(off, off + cw)
                zc = z_ref[:, cols].astype(F32)
                hp = jnp.where(first, 0.0, zp_ref[:, cols].astype(F32))
                z1, z2 = _shift_down(zc, hp)
                us.append(w_ref[2:3, cols] * zc + w_ref[1:2, cols] * z1 + w_ref[0:1, cols] * z2 + b_ref[:, cols])
            sil, _ = _silu_and_grad(us[0])
            a_ref[:, j * cw:(j + 1) * cw] = (sil * us[1]).astype(BF16)

    return pl.pallas_call(
        body, name=name, grid=(nt,),
        in_specs=[_row_spec(tr, f2), _prev_halo_spec(tr, f2, nt), _full_spec((CONV_WIDTH, f2)), _full_spec((1, f2))],
        out_specs=_row_spec(tr, f),
        out_shape=jax.ShapeDtypeStruct((t, f), BF16),
        compiler_params=_params((_PAR,)),
    )(z, z, conv_w, conv_b)


def ffn_act_bwd(z, da, conv_w, conv_b, name):
    t, f2 = z.shape
    f = f2 // 2
    tr = _tile(t, (128, 64, 32, 16))
    nt = t // tr
    cw = _tile(f, (512, 256, 128))

    def body(z_ref, zp_ref, da_ref, w_ref, b_ref, du_ref, acc_ref):
        first = pl.program_id(0) == 0

        @pl.when(first)
        def _():
            acc_ref[...] = jnp.zeros_like(acc_ref)

        for j in range(f // cw):
            us, zs = [], []
            for off in (j * cw, f + j * cw):
                cols = slice(off, off + cw)
                zc = z_ref[:, cols].astype(F32)
                hp = jnp.where(first, 0.0, zp_ref[:, cols].astype(F32))
                z1, z2 = _shift_down(zc, hp)
                zs.append((z2, z1, zc))
                us.append(w_ref[2:3, cols] * zc + w_ref[1:2, cols] * z1 + w_ref[0:1, cols] * z2 + b_ref[:, cols])
            dav = da_ref[:, j * cw:(j + 1) * cw].astype(F32)
            sil, dsil = _silu_and_grad(us[0])
            dus = (dav * us[1] * dsil, dav * sil)
            for off, du, zsh in zip((j * cw, f + j * cw), dus, zs):
                cols = slice(off, off + cw)
                du_ref[:, cols] = du.astype(BF16)
                for k in range(CONV_WIDTH):
                    acc_ref[k:k + 1, cols] += jnp.sum(du * zsh[k], axis=0, keepdims=True)
                acc_ref[3:4, cols] += jnp.sum(du, axis=0, keepdims=True)

    return pl.pallas_call(
        body, name=name, grid=(nt,),
        in_specs=[_row_spec(tr, f2), _prev_halo_spec(tr, f2, nt), _row_spec(tr, f),
                  _full_spec((CONV_WIDTH, f2)), _full_spec((1, f2))],
        out_specs=[_row_spec(tr, f2), _full_spec((8, f2))],
        out_shape=[jax.ShapeDtypeStruct((t, f2), BF16), jax.ShapeDtypeStruct((8, f2), F32)],
        compiler_params=_params((_ARB,)),
    )(z, z, da, conv_w, conv_b)


def conv_transpose(du, conv_w, name):
    t, w = du.shape
    tr = _tile(t, (128, 64, 32, 16))
    nt = t // tr
    cw = _tile(w, (512, 256, 128))

    def body(d_ref, dn_ref, w_ref, o_ref):
        last = pl.program_id(0) == nt - 1
        for j in range(w // cw):
            cols = slice(j * cw, (j + 1) * cw)
            dc = d_ref[:, cols].astype(F32)
            hn = jnp.where(last, 0.0, dn_ref[:, cols].astype(F32))
            d1, d2 = _shift_up(dc, hn)
            o_ref[:, cols] = (w_ref[2:3, cols] * dc + w_ref[1:2, cols] * d1 + w_ref[0:1, cols] * d2).astype(BF16)

    return pl.pallas_call(
        body, name=name, grid=(nt,),
        in_specs=[_row_spec(tr, w), _next_halo_spec(tr, w, nt), _full_spec((CONV_WIDTH, w))],
        out_specs=_row_spec(tr, w),
        out_shape=jax.ShapeDtypeStruct((t, w), BF16),
        compiler_params=_params((_PAR,)),
    )(du, du, conv_w)


def sconv_fwd(proj, conv_w, name):
    t, w3 = proj.shape
    d = w3 // 3
    tr = _tile(t, (128, 64, 32, 16))
    nt = t // tr
    cw = _tile(d, (512, 256, 128))

    def body(p_ref, pp_ref, w_ref, o_ref):
        first = pl.program_id(0) == 0
        for j in range(d // cw):
            cb, cc, cx = (slice(k * d + j * cw, k * d + (j + 1) * cw) for k in range(3))
            zc = p_ref[:, cc].astype(F32) * p_ref[:, cx].astype(F32)
            hp = jnp.where(first, 0.0, pp_ref[:, cc].astype(F32) * pp_ref[:, cx].astype(F32))
            z1, z2 = _shift_down(zc, hp)
            wc = slice(j * cw, (j + 1) * cw)
            cz = w_ref[2:3, wc] * zc + w_ref[1:2, wc] * z1 + w_ref[0:1, wc] * z2
            o_ref[:, wc] = (p_ref[:, cb].astype(F32) * cz).astype(BF16)

    return pl.pallas_call(
        body, name=name, grid=(nt,),
        in_specs=[_row_spec(tr, w3), _prev_halo_spec(tr, w3, nt), _full_spec((CONV_WIDTH, d))],
        out_specs=_row_spec(tr, d),
        out_shape=jax.ShapeDtypeStruct((t, d), BF16),
        compiler_params=_params((_PAR,)),
    )(proj, proj, conv_w)


def sconv_bwd(proj, dyb, conv_w, name):
    t, w3 = proj.shape
    d = w3 // 3
    tr = _tile(t, (128, 64, 32, 16))
    nt = t // tr
    cw = _tile(d, (512, 256, 128))

    def body(p_ref, pp_ref, pn_ref, dy_ref, dyn_ref, w_ref, o_ref, acc_ref):
        first = pl.program_id(0) == 0
        last = pl.program_id(0) == nt - 1

        @pl.when(first)
        def _():
            acc_ref[...] = jnp.zeros_like(acc_ref)

        for j in range(d // cw):
            cb, cc, cx = (slice(k * d + j * cw, k * d + (j + 1) * cw) for k in range(3))
            wc = slice(j * cw, (j + 1) * cw)
            bv, cv, xv = p_ref[:, cb].astype(F32), p_ref[:, cc].astype(F32), p_ref[:, cx].astype(F32)
            zc = cv * xv
            hp = jnp.where(first, 0.0, pp_ref[:, cc].astype(F32) * pp_ref[:, cx].astype(F32))
            z1, z2 = _shift_down(zc, hp)
            w0, w1, w2 = w_ref[0:1, wc], w_ref[1:2, wc], w_ref[2:3, wc]
            cz = w2 * zc + w1 * z1 + w0 * z2
            dyv = dy_ref[:, wc].astype(F32)
            dcz = dyv * bv
            hn = jnp.where(last, 0.0, dyn_ref[:, wc].astype(F32) * pn_ref[:, cb].astype(F32))
            n1, n2 = _shift_up(dcz, hn)
            dz = w2 * dcz + w1 * n1 + w0 * n2
            o_ref[:, cb] = (dyv * cz).astype(BF16)
            o_ref[:, cc] = (dz * xv).astype(BF16)
            o_ref[:, cx] = (dz * cv).astype(BF16)
            for k, zsh in enumerate((z2, z1, zc)):
                acc_ref[k:k + 1, wc] += jnp.sum(dcz * zsh, axis=0, keepdims=True)

    return pl.pallas_call(
        body, name=name, grid=(nt,),
        in_specs=[_row_spec(tr, w3), _prev_halo_spec(tr, w3, nt), _next_halo_spec(tr, w3, nt),
                  _row_spec(tr, d), _next_halo_spec(tr, d, nt), _full_spec((CONV_WIDTH, d))],
        out_specs=[_row_spec(tr, w3), _full_spec((8, d))],
        out_shape=[jax.ShapeDtypeStruct((t, w3), BF16), jax.ShapeDtypeStruct((8, d), F32)],
        compiler_params=_params((_ARB,)),
    )(proj, proj, proj, dyb, dyb, conv_w)


def rope_tables(positions):
    half = ROT_DIM // 2
    inv_freq = ROPE_THETA ** (-jnp.arange(half, dtype=F32) / half)
    ang = positions.astype(F32)[:, None] * inv_freq[None, :]
    cos, sin = jnp.cos(ang), jnp.sin(ang)
    ones = jnp.ones((positions.shape[0], ATT_HEAD - ROT_DIM), F32)
    c64 = jnp.concatenate([cos, cos, ones], axis=1)
    s64 = jnp.concatenate([-sin, sin, 0.0 * ones], axis=1)
    perm = np.zeros((LANES, LANES), np.float32)
    for lane in range(LANES):
        dim = lane % ATT_HEAD
        if dim < half:
            perm[lane + half, lane] = 1.0
        elif dim < ROT_DIM:
            perm[lane - half, lane] = 1.0
    return jnp.tile(c64, (1, 2)), jnp.tile(s64, (1, 2)), jnp.asarray(perm, BF16)


def rope(xin, ctab, stab, perm, n_rot, sign, name):
    t, w = xin.shape
    tr = _tile(t, (256, 128, 64, 32, 16))

    def body(x_ref, c_ref, s_ref, p_ref, o_ref):
        cv, sv = c_ref[...], s_ref[...] * sign
        for j in range(n_rot // LANES):
            cols = slice(j * LANES, (j + 1) * LANES)
            xb = x_ref[:, cols]
            o_ref[:, cols] = (xb.astype(F32) * cv + _dot_nn(xb, p_ref[...]) * sv).astype(BF16)
        if n_rot < w:
            o_ref[:, n_rot:] = x_ref[:, n_rot:]

    return pl.pallas_call(
        body, name=name, grid=(t // tr,),
        in_specs=[_row_spec(tr, w), _row_spec(tr, LANES), _row_spec(tr, LANES), _full_spec((LANES, LANES))],
        out_specs=_row_spec(tr, w),
        out_shape=jax.ShapeDtypeStruct((t, w), BF16),
        compiler_params=_params((_PAR,)),
    )(xin, ctab, stab, perm)


NEG = -1e30


def _half(shape, h):
    return (lax.broadcasted_iota(jnp.int32, shape, 1) // ATT_HEAD) == h


def _dup_head(xb, kvh):
    xf = jnp.where(_half(xb.shape, kvh), xb.astype(F32), 0.0)
    return (xf + pltpu.roll(xf, ATT_HEAD, 1)).astype(BF16)


def _swa_mask(n, rows, cur_only):
    w = SWA_WINDOW
    shape = (w, w) if cur_only else (w, 2 * w)
    qi = lax.broadcasted_iota(jnp.int32, shape, 0)
    kj = lax.broadcasted_iota(jnp.int32, shape, 1) + (w if cur_only else 0)
    diff = qi + w - kj
    ok = (diff >= 0) & (diff < w)
    return ok & ((kj >= w) | (n > 0))


def swa_fwd(qkv, sinks, hq, name):
    t = qkv.shape[0]
    w = SWA_WINDOW
    nb = t // w
    hkv = hq // SWA_GROUP
    npair = hkv // 2
    qw = 2 * SWA_GROUP * ATT_HEAD
    kcol = hq * ATT_HEAD // LANES
    vcol = kcol + npair
    scale = ATT_HEAD ** -0.5

    def body(sink_ref, q_ref, kp_ref, kc_ref, vp_ref, vc_ref, o_ref, lse_ref):
        m, n = pl.program_id(0), pl.program_id(1)
        kb = jnp.concatenate([kp_ref[...], kc_ref[...]], axis=0)
        vb = jnp.concatenate([vp_ref[...], vc_ref[...]], axis=0)
        ok = _swa_mask(n, w, False)
        for kvh in range(2):
            kd, vd = _dup_head(kb, kvh), _dup_head(vb, kvh)
            for jj in range(SWA_GROUP // 2):
                jp = kvh * (SWA_GROUP // 2) + jj
                q2 = q_ref[:, jp * LANES:(jp + 1) * LANES]
                outs = []
                for a in range(2):
                    qa = jnp.where(_half(q2.shape, a), q2, jnp.zeros_like(q2))
                    s = jnp.where(ok, _dot_nt(qa, kd) * scale, NEG)
                    sink = sink_ref[m * 2 * SWA_GROUP + jp * 2 + a]
                    mx = jnp.maximum(jnp.max(s, axis=1, keepdims=True), sink)
                    e = jnp.exp(s - mx)
                    den = jnp.sum(e, axis=1, keepdims=True) + jnp.exp(sink - mx)
                    p = (e / den).astype(BF16)
                    outs.append(_dot_nn(p, vd))
                    lse_ref[jp * 2 + a] = jnp.broadcast_to(mx + jnp.log(den), (w, LANES))
                o_ref[:, jp * LANES:(jp + 1) * LANES] = jnp.where(_half(outs[0].shape, 0), outs[0], outs[1]).astype(BF16)

    prev = lambda m, n: jnp.maximum(n - 1, 0)
    grid_spec = pltpu.PrefetchScalarGridSpec(
        num_scalar_prefetch=1, grid=(npair, nb),
        in_specs=[
            pl.BlockSpec((w, qw), lambda m, n, s: (n, m)),
            pl.BlockSpec((w, LANES), lambda m, n, s: (prev(m, n), kcol + m)),
            pl.BlockSpec((w, LANES), lambda m, n, s: (n, kcol + m)),
            pl.BlockSpec((w, LANES), lambda m, n, s: (prev(m, n), vcol + m)),
            pl.BlockSpec((w, LANES), lambda m, n, s: (n, vcol + m)),
        ],
        out_specs=[
            pl.BlockSpec((w, qw), lambda m, n, s: (n, m)),
            pl.BlockSpec((2 * SWA_GROUP, w, LANES), lambda m, n, s: (m, n, 0)),
        ],
    )
    return pl.pallas_call(
        body, name=name, grid_spec=grid_spec,
        out_shape=[jax.ShapeDtypeStruct((t, hq * ATT_HEAD), BF16), jax.ShapeDtypeStruct((hq, t, LANES), F32)],
        compiler_params=_params((_PAR, _PAR)),
    )(sinks, qkv, qkv, qkv, qkv, qkv)


def swa_bwd(qkv, o, lse, do, sinks, hq, name):
    t = qkv.shape[0]
    w = SWA_WINDOW
    nb = t // w
    hkv = hq // SWA_GROUP
    npair = hkv // 2
    qw = 2 * SWA_GROUP * ATT_HEAD
    kcol = hq * ATT_HEAD // LANES
    vcol = kcol + npair
    scale = ATT_HEAD ** -0.5
    gh = 2 * SWA_GROUP

    def body(sink_ref, qc_ref, qn_ref, kp_ref, kc_ref, vp_ref, vc_ref, oc_ref, on_ref, dc_ref, dn_ref,
             lc_ref, ln_ref, dq_ref, dk_ref, dv_ref, ds_ref):
        m, n = pl.program_id(0), pl.program_id(1)
        kb = jnp.concatenate([kp_ref[...], kc_ref[...]], axis=0)
        vb = jnp.concatenate([vp_ref[...], vc_ref[...]], axis=0)
        ok_band = _swa_mask(n, w, False)
        ok_cur = _swa_mask(n, w, True)
        qi = lax.broadcasted_iota(jnp.int32, (w, w), 0)
        kj = lax.broadcasted_iota(jnp.int32, (w, w), 1)
        ok_next = (kj > qi) & (n < nb - 1)
        row16 = lax.broadcasted_iota(jnp.int32, (gh, LANES), 0)
        dsink = jnp.zeros((gh, LANES), F32)
        dk_tot = jnp.zeros((w, LANES), F32)
        dv_tot = jnp.zeros((w, LANES), F32)
        for kvh in range(2):
            kd, vd = _dup_head(kb, kvh), _dup_head(vb, kvh)
            kdc, vdc = kd[w:, :], vd[w:, :]
            acc_k = [jnp.zeros((w, LANES), F32), jnp.zeros((w, LANES), F32)]
            acc_v = [jnp.zeros((w, LANES), F32), jnp.zeros((w, LANES), F32)]
            for jj in range(SWA_GROUP // 2):
                jp = kvh * (SWA_GROUP // 2) + jj
                cols = slice(jp * LANES, (jp + 1) * LANES)
                dqs = []
                for a in range(2):
                    hd = jp * 2 + a
                    sink = sink_ref[m * gh + hd]
                    half = _half((w, LANES), a)
                    q2 = jnp.where(half, qc_ref[:, cols], jnp.zeros((w, LANES), BF16))
                    d2 = jnp.where(half, dc_ref[:, cols], jnp.zeros((w, LANES), BF16))
                    delta = jnp.sum(d2.astype(F32) * oc_ref[:, cols].astype(F32), axis=1, keepdims=True)
                    lse_c = lc_ref[hd][:, 0:1]
                    p = jnp.exp(jnp.where(ok_band, _dot_nt(q2, kd) * scale, NEG) - lse_c)
                    dsv = p * (_dot_nt(d2, vd) - delta)
                    dqs.append(_dot_nn(dsv.astype(BF16), kd) * scale)
                    psink = jnp.exp(sink - lse_c)
                    dsink = jnp.where(row16 == hd, dsink - jnp.sum(psink * delta, axis=0, keepdims=True), dsink)
                    for q_ref, d_ref, o_ref, l_ref, okm in ((qc_ref, dc_ref, oc_ref, lc_ref, ok_cur),
                                                           (qn_ref, dn_ref, on_ref, ln_ref, ok_next)):
                        q2 = jnp.where(half, q_ref[:, cols], jnp.zeros((w, LANES), BF16))
                        d2 = jnp.where(half, d_ref[:, cols], jnp.zeros((w, LANES), BF16))
                        delta = jnp.sum(d2.astype(F32) * o_ref[:, cols].astype(F32), axis=1, keepdims=True)
                        p = jnp.exp(jnp.where(okm, _dot_nt(q2, kdc) * scale, NEG) - l_ref[hd][:, 0:1])
                        dsv = p * (_dot_nt(d2, vdc) - delta)
                        acc_v[a] = acc_v[a] + _dot_tn(p.astype(BF16), d2)
                        acc_k[a] = acc_k[a] + _dot_tn(dsv.astype(BF16), q2) * scale
                dq_ref[:, cols] = jnp.where(_half((w, LANES), 0), dqs[0], dqs[1]).astype(BF16)
            dk_tot = dk_tot + acc_k[kvh] + pltpu.roll(acc_k[1 - kvh], ATT_HEAD, 1)
            dv_tot = dv_tot + acc_v[kvh] + pltpu.roll(acc_v[1 - kvh], ATT_HEAD, 1)
        dk_ref[...] = dk_tot.astype(BF16)
        dv_ref[...] = dv_tot.astype(BF16)
        ds_ref[0, 0] = dsink

    prev = lambda n: jnp.maximum(n - 1, 0)
    nxt = lambda n: jnp.minimum(n + 1, nb - 1)
    qspec = lambda f: pl.BlockSpec((w, qw), lambda m, n, s: (f(n), m))
    lspec = lambda f: pl.BlockSpec((gh, w, LANES), lambda m, n, s: (m, f(n), 0))
    same = lambda n: n
    grid_spec = pltpu.PrefetchScalarGridSpec(
        num_scalar_prefetch=1, grid=(npair, nb),
        in_specs=[
            qspec(same), qspec(nxt),
            pl.BlockSpec((w, LANES), lambda m, n, s: (prev(n), kcol + m)),
            pl.BlockSpec((w, LANES), lambda m, n, s: (n, kcol + m)),
            pl.BlockSpec((w, LANES), lambda m, n, s: (prev(n), vcol + m)),
            pl.BlockSpec((w, LANES), lambda m, n, s: (n, vcol + m)),
            qspec(same), qspec(nxt), qspec(same), qspec(nxt),
            lspec(same), lspec(nxt),
        ],
        out_specs=[
            pl.BlockSpec((w, qw), lambda m, n, s: (n, m)),
            pl.BlockSpec((w, LANES), lambda m, n, s: (n, m)),
            pl.BlockSpec((w, LANES), lambda m, n, s: (n, m)),
            pl.BlockSpec((1, 1, gh, LANES), lambda m, n, s: (m, n, 0, 0)),
        ],
    )
    return pl.pallas_call(
        body, name=name, grid_spec=grid_spec,
        out_shape=[jax.ShapeDtypeStruct((t, hq * ATT_HEAD), BF16),
                   jax.ShapeDtypeStruct((t, hkv * ATT_HEAD), BF16),
                   jax.ShapeDtypeStruct((t, hkv * ATT_HEAD), BF16),
                   jax.ShapeDtypeStruct((npair, nb, gh, LANES), F32)],
        compiler_params=_params((_PAR, _PAR)),
    )(sinks, qkv, qkv, qkv, qkv, qkv, qkv, o, o, do, do, lse, lse)


def swa_mixer_fwd(hn, w_in, w_out, sinks, positions, name):
    hq = sinks.shape[0]
    n_rot = (hq + hq // SWA_GROUP) * ATT_HEAD
    tabs = rope_tables(positions)
    proj = mm(hn, w_in, "nn", BF16, name + "_in")
    qkv = rope(proj, *tabs, n_rot, 1.0, name + "_rope")
    o, lse = swa_fwd(qkv, sinks, hq, name + "_att")
    y = mm(o, w_out, "nn", F32, name + "_out")
    return y, (qkv, o, lse)


def swa_mixer_bwd(dy, hn, w_in, w_out, sinks, positions, saved, name):
    qkv, o, lse = saved
    hq = sinks.shape[0]
    n_rot = (hq + hq // SWA_GROUP) * ATT_HEAD
    tabs = rope_tables(positions)
    dwo = wgrad(o, dy, name + "_dwo")
    do = mm(dy, w_out, "nt", BF16, name + "_do")
    dq, dk, dv, dsp = swa_bwd(qkv, o, lse, do, sinks, hq, name + "_attb")
    dproj = rope(jnp.concatenate([dq, dk, dv], axis=1), *tabs, n_rot, -1.0, name + "_ropeb")
    dwi = wgrad(hn, dproj, name + "_dwi")
    dhn = mm(dproj, w_in, "nt", BF16, name + "_dhn")
    dsinks = jnp.sum(dsp[:, :, :, 0], axis=1).reshape(hq)
    return dhn, dwi, dwo, dsinks


FOX_FPAD = 512


def _log_sigmoid(x):
    return jnp.minimum(x, 0.0) - jnp.log(1.0 + jnp.exp(-jnp.abs(x)))


def _tri(n, upper):
    r = lax.broadcasted_iota(jnp.int32, (n, n), 0)
    c = lax.broadcasted_iota(jnp.int32, (n, n), 1)
    return jnp.where((c >= r) if upper else (c <= r), 1.0, 0.0).astype(F32)


def _dot_exact(a, b):
    return jnp.dot(a, b, precision=lax.Precision.HIGHEST, preferred_element_type=F32)


def fox_cumsum(fl, b_pad, name):
    t = fl.shape[0]
    tr = _tile(t, (256, 128, 64, 32, 16, 8))

    def body(f_ref, b_ref, c_ref, carry_ref):
        @pl.when(pl.program_id(0) == 0)
        def _():
            carry_ref[...] = jnp.zeros_like(carry_ref)

        c = _dot_exact(_tri(tr, False), _log_sigmoid(f_ref[...] + b_ref[...])) + carry_ref[...]
        c_ref[...] = c
        carry_ref[...] = c[tr - 1:tr, :]

    return pl.pallas_call(
        body, name=name, grid=(t // tr,),
        in_specs=[_row_spec(tr, LANES), _full_spec((1, LANES))],
        out_specs=_row_spec(tr, LANES),
        out_shape=jax.ShapeDtypeStruct((t, LANES), F32),
        scratch_shapes=[pltpu.VMEM((1, LANES), F32)],
        compiler_params=_params((_ARB,)),
    )(fl, b_pad)


def fox_cumsum_bwd(dc, fl, b_pad, name):
    t = fl.shape[0]
    tr = _tile(t, (256, 128, 64, 32, 16, 8))
    nt = t // tr

    def body(d_ref, f_ref, b_ref, o_ref, db_ref, carry_ref):
        @pl.when(pl.program_id(0) == 0)
        def _():
            carry_ref[...] = jnp.zeros_like(carry_ref)
            db_ref[...] = jnp.zeros_like(db_ref)

        dlf = _dot_exact(_tri(tr, True), d_ref[...]) + carry_ref[...]
        carry_ref[...] = dlf[0:1, :]
        dfl = dlf * _sigmoid(-(f_ref[...] + b_ref[...]))
        o_ref[...] = dfl.astype(BF16)
        db_ref[...] += jnp.sum(dfl, axis=0, keepdims=True)

    rev = pl.BlockSpec((tr, LANES), lambda i: (nt - 1 - i, 0))
    return pl.pallas_call(
        body, name=name, grid=(nt,),
        in_specs=[rev, rev, _full_spec((1, LANES))],
        out_specs=[rev, _full_spec((1, LANES))],
        out_shape=[jax.ShapeDtypeStruct((t, LANES), BF16), jax.ShapeDtypeStruct((1, LANES), F32)],
        scratch_shapes=[pltpu.VMEM((1, LANES), F32)],
        compiler_params=_params((_ARB,)),
    )(dc, fl, b_pad)


def _fox_tile(t):
    return _tile(t, (256, 128))


AUG_C, AUG_ONE, AUG_LSE = ATT_HEAD, ATT_HEAD + 3, ATT_HEAD + 6


def _split3(x):
    hi = x.astype(BF16).astype(F32)
    mid = (x - hi).astype(BF16).astype(F32)
    return hi, mid, (x - hi - mid).astype(BF16).astype(F32)


def _aug(base, lane, entries):
    out = jnp.where(lane < ATT_HEAD, base, 0.0)
    for first, parts in entries:
        if parts is None:
            out = jnp.where((lane >= first) & (lane < first + 3), 1.0, out)
        else:
            for k, part in enumerate(parts):
                out = jnp.where(lane == first + k, part, out)
    return out.astype(BF16)


def _head_of_pair(x2, a):
    xf = x2.astype(F32)
    return xf if a == 0 else pltpu.roll(xf, ATT_HEAD, 1)


def fa_prep(proj, c, nh, name):
    t = proj.shape[0]
    npair = nh // 2
    tr = _tile(t, (256, 128))
    scale = ATT_HEAD ** -0.5

    def body(q_ref, k_ref, v_ref, c_ref, qa_ref, ka_ref, va_ref):
        lane = lax.broadcasted_iota(jnp.int32, (tr, LANES), 1)
        for p in range(npair):
            pc = slice(p * LANES, (p + 1) * LANES)
            for a in range(2):
                h = 2 * p + a
                hc = slice(h * LANES, (h + 1) * LANES)
                ch = c_ref[:, h:h + 1]
                qa_ref[:, hc] = _aug(_head_of_pair(q_ref[:, pc], a) * scale, lane,
                                     [(AUG_C, _split3(ch)), (AUG_ONE, None)])
                ka_ref[:, hc] = _aug(_head_of_pair(k_ref[:, pc], a), lane,
                                     [(AUG_C, None), (AUG_ONE, _split3(-ch)), (AUG_LSE, None)])
                va_ref[:, hc] = _aug(_head_of_pair(v_ref[:, pc], a), lane, [(AUG_C, None)])

    hd = nh * ATT_HEAD
    part = lambda k: pl.BlockSpec((tr, hd), lambda i: (i, k))
    out = pl.BlockSpec((tr, nh * LANES), lambda i: (i, 0))
    return pl.pallas_call(
        body, name=name, grid=(t // tr,),
        in_specs=[part(0), part(1), part(2), _row_spec(tr, LANES)],
        out_specs=[out, out, out],
        out_shape=[jax.ShapeDtypeStruct((t, nh * LANES), BF16)] * 3,
        compiler_params=_params((_PAR,)),
    )(proj, proj, proj, c)


def _diag_mask(n, transposed=False):
    r = lax.broadcasted_iota(jnp.int32, (n, n), 0)
    c = lax.broadcasted_iota(jnp.int32, (n, n), 1)
    return (r <= c) if transposed else (c <= r)


def fa_fwd(qa, ka, va, proj, nh, name):
    t = qa.shape[0]
    hd = nh * ATT_HEAD
    npair = nh // 2
    tq = _fox_tile(t)
    nt = t // tq
    gcol = (3 * hd + FOX_FPAD) // LANES

    def body(q_ref, k_ref, v_ref, g_ref, o_ref, og_ref, lse_ref):
        i = pl.program_id(1)
        heads = [slice(a * LANES, (a + 1) * LANES) for a in range(2)]
        qs = [q_ref[:, cols] for cols in heads]

        def tile(j, carry, masked):
            rows = pl.ds(pl.multiple_of(j * tq, tq), tq)
            out = []
            for (mx, acc), q, cols in zip(carry, qs, heads):
                s = _dot_nt(q, k_ref[rows, cols])
                if masked:
                    s = jnp.where(_diag_mask(tq), s, NEG)
                mnew = jnp.maximum(mx, jnp.max(s, axis=1, keepdims=True))
                p = jnp.exp(s - mnew).astype(BF16)
                out.append((mnew, jnp.exp(mx - mnew) * acc + _dot_nn(p, v_ref[rows, cols])))
            return tuple(out)

        init = ((jnp.full((tq, 1), NEG, F32), jnp.zeros((tq, LANES), F32)),) * 2
        carry = lax.fori_loop(0, i, functools.partial(tile, masked=False), init)
        outs = []
        for a, (mx, acc) in enumerate(tile(i, carry, True)):
            l = acc[:, AUG_C:AUG_C + 1]
            outs.append(acc / l)
            lse_ref[a] = jnp.broadcast_to(mx + jnp.log(l), (tq, LANES))
        o = jnp.where(_half((tq, LANES), 0), outs[0], pltpu.roll(outs[1], ATT_HEAD, 1))
        o_ref[...] = o.astype(BF16)
        og_ref[...] = (o * _sigmoid(g_ref[...].astype(F32))).astype(BF16)

    pair = pl.BlockSpec((tq, LANES), lambda p, i: (i, p))
    return pl.pallas_call(
        body, name=name, grid=(npair, nt),
        in_specs=[pl.BlockSpec((tq, 2 * LANES), lambda p, i: (i, p)),
                  pl.BlockSpec((t, 2 * LANES), lambda p, i: (0, p)),
                  pl.BlockSpec((t, 2 * LANES), lambda p, i: (0, p)),
                  pl.BlockSpec((tq, LANES), lambda p, i: (i, gcol + p))],
        out_specs=[pair, pair, pl.BlockSpec((2, tq, LANES), lambda p, i: (p, i, 0))],
        out_shape=[jax.ShapeDtypeStruct((t, hd), BF16), jax.ShapeDtypeStruct((t, hd), BF16),
                   jax.ShapeDtypeStruct((nh, t, LANES), F32)],
        compiler_params=_params((_PAR, _PAR)),
    )(qa, ka, va, proj)


def fa_prep_bwd(dog, o, proj, qa, lse, nh, name):
    t, hd = o.shape
    npair = nh // 2
    tr = _tile(t, (256, 128))
    gcol = (3 * hd + FOX_FPAD) // LANES

    def body(d_ref, o_ref, g_ref, q_ref, l_ref, dg_ref, qb_ref, da_ref):
        lane = lax.broadcasted_iota(jnp.int32, (tr, LANES), 1)
        dv, ov = d_ref[...].astype(F32), o_ref[...].astype(F32)
        sg = _sigmoid(g_ref[...].astype(F32))
        do = (dv * sg).astype(BF16).astype(F32)
        dg_ref[...] = (dv * ov * sg * (1.0 - sg)).astype(BF16)
        prod = do * ov
        for a in range(2):
            cols = slice(a * LANES, (a + 1) * LANES)
            delta = jnp.sum(jnp.where(_half(prod.shape, a), prod, 0.0), axis=1, keepdims=True)
            da_ref[:, cols] = _aug(_head_of_pair(do, a), lane, [(AUG_C, _split3(-delta))])
            nl = _split3(-l_ref[a][:, 0:1])
            qb = q_ref[:, cols].astype(F32)
            for k in range(3):
                qb = jnp.where(lane == AUG_LSE + k, nl[k], qb)
            qb_ref[:, cols] = qb.astype(BF16)

    pair = pl.BlockSpec((tr, LANES), lambda p, i: (i, p))
    wide = pl.BlockSpec((tr, 2 * LANES), lambda p, i: (i, p))
    return pl.pallas_call(
        body, name=name, grid=(npair, t // tr),
        in_specs=[pair, pair, pl.BlockSpec((tr, LANES), lambda p, i: (i, gcol + p)), wide,
                  pl.BlockSpec((2, tr, LANES), lambda p, i: (p, i, 0))],
        out_specs=[pair, wide, wide],
        out_shape=[jax.ShapeDtypeStruct((t, hd), BF16), jax.ShapeDtypeStruct((t, nh * LANES), BF16),
                   jax.ShapeDtypeStruct((t, nh * LANES), BF16)],
        compiler_params=_params((_PAR, _PAR)),
    )(dog, o, proj, qa, lse)


def fa_dq(qb, ka, va, da, nh, name):
    t = qb.shape[0]
    hd = nh * ATT_HEAD
    npair = nh // 2
    tq = _fox_tile(t)
    nt = t // tq
    scale = ATT_HEAD ** -0.5

    def body(q_ref, k_ref, v_ref, d_ref, dq_ref, rs_ref):
        i = pl.program_id(1)
        heads = [slice(a * LANES, (a + 1) * LANES) for a in range(2)]
        qs = [q_ref[:, cols] for cols in heads]
        ds = [d_ref[:, cols] for cols in heads]

        def tile(j, carry, masked):
            rows = pl.ds(pl.multiple_of(j * tq, tq), tq)
            out = []
            for acc, q, d, cols in zip(carry, qs, ds, heads):
                kj = k_ref[rows, cols]
                s = _dot_nt(q, kj)
                if masked:
                    s = jnp.where(_diag_mask(tq), s, NEG)
                dsv = jnp.exp(s) * _dot_nt(d, v_ref[rows, cols])
                out.append(acc + _dot_nn(dsv.astype(BF16), kj))
            return tuple(out)

        init = (jnp.zeros((tq, LANES), F32),) * 2
        accs = tile(i, lax.fori_loop(0, i, functools.partial(tile, masked=False), init), True)
        dq_ref[...] = (jnp.where(_half((tq, LANES), 0), accs[0], pltpu.roll(accs[1], ATT_HEAD, 1)) * scale).astype(BF16)
        lane = lax.broadcasted_iota(jnp.int32, (tq, LANES), 1)
        rs_ref[...] = jnp.where(lane == 0, accs[0][:, AUG_C:AUG_C + 1],
                                jnp.where(lane == 1, accs[1][:, AUG_C:AUG_C + 1], 0.0))

    wide = pl.BlockSpec((tq, 2 * LANES), lambda p, i: (i, p))
    resident = pl.BlockSpec((t, 2 * LANES), lambda p, i: (0, p))
    pair = pl.BlockSpec((tq, LANES), lambda p, i: (i, p))
    return pl.pallas_call(
        body, name=name, grid=(npair, nt),
        in_specs=[wide, resident, resident, wide],
        out_specs=[pair, pair],
        out_shape=[jax.ShapeDtypeStruct((t, hd), BF16), jax.ShapeDtypeStruct((t, npair * LANES), F32)],
        compiler_params=_params((_PAR, _PAR)),
    )(qb, ka, va, da)


def fa_dkv(qb, ka, va, da, nh, name):
    t = qb.shape[0]
    hd = nh * ATT_HEAD
    npair = nh // 2
    tk = _fox_tile(t)
    nt = t // tk

    def body(q_ref, k_ref, v_ref, d_ref, dk_ref, dv_ref, cs_ref):
        j = pl.program_id(1)
        heads = [slice(a * LANES, (a + 1) * LANES) for a in range(2)]
        ks = [k_ref[:, cols] for cols in heads]
        vs = [v_ref[:, cols] for cols in heads]

        def tile(i, carry, masked):
            rows = pl.ds(pl.multiple_of(i * tk, tk), tk)
            out = []
            for (dk, dv), k, v, cols in zip(carry, ks, vs, heads):
                qi, di = q_ref[rows, cols], d_ref[rows, cols]
                st = _dot_nt(k, qi)
                if masked:
                    st = jnp.where(_diag_mask(tk, True), st, NEG)
                pt = jnp.exp(st)
                dst = pt * _dot_nt(v, di)
                out.append((dk + _dot_nn(dst.astype(BF16), qi), dv + _dot_nn(pt.astype(BF16), di)))
            return tuple(out)

        zero = jnp.zeros((tk, LANES), F32)
        carry = tile(j, ((zero, zero),) * 2, True)
        carry = lax.fori_loop(j + 1, nt, functools.partial(tile, masked=False), carry)
        dks, dvs = [c[0] for c in carry], [c[1] for c in carry]
        first = _half((tk, LANES), 0)
        dk_ref[...] = jnp.where(first, dks[0], pltpu.roll(dks[1], ATT_HEAD, 1)).astype(BF16)
        dv_ref[...] = jnp.where(first, dvs[0], pltpu.roll(dvs[1], ATT_HEAD, 1)).astype(BF16)
        lane = lax.broadcasted_iota(jnp.int32, (tk, LANES), 1)
        cs_ref[...] = jnp.where(lane == 0, dks[0][:, AUG_ONE:AUG_ONE + 1],
                                jnp.where(lane == 1, dks[1][:, AUG_ONE:AUG_ONE + 1], 0.0))

    wide = pl.BlockSpec((tk, 2 * LANES), lambda p, j: (j, p))
    resident = pl.BlockSpec((t, 2 * LANES), lambda p, j: (0, p))
    pair = pl.BlockSpec((tk, LANES), lambda p, j: (j, p))
    return pl.pallas_call(
        body, name=name, grid=(npair, nt),
        in_specs=[resident, wide, wide, resident],
        out_specs=[pair, pair, pair],
        out_shape=[jax.ShapeDtypeStruct((t, hd), BF16), jax.ShapeDtypeStruct((t, hd), BF16),
                   jax.ShapeDtypeStruct((t, npair * LANES), F32)],
        compiler_params=_params((_PAR, _PAR)),
    )(qb, ka, va, da)


def fox_pad_w_in(w_in, nh):
    hd = nh * ATT_HEAD
    pad = jnp.zeros((w_in.shape[0], FOX_FPAD - nh), w_in.dtype)
    return jnp.concatenate([w_in[:, :3 * hd + nh], pad, w_in[:, 3 * hd + nh:]], axis=1)


def fox_unpad_dw(dw, nh):
    hd = nh * ATT_HEAD
    return jnp.concatenate([dw[:, :3 * hd + nh], dw[:, 3 * hd + FOX_FPAD:]], axis=1)


def _pad_lanes(v):
    return jnp.pad(v.reshape(1, -1).astype(F32), ((0, 0), (0, LANES - v.size)))


def fox_mixer_fwd(hn, w_pad, w_out, b_f, name):
    nh = b_f.shape[0]
    hd = nh * ATT_HEAD
    proj = mm(hn, w_pad, "nn", BF16, name + "_in")
    fl = mm(hn, w_pad[:, 3 * hd:3 * hd + LANES], "nn", F32, name + "_fl")
    c = fox_cumsum(fl, _pad_lanes(b_f), name + "_cum")
    qa, ka, va = fa_prep(proj, c, nh, name + "_prep")
    o, og, lse = fa_fwd(qa, ka, va, proj, nh, name + "_att")
    y = mm(og, w_out, "nn", F32, name + "_out")
    return y, (proj, fl, qa, ka, va, o, og, lse)


def fox_mixer_bwd(dy, hn, w_pad, w_out, b_f, saved, name):
    proj, fl, qa, ka, va, o, og, lse = saved
    nh = b_f.shape[0]
    t = hn.shape[0]
    dwo = wgrad(og, dy, name + "_dwo")
    dog = mm(dy, w_out, "nt", BF16, name + "_dog")
    dg, qb, da = fa_prep_bwd(dog, o, proj, qa, lse, nh, name + "_prepb")
    dq, rsum = fa_dq(qb, ka, va, da, nh, name + "_dq")
    dk, dv, csum = fa_dkv(qb, ka, va, da, nh, name + "_dkv")
    dc = (rsum - csum).reshape(t, nh // 2, LANES)[:, :, :2].reshape(t, nh)
    dc = jnp.pad(dc, ((0, 0), (0, LANES - nh)))
    dfl, db = fox_cumsum_bwd(dc, fl, _pad_lanes(b_f), name + "_cumb")
    dfl = jnp.pad(dfl, ((0, 0), (0, FOX_FPAD - LANES)))
    dproj = jnp.concatenate([dq, dk, dv, dfl, dg], axis=1)
    dwi = wgrad(hn, dproj, name + "_dwi")
    dhn = mm(dproj, w_pad, "nt", BF16, name + "_dhn")
    return dhn, dwi, dwo, db[0, :nh]


HGRN_ROWS = 256


def lb_table_row(lb_param, idx, name):
    nrow, w = lb_param.shape

    def body(p_ref, o_ref):
        rows = [p_ref[r:r + 1, :] for r in range(nrow)]
        mx = functools.reduce(jnp.maximum, rows)
        es = [jnp.exp(r - mx) for r in rows]
        o_ref[...] = sum(es[:idx + 1]) / sum(es)

    return pl.pallas_call(
        body, name=name, in_specs=[_full_spec((nrow, w))], out_specs=_full_spec((1, w)), grid=(1,),
        out_shape=jax.ShapeDtypeStruct((1, w), F32),
    )(lb_param)


def lb_table_row_bwd(lb_param, dlb, idx, name):
    nrow, w = lb_param.shape

    def body(p_ref, d_ref, o_ref):
        rows = [p_ref[r:r + 1, :] for r in range(nrow)]
        mx = functools.reduce(jnp.maximum, rows)
        es = [jnp.exp(r - mx) for r in rows]
        tot = sum(es)
        ps = [e / tot for e in es]
        dv = d_ref[...]
        inner = sum(ps[:idx + 1]) * dv
        for r in range(nrow):
            o_ref[r:r + 1, :] = ps[r] * ((dv if r <= idx else 0.0) - inner)

    return pl.pallas_call(
        body, name=name, in_specs=[_full_spec((nrow, w)), _full_spec((1, w))], out_specs=_full_spec((nrow, w)),
        grid=(1,), out_shape=jax.ShapeDtypeStruct((nrow, w), F32),
    )(lb_param, dlb)


def _hgrn_gates(qraw, fraw, lb):
    sq = _sigmoid(qraw)
    sf = _sigmoid(fraw)
    f = lb + (1.0 - lb) * sf
    return qraw * sq, sq, sf, f, 1.0 - f


def _hgrn_chunk(q, k, f):
    c = HGRN_CHUNK
    b = _dot_exact(_tri(c, False), jnp.log(f))
    bl = b[c - 1:c, :]
    bm = b[c // 2 - 1:c // 2, :]
    eq, ek = jnp.exp(b - bm), jnp.exp(bm - b)
    eb, el = jnp.exp(b), jnp.exp(bl - b)
    qt, kt = (q * eq).astype(BF16), (k * ek).astype(BF16)
    causal = _tri(c, False) > 0.5
    amat = jnp.where(causal, _dot_nt(qt, kt), 0.0).astype(BF16)
    return amat, qt, kt, (q * eb).astype(BF16), (k * el).astype(BF16), eq, ek, eb, el, jnp.exp(bl), causal


def hgrn_fwd(proj, lb, norm_g, nh, name):
    t = proj.shape[0]
    w = nh * HGRN_HEAD
    c = HGRN_CHUNK
    rows = _tile(t, (HGRN_ROWS, 128, 64, 32))
    nr, nc = t // rows, rows // c

    def body(q_ref, f_ref, i_ref, g_ref, lb_ref, ng_ref, og_ref, o_ref, st_ref, state):
        @pl.when(pl.program_id(1) == 0)
        def _():
            state[...] = jnp.zeros_like(state)

        def step(cc, carry):
            sl = pl.ds(pl.multiple_of(cc * c, c), c)
            q, _, _, f, k = _hgrn_gates(q_ref[sl, :].astype(F32), f_ref[sl, :].astype(F32), lb_ref[...])
            v = i_ref[sl, :]
            amat, _, _, qd, kd, _, _, _, _, ebl, _ = _hgrn_chunk(q, k, f)
            st = state[...]
            st_ref[0, cc] = st.astype(BF16)
            o = _dot_nt(qd, st.astype(BF16)) + _dot_nn(amat, v)
            state[...] = st * ebl + _dot_tn(v, kd)
            o_ref[sl, :] = o
            graw = g_ref[sl, :].astype(F32)
            og_ref[sl, :] = (o * _rstd(o) * ng_ref[...] * (graw * _sigmoid(graw))).astype(BF16)
            return carry

        lax.fori_loop(0, nc, step, 0)

    col = lambda off: pl.BlockSpec((rows, HGRN_HEAD), lambda h, r: (r, off + h))
    return pl.pallas_call(
        body, name=name, grid=(nh, nr),
        in_specs=[col(0), col(nh), col(2 * nh), col(3 * nh),
                  pl.BlockSpec((1, HGRN_HEAD), lambda h, r: (0, h)), _full_spec((1, HGRN_HEAD))],
        out_specs=[col(0), col(0), pl.BlockSpec((1, nc, HGRN_HEAD, HGRN_HEAD), lambda h, r: (h, r, 0, 0))],
        out_shape=[jax.ShapeDtypeStruct((t, w), BF16), jax.ShapeDtypeStruct((t, w), F32),
                   jax.ShapeDtypeStruct((nh, t // c, HGRN_HEAD, HGRN_HEAD), BF16)],
        scratch_shapes=[pltpu.VMEM((HGRN_HEAD, HGRN_HEAD), F32)],
        compiler_params=_params((_PAR, _ARB)),
    )(proj, proj, proj, proj, lb, norm_g)


def hgrn_bwd(proj, lb, norm_g, o, states, dog, nh, name):
    t = proj.shape[0]
    w = nh * HGRN_HEAD
    c = HGRN_CHUNK
    rows = _tile(t, (HGRN_ROWS, 128, 64, 32))
    nr, nc = t // rows, rows // c

    def body(q_ref, f_ref, i_ref, g_ref, lb_ref, ng_ref, o_ref, st_ref, dog_ref,
             dq_ref, df_ref, di_ref, dg_ref, dlb_ref, dng_ref, dstate):
        @pl.when(pl.program_id(1) == 0)
        def _():
            dstate[...] = jnp.zeros_like(dstate)
            dlb_ref[...] = jnp.zeros_like(dlb_ref)
            dng_ref[...] = jnp.zeros_like(dng_ref)

        lb = lb_ref[...]
        ng = ng_ref[...]

        def step(idx, carry):
            cc = nc - 1 - idx
            sl = pl.ds(pl.multiple_of(cc * c, c), c)
            qraw, fraw = q_ref[sl, :].astype(F32), f_ref[sl, :].astype(F32)
            q, sq, sf, f, k = _hgrn_gates(qraw, fraw, lb)
            v = i_ref[sl, :]
            amat, qt, kt, qd, kd, eq, ek, eb, el, ebl, causal = _hgrn_chunk(q, k, f)
            ov = o_ref[sl, :]
            graw = g_ref[sl, :].astype(F32)
            dogv = dog_ref[sl, :].astype(F32)
            sil, dsil = _silu_and_grad(graw)
            r = _rstd(ov)
            oh = ov * r
            don = dogv * sil
            dg_ref[sl, :] = (dogv * oh * ng * dsil).astype(BF16)
            dng_ref[0] += jnp.sum(don * oh, axis=0, keepdims=True)
            doh = don * ng
            do = (r * (doh - oh * jnp.mean(doh * oh, axis=1, keepdims=True))).astype(BF16)
            dst = dstate[...]
            dstb = dst.astype(BF16)
            da = jnp.where(causal, _dot_nt(do, v), 0.0).astype(BF16)
            dv = _dot_tn(amat, do) + _dot_nt(kd, dstb)
            st0 = st_ref[0, cc]
            dq = _dot_nn(da, kt) * eq + _dot_nn(do, st0) * eb
            dk_inter = _dot_nn(v, dstb) * el
            dk = _dot_tn(da, qt) * ek + dk_inter
            dstate[...] = dst * ebl + _dot_tn(do, qd)
            through = jnp.sum(dst * st0.astype(F32), axis=0, keepdims=True) * ebl
            later = jnp.sum(k * dk_inter, axis=0, keepdims=True) + through
            dlf = _dot_exact(_tri(c, True), q * dq - k * dk) + later
            df = dlf / f - dk
            dq_ref[sl, :] = (dq * (sq * (1.0 + qraw * (1.0 - sq)))).astype(BF16)
            df_ref[sl, :] = (df * (1.0 - lb) * sf * (1.0 - sf)).astype(BF16)
            di_ref[sl, :] = dv.astype(BF16)
            dlb_ref[...] += jnp.sum(df * (1.0 - sf), axis=0, keepdims=True)
            return carry

        lax.fori_loop(0, nc, step, 0)

    col = lambda off: pl.BlockSpec((rows, HGRN_HEAD), lambda h, r: (nr - 1 - r, off + h))
    out = col(0)
    return pl.pallas_call(
        body, name=name, grid=(nh, nr),
        in_specs=[col(0), col(nh), col(2 * nh), col(3 * nh),
                  pl.BlockSpec((1, HGRN_HEAD), lambda h, r: (0, h)), _full_spec((1, HGRN_HEAD)),
                  out, pl.BlockSpec((1, nc, HGRN_HEAD, HGRN_HEAD), lambda h, r: (h, nr - 1 - r, 0, 0)), out],
        out_specs=[out, out, out, out, pl.BlockSpec((1, HGRN_HEAD), lambda h, r: (0, h)),
                   pl.BlockSpec((1, 1, HGRN_HEAD), lambda h, r: (h, 0, 0))],
        out_shape=[jax.ShapeDtypeStruct((t, w), BF16)] * 4 + [jax.ShapeDtypeStruct((1, w), F32),
                                                             jax.ShapeDtypeStruct((nh, 1, HGRN_HEAD), F32)],
        scratch_shapes=[pltpu.VMEM((HGRN_HEAD, HGRN_HEAD), F32)],
        compiler_params=_params((_PAR, _ARB)),
    )(proj, proj, proj, proj, lb, norm_g, o, states, dog)


def hgrn_mixer_fwd(hn, w_in, w_out, norm_g, lb_param, idx, name):
    nh = w_out.shape[0] // HGRN_HEAD
    lb = lb_table_row(lb_param, idx, name + "_lb")
    proj = mm(hn, w_in, "nn", BF16, name + "_in")
    og, o, states = hgrn_fwd(proj, lb, norm_g, nh, name + "_rec")
    y = mm(og, w_out, "nn", F32, name + "_out")
    return y, (proj, lb, og, o, states)


def hgrn_mixer_bwd(dy, hn, w_in, w_out, norm_g, lb_param, idx, saved, name):
    proj, lb, og, o, states = saved
    nh = w_out.shape[0] // HGRN_HEAD
    dwo = wgrad(og, dy, name + "_dwo")
    dog = mm(dy, w_out, "nt", BF16, name + "_dog")
    dq, df, di, dg, dlb, dng = hgrn_bwd(proj, lb, norm_g, o, states, dog, nh, name + "_recb")
    dproj = jnp.concatenate([dq, df, di, dg], axis=1)
    dwi = wgrad(hn, dproj, name + "_dwi")
    dhn = mm(dproj, w_in, "nt", BF16, name + "_dhn")
    dlbp = lb_table_row_bwd(lb_param, dlb, idx, name + "_lbb")
    return dhn, dwi, dwo, jnp.sum(dng, axis=0), dlbp


_ANY = pl.BlockSpec(memory_space=pl.ANY)
_MESH = pl.DeviceIdType.MESH


def _place():
    x, y, c = lax.axis_index("x"), lax.axis_index("y"), lax.axis_index("c")
    chips = [(1 - x, y), (x, 1 - y), (1 - x, 1 - y)]
    return x, y, c, N_CHIPS // 2 * x + y, chips


def _chip_index(chip):
    return N_CHIPS // 2 * chip[0] + chip[1]


def _window(ref, axis, start, size):
    idx = [slice(None)] * len(ref.shape)
    idx[axis] = pl.ds(start, size)
    return ref.at[tuple(idx)]


_HBM = pl.BlockSpec(memory_space=pltpu.HBM)
_SEMS = pl.BlockSpec(memory_space=pltpu.SEMAPHORE)
_DATAFLOW = pltpu.SideEffectType.DATAFLOW_SIDE_EFFECTING


def _exchange_copy(src, land, axis, sems, k, j, chip, c, sender_side):
    x, y, _, me, _ = _place()
    peer = _chip_index(chip)
    if axis is None:
        src_part = src.at[peer]
        land_part = land.at[me if sender_side else peer]
    else:
        size = src.shape[axis]
        src_part = src
        land_part = _window(land, axis, (me if sender_side else peer) * size, size)
    which = k * (N_CHIPS - 1) + j
    return pltpu.make_async_remote_copy(src_ref=src_part, dst_ref=land_part, send_sem=sems[0].at[which],
                                        recv_sem=sems[1].at[which], device_id=(chip[0], chip[1], c),
                                        device_id_type=_MESH)


def place_own(shard, axis):
    _, _, _, me, _ = _place()
    shape = list(shard.shape)
    shape[axis] *= N_CHIPS
    return lax.dynamic_update_slice_in_dim(lax.empty(tuple(shape), shard.dtype), shard, me * shard.shape[axis], axis)


def place_own_slot(p):
    _, _, _, me, _ = _place()
    mine = lax.dynamic_index_in_dim(p, me, 0, keepdims=True)
    return lax.dynamic_update_slice_in_dim(lax.empty(p.shape, p.dtype), mine, me, 0)


def exchange_start(srcs, lands, axes, name):
    n = len(srcs)

    def body(*refs):
        ins, lnd, sems = refs[:n], refs[n:2 * n], refs[2 * n:2 * n + 2]
        _, _, c, _, chips = _place()
        for k in range(n):
            for j, chip in enumerate(chips):
                _exchange_copy(ins[k], lnd[k], axes[k], sems, k, j, chip, c, True).start()
        refs[-1][...] = jnp.zeros_like(refs[-1])

    sem = pltpu.SemaphoreType.DMA((n * (N_CHIPS - 1),))
    arrays = list(srcs) + list(lands)
    out = pl.pallas_call(
        body, name=name,
        in_specs=[_HBM] * (2 * n),
        out_specs=(_SEMS, _SEMS) + (_HBM,) * (2 * n) + (pl.BlockSpec(memory_space=pltpu.VMEM),),
        out_shape=(sem, sem) + tuple(pltpu.HBM(a.shape, a.dtype) for a in arrays)
        + (jax.ShapeDtypeStruct((8, LANES), F32),),
        input_output_aliases={i: 2 + i for i in range(2 * n)},
        compiler_params=pltpu.CompilerParams(has_side_effects=_DATAFLOW),
    )(*[pltpu.with_memory_space_constraint(a, pltpu.HBM) for a in arrays])
    return (out[0], out[1], list(out[2:2 + n]), list(out[2 + n:2 + 2 * n]), list(axes)), out[-1]


def exchange_wait(started, ks, after, name):
    send, recv, srcs, lands, axes = started
    m = len(ks)

    def body(*refs):
        ins, lnd, sems = refs[:m], refs[m:2 * m], refs[2 * m:2 * m + 2]
        _, _, c, _, chips = _place()
        for q, k in enumerate(ks):
            for j, chip in enumerate(chips):
                cp = _exchange_copy(ins[q], lnd[q], axes[k], sems, k, j, chip, c, False)
                cp.wait_send()
                cp.wait_recv()

    arrays = [srcs[k] for k in ks] + [lands[k] for k in ks]
    out = pl.pallas_call(
        body, name=name,
        in_specs=[_HBM] * (2 * m) + [_SEMS, _SEMS, _ANY],
        out_specs=(_HBM,) * (2 * m),
        out_shape=tuple(pltpu.HBM(a.shape, a.dtype) for a in arrays),
        input_output_aliases={i: i for i in range(2 * m)},
        compiler_params=pltpu.CompilerParams(has_side_effects=_DATAFLOW),
    )(*arrays, send, recv, after)
    return list(out[m:])


def allreduce_small(buf, name):
    rows = buf.shape[0]
    n_dev = 2 * N_CHIPS

    def body(in_ref, out_ref, slots, send, recv):
        x, y, c, me, chips = _place()
        my_id = 2 * me + c
        slots[my_id] = in_ref[...]
        for j in range(1, n_dev):
            fx, fy, fc = (j >> 2) & 1, (j >> 1) & 1, j & 1
            peer = ((1 - x) if fx else x, (1 - y) if fy else y, (1 - c) if fc else c)
            pltpu.make_async_remote_copy(
                src_ref=in_ref, dst_ref=slots.at[my_id], send_sem=send.at[j], recv_sem=recv.at[j],
                device_id=peer, device_id_type=_MESH).start()
        for j in range(1, n_dev):
            fx, fy, fc = (j >> 2) & 1, (j >> 1) & 1, j & 1
            peer = ((1 - x) if fx else x, (1 - y) if fy else y, (1 - c) if fc else c)
            peer_id = 2 * _chip_index(peer) + peer[2]
            landed = pltpu.make_async_remote_copy(
                src_ref=in_ref, dst_ref=slots.at[peer_id], send_sem=send.at[j], recv_sem=recv.at[j],
                device_id=peer, device_id_type=_MESH)
            landed.wait_recv()
            landed.wait_send()
        tot = slots[0]
        for d in range(1, n_dev):
            tot = tot + slots[d]
        out_ref[...] = tot

    return pl.pallas_call(
        body, name=name,
        in_specs=[pl.BlockSpec(memory_space=pltpu.VMEM)], out_specs=pl.BlockSpec(memory_space=pltpu.VMEM),
        out_shape=jax.ShapeDtypeStruct(buf.shape, F32),
        scratch_shapes=[pltpu.VMEM((n_dev, rows, LANES), F32), pltpu.SemaphoreType.DMA((n_dev,)),
                        pltpu.SemaphoreType.DMA((n_dev,))],
        compiler_params=pltpu.CompilerParams(vmem_limit_bytes=VMEM_LIMIT),
    )(buf)


def _sibling_step(src_ref, slots, send, recv, credit, step, n_steps):
    x, y, c = lax.axis_index("x"), lax.axis_index("y"), lax.axis_index("c")
    slot = step % 2

    @pl.when(step >= 2)
    def _():
        pl.semaphore_wait(credit, 1)

    cp = pltpu.make_async_remote_copy(src_ref=src_ref, dst_ref=slots.at[slot], send_sem=send.at[slot],
                                      recv_sem=recv.at[slot], device_id=(x, y, 1 - c), device_id_type=_MESH)
    cp.start()
    cp.wait_recv()
    return cp, slot


def _sibling_done(cp, credit, step, n_steps):
    x, y, c = lax.axis_index("x"), lax.axis_index("y"), lax.axis_index("c")
    cp.wait_send()

    @pl.when(step < n_steps - 2)
    def _():
        pl.semaphore_signal(credit, 1, device_id=(x, y, 1 - c), device_id_type=_MESH)


def pair_sum(g, name):
    by_cols = g.ndim == 2
    s = N_CHIPS if by_cols else g.shape[0]
    r = g.shape[-2]
    cols = g.shape[-1] // s if by_cols else g.shape[-1]
    half = r // 2
    tr = _row_tile(half, cols, 1 << 19)
    nt = half // tr
    n_steps = s * nt

    def body(g_ref, o_ref, slots, send, recv, credit):
        c = lax.axis_index("c")
        step = pl.program_id(0) * nt + pl.program_id(1)
        cp, slot = _sibling_step(g_ref.at[0, 1 - c], slots, send, recv, credit, step, n_steps)
        o_ref[0] = (g_ref[0, c].astype(F32) + slots[slot].astype(F32)).astype(BF16)
        _sibling_done(cp, credit, step, n_steps)

    if by_cols:
        in_spec = pl.BlockSpec((1, 2, tr, cols), lambda k, i: (0, 0, i, k))
        g4 = g.reshape(1, 2, half, s * cols)
    else:
        in_spec = pl.BlockSpec((1, 2, tr, cols), lambda k, i: (k, 0, i, 0))
        g4 = g.reshape(s, 2, half, cols)
    return pl.pallas_call(
        body, name=name, grid=(s, nt),
        in_specs=[in_spec],
        out_specs=pl.BlockSpec((1, tr, cols), lambda k, i: (k, i, 0)),
        out_shape=jax.ShapeDtypeStruct((s, half, cols), BF16),
        scratch_shapes=[pltpu.VMEM((2, tr, cols), BF16), pltpu.SemaphoreType.DMA((2,)), pltpu.SemaphoreType.DMA((2,)),
                        pltpu.SemaphoreType.REGULAR],
        compiler_params=_params((_ARB, _ARB)),
    )(g4)


def chip_sum_share(q, acc, layer, name):
    s, r2, cols = q.shape
    tr = _row_tile(r2, cols, 1 << 18)
    nt = r2 // tr

    def body(q_ref, acc_ref, o_ref, slots, send, recv, credit):
        c = lax.axis_index("c")
        step = pl.program_id(0)
        tot = q_ref[0].astype(F32)
        for k in range(1, s):
            tot = tot + q_ref[k].astype(F32)
        o_ref[0, c] = tot
        cp, slot = _sibling_step(o_ref.at[0, c], slots, send, recv, credit, step, nt)
        o_ref[0, 1 - c] = slots[slot]
        _sibling_done(cp, credit, step, nt)

    return pl.pallas_call(
        body, name=name, grid=(nt,),
        in_specs=[pl.BlockSpec((s, tr, cols), lambda i: (0, i, 0)), _ANY],
        out_specs=pl.BlockSpec((1, 2, tr, cols), lambda i: (layer, 0, i, 0)),
        out_shape=jax.ShapeDtypeStruct(acc.shape, F32),
        input_output_aliases={1: 0},
        scratch_shapes=[pltpu.VMEM((2, tr, cols), F32), pltpu.SemaphoreType.DMA((2,)), pltpu.SemaphoreType.DMA((2,)),
                        pltpu.SemaphoreType.REGULAR],
        compiler_params=_params((_ARB,)),
    )(q, acc)


def _row_tile(rows, cols, budget):
    for tr in (1024, 512, 256, 128, 64, 32, 16, 8):
        if rows % tr == 0 and tr * cols <= budget:
            return tr
    return rows


def adamw(w, g, m, v, name):
    rows, cols = w.shape
    tr = _row_tile(rows, cols, 1 << 18)
    c1 = 1.0 - ADAM_B1 ** ADAM_STEP
    c2 = 1.0 - ADAM_B2 ** ADAM_STEP

    def body(w_ref, g_ref, m_ref, v_ref, d_ref, nm_ref, nv_ref):
        gv = g_ref[...]
        nm = ADAM_B1 * m_ref[...] + (1.0 - ADAM_B1) * gv
        nv = ADAM_B2 * v_ref[...] + (1.0 - ADAM_B2) * (gv * gv)
        nm_ref[...] = nm
        nv_ref[...] = nv
        d_ref[...] = -ADAM_LR * ((nm / c1) / (jnp.sqrt(nv / c2) + ADAM_EPS) + ADAM_WD * w_ref[...])

    spec = pl.BlockSpec((tr, cols), lambda i: (i, 0))
    return pl.pallas_call(
        body, name=name, grid=(rows // tr,),
        in_specs=[spec] * 4, out_specs=[spec] * 3,
        out_shape=[jax.ShapeDtypeStruct((rows, cols), F32)] * 3,
        compiler_params=_params((_PAR,)),
    )(w, g, m, v)


WEIGHTS = ("mix_pre_g", "mix_post_g", "ffn_pre_g", "ffn_post_g", "hgrn_w_in", "hgrn_w_out", "hgrn_norm_g",
           "hgrn_lb_param", "swa_w_in", "swa_w_out", "swa_sinks", "sc_w_in", "sc_conv_w", "sc_w_out", "fox_w_in",
           "fox_b_f", "fox_w_out", "ffn_w_up", "ffn_conv_w", "ffn_conv_b", "ffn_w_down")
N_MIXERS = 4


def _pack_small(parts):
    flat = jnp.concatenate([p.reshape(-1).astype(F32) for p in parts])
    rows = -(-flat.shape[0] // (8 * LANES)) * 8
    return jnp.pad(flat, (0, rows * LANES - flat.shape[0])).reshape(rows, LANES)


def _unpack_small(buf, shapes):
    flat, out, off = buf.reshape(-1), [], 0
    for s in shapes:
        n = math.prod(s)
        out.append(flat[off:off + n].reshape(s))
        off += n
    return out


def _stack_rows(dw):
    return dw.reshape(N_CHIPS, dw.shape[0] // N_CHIPS, dw.shape[1])


def kernel(x, positions, mix_pre_g, mix_post_g, ffn_pre_g, ffn_post_g, hgrn_w_in, hgrn_w_out, hgrn_norm_g, hgrn_lb_param, swa_w_in, swa_w_out, swa_sinks, sc_w_in, sc_conv_w, sc_w_out, fox_w_in, fox_b_f, fox_w_out, ffn_w_up, ffn_conv_w, ffn_conv_b, ffn_w_down, loss_target, m_mix_pre_g, m_mix_post_g, m_ffn_pre_g, m_ffn_post_g, m_hgrn_w_in, m_hgrn_w_out, m_hgrn_norm_g, m_hgrn_lb_param, m_swa_w_in, m_swa_w_out, m_swa_sinks, m_sc_w_in, m_sc_conv_w, m_sc_w_out, m_fox_w_in, m_fox_b_f, m_fox_w_out, m_ffn_w_up, m_ffn_conv_w, m_ffn_conv_b, m_ffn_w_down, v_mix_pre_g, v_mix_post_g, v_ffn_pre_g, v_ffn_post_g, v_hgrn_w_in, v_hgrn_w_out, v_hgrn_norm_g, v_hgrn_lb_param, v_swa_w_in, v_swa_w_out, v_swa_sinks, v_sc_w_in, v_sc_conv_w, v_sc_w_out, v_fox_w_in, v_fox_b_f, v_fox_w_out, v_ffn_w_up, v_ffn_conv_w, v_ffn_conv_b, v_ffn_w_down):
    given = dict(locals())
    depth = mix_pre_g.shape[0]
    assert depth == N_MIXERS and x.shape[0] == 1, "one batch element per device, one layer of each mixer"
    xi, target = x[0], loss_target[0]
    chip = N_CHIPS // 2 * lax.axis_index("x") + lax.axis_index("y")
    nh_fox = fox_b_f.shape[1]
    row = lambda a, i: a[i:i + 1]

    bf = lambda a: a.astype(BF16)
    units = {"hg_in": (bf(hgrn_w_in[0]), 1), "hg_out": (bf(hgrn_w_out[0]), 0), "f_cw": (ffn_conv_w, 2),
             "sw_in": (bf(swa_w_in[0]), 1), "sw_out": (bf(swa_w_out[0]), 0),
             "sc_in": (bf(sc_w_in[0]), 1), "sc_out": (bf(sc_w_out[0]), 0), "sc_cw": (sc_conv_w[0], 1),
             "fx_in": (bf(fox_w_in), 0), "fx_out": (bf(fox_w_out[0]), 0)}
    mix_units = (("hg_in", "hg_out"), ("sw_in", "sw_out"), ("sc_in", "sc_out", "sc_cw"), ("fx_in", "fx_out"))
    ffn_units = []
    for i in range(depth):
        units[f"up{i}"], units[f"down{i}"] = (bf(ffn_w_up[i]), 1), (bf(ffn_w_down[i]), 0)
        ffn_units.append((f"up{i}", f"down{i}") + (("f_cw",) if i == 0 else ()))
    order = [n for i in range(depth) for n in mix_units[i] + ffn_units[i]]
    axes = [units[n][1] for n in order]
    gather, _ = exchange_start([units[n][0] for n in order], [place_own(*units[n]) for n in order], axes,
                               "gather_start")
    wt = {}

    def arrive(names, after, name):
        wt.update(zip(names, exchange_wait(gather, [order.index(n) for n in names], after, name)))

    saved = []
    xs = xi
    hn = rms_fwd(xs, row(mix_pre_g, 0), "pre_norm0")
    dx = loss = None
    for i in range(depth):
        nm = f"l{i}"
        arrive(mix_units[i], hn, nm + "_w_mix")
        if i == 0:
            y, sv = hgrn_mixer_fwd(hn, wt["hg_in"], wt["hg_out"], hgrn_norm_g, hgrn_lb_param, i, nm + "_hgrn")
        elif i == 1:
            y, sv = swa_mixer_fwd(hn, wt["sw_in"], wt["sw_out"], swa_sinks[0], positions, nm + "_swa")
        elif i == 2:
            proj = mm(hn, wt["sc_in"], "nn", BF16, nm + "_sc_in")
            yb = sconv_fwd(proj, wt["sc_cw"], nm + "_sc_conv")
            y, sv = mm(yb, wt["sc_out"], "nn", F32, nm + "_sc_out"), (proj, yb)
        else:
            wt["fx_pad"] = fox_pad_w_in(jnp.concatenate([wt["fx_in"][s] for s in range(N_CHIPS)], axis=1), nh_fox)
            y, sv = fox_mixer_fwd(hn, wt["fx_pad"], wt["fx_out"], fox_b_f[0], nm + "_fox")
        x1, hn2 = resid_norm(xs, y, row(mix_post_g, i), row(ffn_pre_g, i), nm + "_mix_resid")
        arrive(ffn_units[i], hn2, nm + "_w_ffn")
        z = mm(hn2, wt[f"up{i}"], "nn", BF16, nm + "_ffn_up")
        a = ffn_act(z, wt["f_cw"][i], row(ffn_conv_b, i), nm + "_ffn_act")
        y2 = mm(a, wt[f"down{i}"], "nn", F32, nm + "_ffn_down")
        saved.append((xs, hn, y, sv, x1, hn2, z, a, y2))
        if i < depth - 1:
            xs, hn = resid_norm(x1, y2, row(ffn_post_g, i), row(mix_pre_g, i + 1), nm + "_ffn_resid")
        else:
            dx, loss = resid_loss(x1, y2, row(ffn_post_g, i), target, nm + "_loss")

    grads = {}

    def start_reduce(tag, named):
        ps = [pair_sum(g, f"{tag}_pair_{n}") for n, _, g in named]
        started, token = exchange_start(ps, [place_own_slot(p) for p in ps], [None] * len(ps), tag + "_chips_start")
        return (named, started), token

    def finish_reduce(tag, pending, after):
        named, started = pending
        qs = exchange_wait(started, list(range(len(named))), after, tag + "_chips_wait")
        for (n, l, _), q in zip(named, qs):
            if n not in grads:
                grads[n] = lax.empty((given[n].shape[0], 2) + q.shape[1:], F32)
            grads[n] = chip_sum_share(q, grads[n], l, f"{tag}_share_{n}")

    d_pre, d_post, d_fpre, d_fpost = [None] * depth, [None] * depth, [None] * depth, [None] * depth
    d_fcw, d_fcb = [None] * depth, [None] * depth
    small = {}
    pending = token = None
    for i in reversed(range(depth)):
        nm = f"l{i}b"
        xs, hn, y, sv, x1, hn2, z, a, y2 = saved[i]
        f_cw = wt["f_cw"][i]
        dy2, d_fpost[i] = norm_bwd(y2, row(ffn_post_g, i), dx, None, BF16, nm + "_ffn_post", after=token)
        d_down = _stack_rows(wgrad(a, dy2, nm + "_dw_down"))
        da = mm(dy2, wt[f"down{i}"], "nt", BF16, nm + "_da")
        du, acc = ffn_act_bwd(z, da, f_cw, row(ffn_conv_b, i), nm + "_ffn_actb")
        d_fcw[i], d_fcb[i] = acc[0:CONV_WIDTH], acc[CONV_WIDTH]
        dz = conv_transpose(du, f_cw, nm + "_ffn_convT")
        d_up = wgrad(hn2, dz, nm + "_dw_up")
        dhn2 = mm(dz, wt[f"up{i}"], "nt", BF16, nm + "_dhn2")
        dx1, d_fpre[i] = norm_bwd(x1, row(ffn_pre_g, i), dhn2, dx, F32, nm + "_ffn_pre")
        dy, d_post[i] = norm_bwd(y, row(mix_post_g, i), dx1, None, BF16, nm + "_mix_post")
        if i == 0:
            dhn, dwi, dwo, small["hgrn_norm_g"], small["hgrn_lb_param"] = hgrn_mixer_bwd(
                dy, hn, wt["hg_in"], wt["hg_out"], hgrn_norm_g, hgrn_lb_param, i, sv, nm + "_hgrn")
            w_in, w_out = "hgrn_w_in", "hgrn_w_out"
        elif i == 1:
            dhn, dwi, dwo, small["swa_sinks"] = swa_mixer_bwd(dy, hn, wt["sw_in"], wt["sw_out"], swa_sinks[0],
                                                             positions, sv, nm + "_swa")
            w_in, w_out = "swa_w_in", "swa_w_out"
        elif i == 2:
            proj, yb = sv
            dwo = wgrad(yb, dy, nm + "_sc_dwo")
            dyb = mm(dy, wt["sc_out"], "nt", BF16, nm + "_sc_dyb")
            dproj, acc = sconv_bwd(proj, dyb, wt["sc_cw"], nm + "_sc_convb")
            dwi = wgrad(hn, dproj, nm + "_sc_dwi")
            dhn = mm(dproj, wt["sc_in"], "nt", BF16, nm + "_sc_dhn")
            small["sc_conv_w"] = acc[0:CONV_WIDTH]
            w_in, w_out = "sc_w_in", "sc_w_out"
        else:
            dhn, dwi, dwo, small["fox_b_f"] = fox_mixer_bwd(dy, hn, wt["fx_pad"], wt["fx_out"], fox_b_f[0], sv,
                                                            nm + "_fox")
            dwi = fox_unpad_dw(dwi, nh_fox)
            cols = dwi.shape[1] // N_CHIPS
            dwi = jnp.stack([dwi[:, s * cols:(s + 1) * cols] for s in range(N_CHIPS)])
            w_in, w_out = "fox_w_in", "fox_w_out"
        dx, d_pre[i] = norm_bwd(xs, row(mix_pre_g, i), dhn, dx1, F32, nm + "_mix_pre")
        if pending is not None:
            finish_reduce(f"l{i + 1}b", pending, dx)
        pending, token = start_reduce(nm, [(w_in, 0, dwi), (w_out, 0, _stack_rows(dwo)), ("ffn_w_up", i, d_up),
                                           ("ffn_w_down", i, d_down)])
    small.update(mix_pre_g=jnp.concatenate(d_pre), mix_post_g=jnp.concatenate(d_post),
                 ffn_pre_g=jnp.concatenate(d_fpre), ffn_post_g=jnp.concatenate(d_fpost),
                 ffn_conv_w=jnp.stack(d_fcw), ffn_conv_b=jnp.stack(d_fcb))

    small_names = [n for n in WEIGHTS if n in small]
    full_shape = {n: tuple(given[n].shape) for n in small_names}
    full_shape["sc_conv_w"] = (1, CONV_WIDTH, wt["sc_cw"].shape[1])
    full_shape["ffn_conv_w"] = tuple(wt["f_cw"].shape)
    small_sum = allreduce_small(_pack_small([small[n] for n in small_names] + [loss]), "small_sum")
    finish_reduce("l0b", pending, small_sum)
    summed = _unpack_small(small_sum, [full_shape[n] for n in small_names] + [()])
    loss = summed[-1]
    for n, g in zip(small_names, summed):
        if g.shape != given[n].shape:
            width = given[n].shape[-1]
            g = lax.dynamic_slice_in_dim(g, chip * width, width, axis=g.ndim - 1)
        grads[n] = g

    deltas, new_m, new_v = {}, {}, {}
    for n in WEIGHTS:
        w = given[n]
        flat = lambda a: a.reshape(-1, w.shape[-1])
        dl, nm_, nv_ = adamw(flat(w), flat(grads[n]), flat(given["m_" + n]), flat(given["v_" + n]), "adamw_" + n)
        deltas[n], new_m[n], new_v[n] = dl.reshape(w.shape), nm_.reshape(w.shape), nv_.reshape(w.shape)
    return (loss, dx[None], *[grads[n].reshape(given[n].shape) for n in WEIGHTS], *[deltas[n] for n in WEIGHTS],
            *[new_m[n] for n in WEIGHTS], *[new_v[n] for n in WEIGHTS])
```

```python
import functools
import math

import numpy as np
import jax
import jax.numpy as jnp
from jax import lax
from jax.experimental import pallas as pl
from jax.experimental.pallas import tpu as pltpu

F32 = jnp.float32
BF16 = jnp.bfloat16

RMS_EPS = 1e-6
HGRN_HEAD = 128
HGRN_CHUNK = 32
ATT_HEAD = 64
SWA_WINDOW = 128
SWA_GROUP = 8
ROT_DIM = 16
ROPE_THETA = 500000.0
CONV_WIDTH = 3
ADAM_LR = 0.001
ADAM_B1 = 0.9
ADAM_B2 = 0.999
ADAM_EPS = 1e-08
ADAM_WD = 0.01
ADAM_STEP = 10
N_CHIPS = 4
LANES = 128
BF16_ROWS = 16
VMEM_LIMIT = 48 * 1024 * 1024

_ARB = "arbitrary"
_PAR = "parallel"


def _params(sem, **kw):
    return pltpu.CompilerParams(dimension_semantics=sem, vmem_limit_bytes=VMEM_LIMIT, **kw)


def _tile(n, prefs):
    for p in prefs:
        if n % p == 0:
            return p
    return n


def _sigmoid(x):
    return 1.0 / (1.0 + jnp.exp(-x))


def _dot(a, b, dims):
    return lax.dot_general(a, b, (dims, ((), ())), preferred_element_type=F32)


def _dot_nn(a, b):
    return _dot(a, b, ((1,), (0,)))


def _dot_nt(a, b):
    return _dot(a, b, ((1,), (1,)))


def _dot_tn(a, b):
    return _dot(a, b, ((0,), (0,)))


MM_VMEM_BUDGET = 36 * 1024 * 1024
MM_HBM_RATE = 3.0e12
MM_MXU_RATE = 6.5e14
MM_STEP_S = 0.35e-6
MM_ACC_RATE = 3.0e12


def _mm_tiles(m, n, k, out_bytes):
    best = None
    for tm in (2048, 1024, 512, 256, 128):
        for tn in (2048, 1024, 512, 256, 128):
            for tk in sorted({k, 4096, 2816, 2048, 1408, 1024, 512, 256, 128}, reverse=True):
                if m % tm or n % tn or tk > k or k % tk:
                    continue
                nk = k // tk
                vmem = 4 * (tm * tk + tk * tn) + (4 * tm * tn if nk > 1 else 0) + 2 * tm * tn * out_bytes
                if vmem > MM_VMEM_BUDGET:
                    continue
                steps = (m // tm) * (n // tn) * nk
                traffic = 2 * m * k * (1 if nk == 1 else n // tn) + 2 * k * n * (m // tm) + m * n * out_bytes
                cost = max(traffic / MM_HBM_RATE, 2 * m * n * k / MM_MXU_RATE) + steps * MM_STEP_S
                if nk > 1:
                    cost += steps * 8 * tm * tn / MM_ACC_RATE
                if best is None or cost < best[0]:
                    best = (cost, tm, tn, tk)
    assert best is not None, (m, n, k)
    return best[1:]


def mm(a, b, mode, out_dtype, name="mm"):
    if mode == "nn":
        (m, k), (k2, n) = a.shape, b.shape
    elif mode == "nt":
        (m, k), (n, k2) = a.shape, b.shape
    else:
        (k, m), (k2, n) = a.shape, b.shape
    assert k == k2, (a.shape, b.shape, mode)
    tm, tn, tk = _mm_tiles(m, n, k, jnp.dtype(out_dtype).itemsize)
    nk = k // tk

    def product(a_ref, b_ref):
        av = a_ref[...].astype(BF16)
        bv = b_ref[...].astype(BF16)
        return {"nn": _dot_nn, "nt": _dot_nt, "tn": _dot_tn}[mode](av, bv)

    def body_one(a_ref, b_ref, o_ref):
        o_ref[...] = product(a_ref, b_ref).astype(out_dtype)

    def body_acc(a_ref, b_ref, o_ref, acc_ref):
        kk = pl.program_id(2)

        @pl.when(kk == 0)
        def _():
            acc_ref[...] = jnp.zeros_like(acc_ref)

        acc_ref[...] += product(a_ref, b_ref)

        @pl.when(kk == nk - 1)
        def _():
            o_ref[...] = acc_ref[...].astype(out_dtype)

    if mode == "nn":
        a_spec = pl.BlockSpec((tm, tk), lambda i, j, kk: (i, kk))
        b_spec = pl.BlockSpec((tk, tn), lambda i, j, kk: (kk, j))
    elif mode == "nt":
        a_spec = pl.BlockSpec((tm, tk), lambda i, j, kk: (i, kk))
        b_spec = pl.BlockSpec((tn, tk), lambda i, j, kk: (j, kk))
    else:
        a_spec = pl.BlockSpec((tk, tm), lambda i, j, kk: (kk, i))
        b_spec = pl.BlockSpec((tk, tn), lambda i, j, kk: (kk, j))
    return pl.pallas_call(
        body_one if nk == 1 else body_acc,
        name=name,
        grid=(m // tm, n // tn, nk),
        in_specs=[a_spec, b_spec],
        out_specs=pl.BlockSpec((tm, tn), lambda i, j, kk: (i, j)),
        out_shape=jax.ShapeDtypeStruct((m, n), out_dtype),
        scratch_shapes=[] if nk == 1 else [pltpu.VMEM((tm, tn), F32)],
        compiler_params=_params((_PAR, _PAR, _ARB)),
    )(a, b)


def wgrad(a, b, name):
    return mm(a.T, b, "nn", BF16, name)


def _rstd(xv):
    return lax.rsqrt(jnp.mean(xv * xv, axis=1, keepdims=True) + RMS_EPS)


def _row_spec(tr, w):
    return pl.BlockSpec((tr, w), lambda i: (i, 0))


def _full_spec(shape):
    nd = len(shape)
    return pl.BlockSpec(shape, lambda *_: (0,) * nd)


def rms_fwd(x, g, name):
    t, d = x.shape
    tr = _tile(t, (256, 128, 64, 32, 16))

    def body(x_ref, g_ref, o_ref):
        xv = x_ref[...]
        o_ref[...] = (xv * _rstd(xv) * g_ref[...]).astype(BF16)

    return pl.pallas_call(
        body, name=name, grid=(t // tr,),
        in_specs=[_row_spec(tr, d), _full_spec((1, d))],
        out_specs=_row_spec(tr, d),
        out_shape=jax.ShapeDtypeStruct((t, d), BF16),
        compiler_params=_params((_PAR,)),
    )(x, g)


def resid_norm(x, y, g_post, g_next, name):
    t, d = x.shape
    tr = _tile(t, (256, 128, 64, 32, 16))

    def body(x_ref, y_ref, gp_ref, gn_ref, x1_ref, hn_ref):
        yv = y_ref[...]
        x1 = x_ref[...] + yv * _rstd(yv) * gp_ref[...]
        x1_ref[...] = x1
        hn_ref[...] = (x1 * _rstd(x1) * gn_ref[...]).astype(BF16)

    return pl.pallas_call(
        body, name=name, grid=(t // tr,),
        in_specs=[_row_spec(tr, d), _row_spec(tr, d), _full_spec((1, d)), _full_spec((1, d))],
        out_specs=[_row_spec(tr, d), _row_spec(tr, d)],
        out_shape=[jax.ShapeDtypeStruct((t, d), F32), jax.ShapeDtypeStruct((t, d), BF16)],
        compiler_params=_params((_PAR,)),
    )(x, y, g_post, g_next)


def resid_loss(x, y, g_post, target, name):
    t, d = x.shape
    tr = _tile(t, (256, 128, 64, 32, 16))

    def body(x_ref, y_ref, gp_ref, t_ref, dx_ref, loss_ref):
        @pl.when(pl.program_id(0) == 0)
        def _():
            loss_ref[...] = jnp.zeros_like(loss_ref)

        yv = y_ref[...]
        err = x_ref[...] + yv * _rstd(yv) * gp_ref[...] - t_ref[...]
        dx_ref[...] = err * (1.0 / d)
        loss_ref[...] += 0.5 * jnp.sum(jnp.mean(err * err, axis=1, keepdims=True), axis=0, keepdims=True)

    dx, loss = pl.pallas_call(
        body, name=name, grid=(t // tr,),
        in_specs=[_row_spec(tr, d), _row_spec(tr, d), _full_spec((1, d)), _row_spec(tr, d)],
        out_specs=[_row_spec(tr, d), _full_spec((8, LANES))],
        out_shape=[jax.ShapeDtypeStruct((t, d), F32), jax.ShapeDtypeStruct((8, LANES), F32)],
        compiler_params=_params((_ARB,)),
    )(x, y, g_post, target)
    return dx, loss[0:1, 0:1]


def norm_bwd(yin, g, dout, res, out_dtype, name, after=None):
    t, d = yin.shape
    tr = _tile(t, (256, 128, 64, 32, 16))
    has_res = res is not None

    def body(*refs):
        refs = refs[:3 + has_res] + refs[-2:]
        if has_res:
            y_ref, g_ref, d_ref, r_ref, o_ref, dg_ref = refs
        else:
            y_ref, g_ref, d_ref, o_ref, dg_ref = refs

        @pl.when(pl.program_id(0) == 0)
        def _():
            dg_ref[...] = jnp.zeros_like(dg_ref)

        yv = y_ref[...]
        dv = d_ref[...].astype(F32)
        r = _rstd(yv)
        yh = yv * r
        dyh = dv * g_ref[...]
        dy = r * (dyh - yh * jnp.mean(dyh * yh, axis=1, keepdims=True))
        if has_res:
            dy = dy + r_ref[...]
        o_ref[...] = dy.astype(out_dtype)
        dg_ref[...] += jnp.sum(dv * yh, axis=0, keepdims=True)

    ins = [yin, g, dout] + ([res] if has_res else []) + ([] if after is None else [after])
    in_specs = ([_row_spec(tr, d), _full_spec((1, d)), _row_spec(tr, d)] + ([_row_spec(tr, d)] if has_res else [])
                + ([] if after is None else [pl.BlockSpec(memory_space=pl.ANY)]))
    return pl.pallas_call(
        body, name=name, grid=(t // tr,),
        in_specs=in_specs,
        out_specs=[_row_spec(tr, d), _full_spec((1, d))],
        out_shape=[jax.ShapeDtypeStruct((t, d), out_dtype), jax.ShapeDtypeStruct((1, d), F32)],
        compiler_params=_params((_ARB,)),
    )(*ins)


def _shift_down(x, halo):
    tr = x.shape[0]
    row = lax.broadcasted_iota(jnp.int32, x.shape, 0)
    h1 = halo[BF16_ROWS - 1:BF16_ROWS, :]
    h2 = halo[BF16_ROWS - 2:BF16_ROWS - 1, :]
    x1 = jnp.where(row == 0, h1, pltpu.roll(x, 1, 0))
    x2 = jnp.where(row == 0, h2, jnp.where(row == 1, h1, pltpu.roll(x, 2, 0)))
    return x1, x2


def _shift_up(x, halo):
    tr = x.shape[0]
    row = lax.broadcasted_iota(jnp.int32, x.shape, 0)
    h0 = halo[0:1, :]
    h1 = halo[1:2, :]
    x1 = jnp.where(row == tr - 1, h0, pltpu.roll(x, tr - 1, 0))
    x2 = jnp.where(row == tr - 1, h1, jnp.where(row == tr - 2, h0, pltpu.roll(x, tr - 2, 0)))
    return x1, x2


def _prev_halo_spec(tr, w, nt):
    return pl.BlockSpec((BF16_ROWS, w), lambda i: (jnp.maximum(i * (tr // BF16_ROWS) - 1, 0), 0))


def _next_halo_spec(tr, w, nt):
    last = nt * (tr // BF16_ROWS) - 1
    return pl.BlockSpec((BF16_ROWS, w), lambda i: (jnp.minimum((i + 1) * (tr // BF16_ROWS), last), 0))


def _silu_and_grad(u):
    s = _sigmoid(u)
    return u * s, s * (1.0 + u * (1.0 - s))


def ffn_act(z, conv_w, conv_b, name):
    t, f2 = z.shape
    f = f2 // 2
    tr = _tile(t, (128, 64, 32, 16))
    nt = t // tr
    cw = _tile(f, (512, 256, 128))

    def body(z_ref, zp_ref, w_ref, b_ref, a_ref):
        first = pl.program_id(0) == 0
        for j in range(f // cw):
            us = []
            for off in (j * cw, f + j * cw):
                cols = slice(off, off + cw)
                zc = z_ref[:, cols].astype(F32)
                hp = jnp.where(first, 0.0, zp_ref[:, cols].astype(F32))
                z1, z2 = _shift_down(zc, hp)
                us.append(w_ref[2:3, cols] * zc + w_ref[1:2, cols] * z1 + w_ref[0:1, cols] * z2 + b_ref[:, cols])
            sil, _ = _silu_and_grad(us[0])
            a_ref[:, j * cw:(j + 1) * cw] = (sil * us[1]).astype(BF16)

    return pl.pallas_call(
        body, name=name, grid=(nt,),
        in_specs=[_row_spec(tr, f2), _prev_halo_spec(tr, f2, nt), _full_spec((CONV_WIDTH, f2)), _full_spec((1, f2))],
        out_specs=_row_spec(tr, f),
        out_shape=jax.ShapeDtypeStruct((t, f), BF16),
        compiler_params=_params((_PAR,)),
    )(z, z, conv_w, conv_b)


def ffn_act_bwd(z, da, conv_w, conv_b, name):
    t, f2 = z.shape
    f = f2 // 2
    tr = _tile(t, (128, 64, 32, 16))
    nt = t // tr
    cw = _tile(f, (512, 256, 128))

    def body(z_ref, zp_ref, da_ref, w_ref, b_ref, du_ref, acc_ref):
        first = pl.program_id(0) == 0

        @pl.when(first)
        def _():
            acc_ref[...] = jnp.zeros_like(acc_ref)

        for j in range(f // cw):
            us, zs = [], []
            for off in (j * cw, f + j * cw):
                cols = slice(off, off + cw)
                zc = z_ref[:, cols].astype(F32)
                hp = jnp.where(first, 0.0, zp_ref[:, cols].astype(F32))
                z1, z2 = _shift_down(zc, hp)
                zs.append((z2, z1, zc))
                us.append(w_ref[2:3, cols] * zc + w_ref[1:2, cols] * z1 + w_ref[0:1, cols] * z2 + b_ref[:, cols])
            dav = da_ref[:, j * cw:(j + 1) * cw].astype(F32)
            sil, dsil = _silu_and_grad(us[0])
            dus = (dav * us[1] * dsil, dav * sil)
            for off, du, zsh in zip((j * cw, f + j * cw), dus, zs):
                cols = slice(off, off + cw)
                du_ref[:, cols] = du.astype(BF16)
                for k in range(CONV_WIDTH):
                    acc_ref[k:k + 1, cols] += jnp.sum(du * zsh[k], axis=0, keepdims=True)
                acc_ref[3:4, cols] += jnp.sum(du, axis=0, keepdims=True)

    return pl.pallas_call(
        body, name=name, grid=(nt,),
        in_specs=[_row_spec(tr, f2), _prev_halo_spec(tr, f2, nt), _row_spec(tr, f),
                  _full_spec((CONV_WIDTH, f2)), _full_spec((1, f2))],
        out_specs=[_row_spec(tr, f2), _full_spec((8, f2))],
        out_shape=[jax.ShapeDtypeStruct((t, f2), BF16), jax.ShapeDtypeStruct((8, f2), F32)],
        compiler_params=_params((_ARB,)),
    )(z, z, da, conv_w, conv_b)


def conv_transpose(du, conv_w, name):
    t, w = du.shape
    tr = _tile(t, (128, 64, 32, 16))
    nt = t // tr
    cw = _tile(w, (512, 256, 128))

    def body(d_ref, dn_ref, w_ref, o_ref):
        last = pl.program_id(0) == nt - 1
        for j in range(w // cw):
            cols = slice(j * cw, (j + 1) * cw)
            dc = d_ref[:, cols].astype(F32)
            hn = jnp.where(last, 0.0, dn_ref[:, cols].astype(F32))
            d1, d2 = _shift_up(dc, hn)
            o_ref[:, cols] = (w_ref[2:3, cols] * dc + w_ref[1:2, cols] * d1 + w_ref[0:1, cols] * d2).astype(BF16)

    return pl.pallas_call(
        body, name=name, grid=(nt,),
        in_specs=[_row_spec(tr, w), _next_halo_spec(tr, w, nt), _full_spec((CONV_WIDTH, w))],
        out_specs=_row_spec(tr, w),
        out_shape=jax.ShapeDtypeStruct((t, w), BF16),
        compiler_params=_params((_PAR,)),
    )(du, du, conv_w)


def sconv_fwd(proj, conv_w, name):
    t, w3 = proj.shape
    d = w3 // 3
    tr = _tile(t, (128, 64, 32, 16))
    nt = t // tr
    cw = _tile(d, (512, 256, 128))

    def body(p_ref, pp_ref, w_ref, o_ref):
        first = pl.program_id(0) == 0
        for j in range(d // cw):
            cb, cc, cx = (slice(k * d + j * cw, k * d + (j + 1) * cw) for k in range(3))
            zc = p_ref[:, cc].astype(F32) * p_ref[:, cx].astype(F32)
            hp = jnp.where(first, 0.0, pp_ref[:, cc].astype(F32) * pp_ref[:, cx].astype(F32))
            z1, z2 = _shift_down(zc, hp)
            wc = slice(j * cw, (j + 1) * cw)
            cz = w_ref[2:3, wc] * zc + w_ref[1:2, wc] * z1 + w_ref[0:1, wc] * z2
            o_ref[:, wc] = (p_ref[:, cb].astype(F32) * cz).astype(BF16)

    return pl.pallas_call(
        body, name=name, grid=(nt,),
        in_specs=[_row_spec(tr, w3), _prev_halo_spec(tr, w3, nt), _full_spec((CONV_WIDTH, d))],
        out_specs=_row_spec(tr, d),
        out_shape=jax.ShapeDtypeStruct((t, d), BF16),
        compiler_params=_params((_PAR,)),
    )(proj, proj, conv_w)


def sconv_bwd(proj, dyb, conv_w, name):
    t, w3 = proj.shape
    d = w3 // 3
    tr = _tile(t, (128, 64, 32, 16))
    nt = t // tr
    cw = _tile(d, (512, 256, 128))

    def body(p_ref, pp_ref, pn_ref, dy_ref, dyn_ref, w_ref, o_ref, acc_ref):
        first = pl.program_id(0) == 0
        last = pl.program_id(0) == nt - 1

        @pl.when(first)
        def _():
            acc_ref[...] = jnp.zeros_like(acc_ref)

        for j in range(d // cw):
            cb, cc, cx = (slice(k * d + j * cw, k * d + (j + 1) * cw) for k in range(3))
            wc = slice(j * cw, (j + 1) * cw)
            bv, cv, xv = p_ref[:, cb].astype(F32), p_ref[:, cc].astype(F32), p_ref[:, cx].astype(F32)
            zc = cv * xv
            hp = jnp.where(first, 0.0, pp_ref[:, cc].astype(F32) * pp_ref[:, cx].astype(F32))
            z1, z2 = _shift_down(zc, hp)
            w0, w1, w2 = w_ref[0:1, wc], w_ref[1:2, wc], w_ref[2:3, wc]
            cz = w2 * zc + w1 * z1 + w0 * z2
            dyv = dy_ref[:, wc].astype(F32)
            dcz = dyv * bv
            hn = jnp.where(last, 0.0, dyn_ref[:, wc].astype(F32) * pn_ref[:, cb].astype(F32))
            n1, n2 = _shift_up(dcz, hn)
            dz = w2 * dcz + w1 * n1 + w0 * n2
            o_ref[:, cb] = (dyv * cz).astype(BF16)
            o_ref[:, cc] = (dz * xv).astype(BF16)
            o_ref[:, cx] = (dz * cv).astype(BF16)
            for k, zsh in enumerate((z2, z1, zc)):
                acc_ref[k:k + 1, wc] += jnp.sum(dcz * zsh, axis=0, keepdims=True)

    return pl.pallas_call(
        body, name=name, grid=(nt,),
        in_specs=[_row_spec(tr, w3), _prev_halo_spec(tr, w3, nt), _next_halo_spec(tr, w3, nt),
                  _row_spec(tr, d), _next_halo_spec(tr, d, nt), _full_spec((CONV_WIDTH, d))],
        out_specs=[_row_spec(tr, w3), _full_spec((8, d))],
        out_shape=[jax.ShapeDtypeStruct((t, w3), BF16), jax.ShapeDtypeStruct((8, d), F32)],
        compiler_params=_params((_ARB,)),
    )(proj, proj, proj, dyb, dyb, conv_w)


def rope_tables(positions):
    half = ROT_DIM // 2
    inv_freq = ROPE_THETA ** (-jnp.arange(half, dtype=F32) / half)
    ang = positions.astype(F32)[:, None] * inv_freq[None, :]
    cos, sin = jnp.cos(ang), jnp.sin(ang)
    ones = jnp.ones((positions.shape[0], ATT_HEAD - ROT_DIM), F32)
    c64 = jnp.concatenate([cos, cos, ones], axis=1)
    s64 = jnp.concatenate([-sin, sin, 0.0 * ones], axis=1)
    perm = np.zeros((LANES, LANES), np.float32)
    for lane in range(LANES):
        dim = lane % ATT_HEAD
        if dim < half:
            perm[lane + half, lane] = 1.0
        elif dim < ROT_DIM:
            perm[lane - half, lane] = 1.0
    return jnp.tile(c64, (1, 2)), jnp.tile(s64, (1, 2)), jnp.asarray(perm, BF16)


def rope(xin, ctab, stab, perm, n_rot, sign, name):
    t, w = xin.shape
    tr = _tile(t, (256, 128, 64, 32, 16))

    def body(x_ref, c_ref, s_ref, p_ref, o_ref):
        cv, sv = c_ref[...], s_ref[...] * sign
        for j in range(n_rot // LANES):
            cols = slice(j * LANES, (j + 1) * LANES)
            xb = x_ref[:, cols]
            o_ref[:, cols] = (xb.astype(F32) * cv + _dot_nn(xb, p_ref[...]) * sv).astype(BF16)
        if n_rot < w:
            o_ref[:, n_rot:] = x_ref[:, n_rot:]

    return pl.pallas_call(
        body, name=name, grid=(t // tr,),
        in_specs=[_row_spec(tr, w), _row_spec(tr, LANES), _row_spec(tr, LANES), _full_spec((LANES, LANES))],
        out_specs=_row_spec(tr, w),
        out_shape=jax.ShapeDtypeStruct((t, w), BF16),
        compiler_params=_params((_PAR,)),
    )(xin, ctab, stab, perm)


NEG = -1e30


def _half(shape, h):
    return (lax.broadcasted_iota(jnp.int32, shape, 1) // ATT_HEAD) == h


def _dup_head(xb, kvh):
    xf = jnp.where(_half(xb.shape, kvh), xb.astype(F32), 0.0)
    return (xf + pltpu.roll(xf, ATT_HEAD, 1)).astype(BF16)


def _swa_mask(n, rows, cur_only):
    w = SWA_WINDOW
    shape = (w, w) if cur_only else (w, 2 * w)
    qi = lax.broadcasted_iota(jnp.int32, shape, 0)
    kj = lax.broadcasted_iota(jnp.int32, shape, 1) + (w if cur_only else 0)
    diff = qi + w - kj
    ok = (diff >= 0) & (diff < w)
    return ok & ((kj >= w) | (n > 0))


def swa_fwd(qkv, sinks, hq, name):
    t = qkv.shape[0]
    w = SWA_WINDOW
    nb = t // w
    hkv = hq // SWA_GROUP
    npair = hkv // 2
    qw = 2 * SWA_GROUP * ATT_HEAD
    kcol = hq * ATT_HEAD // LANES
    vcol = kcol + npair
    scale = ATT_HEAD ** -0.5

    def body(sink_ref, q_ref, kp_ref, kc_ref, vp_ref, vc_ref, o_ref, lse_ref):
        m, n = pl.program_id(0), pl.program_id(1)
        kb = jnp.concatenate([kp_ref[...], kc_ref[...]], axis=0)
        vb = jnp.concatenate([vp_ref[...], vc_ref[...]], axis=0)
        ok = _swa_mask(n, w, False)
        for kvh in range(2):
            kd, vd = _dup_head(kb, kvh), _dup_head(vb, kvh)
            for jj in range(SWA_GROUP // 2):
                jp = kvh * (SWA_GROUP // 2) + jj
                q2 = q_ref[:, jp * LANES:(jp + 1) * LANES]
                outs = []
                for a in range(2):
                    qa = jnp.where(_half(q2.shape, a), q2, jnp.zeros_like(q2))
                    s = jnp.where(ok, _dot_nt(qa, kd) * scale, NEG)
                    sink = sink_ref[m * 2 * SWA_GROUP + jp * 2 + a]
                    mx = jnp.maximum(jnp.max(s, axis=1, keepdims=True), sink)
                    e = jnp.exp(s - mx)
                    den = jnp.sum(e, axis=1, keepdims=True) + jnp.exp(sink - mx)
                    p = (e / den).astype(BF16)
                    outs.append(_dot_nn(p, vd))
                    lse_ref[jp * 2 + a] = jnp.broadcast_to(mx + jnp.log(den), (w, LANES))
                o_ref[:, jp * LANES:(jp + 1) * LANES] = jnp.where(_half(outs[0].shape, 0), outs[0], outs[1]).astype(BF16)

    prev = lambda m, n: jnp.maximum(n - 1, 0)
    grid_spec = pltpu.PrefetchScalarGridSpec(
        num_scalar_prefetch=1, grid=(npair, nb),
        in_specs=[
            pl.BlockSpec((w, qw), lambda m, n, s: (n, m)),
            pl.BlockSpec((w, LANES), lambda m, n, s: (prev(m, n), kcol + m)),
            pl.BlockSpec((w, LANES), lambda m, n, s: (n, kcol + m)),
            pl.BlockSpec((w, LANES), lambda m, n, s: (prev(m, n), vcol + m)),
            pl.BlockSpec((w, LANES), lambda m, n, s: (n, vcol + m)),
        ],
        out_specs=[
            pl.BlockSpec((w, qw), lambda m, n, s: (n, m)),
            pl.BlockSpec((2 * SWA_GROUP, w, LANES), lambda m, n, s: (m, n, 0)),
        ],
    )
    return pl.pallas_call(
        body, name=name, grid_spec=grid_spec,
        out_shape=[jax.ShapeDtypeStruct((t, hq * ATT_HEAD), BF16), jax.ShapeDtypeStruct((hq, t, LANES), F32)],
        compiler_params=_params((_PAR, _PAR)),
    )(sinks, qkv, qkv, qkv, qkv, qkv)


def swa_bwd(qkv, o, lse, do, sinks, hq, name):
    t = qkv.shape[0]
    w = SWA_WINDOW
    nb = t // w
    hkv = hq // SWA_GROUP
    npair = hkv // 2
    qw = 2 * SWA_GROUP * ATT_HEAD
    kcol = hq * ATT_HEAD // LANES
    vcol = kcol + npair
    scale = ATT_HEAD ** -0.5
    gh = 2 * SWA_GROUP

    def body(sink_ref, qc_ref, qn_ref, kp_ref, kc_ref, vp_ref, vc_ref, oc_ref, on_ref, dc_ref, dn_ref,
             lc_ref, ln_ref, dq_ref, dk_ref, dv_ref, ds_ref):
        m, n = pl.program_id(0), pl.program_id(1)
        kb = jnp.concatenate([kp_ref[...], kc_ref[...]], axis=0)
        vb = jnp.concatenate([vp_ref[...], vc_ref[...]], axis=0)
        ok_band = _swa_mask(n, w, False)
        ok_cur = _swa_mask(n, w, True)
        qi = lax.broadcasted_iota(jnp.int32, (w, w), 0)
        kj = lax.broadcasted_iota(jnp.int32, (w, w), 1)
        ok_next = (kj > qi) & (n < nb - 1)
        row16 = lax.broadcasted_iota(jnp.int32, (gh, LANES), 0)
        dsink = jnp.zeros((gh, LANES), F32)
        dk_tot = jnp.zeros((w, LANES), F32)
        dv_tot = jnp.zeros((w, LANES), F32)
        for kvh in range(2):
            kd, vd = _dup_head(kb, kvh), _dup_head(vb, kvh)
            kdc, vdc = kd[w:, :], vd[w:, :]
            acc_k = [jnp.zeros((w, LANES), F32), jnp.zeros((w, LANES), F32)]
            acc_v = [jnp.zeros((w, LANES), F32), jnp.zeros((w, LANES), F32)]
            for jj in range(SWA_GROUP // 2):
                jp = kvh * (SWA_GROUP // 2) + jj
                cols = slice(jp * LANES, (jp + 1) * LANES)
                dqs = []
                for a in range(2):
                    hd = jp * 2 + a
                    sink = sink_ref[m * gh + hd]
                    half = _half((w, LANES), a)
                    q2 = jnp.where(half, qc_ref[:, cols], jnp.zeros((w, LANES), BF16))
                    d2 = jnp.where(half, dc_ref[:, cols], jnp.zeros((w, LANES), BF16))
                    delta = jnp.sum(d2.astype(F32) * oc_ref[:, cols].astype(F32), axis=1, keepdims=True)
                    lse_c = lc_ref[hd][:, 0:1]
                    p = jnp.exp(jnp.where(ok_band, _dot_nt(q2, kd) * scale, NEG) - lse_c)
                    dsv = p * (_dot_nt(d2, vd) - delta)
                    dqs.append(_dot_nn(dsv.astype(BF16), kd) * scale)
                    psink = jnp.exp(sink - lse_c)
                    dsink = jnp.where(row16 == hd, dsink - jnp.sum(psink * delta, axis=0, keepdims=True), dsink)
                    for q_ref, d_ref, o_ref, l_ref, okm in ((qc_ref, dc_ref, oc_ref, lc_ref, ok_cur),
                                                           (qn_ref, dn_ref, on_ref, ln_ref, ok_next)):
                        q2 = jnp.where(half, q_ref[:, cols], jnp.zeros((w, LANES), BF16))
                        d2 = jnp.where(half, d_ref[:, cols], jnp.zeros((w, LANES), BF16))
                        delta = jnp.sum(d2.astype(F32) * o_ref[:, cols].astype(F32), axis=1, keepdims=True)
                        p = jnp.exp(jnp.where(okm, _dot_nt(q2, kdc) * scale, NEG) - l_ref[hd][:, 0:1])
                        dsv = p * (_dot_nt(d2, vdc) - delta)
                        acc_v[a] = acc_v[a] + _dot_tn(p.astype(BF16), d2)
                        acc_k[a] = acc_k[a] + _dot_tn(dsv.astype(BF16), q2) * scale
                dq_ref[:, cols] = jnp.where(_half((w, LANES), 0), dqs[0], dqs[1]).astype(BF16)
            dk_tot = dk_tot + acc_k[kvh] + pltpu.roll(acc_k[1 - kvh], ATT_HEAD, 1)
            dv_tot = dv_tot + acc_v[kvh] + pltpu.roll(acc_v[1 - kvh], ATT_HEAD, 1)
        dk_ref[...] = dk_tot.astype(BF16)
        dv_ref[...] = dv_tot.astype(BF16)
        ds_ref[0, 0] = dsink

    prev = lambda n: jnp.maximum(n - 1, 0)
    nxt = lambda n: jnp.minimum(n + 1, nb - 1)
    qspec = lambda f: pl.BlockSpec((w, qw), lambda m, n, s: (f(n), m))
    lspec = lambda f: pl.BlockSpec((gh, w, LANES), lambda m, n, s: (m, f(n), 0))
    same = lambda n: n
    grid_spec = pltpu.PrefetchScalarGridSpec(
        num_scalar_prefetch=1, grid=(npair, nb),
        in_specs=[
            qspec(same), qspec(nxt),
            pl.BlockSpec((w, LANES), lambda m, n, s: (prev(n), kcol + m)),
            pl.BlockSpec((w, LANES), lambda m, n, s: (n, kcol + m)),
            pl.BlockSpec((w, LANES), lambda m, n, s: (prev(n), vcol + m)),
            pl.BlockSpec((w, LANES), lambda m, n, s: (n, vcol + m)),
            qspec(same), qspec(nxt), qspec(same), qspec(nxt),
            lspec(same), lspec(nxt),
        ],
        out_specs=[
            pl.BlockSpec((w, qw), lambda m, n, s: (n, m)),
            pl.BlockSpec((w, LANES), lambda m, n, s: (n, m)),
            pl.BlockSpec((w, LANES), lambda m, n, s: (n, m)),
            pl.BlockSpec((1, 1, gh, LANES), lambda m, n, s: (m, n, 0, 0)),
        ],
    )
    return pl.pallas_call(
        body, name=name, grid_spec=grid_spec,
        out_shape=[jax.ShapeDtypeStruct((t, hq * ATT_HEAD), BF16),
                   jax.ShapeDtypeStruct((t, hkv * ATT_HEAD), BF16),
                   jax.ShapeDtypeStruct((t, hkv * ATT_HEAD), BF16),
                   jax.ShapeDtypeStruct((npair, nb, gh, LANES), F32)],
        compiler_params=_params((_PAR, _PAR)),
    )(sinks, qkv, qkv, qkv, qkv, qkv, qkv, o, o, do, do, lse, lse)


def swa_mixer_fwd(hn, w_in, w_out, sinks, positions, name):
    hq = sinks.shape[0]
    n_rot = (hq + hq // SWA_GROUP) * ATT_HEAD
    tabs = rope_tables(positions)
    proj = mm(hn, w_in, "nn", BF16, name + "_in")
    qkv = rope(proj, *tabs, n_rot, 1.0, name + "_rope")
    o, lse = swa_fwd(qkv, sinks, hq, name + "_att")
    y = mm(o, w_out, "nn", F32, name + "_out")
    return y, (qkv, o, lse)


def swa_mixer_bwd(dy, hn, w_in, w_out, sinks, positions, saved, name):
    qkv, o, lse = saved
    hq = sinks.shape[0]
    n_rot = (hq + hq // SWA_GROUP) * ATT_HEAD
    tabs = rope_tables(positions)
    dwo = wgrad(o, dy, name + "_dwo")
    do = mm(dy, w_out, "nt", BF16, name + "_do")
    dq, dk, dv, dsp = swa_bwd(qkv, o, lse, do, sinks, hq, name + "_attb")
    dproj = rope(jnp.concatenate([dq, dk, dv], axis=1), *tabs, n_rot, -1.0, name + "_ropeb")
    dwi = wgrad(hn, dproj, name + "_dwi")
    dhn = mm(dproj, w_in, "nt", BF16, name + "_dhn")
    dsinks = jnp.sum(dsp[:, :, :, 0], axis=1).reshape(hq)
    return dhn, dwi, dwo, dsinks


FOX_FPAD = 512


def _log_sigmoid(x):
    return jnp.minimum(x, 0.0) - jnp.log(1.0 + jnp.exp(-jnp.abs(x)))


def _tri(n, upper):
    r = lax.broadcasted_iota(jnp.int32, (n, n), 0)
    c = lax.broadcasted_iota(jnp.int32, (n, n), 1)
    return jnp.where((c >= r) if upper else (c <= r), 1.0, 0.0).astype(F32)


def _dot_exact(a, b):
    return jnp.dot(a, b, precision=lax.Precision.HIGHEST, preferred_element_type=F32)


def fox_cumsum(fl, b_pad, name):
    t = fl.shape[0]
    tr = _tile(t, (256, 128, 64, 32, 16, 8))

    def body(f_ref, b_ref, c_ref, carry_ref):
        @pl.when(pl.program_id(0) == 0)
        def _():
            carry_ref[...] = jnp.zeros_like(carry_ref)

        c = _dot_exact(_tri(tr, False), _log_sigmoid(f_ref[...] + b_ref[...])) + carry_ref[...]
        c_ref[...] = c
        carry_ref[...] = c[tr - 1:tr, :]

    return pl.pallas_call(
        body, name=name, grid=(t // tr,),
        in_specs=[_row_spec(tr, LANES), _full_spec((1, LANES))],
        out_specs=_row_spec(tr, LANES),
        out_shape=jax.ShapeDtypeStruct((t, LANES), F32),
        scratch_shapes=[pltpu.VMEM((1, LANES), F32)],
        compiler_params=_params((_ARB,)),
    )(fl, b_pad)


def fox_cumsum_bwd(dc, fl, b_pad, name):
    t = fl.shape[0]
    tr = _tile(t, (256, 128, 64, 32, 16, 8))
    nt = t // tr

    def body(d_ref, f_ref, b_ref, o_ref, db_ref, carry_ref):
        @pl.when(pl.program_id(0) == 0)
        def _():
            carry_ref[...] = jnp.zeros_like(carry_ref)
            db_ref[...] = jnp.zeros_like(db_ref)

        dlf = _dot_exact(_tri(tr, True), d_ref[...]) + carry_ref[...]
        carry_ref[...] = dlf[0:1, :]
        dfl = dlf * _sigmoid(-(f_ref[...] + b_ref[...]))
        o_ref[...] = dfl.astype(BF16)
        db_ref[...] += jnp.sum(dfl, axis=0, keepdims=True)

    rev = pl.BlockSpec((tr, LANES), lambda i: (nt - 1 - i, 0))
    return pl.pallas_call(
        body, name=name, grid=(nt,),
        in_specs=[rev, rev, _full_spec((1, LANES))],
        out_specs=[rev, _full_spec((1, LANES))],
        out_shape=[jax.ShapeDtypeStruct((t, LANES), BF16), jax.ShapeDtypeStruct((1, LANES), F32)],
        scratch_shapes=[pltpu.VMEM((1, LANES), F32)],
        compiler_params=_params((_ARB,)),
    )(dc, fl, b_pad)


AUG_C, AUG_ONE, AUG_LSE = ATT_HEAD, ATT_HEAD + 3, ATT_HEAD + 6


def _split3(x):
    hi = x.astype(BF16).astype(F32)
    mid = (x - hi).astype(BF16).astype(F32)
    return hi, mid, (x - hi - mid).astype(BF16).astype(F32)


def _aug(base, lane, entries):
    out = jnp.where(lane < ATT_HEAD, base, 0.0)
    for first, parts in entries:
        if parts is None:
            out = jnp.where((lane >= first) & (lane < first + 3), 1.0, out)
        else:
            for k, part in enumerate(parts):
                out = jnp.where(lane == first + k, part, out)
    return out.astype(BF16)


def _head_of_pair(x2, a):
    xf = x2.astype(F32)
    return xf if a == 0 else pltpu.roll(xf, ATT_HEAD, 1)


def fa_prep(proj, c, nh, name):
    t = proj.shape[0]
    npair = nh // 2
    tr = _tile(t, (256, 128))
    scale = ATT_HEAD ** -0.5

    def body(q_ref, k_ref, v_ref, c_ref, qa_ref, ka_ref, va_ref):
        lane = lax.broadcasted_iota(jnp.int32, (tr, LANES), 1)
        for p in range(npair):
            pc = slice(p * LANES, (p + 1) * LANES)
            for a in range(2):
                h = 2 * p + a
                hc = slice(h * LANES, (h + 1) * LANES)
                ch = c_ref[:, h:h + 1]
                qa_ref[:, hc] = _aug(_head_of_pair(q_ref[:, pc], a) * scale, lane,
                                     [(AUG_C, _split3(ch)), (AUG_ONE, None)])
                ka_ref[:, hc] = _aug(_head_of_pair(k_ref[:, pc], a), lane,
                                     [(AUG_C, None), (AUG_ONE, _split3(-ch)), (AUG_LSE, None)])
                va_ref[:, hc] = _aug(_head_of_pair(v_ref[:, pc], a), lane, [(AUG_C, None)])

    hd = nh * ATT_HEAD
    part = lambda k: pl.BlockSpec((tr, hd), lambda i: (i, k))
    out = pl.BlockSpec((tr, nh * LANES), lambda i: (i, 0))
    return pl.pallas_call(
        body, name=name, grid=(t // tr,),
        in_specs=[part(0), part(1), part(2), _row_spec(tr, LANES)],
        out_specs=[out, out, out],
        out_shape=[jax.ShapeDtypeStruct((t, nh * LANES), BF16)] * 3,
        compiler_params=_params((_PAR,)),
    )(proj, proj, proj, c)


def _fox_tiles(t):
    outer = _tile(t, (512, 256, 128))
    return outer, min(outer, 256)


def _diag_mask(outer, inner, d, transposed=False):
    r = lax.broadcasted_iota(jnp.int32, (outer, inner), 0)
    c = lax.broadcasted_iota(jnp.int32, (outer, inner), 1) + d * inner
    return (r <= c) if transposed else (c <= r)


def fa_fwd(qa, ka, va, proj, nh, name):
    t = qa.shape[0]
    hd = nh * ATT_HEAD
    npair = nh // 2
    tq, tk = _fox_tiles(t)
    nt, ratio = t // tq, tq // tk
    gcol = (3 * hd + FOX_FPAD) // LANES

    def body(q_ref, k_ref, v_ref, g_ref, o_ref, og_ref, lse_ref):
        i = pl.program_id(1)
        heads = [slice(a * LANES, (a + 1) * LANES) for a in range(2)]
        qs = [q_ref[:, cols] for cols in heads]

        def tile(j, carry, diag):
            rows = pl.ds(pl.multiple_of(j * tk, tk), tk)
            out = []
            for (mx, acc), q, cols in zip(carry, qs, heads):
                s = _dot_nt(q, k_ref[rows, cols])
                if diag is not None:
                    s = jnp.where(_diag_mask(tq, tk, diag), s, NEG)
                mnew = jnp.maximum(mx, jnp.max(s, axis=1, keepdims=True))
                p = jnp.exp(s - mnew).astype(BF16)
                out.append((mnew, jnp.exp(mx - mnew) * acc + _dot_nn(p, v_ref[rows, cols])))
            return tuple(out)

        carry = ((jnp.full((tq, 1), NEG, F32), jnp.zeros((tq, LANES), F32)),) * 2
        carry = lax.fori_loop(0, i * ratio, functools.partial(tile, diag=None), carry)
        for d in range(ratio):
            carry = tile(i * ratio + d, carry, d)
        outs = []
        for a, (mx, acc) in enumerate(carry):
            l = acc[:, AUG_C:AUG_C + 1]
            outs.append(acc / l)
            lse_ref[a] = jnp.broadcast_to(mx + jnp.log(l), (tq, LANES))
        o = jnp.where(_half((tq, LANES), 0), outs[0], pltpu.roll(outs[1], ATT_HEAD, 1))
        o_ref[...] = o.astype(BF16)
        og_ref[...] = (o * _sigmoid(g_ref[...].astype(F32))).astype(BF16)

    pair = pl.BlockSpec((tq, LANES), lambda p, i: (i, p))
    return pl.pallas_call(
        body, name=name, grid=(npair, nt),
        in_specs=[pl.BlockSpec((tq, 2 * LANES), lambda p, i: (i, p)),
                  pl.BlockSpec((t, 2 * LANES), lambda p, i: (0, p)),
                  pl.BlockSpec((t, 2 * LANES), lambda p, i: (0, p)),
                  pl.BlockSpec((tq, LANES), lambda p, i: (i, gcol + p))],
        out_specs=[pair, pair, pl.BlockSpec((2, tq, LANES), lambda p, i: (p, i, 0))],
        out_shape=[jax.ShapeDtypeStruct((t, hd), BF16), jax.ShapeDtypeStruct((t, hd), BF16),
                   jax.ShapeDtypeStruct((nh, t, LANES), F32)],
        compiler_params=_params((_PAR, _PAR)),
    )(qa, ka, va, proj)


def fa_prep_bwd(dog, o, proj, qa, lse, nh, name):
    t, hd = o.shape
    npair = nh // 2
    tr = _tile(t, (256, 128))
    gcol = (3 * hd + FOX_FPAD) // LANES

    def body(d_ref, o_ref, g_ref, q_ref, l_ref, dg_ref, qb_ref, da_ref):
        lane = lax.broadcasted_iota(jnp.int32, (tr, LANES), 1)
        dv, ov = d_ref[...].astype(F32), o_ref[...].astype(F32)
        sg = _sigmoid(g_ref[...].astype(F32))
        do = (dv * sg).astype(BF16).astype(F32)
        dg_ref[...] = (dv * ov * sg * (1.0 - sg)).astype(BF16)
        prod = do * ov
        for a in range(2):
            cols = slice(a * LANES, (a + 1) * LANES)
            delta = jnp.sum(jnp.where(_half(prod.shape, a), prod, 0.0), axis=1, keepdims=True)
            da_ref[:, cols] = _aug(_head_of_pair(do, a), lane, [(AUG_C, _split3(-delta))])
            nl = _split3(-l_ref[a][:, 0:1])
            qb = q_ref[:, cols].astype(F32)
            for k in range(3):
                qb = jnp.where(lane == AUG_LSE + k, nl[k], qb)
            qb_ref[:, cols] = qb.astype(BF16)

    pair = pl.BlockSpec((tr, LANES), lambda p, i: (i, p))
    wide = pl.BlockSpec((tr, 2 * LANES), lambda p, i: (i, p))
    return pl.pallas_call(
        body, name=name, grid=(npair, t // tr),
        in_specs=[pair, pair, pl.BlockSpec((tr, LANES), lambda p, i: (i, gcol + p)), wide,
                  pl.BlockSpec((2, tr, LANES), lambda p, i: (p, i, 0))],
        out_specs=[pair, wide, wide],
        out_shape=[jax.ShapeDtypeStruct((t, hd), BF16), jax.ShapeDtypeStruct((t, nh * LANES), BF16),
                   jax.ShapeDtypeStruct((t, nh * LANES), BF16)],
        compiler_params=_params((_PAR, _PAR)),
    )(dog, o, proj, qa, lse)


def fa_dq(qb, ka, va, da, nh, name):
    t = qb.shape[0]
    hd = nh * ATT_HEAD
    npair = nh // 2
    tq, tk = _fox_tiles(t)
    nt, ratio = t // tq, tq // tk
    scale = ATT_HEAD ** -0.5

    def body(q_ref, k_ref, v_ref, d_ref, dq_ref, rs_ref):
        i = pl.program_id(1)
        heads = [slice(a * LANES, (a + 1) * LANES) for a in range(2)]
        qs = [q_ref[:, cols] for cols in heads]
        ds = [d_ref[:, cols] for cols in heads]

        def tile(j, carry, diag):
            rows = pl.ds(pl.multiple_of(j * tk, tk), tk)
            out = []
            for acc, q, d, cols in zip(carry, qs, ds, heads):
                kj = k_ref[rows, cols]
                s = _dot_nt(q, kj)
                if diag is not None:
                    s = jnp.where(_diag_mask(tq, tk, diag), s, NEG)
                dsv = jnp.exp(s) * _dot_nt(d, v_ref[rows, cols])
                out.append(acc + _dot_nn(dsv.astype(BF16), kj))
            return tuple(out)

        accs = lax.fori_loop(0, i * ratio, functools.partial(tile, diag=None), (jnp.zeros((tq, LANES), F32),) * 2)
        for d in range(ratio):
            accs = tile(i * ratio + d, accs, d)
        dq_ref[...] = (jnp.where(_half((tq, LANES), 0), accs[0], pltpu.roll(accs[1], ATT_HEAD, 1)) * scale).astype(BF16)
        lane = lax.broadcasted_iota(jnp.int32, (tq, LANES), 1)
        rs_ref[...] = jnp.where(lane == 0, accs[0][:, AUG_C:AUG_C + 1],
                                jnp.where(lane == 1, accs[1][:, AUG_C:AUG_C + 1], 0.0))

    wide = pl.BlockSpec((tq, 2 * LANES), lambda p, i: (i, p))
    resident = pl.BlockSpec((t, 2 * LANES), lambda p, i: (0, p))
    pair = pl.BlockSpec((tq, LANES), lambda p, i: (i, p))
    return pl.pallas_call(
        body, name=name, grid=(npair, nt),
        in_specs=[wide, resident, resident, wide],
        out_specs=[pair, pair],
        out_shape=[jax.ShapeDtypeStruct((t, hd), BF16), jax.ShapeDtypeStruct((t, npair * LANES), F32)],
        compiler_params=_params((_PAR, _PAR)),
    )(qb, ka, va, da)


def fa_dkv(qb, ka, va, da, nh, name):
    t = qb.shape[0]
    hd = nh * ATT_HEAD
    npair = nh // 2
    tk, tq = _fox_tiles(t)
    nt, ratio = t // tk, tk // tq

    def body(q_ref, k_ref, v_ref, d_ref, dk_ref, dv_ref, cs_ref):
        j = pl.program_id(1)
        heads = [slice(a * LANES, (a + 1) * LANES) for a in range(2)]
        ks = [k_ref[:, cols] for cols in heads]
        vs = [v_ref[:, cols] for cols in heads]

        def tile(i, carry, diag):
            rows = pl.ds(pl.multiple_of(i * tq, tq), tq)
            out = []
            for (dk, dv), k, v, cols in zip(carry, ks, vs, heads):
                qi, di = q_ref[rows, cols], d_ref[rows, cols]
                st = _dot_nt(k, qi)
                if diag is not None:
                    st = jnp.where(_diag_mask(tk, tq, diag, True), st, NEG)
                pt = jnp.exp(st)
                dst = pt * _dot_nt(v, di)
                out.append((dk + _dot_nn(dst.astype(BF16), qi), dv + _dot_nn(pt.astype(BF16), di)))
            return tuple(out)

        zero = jnp.zeros((tk, LANES), F32)
        carry = ((zero, zero),) * 2
        for d in range(ratio):
            carry = tile(j * ratio + d, carry, d)
        carry = lax.fori_loop((j + 1) * ratio, t // tq, functools.partial(tile, diag=None), carry)
        dks, dvs = [c[0] for c in carry], [c[1] for c in carry]
        first = _half((tk, LANES), 0)
        dk_ref[...] = jnp.where(first, dks[0], pltpu.roll(dks[1], ATT_HEAD, 1)).astype(BF16)
        dv_ref[...] = jnp.where(first, dvs[0], pltpu.roll(dvs[1], ATT_HEAD, 1)).astype(BF16)
        lane = lax.broadcasted_iota(jnp.int32, (tk, LANES), 1)
        cs_ref[...] = jnp.where(lane == 0, dks[0][:, AUG_ONE:AUG_ONE + 1],
                                jnp.where(lane == 1, dks[1][:, AUG_ONE:AUG_ONE + 1], 0.0))

    wide = pl.BlockSpec((tk, 2 * LANES), lambda p, j: (j, p))
    resident = pl.BlockSpec((t, 2 * LANES), lambda p, j: (0, p))
    pair = pl.BlockSpec((tk, LANES), lambda p, j: (j, p))
    return pl.pallas_call(
        body, name=name, grid=(npair, nt),
        in_specs=[resident, wide, wide, resident],
        out_specs=[pair, pair, pair],
        out_shape=[jax.ShapeDtypeStruct((t, hd), BF16), jax.ShapeDtypeStruct((t, hd), BF16),
                   jax.ShapeDtypeStruct((t, npair * LANES), F32)],
        compiler_params=_params((_PAR, _PAR)),
    )(qb, ka, va, da)


def fox_pad_w_in(w_in, nh):
    hd = nh * ATT_HEAD
    pad = jnp.zeros((w_in.shape[0], FOX_FPAD - nh), w_in.dtype)
    return jnp.concatenate([w_in[:, :3 * hd + nh], pad, w_in[:, 3 * hd + nh:]], axis=1)


def fox_unpad_dw(dw, nh):
    hd = nh * ATT_HEAD
    return jnp.concatenate([dw[:, :3 * hd + nh], dw[:, 3 * hd + FOX_FPAD:]], axis=1)


def _pad_lanes(v):
    return jnp.pad(v.reshape(1, -1).astype(F32), ((0, 0), (0, LANES - v.size)))


def fox_mixer_fwd(hn, w_pad, w_out, b_f, name):
    nh = b_f.shape[0]
    hd = nh * ATT_HEAD
    proj = mm(hn, w_pad, "nn", BF16, name + "_in")
    fl = mm(hn, w_pad[:, 3 * hd:3 * hd + LANES], "nn", F32, name + "_fl")
    c = fox_cumsum(fl, _pad_lanes(b_f), name + "_cum")
    qa, ka, va = fa_prep(proj, c, nh, name + "_prep")
    o, og, lse = fa_fwd(qa, ka, va, proj, nh, name + "_att")
    y = mm(og, w_out, "nn", F32, name + "_out")
    return y, (proj, fl, qa, ka, va, o, og, lse)


def fox_mixer_bwd(dy, hn, w_pad, w_out, b_f, saved, name):
    proj, fl, qa, ka, va, o, og, lse = saved
    nh = b_f.shape[0]
    t = hn.shape[0]
    dwo = wgrad(og, dy, name + "_dwo")
    dog = mm(dy, w_out, "nt", BF16, name + "_dog")
    dg, qb, da = fa_prep_bwd(dog, o, proj, qa, lse, nh, name + "_prepb")
    dq, rsum = fa_dq(qb, ka, va, da, nh, name + "_dq")
    dk, dv, csum = fa_dkv(qb, ka, va, da, nh, name + "_dkv")
    dc = (rsum - csum).reshape(t, nh // 2, LANES)[:, :, :2].reshape(t, nh)
    dc = jnp.pad(dc, ((0, 0), (0, LANES - nh)))
    dfl, db = fox_cumsum_bwd(dc, fl, _pad_lanes(b_f), name + "_cumb")
    dfl = jnp.pad(dfl, ((0, 0), (0, FOX_FPAD - LANES)))
    dproj = jnp.concatenate([dq, dk, dv, dfl, dg], axis=1)
    dwi = wgrad(hn, dproj, name + "_dwi")
    dhn = mm(dproj, w_pad, "nt", BF16, name + "_dhn")
    return dhn, dwi, dwo, db[0, :nh]


HGRN_ROWS = 256
HGRN_TOGETHER = 4


def lb_table_row(lb_param, idx, name):
    nrow, w = lb_param.shape

    def body(p_ref, o_ref):
        rows = [p_ref[r:r + 1, :] for r in range(nrow)]
        mx = functools.reduce(jnp.maximum, rows)
        es = [jnp.exp(r - mx) for r in rows]
        o_ref[...] = sum(es[:idx + 1]) / sum(es)

    return pl.pallas_call(
        body, name=name, in_specs=[_full_spec((nrow, w))], out_specs=_full_spec((1, w)), grid=(1,),
        out_shape=jax.ShapeDtypeStruct((1, w), F32),
    )(lb_param)


def lb_table_row_bwd(lb_param, dlb, idx, name):
    nrow, w = lb_param.shape

    def body(p_ref, d_ref, o_ref):
        rows = [p_ref[r:r + 1, :] for r in range(nrow)]
        mx = functools.reduce(jnp.maximum, rows)
        es = [jnp.exp(r - mx) for r in rows]
        tot = sum(es)
        ps = [e / tot for e in es]
        dv = d_ref[...]
        inner = sum(ps[:idx + 1]) * dv
        for r in range(nrow):
            o_ref[r:r + 1, :] = ps[r] * ((dv if r <= idx else 0.0) - inner)

    return pl.pallas_call(
        body, name=name, in_specs=[_full_spec((nrow, w)), _full_spec((1, w))], out_specs=_full_spec((nrow, w)),
        grid=(1,), out_shape=jax.ShapeDtypeStruct((nrow, w), F32),
    )(lb_param, dlb)


def _hgrn_gates(qraw, fraw, lb):
    sq = _sigmoid(qraw)
    sf = _sigmoid(fraw)
    f = lb + (1.0 - lb) * sf
    return qraw * sq, sq, sf, f, 1.0 - f


def _hgrn_chunk(q, k, f):
    c = HGRN_CHUNK
    b = _dot_exact(_tri(c, False), jnp.log(f))
    bl = b[c - 1:c, :]
    bm = b[c // 2 - 1:c // 2, :]
    eq, ek = jnp.exp(b - bm), jnp.exp(bm - b)
    eb, el = jnp.exp(b), jnp.exp(bl - b)
    qt, kt = (q * eq).astype(BF16), (k * ek).astype(BF16)
    causal = _tri(c, False) > 0.5
    amat = jnp.where(causal, _dot_nt(qt, kt), 0.0).astype(BF16)
    return amat, qt, kt, (q * eb).astype(BF16), (k * el).astype(BF16), eq, ek, eb, el, jnp.exp(bl), causal


def hgrn_fwd(proj, lb, norm_g, nh, name):
    t = proj.shape[0]
    w = nh * HGRN_HEAD
    c = HGRN_CHUNK
    rows = _tile(t, (HGRN_ROWS, 128, 64, 32))
    nr, nc = t // rows, rows // c
    hp = _tile(nh, (HGRN_TOGETHER, 2, 1))
    wide = hp * HGRN_HEAD

    def body(q_ref, f_ref, i_ref, g_ref, lb_ref, ng_ref, og_ref, o_ref, st_ref, state):
        @pl.when(pl.program_id(1) == 0)
        def _():
            state[...] = jnp.zeros_like(state)

        def step(cc, carry):
            sl = pl.ds(pl.multiple_of(cc * c, c), c)
            for a in range(hp):
                hc = slice(a * HGRN_HEAD, (a + 1) * HGRN_HEAD)
                q, _, _, f, k = _hgrn_gates(q_ref[sl, hc].astype(F32), f_ref[sl, hc].astype(F32), lb_ref[:, hc])
                v = i_ref[sl, hc]
                amat, _, _, qd, kd, _, _, _, _, ebl, _ = _hgrn_chunk(q, k, f)
                st = state[a]
                st_ref[a, cc] = st.astype(BF16)
                o = _dot_nt(qd, st.astype(BF16)) + _dot_nn(amat, v)
                state[a] = st * ebl + _dot_tn(v, kd)
                o_ref[sl, hc] = o
                graw = g_ref[sl, hc].astype(F32)
                og_ref[sl, hc] = (o * _rstd(o) * ng_ref[...] * (graw * _sigmoid(graw))).astype(BF16)
            return carry

        lax.fori_loop(0, nc, step, 0)

    ng = nh // hp
    col = lambda off: pl.BlockSpec((rows, wide), lambda h, r: (r, off + h))
    return pl.pallas_call(
        body, name=name, grid=(ng, nr),
        in_specs=[col(0), col(ng), col(2 * ng), col(3 * ng),
                  pl.BlockSpec((1, wide), lambda h, r: (0, h)), _full_spec((1, HGRN_HEAD))],
        out_specs=[col(0), col(0), pl.BlockSpec((hp, nc, HGRN_HEAD, HGRN_HEAD), lambda h, r: (h, r, 0, 0))],
        out_shape=[jax.ShapeDtypeStruct((t, w), BF16), jax.ShapeDtypeStruct((t, w), F32),
                   jax.ShapeDtypeStruct((nh, t // c, HGRN_HEAD, HGRN_HEAD), BF16)],
        scratch_shapes=[pltpu.VMEM((hp, HGRN_HEAD, HGRN_HEAD), F32)],
        compiler_params=_params((_PAR, _ARB)),
    )(proj, proj, proj, proj, lb, norm_g)


def hgrn_bwd(proj, lb, norm_g, o, states, dog, nh, name):
    t = proj.shape[0]
    w = nh * HGRN_HEAD
    c = HGRN_CHUNK
    rows = _tile(t, (HGRN_ROWS, 128, 64, 32))
    nr, nc = t // rows, rows // c
    hp = _tile(nh, (HGRN_TOGETHER, 2, 1))
    wide = hp * HGRN_HEAD

    def body(q_ref, f_ref, i_ref, g_ref, lb_ref, ng_ref, o_ref, st_ref, dog_ref,
             dq_ref, df_ref, di_ref, dg_ref, dlb_ref, dng_ref, dstate):
        @pl.when(pl.program_id(1) == 0)
        def _():
            dstate[...] = jnp.zeros_like(dstate)
            dlb_ref[...] = jnp.zeros_like(dlb_ref)
            dng_ref[...] = jnp.zeros_like(dng_ref)

        ng = ng_ref[...]

        def step(idx, carry):
            cc = nc - 1 - idx
            sl = pl.ds(pl.multiple_of(cc * c, c), c)
            for a in range(hp):
                hc = slice(a * HGRN_HEAD, (a + 1) * HGRN_HEAD)
                lb = lb_ref[:, hc]
                qraw, fraw = q_ref[sl, hc].astype(F32), f_ref[sl, hc].astype(F32)
                q, sq, sf, f, k = _hgrn_gates(qraw, fraw, lb)
                v = i_ref[sl, hc]
                amat, qt, kt, qd, kd, eq, ek, eb, el, ebl, causal = _hgrn_chunk(q, k, f)
                ov = o_ref[sl, hc]
                graw = g_ref[sl, hc].astype(F32)
                dogv = dog_ref[sl, hc].astype(F32)
                sil, dsil = _silu_and_grad(graw)
                r = _rstd(ov)
                oh = ov * r
                don = dogv * sil
                dg_ref[sl, hc] = (dogv * oh * ng * dsil).astype(BF16)
                dng_ref[a] += jnp.sum(don * oh, axis=0, keepdims=True)
                doh = don * ng
                do = (r * (doh - oh * jnp.mean(doh * oh, axis=1, keepdims=True))).astype(BF16)
                dst = dstate[a]
                dstb = dst.astype(BF16)
                da = jnp.where(causal, _dot_nt(do, v), 0.0).astype(BF16)
                dv = _dot_tn(amat, do) + _dot_nt(kd, dstb)
                st0 = st_ref[a, cc]
                dq = _dot_nn(da, kt) * eq + _dot_nn(do, st0) * eb
                dk_inter = _dot_nn(v, dstb) * el
                dk = _dot_tn(da, qt) * ek + dk_inter
                dstate[a] = dst * ebl + _dot_tn(do, qd)
                through = jnp.sum(dst * st0.astype(F32), axis=0, keepdims=True) * ebl
                later = jnp.sum(k * dk_inter, axis=0, keepdims=True) + through
                dlf = _dot_exact(_tri(c, True), q * dq - k * dk) + later
                df = dlf / f - dk
                dq_ref[sl, hc] = (dq * (sq * (1.0 + qraw * (1.0 - sq)))).astype(BF16)
                df_ref[sl, hc] = (df * (1.0 - lb) * sf * (1.0 - sf)).astype(BF16)
                di_ref[sl, hc] = dv.astype(BF16)
                dlb_ref[:, hc] += jnp.sum(df * (1.0 - sf), axis=0, keepdims=True)
            return carry

        lax.fori_loop(0, nc, step, 0)

    ngr = nh // hp
    col = lambda off: pl.BlockSpec((rows, wide), lambda h, r: (nr - 1 - r, off + h))
    out = col(0)
    return pl.pallas_call(
        body, name=name, grid=(ngr, nr),
        in_specs=[col(0), col(ngr), col(2 * ngr), col(3 * ngr),
                  pl.BlockSpec((1, wide), lambda h, r: (0, h)), _full_spec((1, HGRN_HEAD)),
                  out, pl.BlockSpec((hp, nc, HGRN_HEAD, HGRN_HEAD), lambda h, r: (h, nr - 1 - r, 0, 0)), out],
        out_specs=[out, out, out, out, pl.BlockSpec((1, wide), lambda h, r: (0, h)),
                   pl.BlockSpec((hp, 1, HGRN_HEAD), lambda h, r: (h, 0, 0))],
        out_shape=[jax.ShapeDtypeStruct((t, w), BF16)] * 4 + [jax.ShapeDtypeStruct((1, w), F32),
                                                             jax.ShapeDtypeStruct((nh, 1, HGRN_HEAD), F32)],
        scratch_shapes=[pltpu.VMEM((hp, HGRN_HEAD, HGRN_HEAD), F32)],
        compiler_params=_params((_PAR, _ARB)),
    )(proj, proj, proj, proj, lb, norm_g, o, states, dog)


def hgrn_mixer_fwd(hn, w_in, w_out, norm_g, lb_param, idx, name):
    nh = w_out.shape[0] // HGRN_HEAD
    lb = lb_table_row(lb_param, idx, name + "_lb")
    proj = mm(hn, w_in, "nn", BF16, name + "_in")
    og, o, states = hgrn_fwd(proj, lb, norm_g, nh, name + "_rec")
    y = mm(og, w_out, "nn", F32, name + "_out")
    return y, (proj, lb, og, o, states)


def hgrn_mixer_bwd(dy, hn, w_in, w_out, norm_g, lb_param, idx, saved, name):
    proj, lb, og, o, states = saved
    nh = w_out.shape[0] // HGRN_HEAD
    dwo = wgrad(og, dy, name + "_dwo")
    dog = mm(dy, w_out, "nt", BF16, name + "_dog")
    dq, df, di, dg, dlb, dng = hgrn_bwd(proj, lb, norm_g, o, states, dog, nh, name + "_recb")
    dproj = jnp.concatenate([dq, df, di, dg], axis=1)
    dwi = wgrad(hn, dproj, name + "_dwi")
    dhn = mm(dproj, w_in, "nt", BF16, name + "_dhn")
    dlbp = lb_table_row_bwd(lb_param, dlb, idx, name + "_lbb")
    return dhn, dwi, dwo, jnp.sum(dng, axis=0), dlbp


_ANY = pl.BlockSpec(memory_space=pl.ANY)
_MESH = pl.DeviceIdType.MESH


def _place():
    x, y, c = lax.axis_index("x"), lax.axis_index("y"), lax.axis_index("c")
    chips = [(1 - x, y), (x, 1 - y), (1 - x, 1 - y)]
    return x, y, c, N_CHIPS // 2 * x + y, chips


def _chip_index(chip):
    return N_CHIPS // 2 * chip[0] + chip[1]


def _window(ref, axis, start, size):
    idx = [slice(None)] * len(ref.shape)
    idx[axis] = pl.ds(start, size)
    return ref.at[tuple(idx)]


_HBM = pl.BlockSpec(memory_space=pltpu.HBM)
_SEMS = pl.BlockSpec(memory_space=pltpu.SEMAPHORE)
_DATAFLOW = pltpu.SideEffectType.DATAFLOW_SIDE_EFFECTING


def _exchange_copy(src, land, axis, sems, k, j, chip, c, sender_side):
    x, y, _, me, _ = _place()
    peer = _chip_index(chip)
    if axis is None:
        src_part = src.at[peer]
        land_part = land.at[me if sender_side else peer]
    else:
        size = src.shape[axis]
        src_part = src
        land_part = _window(land, axis, (me if sender_side else peer) * size, size)
    which = k * (N_CHIPS - 1) + j
    return pltpu.make_async_remote_copy(src_ref=src_part, dst_ref=land_part, send_sem=sems[0].at[which],
                                        recv_sem=sems[1].at[which], device_id=(chip[0], chip[1], c),
                                        device_id_type=_MESH)


def place_own(shard, axis):
    _, _, _, me, _ = _place()
    shape = list(shard.shape)
    shape[axis] *= N_CHIPS
    return lax.dynamic_update_slice_in_dim(lax.empty(tuple(shape), shard.dtype), shard, me * shard.shape[axis], axis)


def place_own_slot(p):
    _, _, _, me, _ = _place()
    mine = lax.dynamic_index_in_dim(p, me, 0, keepdims=True)
    return lax.dynamic_update_slice_in_dim(lax.empty(p.shape, p.dtype), mine, me, 0)


def exchange_start(srcs, lands, axes, name):
    n = len(srcs)

    def body(*refs):
        ins, lnd, sems = refs[:n], refs[n:2 * n], refs[2 * n:2 * n + 2]
        _, _, c, _, chips = _place()
        for k in range(n):
            for j, chip in enumerate(chips):
                _exchange_copy(ins[k], lnd[k], axes[k], sems, k, j, chip, c, True).start()
        refs[-1][...] = jnp.zeros_like(refs[-1])

    sem = pltpu.SemaphoreType.DMA((n * (N_CHIPS - 1),))
    arrays = list(srcs) + list(lands)
    out = pl.pallas_call(
        body, name=name,
        in_specs=[_HBM] * (2 * n),
        out_specs=(_SEMS, _SEMS) + (_HBM,) * (2 * n) + (pl.BlockSpec(memory_space=pltpu.VMEM),),
        out_shape=(sem, sem) + tuple(pltpu.HBM(a.shape, a.dtype) for a in arrays)
        + (jax.ShapeDtypeStruct((8, LANES), F32),),
        input_output_aliases={i: 2 + i for i in range(2 * n)},
        compiler_params=pltpu.CompilerParams(has_side_effects=_DATAFLOW),
    )(*[pltpu.with_memory_space_constraint(a, pltpu.HBM) for a in arrays])
    return (out[0], out[1], list(out[2:2 + n]), list(out[2 + n:2 + 2 * n]), list(axes)), out[-1]


def exchange_wait(started, ks, after, name):
    send, recv, srcs, lands, axes = started
    m = len(ks)

    def body(*refs):
        ins, lnd, sems = refs[:m], refs[m:2 * m], refs[2 * m:2 * m + 2]
        _, _, c, _, chips = _place()
        for q, k in enumerate(ks):
            for j, chip in enumerate(chips):
                cp = _exchange_copy(ins[q], lnd[q], axes[k], sems, k, j, chip, c, False)
                cp.wait_send()
                cp.wait_recv()

    arrays = [srcs[k] for k in ks] + [lands[k] for k in ks]
    out = pl.pallas_call(
        body, name=name,
        in_specs=[_HBM] * (2 * m) + [_SEMS, _SEMS, _ANY],
        out_specs=(_HBM,) * (2 * m),
        out_shape=tuple(pltpu.HBM(a.shape, a.dtype) for a in arrays),
        input_output_aliases={i: i for i in range(2 * m)},
        compiler_params=pltpu.CompilerParams(has_side_effects=_DATAFLOW),
    )(*arrays, send, recv, after)
    return list(out[m:])


def allreduce_small(buf, name):
    rows = buf.shape[0]
    n_dev = 2 * N_CHIPS

    def body(in_ref, out_ref, slots, send, recv):
        x, y, c, me, chips = _place()
        my_id = 2 * me + c
        slots[my_id] = in_ref[...]
        for j in range(1, n_dev):
            fx, fy, fc = (j >> 2) & 1, (j >> 1) & 1, j & 1
            peer = ((1 - x) if fx else x, (1 - y) if fy else y, (1 - c) if fc else c)
            pltpu.make_async_remote_copy(
                src_ref=in_ref, dst_ref=slots.at[my_id], send_sem=send.at[j], recv_sem=recv.at[j],
                device_id=peer, device_id_type=_MESH).start()
        for j in range(1, n_dev):
            fx, fy, fc = (j >> 2) & 1, (j >> 1) & 1, j & 1
            peer = ((1 - x) if fx else x, (1 - y) if fy else y, (1 - c) if fc else c)
            peer_id = 2 * _chip_index(peer) + peer[2]
            landed = pltpu.make_async_remote_copy(
                src_ref=in_ref, dst_ref=slots.at[peer_id], send_sem=send.at[j], recv_sem=recv.at[j],
                device_id=peer, device_id_type=_MESH)
            landed.wait_recv()
            landed.wait_send()
        tot = slots[0]
        for d in range(1, n_dev):
            tot = tot + slots[d]
        out_ref[...] = tot

    return pl.pallas_call(
        body, name=name,
        in_specs=[pl.BlockSpec(memory_space=pltpu.VMEM)], out_specs=pl.BlockSpec(memory_space=pltpu.VMEM),
        out_shape=jax.ShapeDtypeStruct(buf.shape, F32),
        scratch_shapes=[pltpu.VMEM((n_dev, rows, LANES), F32), pltpu.SemaphoreType.DMA((n_dev,)),
                        pltpu.SemaphoreType.DMA((n_dev,))],
        compiler_params=pltpu.CompilerParams(vmem_limit_bytes=VMEM_LIMIT),
    )(buf)


def _sibling_step(src_ref, slots, send, recv, credit, step, n_steps):
    x, y, c = lax.axis_index("x"), lax.axis_index("y"), lax.axis_index("c")
    slot = step % 2

    @pl.when(step >= 2)
    def _():
        pl.semaphore_wait(credit, 1)

    cp = pltpu.make_async_remote_copy(src_ref=src_ref, dst_ref=slots.at[slot], send_sem=send.at[slot],
                                      recv_sem=recv.at[slot], device_id=(x, y, 1 - c), device_id_type=_MESH)
    cp.start()
    cp.wait_recv()
    return cp, slot


def _sibling_done(cp, credit, step, n_steps):
    x, y, c = lax.axis_index("x"), lax.axis_index("y"), lax.axis_index("c")
    cp.wait_send()

    @pl.when(step < n_steps - 2)
    def _():
        pl.semaphore_signal(credit, 1, device_id=(x, y, 1 - c), device_id_type=_MESH)


def pair_sum(g, name):
    by_cols = g.ndim == 2
    s = N_CHIPS if by_cols else g.shape[0]
    r = g.shape[-2]
    cols = g.shape[-1] // s if by_cols else g.shape[-1]
    half = r // 2
    tr = _row_tile(half, cols, 1 << 19)
    nt = half // tr
    n_steps = s * nt

    def body(g_ref, o_ref, slots, send, recv, credit):
        c = lax.axis_index("c")
        step = pl.program_id(0) * nt + pl.program_id(1)
        cp, slot = _sibling_step(g_ref.at[0, 1 - c], slots, send, recv, credit, step, n_steps)
        o_ref[0] = (g_ref[0, c].astype(F32) + slots[slot].astype(F32)).astype(BF16)
        _sibling_done(cp, credit, step, n_steps)

    if by_cols:
        in_spec = pl.BlockSpec((1, 2, tr, cols), lambda k, i: (0, 0, i, k))
        g4 = g.reshape(1, 2, half, s * cols)
    else:
        in_spec = pl.BlockSpec((1, 2, tr, cols), lambda k, i: (k, 0, i, 0))
        g4 = g.reshape(s, 2, half, cols)
    return pl.pallas_call(
        body, name=name, grid=(s, nt),
        in_specs=[in_spec],
        out_specs=pl.BlockSpec((1, tr, cols), lambda k, i: (k, i, 0)),
        out_shape=jax.ShapeDtypeStruct((s, half, cols), BF16),
        scratch_shapes=[pltpu.VMEM((2, tr, cols), BF16), pltpu.SemaphoreType.DMA((2,)), pltpu.SemaphoreType.DMA((2,)),
                        pltpu.SemaphoreType.REGULAR],
        compiler_params=_params((_ARB, _ARB)),
    )(g4)


def chip_sum_share(q, acc, layer, name):
    s, r2, cols = q.shape
    tr = _row_tile(r2, cols, 1 << 18)
    nt = r2 // tr

    def body(q_ref, acc_ref, o_ref, slots, send, recv, credit):
        c = lax.axis_index("c")
        step = pl.program_id(0)
        tot = q_ref[0].astype(F32)
        for k in range(1, s):
            tot = tot + q_ref[k].astype(F32)
        o_ref[0, c] = tot
        cp, slot = _sibling_step(o_ref.at[0, c], slots, send, recv, credit, step, nt)
        o_ref[0, 1 - c] = slots[slot]
        _sibling_done(cp, credit, step, nt)

    return pl.pallas_call(
        body, name=name, grid=(nt,),
        in_specs=[pl.BlockSpec((s, tr, cols), lambda i: (0, i, 0)), _ANY],
        out_specs=pl.BlockSpec((1, 2, tr, cols), lambda i: (layer, 0, i, 0)),
        out_shape=jax.ShapeDtypeStruct(acc.shape, F32),
        input_output_aliases={1: 0},
        scratch_shapes=[pltpu.VMEM((2, tr, cols), F32), pltpu.SemaphoreType.DMA((2,)), pltpu.SemaphoreType.DMA((2,)),
                        pltpu.SemaphoreType.REGULAR],
        compiler_params=_params((_ARB,)),
    )(q, acc)


def _row_tile(rows, cols, budget):
    for tr in (1024, 512, 256, 128, 64, 32, 16, 8):
        if rows % tr == 0 and tr * cols <= budget:
            return tr
    return rows


def adamw(w, g, m, v, name):
    rows, cols = w.shape
    tr = _row_tile(rows, cols, 1 << 18)
    c1 = 1.0 - ADAM_B1 ** ADAM_STEP
    c2 = 1.0 - ADAM_B2 ** ADAM_STEP

    def body(w_ref, g_ref, m_ref, v_ref, d_ref, nm_ref, nv_ref):
        gv = g_ref[...]
        nm = ADAM_B1 * m_ref[...] + (1.0 - ADAM_B1) * gv
        nv = ADAM_B2 * v_ref[...] + (1.0 - ADAM_B2) * (gv * gv)
        nm_ref[...] = nm
        nv_ref[...] = nv
        d_ref[...] = -ADAM_LR * ((nm / c1) / (jnp.sqrt(nv / c2) + ADAM_EPS) + ADAM_WD * w_ref[...])

    spec = pl.BlockSpec((tr, cols), lambda i: (i, 0))
    return pl.pallas_call(
        body, name=name, grid=(rows // tr,),
        in_specs=[spec] * 4, out_specs=[spec] * 3,
        out_shape=[jax.ShapeDtypeStruct((rows, cols), F32)] * 3,
        compiler_params=_params((_PAR,)),
    )(w, g, m, v)


WEIGHTS = ("mix_pre_g", "mix_post_g", "ffn_pre_g", "ffn_post_g", "hgrn_w_in", "hgrn_w_out", "hgrn_norm_g",
           "hgrn_lb_param", "swa_w_in", "swa_w_out", "swa_sinks", "sc_w_in", "sc_conv_w", "sc_w_out", "fox_w_in",
           "fox_b_f", "fox_w_out", "ffn_w_up", "ffn_conv_w", "ffn_conv_b", "ffn_w_down")
N_MIXERS = 4


def _pack_small(parts):
    flat = jnp.concatenate([p.reshape(-1).astype(F32) for p in parts])
    rows = -(-flat.shape[0] // (8 * LANES)) * 8
    return jnp.pad(flat, (0, rows * LANES - flat.shape[0])).reshape(rows, LANES)


def _unpack_small(buf, shapes):
    flat, out, off = buf.reshape(-1), [], 0
    for s in shapes:
        n = math.prod(s)
        out.append(flat[off:off + n].reshape(s))
        off += n
    return out


def _stack_rows(dw):
    return dw.reshape(N_CHIPS, dw.shape[0] // N_CHIPS, dw.shape[1])


def kernel(x, positions, mix_pre_g, mix_post_g, ffn_pre_g, ffn_post_g, hgrn_w_in, hgrn_w_out, hgrn_norm_g, hgrn_lb_param, swa_w_in, swa_w_out, swa_sinks, sc_w_in, sc_conv_w, sc_w_out, fox_w_in, fox_b_f, fox_w_out, ffn_w_up, ffn_conv_w, ffn_conv_b, ffn_w_down, loss_target, m_mix_pre_g, m_mix_post_g, m_ffn_pre_g, m_ffn_post_g, m_hgrn_w_in, m_hgrn_w_out, m_hgrn_norm_g, m_hgrn_lb_param, m_swa_w_in, m_swa_w_out, m_swa_sinks, m_sc_w_in, m_sc_conv_w, m_sc_w_out, m_fox_w_in, m_fox_b_f, m_fox_w_out, m_ffn_w_up, m_ffn_conv_w, m_ffn_conv_b, m_ffn_w_down, v_mix_pre_g, v_mix_post_g, v_ffn_pre_g, v_ffn_post_g, v_hgrn_w_in, v_hgrn_w_out, v_hgrn_norm_g, v_hgrn_lb_param, v_swa_w_in, v_swa_w_out, v_swa_sinks, v_sc_w_in, v_sc_conv_w, v_sc_w_out, v_fox_w_in, v_fox_b_f, v_fox_w_out, v_ffn_w_up, v_ffn_conv_w, v_ffn_conv_b, v_ffn_w_down):
    given = dict(locals())
    depth = mix_pre_g.shape[0]
    assert depth == N_MIXERS and x.shape[0] == 1, "one batch element per device, one layer of each mixer"
    xi, target = x[0], loss_target[0]
    chip = N_CHIPS // 2 * lax.axis_index("x") + lax.axis_index("y")
    nh_fox = fox_b_f.shape[1]
    row = lambda a, i: a[i:i + 1]

    bf = lambda a: a.astype(BF16)
    units = {"hg_in": (bf(hgrn_w_in[0]), 1), "hg_out": (bf(hgrn_w_out[0]), 0), "f_cw": (ffn_conv_w, 2),
             "sw_in": (bf(swa_w_in[0]), 1), "sw_out": (bf(swa_w_out[0]), 0),
             "sc_in": (bf(sc_w_in[0]), 1), "sc_out": (bf(sc_w_out[0]), 0), "sc_cw": (sc_conv_w[0], 1),
             "fx_in": (bf(fox_w_in), 0), "fx_out": (bf(fox_w_out[0]), 0)}
    mix_units = (("hg_in", "hg_out"), ("sw_in", "sw_out"), ("sc_in", "sc_out", "sc_cw"), ("fx_in", "fx_out"))
    ffn_units = []
    for i in range(depth):
        units[f"up{i}"], units[f"down{i}"] = (bf(ffn_w_up[i]), 1), (bf(ffn_w_down[i]), 0)
        ffn_units.append((f"up{i}", f"down{i}") + (("f_cw",) if i == 0 else ()))
    order = [n for i in range(depth) for n in mix_units[i] + ffn_units[i]]
    axes = [units[n][1] for n in order]
    gather, _ = exchange_start([units[n][0] for n in order], [place_own(*units[n]) for n in order], axes,
                               "gather_start")
    wt = {}

    def arrive(names, after, name):
        wt.update(zip(names, exchange_wait(gather, [order.index(n) for n in names], after, name)))

    saved = []
    xs = xi
    hn = rms_fwd(xs, row(mix_pre_g, 0), "pre_norm0")
    dx = loss = None
    for i in range(depth):
        nm = f"l{i}"
        arrive(mix_units[i], hn, nm + "_w_mix")
        if i == 0:
            y, sv = hgrn_mixer_fwd(hn, wt["hg_in"], wt["hg_out"], hgrn_norm_g, hgrn_lb_param, i, nm + "_hgrn")
        elif i == 1:
            y, sv = swa_mixer_fwd(hn, wt["sw_in"], wt["sw_out"], swa_sinks[0], positions, nm + "_swa")
        elif i == 2:
            proj = mm(hn, wt["sc_in"], "nn", BF16, nm + "_sc_in")
            yb = sconv_fwd(proj, wt["sc_cw"], nm + "_sc_conv")
            y, sv = mm(yb, wt["sc_out"], "nn", F32, nm + "_sc_out"), (proj, yb)
        else:
            wt["fx_pad"] = fox_pad_w_in(jnp.concatenate([wt["fx_in"][s] for s in range(N_CHIPS)], axis=1), nh_fox)
            y, sv = fox_mixer_fwd(hn, wt["fx_pad"], wt["fx_out"], fox_b_f[0], nm + "_fox")
        x1, hn2 = resid_norm(xs, y, row(mix_post_g, i), row(ffn_pre_g, i), nm + "_mix_resid")
        arrive(ffn_units[i], hn2, nm + "_w_ffn")
        z = mm(hn2, wt[f"up{i}"], "nn", BF16, nm + "_ffn_up")
        a = ffn_act(z, wt["f_cw"][i], row(ffn_conv_b, i), nm + "_ffn_act")
        y2 = mm(a, wt[f"down{i}"], "nn", F32, nm + "_ffn_down")
        saved.append((xs, hn, y, sv, x1, hn2, z, a, y2))
        if i < depth - 1:
            xs, hn = resid_norm(x1, y2, row(ffn_post_g, i), row(mix_pre_g, i + 1), nm + "_ffn_resid")
        else:
            dx, loss = resid_loss(x1, y2, row(ffn_post_g, i), target, nm + "_loss")

    grads = {}

    def start_reduce(tag, named):
        ps = [pair_sum(g, f"{tag}_pair_{n}") for n, _, g in named]
        started, token = exchange_start(ps, [place_own_slot(p) for p in ps], [None] * len(ps), tag + "_chips_start")
        return (named, started), token

    def finish_reduce(tag, pending, after):
        named, started = pending
        qs = exchange_wait(started, list(range(len(named))), after, tag + "_chips_wait")
        for (n, l, _), q in zip(named, qs):
            if n not in grads:
                grads[n] = lax.empty((given[n].shape[0], 2) + q.shape[1:], F32)
            grads[n] = chip_sum_share(q, grads[n], l, f"{tag}_share_{n}")

    d_pre, d_post, d_fpre, d_fpost = [None] * depth, [None] * depth, [None] * depth, [None] * depth
    d_fcw, d_fcb = [None] * depth, [None] * depth
    small = {}
    pending = token = None
    for i in reversed(range(depth)):
        nm = f"l{i}b"
        xs, hn, y, sv, x1, hn2, z, a, y2 = saved[i]
        f_cw = wt["f_cw"][i]
        dy2, d_fpost[i] = norm_bwd(y2, row(ffn_post_g, i), dx, None, BF16, nm + "_ffn_post", after=token)
        d_down = _stack_rows(wgrad(a, dy2, nm + "_dw_down"))
        da = mm(dy2, wt[f"down{i}"], "nt", BF16, nm + "_da")
        du, acc = ffn_act_bwd(z, da, f_cw, row(ffn_conv_b, i), nm + "_ffn_actb")
        d_fcw[i], d_fcb[i] = acc[0:CONV_WIDTH], acc[CONV_WIDTH]
        dz = conv_transpose(du, f_cw, nm + "_ffn_convT")
        d_up = wgrad(hn2, dz, nm + "_dw_up")
        dhn2 = mm(dz, wt[f"up{i}"], "nt", BF16, nm + "_dhn2")
        dx1, d_fpre[i] = norm_bwd(x1, row(ffn_pre_g, i), dhn2, dx, F32, nm + "_ffn_pre")
        dy, d_post[i] = norm_bwd(y, row(mix_post_g, i), dx1, None, BF16, nm + "_mix_post")
        if i == 0:
            dhn, dwi, dwo, small["hgrn_norm_g"], small["hgrn_lb_param"] = hgrn_mixer_bwd(
                dy, hn, wt["hg_in"], wt["hg_out"], hgrn_norm_g, hgrn_lb_param, i, sv, nm + "_hgrn")
            w_in, w_out = "hgrn_w_in", "hgrn_w_out"
        elif i == 1:
            dhn, dwi, dwo, small["swa_sinks"] = swa_mixer_bwd(dy, hn, wt["sw_in"], wt["sw_out"], swa_sinks[0],
                                                             positions, sv, nm + "_swa")
            w_in, w_out = "swa_w_in", "swa_w_out"
        elif i == 2:
            proj, yb = sv
            dwo = wgrad(yb, dy, nm + "_sc_dwo")
            dyb = mm(dy, wt["sc_out"], "nt", BF16, nm + "_sc_dyb")
            dproj, acc = sconv_bwd(proj, dyb, wt["sc_cw"], nm + "_sc_convb")
            dwi = wgrad(hn, dproj, nm + "_sc_dwi")
            dhn = mm(dproj, wt["sc_in"], "nt", BF16, nm + "_sc_dhn")
            small["sc_conv_w"] = acc[0:CONV_WIDTH]
            w_in, w_out = "sc_w_in", "sc_w_out"
        else:
            dhn, dwi, dwo, small["fox_b_f"] = fox_mixer_bwd(dy, hn, wt["fx_pad"], wt["fx_out"], fox_b_f[0], sv,
                                                            nm + "_fox")
            dwi = fox_unpad_dw(dwi, nh_fox)
            cols = dwi.shape[1] // N_CHIPS
            dwi = jnp.stack([dwi[:, s * cols:(s + 1) * cols] for s in range(N_CHIPS)])
            w_in, w_out = "fox_w_in", "fox_w_out"
        dx, d_pre[i] = norm_bwd(xs, row(mix_pre_g, i), dhn, dx1, F32, nm + "_mix_pre")
        if pending is not None:
            finish_reduce(f"l{i + 1}b", pending, dx)
        pending, token = start_reduce(nm, [(w_in, 0, dwi), (w_out, 0, _stack_rows(dwo)), ("ffn_w_up", i, d_up),
                                           ("ffn_w_down", i, d_down)])
    small.update(mix_pre_g=jnp.concatenate(d_pre), mix_post_g=jnp.concatenate(d_post),
                 ffn_pre_g=jnp.concatenate(d_fpre), ffn_post_g=jnp.concatenate(d_fpost),
                 ffn_conv_w=jnp.stack(d_fcw), ffn_conv_b=jnp.stack(d_fcb))

    small_names = [n for n in WEIGHTS if n in small]
    full_shape = {n: tuple(given[n].shape) for n in small_names}
    full_shape["sc_conv_w"] = (1, CONV_WIDTH, wt["sc_cw"].shape[1])
    full_shape["ffn_conv_w"] = tuple(wt["f_cw"].shape)
    small_sum = allreduce_small(_pack_small([small[n] for n in small_names] + [loss]), "small_sum")
    finish_reduce("l0b", pending, small_sum)
    summed = _unpack_small(small_sum, [full_shape[n] for n in small_names] + [()])
    loss = summed[-1]
    for n, g in zip(small_names, summed):
        if g.shape != given[n].shape:
            width = given[n].shape[-1]
            g = lax.dynamic_slice_in_dim(g, chip * width, width, axis=g.ndim - 1)
        grads[n] = g

    deltas, new_m, new_v = {}, {}, {}
    for n in WEIGHTS:
        w = given[n]
        flat = lambda a: a.reshape(-1, w.shape[-1])
        dl, nm_, nv_ = adamw(flat(w), flat(grads[n]), flat(given["m_" + n]), flat(given["v_" + n]), "adamw_" + n)
        deltas[n], new_m[n], new_v[n] = dl.reshape(w.shape), nm_.reshape(w.shape), nv_.reshape(w.shape)
    return (loss, dx[None], *[grads[n].reshape(given[n].shape) for n in WEIGHTS], *[deltas[n] for n in WEIGHTS],
            *[new_m[n] for n in WEIGHTS], *[new_v[n] for n in WEIGHTS])
```

```python
import functools
import math

import numpy as np
import jax
import jax.numpy as jnp
from jax import lax
from jax.experimental import pallas as pl
from jax.experimental.pallas import tpu as pltpu

F32 = jnp.float32
BF16 = jnp.bfloat16

RMS_EPS = 1e-6
HGRN_HEAD = 128
HGRN_CHUNK = 32
ATT_HEAD = 64
SWA_WINDOW = 128
SWA_GROUP = 8
ROT_DIM = 16
ROPE_THETA = 500000.0
CONV_WIDTH = 3
ADAM_LR = 0.001
ADAM_B1 = 0.9
ADAM_B2 = 0.999
ADAM_EPS = 1e-08
ADAM_WD = 0.01
ADAM_STEP = 10
N_CHIPS = 4
LANES = 128
BF16_ROWS = 16
VMEM_LIMIT = 48 * 1024 * 1024

_ARB = "arbitrary"
_PAR = "parallel"


def _params(sem, **kw):
    return pltpu.CompilerParams(dimension_semantics=sem, vmem_limit_bytes=VMEM_LIMIT, **kw)


def _tile(n, prefs):
    for p in prefs:
        if n % p == 0:
            return p
    return n


def _sigmoid(x):
    return 1.0 / (1.0 + jnp.exp(-x))


def _dot(a, b, dims):
    return lax.dot_general(a, b, (dims, ((), ())), preferred_element_type=F32)


def _dot_nn(a, b):
    return _dot(a, b, ((1,), (0,)))


def _dot_nt(a, b):
    return _dot(a, b, ((1,), (1,)))


def _dot_tn(a, b):
    return _dot(a, b, ((0,), (0,)))


MM_VMEM_BUDGET = 36 * 1024 * 1024
MM_HBM_RATE = 3.0e12
MM_MXU_RATE = 6.5e14
MM_STEP_S = 0.35e-6
MM_ACC_RATE = 3.0e12


def _mm_tiles(m, n, k, out_bytes):
    best = None
    for tm in (2048, 1024, 512, 256, 128):
        for tn in (2048, 1024, 512, 256, 128):
            for tk in sorted({k, 4096, 2816, 2048, 1408, 1024, 512, 256, 128}, reverse=True):
                if m % tm or n % tn or tk > k or k % tk:
                    continue
                nk = k // tk
                vmem = 4 * (tm * tk + tk * tn) + (4 * tm * tn if nk > 1 else 0) + 2 * tm * tn * out_bytes
                if vmem > MM_VMEM_BUDGET:
                    continue
                steps = (m // tm) * (n // tn) * nk
                traffic = 2 * m * k * (1 if nk == 1 else n // tn) + 2 * k * n * (m // tm) + m * n * out_bytes
                cost = max(traffic / MM_HBM_RATE, 2 * m * n * k / MM_MXU_RATE) + steps * MM_STEP_S
                if nk > 1:
                    cost += steps * 8 * tm * tn / MM_ACC_RATE
                if best is None or cost < best[0]:
                    best = (cost, tm, tn, tk)
    assert best is not None, (m, n, k)
    return best[1:]


def mm(a, b, mode, out_dtype, name="mm"):
    if mode == "nn":
        (m, k), (k2, n) = a.shape, b.shape
    elif mode == "nt":
        (m, k), (n, k2) = a.shape, b.shape
    else:
        (k, m), (k2, n) = a.shape, b.shape
    assert k == k2, (a.shape, b.shape, mode)
    tm, tn, tk = _mm_tiles(m, n, k, jnp.dtype(out_dtype).itemsize)
    nk = k // tk

    def product(a_ref, b_ref):
        av = a_ref[...].astype(BF16)
        bv = b_ref[...].astype(BF16)
        return {"nn": _dot_nn, "nt": _dot_nt, "tn": _dot_tn}[mode](av, bv)

    def body_one(a_ref, b_ref, o_ref):
        o_ref[...] = product(a_ref, b_ref).astype(out_dtype)

    def body_acc(a_ref, b_ref, o_ref, acc_ref):
        kk = pl.program_id(2)

        @pl.when(kk == 0)
        def _():
            acc_ref[...] = jnp.zeros_like(acc_ref)

        acc_ref[...] += product(a_ref, b_ref)

        @pl.when(kk == nk - 1)
        def _():
            o_ref[...] = acc_ref[...].astype(out_dtype)

    if mode == "nn":
        a_spec = pl.BlockSpec((tm, tk), lambda i, j, kk: (i, kk))
        b_spec = pl.BlockSpec((tk, tn), lambda i, j, kk: (kk, j))
    elif mode == "nt":
        a_spec = pl.BlockSpec((tm, tk), lambda i, j, kk: (i, kk))
        b_spec = pl.BlockSpec((tn, tk), lambda i, j, kk: (j, kk))
    else:
        a_spec = pl.BlockSpec((tk, tm), lambda i, j, kk: (kk, i))
        b_spec = pl.BlockSpec((tk, tn), lambda i, j, kk: (kk, j))
    return pl.pallas_call(
        body_one if nk == 1 else body_acc,
        name=name,
        grid=(m // tm, n // tn, nk),
        in_specs=[a_spec, b_spec],
        out_specs=pl.BlockSpec((tm, tn), lambda i, j, kk: (i, j)),
        out_shape=jax.ShapeDtypeStruct((m, n), out_dtype),
        scratch_shapes=[] if nk == 1 else [pltpu.VMEM((tm, tn), F32)],
        compiler_params=_params((_PAR, _PAR, _ARB)),
    )(a, b)


def wgrad(a, b, name):
    return mm(a, b, "tn", BF16, name)


def _rstd(xv):
    return lax.rsqrt(jnp.mean(xv * xv, axis=1, keepdims=True) + RMS_EPS)


def _row_spec(tr, w):
    return pl.BlockSpec((tr, w), lambda i: (i, 0))


def _full_spec(shape):
    nd = len(shape)
    return pl.BlockSpec(shape, lambda *_: (0,) * nd)


def rms_fwd(x, g, name):
    t, d = x.shape
    tr = _tile(t, (256, 128, 64, 32, 16))

    def body(x_ref, g_ref, o_ref):
        xv = x_ref[...]
        o_ref[...] = (xv * _rstd(xv) * g_ref[...]).astype(BF16)

    return pl.pallas_call(
        body, name=name, grid=(t // tr,),
        in_specs=[_row_spec(tr, d), _full_spec((1, d))],
        out_specs=_row_spec(tr, d),
        out_shape=jax.ShapeDtypeStruct((t, d), BF16),
        compiler_params=_params((_PAR,)),
    )(x, g)


def resid_norm(x, y, g_post, g_next, name):
    t, d = x.shape
    tr = _tile(t, (256, 128, 64, 32, 16))

    def body(x_ref, y_ref, gp_ref, gn_ref, x1_ref, hn_ref):
        yv = y_ref[...]
        x1 = x_ref[...] + yv * _rstd(yv) * gp_ref[...]
        x1_ref[...] = x1
        hn_ref[...] = (x1 * _rstd(x1) * gn_ref[...]).astype(BF16)

    return pl.pallas_call(
        body, name=name, grid=(t // tr,),
        in_specs=[_row_spec(tr, d), _row_spec(tr, d), _full_spec((1, d)), _full_spec((1, d))],
        out_specs=[_row_spec(tr, d), _row_spec(tr, d)],
        out_shape=[jax.ShapeDtypeStruct((t, d), F32), jax.ShapeDtypeStruct((t, d), BF16)],
        compiler_params=_params((_PAR,)),
    )(x, y, g_post, g_next)


def resid_loss(x, y, g_post, target, name):
    t, d = x.shape
    tr = _tile(t, (256, 128, 64, 32, 16))

    def body(x_ref, y_ref, gp_ref, t_ref, dx_ref, loss_ref):
        @pl.when(pl.program_id(0) == 0)
        def _():
            loss_ref[...] = jnp.zeros_like(loss_ref)

        yv = y_ref[...]
        err = x_ref[...] + yv * _rstd(yv) * gp_ref[...] - t_ref[...]
        dx_ref[...] = err * (1.0 / d)
        loss_ref[...] += 0.5 * jnp.sum(jnp.mean(err * err, axis=1, keepdims=True), axis=0, keepdims=True)

    dx, loss = pl.pallas_call(
        body, name=name, grid=(t // tr,),
        in_specs=[_row_spec(tr, d), _row_spec(tr, d), _full_spec((1, d)), _row_spec(tr, d)],
        out_specs=[_row_spec(tr, d), _full_spec((8, LANES))],
        out_shape=[jax.ShapeDtypeStruct((t, d), F32), jax.ShapeDtypeStruct((8, LANES), F32)],
        compiler_params=_params((_ARB,)),
    )(x, y, g_post, target)
    return dx, loss[0:1, 0:1]


def norm_bwd(yin, g, dout, res, out_dtype, name, after=None):
    t, d = yin.shape
    tr = _tile(t, (256, 128, 64, 32, 16))
    has_res = res is not None

    def body(*refs):
        refs = refs[:3 + has_res] + refs[-2:]
        if has_res:
            y_ref, g_ref, d_ref, r_ref, o_ref, dg_ref = refs
        else:
            y_ref, g_ref, d_ref, o_ref, dg_ref = refs

        @pl.when(pl.program_id(0) == 0)
        def _():
            dg_ref[...] = jnp.zeros_like(dg_ref)

        yv = y_ref[...]
        dv = d_ref[...].astype(F32)
        r = _rstd(yv)
        yh = yv * r
        dyh = dv * g_ref[...]
        dy = r * (dyh - yh * jnp.mean(dyh * yh, axis=1, keepdims=True))
        if has_res:
            dy = dy + r_ref[...]
        o_ref[...] = dy.astype(out_dtype)
        dg_ref[...] += jnp.sum(dv * yh, axis=0, keepdims=True)

    ins = [yin, g, dout] + ([res] if has_res else []) + ([] if after is None else [after])
    in_specs = ([_row_spec(tr, d), _full_spec((1, d)), _row_spec(tr, d)] + ([_row_spec(tr, d)] if has_res else [])
                + ([] if after is None else [pl.BlockSpec(memory_space=pl.ANY)]))
    return pl.pallas_call(
        body, name=name, grid=(t // tr,),
        in_specs=in_specs,
        out_specs=[_row_spec(tr, d), _full_spec((1, d))],
        out_shape=[jax.ShapeDtypeStruct((t, d), out_dtype), jax.ShapeDtypeStruct((1, d), F32)],
        compiler_params=_params((_ARB,)),
    )(*ins)


def _shift_down(x, halo):
    tr = x.shape[0]
    row = lax.broadcasted_iota(jnp.int32, x.shape, 0)
    h1 = halo[BF16_ROWS - 1:BF16_ROWS, :]
    h2 = halo[BF16_ROWS - 2:BF16_ROWS - 1, :]
    x1 = jnp.where(row == 0, h1, pltpu.roll(x, 1, 0))
    x2 = jnp.where(row == 0, h2, jnp.where(row == 1, h1, pltpu.roll(x, 2, 0)))
    return x1, x2


def _shift_up(x, halo):
    tr = x.shape[0]
    row = lax.broadcasted_iota(jnp.int32, x.shape, 0)
    h0 = halo[0:1, :]
    h1 = halo[1:2, :]
    x1 = jnp.where(row == tr - 1, h0, pltpu.roll(x, tr - 1, 0))
    x2 = jnp.where(row == tr - 1, h1, jnp.where(row == tr - 2, h0, pltpu.roll(x, tr - 2, 0)))
    return x1, x2


def _prev_halo_spec(tr, w, nt):
    return pl.BlockSpec((BF16_ROWS, w), lambda i: (jnp.maximum(i * (tr // BF16_ROWS) - 1, 0), 0))


def _next_halo_spec(tr, w, nt):
    last = nt * (tr // BF16_ROWS) - 1
    return pl.BlockSpec((BF16_ROWS, w), lambda i: (jnp.minimum((i + 1) * (tr // BF16_ROWS), last), 0))


def _silu_and_grad(u):
    s = _sigmoid(u)
    return u * s, s * (1.0 + u * (1.0 - s))


def ffn_act(z, conv_w, conv_b, name):
    t, f2 = z.shape
    f = f2 // 2
    tr = _tile(t, (128, 64, 32, 16))
    nt = t // tr
    cw = _tile(f, (512, 256, 128))

    def body(z_ref, zp_ref, w_ref, b_ref, a_ref):
        first = pl.program_id(0) == 0
        for j in range(f // cw):
            us = []
            for off in (j * cw, f + j * cw):
                cols = slice(off, off + cw)
                zc = z_ref[:, cols].astype(F32)
                hp = jnp.where(first, 0.0, zp_ref[:, cols].astype(F32))
                z1, z2 = _shift_down(zc, hp)
                us.append(w_ref[2:3, cols] * zc + w_ref[1:2, cols] * z1 + w_ref[0:1, cols] * z2 + b_ref[:, cols])
            sil, _ = _silu_and_grad(us[0])
            a_ref[:, j * cw:(j + 1) * cw] = (sil * us[1]).astype(BF16)

    return pl.pallas_call(
        body, name=name, grid=(nt,),
        in_specs=[_row_spec(tr, f2), _prev_halo_spec(tr, f2, nt), _full_spec((CONV_WIDTH, f2)), _full_spec((1, f2))],
        out_specs=_row_spec(tr, f),
        out_shape=jax.ShapeDtypeStruct((t, f), BF16),
        compiler_params=_params((_PAR,)),
    )(z, z, conv_w, conv_b)


def ffn_act_bwd(z, da, conv_w, conv_b, name):
    t, f2 = z.shape
    f = f2 // 2
    tr = _tile(t, (128, 64, 32, 16))
    nt = t // tr
    cw = _tile(f, (512, 256, 128))

    def body(z_ref, zp_ref, da_ref, w_ref, b_ref, du_ref, acc_ref):
        first = pl.program_id(0) == 0

        @pl.when(first)
        def _():
            acc_ref[...] = jnp.zeros_like(acc_ref)

        for j in range(f // cw):
            us, zs = [], []
            for off in (j * cw, f + j * cw):
                cols = slice(off, off + cw)
                zc = z_ref[:, cols].astype(F32)
                hp = jnp.where(first, 0.0, zp_ref[:, cols].astype(F32))
                z1, z2 = _shift_down(zc, hp)
                zs.append((z2, z1, zc))
                us.append(w_ref[2:3, cols] * zc + w_ref[1:2, cols] * z1 + w_ref[0:1, cols] * z2 + b_ref[:, cols])
            dav = da_ref[:, j * cw:(j + 1) * cw].astype(F32)
            sil, dsil = _silu_and_grad(us[0])
            dus = (dav * us[1] * dsil, dav * sil)
            for off, du, zsh in zip((j * cw, f + j * cw), dus, zs):
                cols = slice(off, off + cw)
                du_ref[:, cols] = du.astype(BF16)
                for k in range(CONV_WIDTH):
                    acc_ref[k:k + 1, cols] += jnp.sum(du * zsh[k], axis=0, keepdims=True)
                acc_ref[3:4, cols] += jnp.sum(du, axis=0, keepdims=True)

    return pl.pallas_call(
        body, name=name, grid=(nt,),
        in_specs=[_row_spec(tr, f2), _prev_halo_spec(tr, f2, nt), _row_spec(tr, f),
                  _full_spec((CONV_WIDTH, f2)), _full_spec((1, f2))],
        out_specs=[_row_spec(tr, f2), _full_spec((8, f2))],
        out_shape=[jax.ShapeDtypeStruct((t, f2), BF16), jax.ShapeDtypeStruct((8, f2), F32)],
        compiler_params=_params((_ARB,)),
    )(z, z, da, conv_w, conv_b)


def conv_transpose(du, conv_w, name):
    t, w = du.shape
    tr = _tile(t, (128, 64, 32, 16))
    nt = t // tr
    cw = _tile(w, (512, 256, 128))

    def body(d_ref, dn_ref, w_ref, o_ref):
        last = pl.program_id(0) == nt - 1
        for j in range(w // cw):
            cols = slice(j * cw, (j + 1) * cw)
            dc = d_ref[:, cols].astype(F32)
            hn = jnp.where(last, 0.0, dn_ref[:, cols].astype(F32))
            d1, d2 = _shift_up(dc, hn)
            o_ref[:, cols] = (w_ref[2:3, cols] * dc + w_ref[1:2, cols] * d1 + w_ref[0:1, cols] * d2).astype(BF16)

    return pl.pallas_call(
        body, name=name, grid=(nt,),
        in_specs=[_row_spec(tr, w), _next_halo_spec(tr, w, nt), _full_spec((CONV_WIDTH, w))],
        out_specs=_row_spec(tr, w),
        out_shape=jax.ShapeDtypeStruct((t, w), BF16),
        compiler_params=_params((_PAR,)),
    )(du, du, conv_w)


def sconv_fwd(proj, conv_w, name):
    t, w3 = proj.shape
    d = w3 // 3
    tr = _tile(t, (128, 64, 32, 16))
    nt = t // tr
    cw = _tile(d, (512, 256, 128))

    def body(p_ref, pp_ref, w_ref, o_ref):
        first = pl.program_id(0) == 0
        for j in range(d // cw):
            cb, cc, cx = (slice(k * d + j * cw, k * d + (j + 1) * cw) for k in range(3))
            zc = p_ref[:, cc].astype(F32) * p_ref[:, cx].astype(F32)
            hp = jnp.where(first, 0.0, pp_ref[:, cc].astype(F32) * pp_ref[:, cx].astype(F32))
            z1, z2 = _shift_down(zc, hp)
            wc = slice(j * cw, (j + 1) * cw)
            cz = w_ref[2:3, wc] * zc + w_ref[1:2, wc] * z1 + w_ref[0:1, wc] * z2
            o_ref[:, wc] = (p_ref[:, cb].astype(F32) * cz).astype(BF16)

    return pl.pallas_call(
        body, name=name, grid=(nt,),
        in_specs=[_row_spec(tr, w3), _prev_halo_spec(tr, w3, nt), _full_spec((CONV_WIDTH, d))],
        out_specs=_row_spec(tr, d),
        out_shape=jax.ShapeDtypeStruct((t, d), BF16),
        compiler_params=_params((_PAR,)),
    )(proj, proj, conv_w)


def sconv_bwd(proj, dyb, conv_w, name):
    t, w3 = proj.shape
    d = w3 // 3
    tr = _tile(t, (128, 64, 32, 16))
    nt = t // tr
    cw = _tile(d, (512, 256, 128))

    def body(p_ref, pp_ref, pn_ref, dy_ref, dyn_ref, w_ref, o_ref, acc_ref):
        first = pl.program_id(0) == 0
        last = pl.program_id(0) == nt - 1

        @pl.when(first)
        def _():
            acc_ref[...] = jnp.zeros_like(acc_ref)

        for j in range(d // cw):
            cb, cc, cx = (slice(k * d + j * cw, k * d + (j + 1) * cw) for k in range(3))
            wc = slice(j * cw, (j + 1) * cw)
            bv, cv, xv = p_ref[:, cb].astype(F32), p_ref[:, cc].astype(F32), p_ref[:, cx].astype(F32)
            zc = cv * xv
            hp = jnp.where(first, 0.0, pp_ref[:, cc].astype(F32) * pp_ref[:, cx].astype(F32))
            z1, z2 = _shift_down(zc, hp)
            w0, w1, w2 = w_ref[0:1, wc], w_ref[1:2, wc], w_ref[2:3, wc]
            cz = w2 * zc + w1 * z1 + w0 * z2
            dyv = dy_ref[:, wc].astype(F32)
            dcz = dyv * bv
            hn = jnp.where(last, 0.0, dyn_ref[:, wc].astype(F32) * pn_ref[:, cb].astype(F32))
            n1, n2 = _shift_up(dcz, hn)
            dz = w2 * dcz + w1 * n1 + w0 * n2
            o_ref[:, cb] = (dyv * cz).astype(BF16)
            o_ref[:, cc] = (dz * xv).astype(BF16)
            o_ref[:, cx] = (dz * cv).astype(BF16)
            for k, zsh in enumerate((z2, z1, zc)):
                acc_ref[k:k + 1, wc] += jnp.sum(dcz * zsh, axis=0, keepdims=True)

    return pl.pallas_call(
        body, name=name, grid=(nt,),
        in_specs=[_row_spec(tr, w3), _prev_halo_spec(tr, w3, nt), _next_halo_spec(tr, w3, nt),
                  _row_spec(tr, d), _next_halo_spec(tr, d, nt), _full_spec((CONV_WIDTH, d))],
        out_specs=[_row_spec(tr, w3), _full_spec((8, d))],
        out_shape=[jax.ShapeDtypeStruct((t, w3), BF16), jax.ShapeDtypeStruct((8, d), F32)],
        compiler_params=_params((_ARB,)),
    )(proj, proj, proj, dyb, dyb, conv_w)


def rope_tables(positions):
    half = ROT_DIM // 2
    inv_freq = ROPE_THETA ** (-jnp.arange(half, dtype=F32) / half)
    ang = positions.astype(F32)[:, None] * inv_freq[None, :]
    cos, sin = jnp.cos(ang), jnp.sin(ang)
    ones = jnp.ones((positions.shape[0], ATT_HEAD - ROT_DIM), F32)
    c64 = jnp.concatenate([cos, cos, ones], axis=1)
    s64 = jnp.concatenate([-sin, sin, 0.0 * ones], axis=1)
    perm = np.zeros((LANES, LANES), np.float32)
    for lane in range(LANES):
        dim = lane % ATT_HEAD
        if dim < half:
            perm[lane + half, lane] = 1.0
        elif dim < ROT_DIM:
            perm[lane - half, lane] = 1.0
    return jnp.tile(c64, (1, 2)), jnp.tile(s64, (1, 2)), jnp.asarray(perm, BF16)


def rope(xin, ctab, stab, perm, n_rot, sign, name):
    t, w = xin.shape
    tr = _tile(t, (256, 128, 64, 32, 16))

    def body(x_ref, c_ref, s_ref, p_ref, o_ref):
        cv, sv = c_ref[...], s_ref[...] * sign
        for j in range(n_rot // LANES):
            cols = slice(j * LANES, (j + 1) * LANES)
            xb = x_ref[:, cols]
            o_ref[:, cols] = (xb.astype(F32) * cv + _dot_nn(xb, p_ref[...]) * sv).astype(BF16)
        if n_rot < w:
            o_ref[:, n_rot:] = x_ref[:, n_rot:]

    return pl.pallas_call(
        body, name=name, grid=(t // tr,),
        in_specs=[_row_spec(tr, w), _row_spec(tr, LANES), _row_spec(tr, LANES), _full_spec((LANES, LANES))],
        out_specs=_row_spec(tr, w),
        out_shape=jax.ShapeDtypeStruct((t, w), BF16),
        compiler_params=_params((_PAR,)),
    )(xin, ctab, stab, perm)


NEG = -1e30


def _half(shape, h):
    return (lax.broadcasted_iota(jnp.int32, shape, 1) // ATT_HEAD) == h


def _dup_head(xb, kvh):
    xf = jnp.where(_half(xb.shape, kvh), xb.astype(F32), 0.0)
    return (xf + pltpu.roll(xf, ATT_HEAD, 1)).astype(BF16)


def _swa_mask(n, rows, cur_only):
    w = SWA_WINDOW
    shape = (w, w) if cur_only else (w, 2 * w)
    qi = lax.broadcasted_iota(jnp.int32, shape, 0)
    kj = lax.broadcasted_iota(jnp.int32, shape, 1) + (w if cur_only else 0)
    diff = qi + w - kj
    ok = (diff >= 0) & (diff < w)
    return ok & ((kj >= w) | (n > 0))


def swa_fwd(qkv, sinks, hq, name):
    t = qkv.shape[0]
    w = SWA_WINDOW
    nb = t // w
    hkv = hq // SWA_GROUP
    npair = hkv // 2
    qw = 2 * SWA_GROUP * ATT_HEAD
    kcol = hq * ATT_HEAD // LANES
    vcol = kcol + npair
    scale = ATT_HEAD ** -0.5

    def body(sink_ref, q_ref, kp_ref, kc_ref, vp_ref, vc_ref, o_ref, lse_ref):
        m, n = pl.program_id(0), pl.program_id(1)
        kb = jnp.concatenate([kp_ref[...], kc_ref[...]], axis=0)
        vb = jnp.concatenate([vp_ref[...], vc_ref[...]], axis=0)
        ok = _swa_mask(n, w, False)
        for kvh in range(2):
            kd, vd = _dup_head(kb, kvh), _dup_head(vb, kvh)
            for jj in range(SWA_GROUP // 2):
                jp = kvh * (SWA_GROUP // 2) + jj
                q2 = q_ref[:, jp * LANES:(jp + 1) * LANES]
                outs = []
                for a in range(2):
                    qa = jnp.where(_half(q2.shape, a), q2, jnp.zeros_like(q2))
                    s = jnp.where(ok, _dot_nt(qa, kd) * scale, NEG)
                    sink = sink_ref[m * 2 * SWA_GROUP + jp * 2 + a]
                    mx = jnp.maximum(jnp.max(s, axis=1, keepdims=True), sink)
                    e = jnp.exp(s - mx)
                    den = jnp.sum(e, axis=1, keepdims=True) + jnp.exp(sink - mx)
                    p = (e / den).astype(BF16)
                    outs.append(_dot_nn(p, vd))
                    lse_ref[jp * 2 + a] = jnp.broadcast_to(mx + jnp.log(den), (w, LANES))
                o_ref[:, jp * LANES:(jp + 1) * LANES] = jnp.where(_half(outs[0].shape, 0), outs[0], outs[1]).astype(BF16)

    prev = lambda m, n: jnp.maximum(n - 1, 0)
    grid_spec = pltpu.PrefetchScalarGridSpec(
        num_scalar_prefetch=1, grid=(npair, nb),
        in_specs=[
            pl.BlockSpec((w, qw), lambda m, n, s: (n, m)),
            pl.BlockSpec((w, LANES), lambda m, n, s: (prev(m, n), kcol + m)),
            pl.BlockSpec((w, LANES), lambda m, n, s: (n, kcol + m)),
            pl.BlockSpec((w, LANES), lambda m, n, s: (prev(m, n), vcol + m)),
            pl.BlockSpec((w, LANES), lambda m, n, s: (n, vcol + m)),
        ],
        out_specs=[
            pl.BlockSpec((w, qw), lambda m, n, s: (n, m)),
            pl.BlockSpec((2 * SWA_GROUP, w, LANES), lambda m, n, s: (m, n, 0)),
        ],
    )
    return pl.pallas_call(
        body, name=name, grid_spec=grid_spec,
        out_shape=[jax.ShapeDtypeStruct((t, hq * ATT_HEAD), BF16), jax.ShapeDtypeStruct((hq, t, LANES), F32)],
        compiler_params=_params((_PAR, _PAR)),
    )(sinks, qkv, qkv, qkv, qkv, qkv)


def swa_bwd(qkv, o, lse, do, sinks, hq, name):
    t = qkv.shape[0]
    w = SWA_WINDOW
    nb = t // w
    hkv = hq // SWA_GROUP
    npair = hkv // 2
    qw = 2 * SWA_GROUP * ATT_HEAD
    kcol = hq * ATT_HEAD // LANES
    vcol = kcol + npair
    scale = ATT_HEAD ** -0.5
    gh = 2 * SWA_GROUP

    def body(sink_ref, qc_ref, qn_ref, kp_ref, kc_ref, vp_ref, vc_ref, oc_ref, on_ref, dc_ref, dn_ref,
             lc_ref, ln_ref, dq_ref, dk_ref, dv_ref, ds_ref):
        m, n = pl.program_id(0), pl.program_id(1)
        kb = jnp.concatenate([kp_ref[...], kc_ref[...]], axis=0)
        vb = jnp.concatenate([vp_ref[...], vc_ref[...]], axis=0)
        ok_band = _swa_mask(n, w, False)
        ok_cur = _swa_mask(n, w, True)
        qi = lax.broadcasted_iota(jnp.int32, (w, w), 0)
        kj = lax.broadcasted_iota(jnp.int32, (w, w), 1)
        ok_next = (kj > qi) & (n < nb - 1)
        row16 = lax.broadcasted_iota(jnp.int32, (gh, LANES), 0)
        dsink = jnp.zeros((gh, LANES), F32)
        dk_tot = jnp.zeros((w, LANES), F32)
        dv_tot = jnp.zeros((w, LANES), F32)
        for kvh in range(2):
            kd, vd = _dup_head(kb, kvh), _dup_head(vb, kvh)
            kdc, vdc = kd[w:, :], vd[w:, :]
            acc_k = [jnp.zeros((w, LANES), F32), jnp.zeros((w, LANES), F32)]
            acc_v = [jnp.zeros((w, LANES), F32), jnp.zeros((w, LANES), F32)]
            for jj in range(SWA_GROUP // 2):
                jp = kvh * (SWA_GROUP // 2) + jj
                cols = slice(jp * LANES, (jp + 1) * LANES)
                dqs = []
                for a in range(2):
                    hd = jp * 2 + a
                    sink = sink_ref[m * gh + hd]
                    half = _half((w, LANES), a)
                    q2 = jnp.where(half, qc_ref[:, cols], jnp.zeros((w, LANES), BF16))
                    d2 = jnp.where(half, dc_ref[:, cols], jnp.zeros((w, LANES), BF16))
                    delta = jnp.sum(d2.astype(F32) * oc_ref[:, cols].astype(F32), axis=1, keepdims=True)
                    lse_c = lc_ref[hd][:, 0:1]
                    p = jnp.exp(jnp.where(ok_band, _dot_nt(q2, kd) * scale, NEG) - lse_c)
                    dsv = p * (_dot_nt(d2, vd) - delta)
                    dqs.append(_dot_nn(dsv.astype(BF16), kd) * scale)
                    psink = jnp.exp(sink - lse_c)
                    dsink = jnp.where(row16 == hd, dsink - jnp.sum(psink * delta, axis=0, keepdims=True), dsink)
                    for q_ref, d_ref, o_ref, l_ref, okm in ((qc_ref, dc_ref, oc_ref, lc_ref, ok_cur),
                                                           (qn_ref, dn_ref, on_ref, ln_ref, ok_next)):
                        q2 = jnp.where(half, q_ref[:, cols], jnp.zeros((w, LANES), BF16))
                        d2 = jnp.where(half, d_ref[:, cols], jnp.zeros((w, LANES), BF16))
                        delta = jnp.sum(d2.astype(F32) * o_ref[:, cols].astype(F32), axis=1, keepdims=True)
                        p = jnp.exp(jnp.where(okm, _dot_nt(q2, kdc) * scale, NEG) - l_ref[hd][:, 0:1])
                        dsv = p * (_dot_nt(d2, vdc) - delta)
                        acc_v[a] = acc_v[a] + _dot_tn(p.astype(BF16), d2)
                        acc_k[a] = acc_k[a] + _dot_tn(dsv.astype(BF16), q2) * scale
                dq_ref[:, cols] = jnp.where(_half((w, LANES), 0), dqs[0], dqs[1]).astype(BF16)
            dk_tot = dk_tot + acc_k[kvh] + pltpu.roll(acc_k[1 - kvh], ATT_HEAD, 1)
            dv_tot = dv_tot + acc_v[kvh] + pltpu.roll(acc_v[1 - kvh], ATT_HEAD, 1)
        dk_ref[...] = dk_tot.astype(BF16)
        dv_ref[...] = dv_tot.astype(BF16)
        ds_ref[0, 0] = dsink

    prev = lambda n: jnp.maximum(n - 1, 0)
    nxt = lambda n: jnp.minimum(n + 1, nb - 1)
    qspec = lambda f: pl.BlockSpec((w, qw), lambda m, n, s: (f(n), m))
    lspec = lambda f: pl.BlockSpec((gh, w, LANES), lambda m, n, s: (m, f(n), 0))
    same = lambda n: n
    grid_spec = pltpu.PrefetchScalarGridSpec(
        num_scalar_prefetch=1, grid=(npair, nb),
        in_specs=[
            qspec(same), qspec(nxt),
            pl.BlockSpec((w, LANES), lambda m, n, s: (prev(n), kcol + m)),
            pl.BlockSpec((w, LANES), lambda m, n, s: (n, kcol + m)),
            pl.BlockSpec((w, LANES), lambda m, n, s: (prev(n), vcol + m)),
            pl.BlockSpec((w, LANES), lambda m, n, s: (n, vcol + m)),
            qspec(same), qspec(nxt), qspec(same), qspec(nxt),
            lspec(same), lspec(nxt),
        ],
        out_specs=[
            pl.BlockSpec((w, qw), lambda m, n, s: (n, m)),
            pl.BlockSpec((w, LANES), lambda m, n, s: (n, m)),
            pl.BlockSpec((w, LANES), lambda m, n, s: (n, m)),
            pl.BlockSpec((1, 1, gh, LANES), lambda m, n, s: (m, n, 0, 0)),
        ],
    )
    return pl.pallas_call(
        body, name=name, grid_spec=grid_spec,
        out_shape=[jax.ShapeDtypeStruct((t, hq * ATT_HEAD), BF16),
                   jax.ShapeDtypeStruct((t, hkv * ATT_HEAD), BF16),
                   jax.ShapeDtypeStruct((t, hkv * ATT_HEAD), BF16),
                   jax.ShapeDtypeStruct((npair, nb, gh, LANES), F32)],
        compiler_params=_params((_PAR, _PAR)),
    )(sinks, qkv, qkv, qkv, qkv, qkv, qkv, o, o, do, do, lse, lse)


def swa_mixer_fwd(hn, w_in, w_out, sinks, positions, name):
    hq = sinks.shape[0]
    n_rot = (hq + hq // SWA_GROUP) * ATT_HEAD
    tabs = rope_tables(positions)
    proj = mm(hn, w_in, "nn", BF16, name + "_in")
    qkv = rope(proj, *tabs, n_rot, 1.0, name + "_rope")
    o, lse = swa_fwd(qkv, sinks, hq, name + "_att")
    y = mm(o, w_out, "nn", F32, name + "_out")
    return y, (qkv, o, lse)


def swa_mixer_bwd(dy, hn, w_in, w_out, sinks, positions, saved, name):
    qkv, o, lse = saved
    hq = sinks.shape[0]
    n_rot = (hq + hq // SWA_GROUP) * ATT_HEAD
    tabs = rope_tables(positions)
    dwo = wgrad(o, dy, name + "_dwo")
    do = mm(dy, w_out, "nt", BF16, name + "_do")
    dq, dk, dv, dsp = swa_bwd(qkv, o, lse, do, sinks, hq, name + "_attb")
    dproj = rope(jnp.concatenate([dq, dk, dv], axis=1), *tabs, n_rot, -1.0, name + "_ropeb")
    dwi = wgrad(hn, dproj, name + "_dwi")
    dhn = mm(dproj, w_in, "nt", BF16, name + "_dhn")
    dsinks = jnp.sum(dsp[:, :, :, 0], axis=1).reshape(hq)
    return dhn, dwi, dwo, dsinks


FOX_FPAD = 512


def _log_sigmoid(x):
    return jnp.minimum(x, 0.0) - jnp.log(1.0 + jnp.exp(-jnp.abs(x)))


def _tri(n, upper):
    r = lax.broadcasted_iota(jnp.int32, (n, n), 0)
    c = lax.broadcasted_iota(jnp.int32, (n, n), 1)
    return jnp.where((c >= r) if upper else (c <= r), 1.0, 0.0).astype(F32)


def _dot_exact(a, b):
    return jnp.dot(a, b, precision=lax.Precision.HIGHEST, preferred_element_type=F32)


def fox_cumsum(fl, b_pad, name):
    t = fl.shape[0]
    tr = _tile(t, (256, 128, 64, 32, 16, 8))

    def body(f_ref, b_ref, c_ref, carry_ref):
        @pl.when(pl.program_id(0) == 0)
        def _():
            carry_ref[...] = jnp.zeros_like(carry_ref)

        c = _dot_exact(_tri(tr, False), _log_sigmoid(f_ref[...] + b_ref[...])) + carry_ref[...]
        c_ref[...] = c
        carry_ref[...] = c[tr - 1:tr, :]

    return pl.pallas_call(
        body, name=name, grid=(t // tr,),
        in_specs=[_row_spec(tr, LANES), _full_spec((1, LANES))],
        out_specs=_row_spec(tr, LANES),
        out_shape=jax.ShapeDtypeStruct((t, LANES), F32),
        scratch_shapes=[pltpu.VMEM((1, LANES), F32)],
        compiler_params=_params((_ARB,)),
    )(fl, b_pad)


def fox_cumsum_bwd(dc, fl, b_pad, name):
    t = fl.shape[0]
    tr = _tile(t, (256, 128, 64, 32, 16, 8))
    nt = t // tr

    def body(d_ref, f_ref, b_ref, o_ref, db_ref, carry_ref):
        @pl.when(pl.program_id(0) == 0)
        def _():
            carry_ref[...] = jnp.zeros_like(carry_ref)
            db_ref[...] = jnp.zeros_like(db_ref)

        dlf = _dot_exact(_tri(tr, True), d_ref[...]) + carry_ref[...]
        carry_ref[...] = dlf[0:1, :]
        dfl = dlf * _sigmoid(-(f_ref[...] + b_ref[...]))
        o_ref[...] = dfl.astype(BF16)
        db_ref[...] += jnp.sum(dfl, axis=0, keepdims=True)

    rev = pl.BlockSpec((tr, LANES), lambda i: (nt - 1 - i, 0))
    return pl.pallas_call(
        body, name=name, grid=(nt,),
        in_specs=[rev, rev, _full_spec((1, LANES))],
        out_specs=[rev, _full_spec((1, LANES))],
        out_shape=[jax.ShapeDtypeStruct((t, LANES), BF16), jax.ShapeDtypeStruct((1, LANES), F32)],
        scratch_shapes=[pltpu.VMEM((1, LANES), F32)],
        compiler_params=_params((_ARB,)),
    )(dc, fl, b_pad)


AUG_C, AUG_ONE, AUG_LSE = ATT_HEAD, ATT_HEAD + 3, ATT_HEAD + 6


def _split3(x):
    hi = x.astype(BF16).astype(F32)
    mid = (x - hi).astype(BF16).astype(F32)
    return hi, mid, (x - hi - mid).astype(BF16).astype(F32)


def _aug(base, lane, entries):
    out = jnp.where(lane < ATT_HEAD, base, 0.0)
    for first, parts in entries:
        if parts is None:
            out = jnp.where((lane >= first) & (lane < first + 3), 1.0, out)
        else:
            for k, part in enumerate(parts):
                out = jnp.where(lane == first + k, part, out)
    return out.astype(BF16)


def _head_of_pair(x2, a):
    xf = x2.astype(F32)
    return xf if a == 0 else pltpu.roll(xf, ATT_HEAD, 1)


def fa_prep(proj, c, nh, name):
    t = proj.shape[0]
    npair = nh // 2
    tr = _tile(t, (256, 128))
    scale = ATT_HEAD ** -0.5

    def body(q_ref, k_ref, v_ref, c_ref, qa_ref, ka_ref, va_ref):
        lane = lax.broadcasted_iota(jnp.int32, (tr, LANES), 1)
        for p in range(npair):
            pc = slice(p * LANES, (p + 1) * LANES)
            for a in range(2):
                h = 2 * p + a
                hc = slice(h * LANES, (h + 1) * LANES)
                ch = c_ref[:, h:h + 1]
                qa_ref[:, hc] = _aug(_head_of_pair(q_ref[:, pc], a) * scale, lane,
                                     [(AUG_C, _split3(ch)), (AUG_ONE, None)])
                ka_ref[:, hc] = _aug(_head_of_pair(k_ref[:, pc], a), lane,
                                     [(AUG_C, None), (AUG_ONE, _split3(-ch)), (AUG_LSE, None)])
                va_ref[:, hc] = _aug(_head_of_pair(v_ref[:, pc], a), lane, [(AUG_C, None)])

    hd = nh * ATT_HEAD
    part = lambda k: pl.BlockSpec((tr, hd), lambda i: (i, k))
    out = pl.BlockSpec((tr, nh * LANES), lambda i: (i, 0))
    return pl.pallas_call(
        body, name=name, grid=(t // tr,),
        in_specs=[part(0), part(1), part(2), _row_spec(tr, LANES)],
        out_specs=[out, out, out],
        out_shape=[jax.ShapeDtypeStruct((t, nh * LANES), BF16)] * 3,
        compiler_params=_params((_PAR,)),
    )(proj, proj, proj, c)


def _fox_tiles(t):
    outer = _tile(t, (1024, 512, 256, 128))
    return outer, min(outer, 256)


def _diag_mask(outer, inner, d, transposed=False):
    r = lax.broadcasted_iota(jnp.int32, (outer, inner), 0)
    c = lax.broadcasted_iota(jnp.int32, (outer, inner), 1) + d * inner
    return (r <= c) if transposed else (c <= r)


def fa_fwd(qa, ka, va, proj, nh, name):
    t = qa.shape[0]
    hd = nh * ATT_HEAD
    npair = nh // 2
    tq, tk = _fox_tiles(t)
    nt, ratio = t // tq, tq // tk
    gcol = (3 * hd + FOX_FPAD) // LANES

    def body(q_ref, k_ref, v_ref, g_ref, o_ref, og_ref, lse_ref):
        i = pl.program_id(1)
        heads = [slice(a * LANES, (a + 1) * LANES) for a in range(2)]
        qs = [q_ref[:, cols] for cols in heads]

        def tile(j, carry, diag):
            rows = pl.ds(pl.multiple_of(j * tk, tk), tk)
            out = []
            for (mx, acc), q, cols in zip(carry, qs, heads):
                s = _dot_nt(q, k_ref[rows, cols])
                if diag is not None:
                    s = jnp.where(_diag_mask(tq, tk, diag), s, NEG)
                mnew = jnp.maximum(mx, jnp.max(s, axis=1, keepdims=True))
                p = jnp.exp(s - mnew).astype(BF16)
                out.append((mnew, jnp.exp(mx - mnew) * acc + _dot_nn(p, v_ref[rows, cols])))
            return tuple(out)

        carry = ((jnp.full((tq, 1), NEG, F32), jnp.zeros((tq, LANES), F32)),) * 2
        carry = lax.fori_loop(0, i * ratio, functools.partial(tile, diag=None), carry)
        for d in range(ratio):
            carry = tile(i * ratio + d, carry, d)
        outs = []
        for a, (mx, acc) in enumerate(carry):
            l = acc[:, AUG_C:AUG_C + 1]
            outs.append(acc / l)
            lse_ref[a] = jnp.broadcast_to(mx + jnp.log(l), (tq, LANES))
        o = jnp.where(_half((tq, LANES), 0), outs[0], pltpu.roll(outs[1], ATT_HEAD, 1))
        o_ref[...] = o.astype(BF16)
        og_ref[...] = (o * _sigmoid(g_ref[...].astype(F32))).astype(BF16)

    pair = pl.BlockSpec((tq, LANES), lambda p, i: (i, p))
    return pl.pallas_call(
        body, name=name, grid=(npair, nt),
        in_specs=[pl.BlockSpec((tq, 2 * LANES), lambda p, i: (i, p)),
                  pl.BlockSpec((t, 2 * LANES), lambda p, i: (0, p)),
                  pl.BlockSpec((t, 2 * LANES), lambda p, i: (0, p)),
                  pl.BlockSpec((tq, LANES), lambda p, i: (i, gcol + p))],
        out_specs=[pair, pair, pl.BlockSpec((2, tq, LANES), lambda p, i: (p, i, 0))],
        out_shape=[jax.ShapeDtypeStruct((t, hd), BF16), jax.ShapeDtypeStruct((t, hd), BF16),
                   jax.ShapeDtypeStruct((nh, t, LANES), F32)],
        compiler_params=_params((_PAR, _PAR)),
    )(qa, ka, va, proj)


def fa_prep_bwd(dog, o, proj, qa, lse, nh, name):
    t, hd = o.shape
    npair = nh // 2
    tr = _tile(t, (256, 128))
    gcol = (3 * hd + FOX_FPAD) // LANES

    def body(d_ref, o_ref, g_ref, q_ref, l_ref, dg_ref, qb_ref, da_ref):
        lane = lax.broadcasted_iota(jnp.int32, (tr, LANES), 1)
        dv, ov = d_ref[...].astype(F32), o_ref[...].astype(F32)
        sg = _sigmoid(g_ref[...].astype(F32))
        do = (dv * sg).astype(BF16).astype(F32)
        dg_ref[...] = (dv * ov * sg * (1.0 - sg)).astype(BF16)
        prod = do * ov
        for a in range(2):
            cols = slice(a * LANES, (a + 1) * LANES)
            delta = jnp.sum(jnp.where(_half(prod.shape, a), prod, 0.0), axis=1, keepdims=True)
            da_ref[:, cols] = _aug(_head_of_pair(do, a), lane, [(AUG_C, _split3(-delta))])
            nl = _split3(-l_ref[a][:, 0:1])
            qb = q_ref[:, cols].astype(F32)
            for k in range(3):
                qb = jnp.where(lane == AUG_LSE + k, nl[k], qb)
            qb_ref[:, cols] = qb.astype(BF16)

    pair = pl.BlockSpec((tr, LANES), lambda p, i: (i, p))
    wide = pl.BlockSpec((tr, 2 * LANES), lambda p, i: (i, p))
    return pl.pallas_call(
        body, name=name, grid=(npair, t // tr),
        in_specs=[pair, pair, pl.BlockSpec((tr, LANES), lambda p, i: (i, gcol + p)), wide,
                  pl.BlockSpec((2, tr, LANES), lambda p, i: (p, i, 0))],
        out_specs=[pair, wide, wide],
        out_shape=[jax.ShapeDtypeStruct((t, hd), BF16), jax.ShapeDtypeStruct((t, nh * LANES), BF16),
                   jax.ShapeDtypeStruct((t, nh * LANES), BF16)],
        compiler_params=_params((_PAR, _PAR)),
    )(dog, o, proj, qa, lse)


def fa_dq(qb, ka, va, da, nh, name):
    t = qb.shape[0]
    hd = nh * ATT_HEAD
    npair = nh // 2
    tq, tk = _fox_tiles(t)
    nt, ratio = t // tq, tq // tk
    scale = ATT_HEAD ** -0.5

    def body(q_ref, k_ref, v_ref, d_ref, dq_ref, rs_ref):
        i = pl.program_id(1)
        heads = [slice(a * LANES, (a + 1) * LANES) for a in range(2)]
        qs = [q_ref[:, cols] for cols in heads]
        ds = [d_ref[:, cols] for cols in heads]

        def tile(j, carry, diag):
            rows = pl.ds(pl.multiple_of(j * tk, tk), tk)
            out = []
            for acc, q, d, cols in zip(carry, qs, ds, heads):
                kj = k_ref[rows, cols]
                s = _dot_nt(q, kj)
                if diag is not None:
                    s = jnp.where(_diag_mask(tq, tk, diag), s, NEG)
                dsv = jnp.exp(s) * _dot_nt(d, v_ref[rows, cols])
                out.append(acc + _dot_nn(dsv.astype(BF16), kj))
            return tuple(out)

        accs = lax.fori_loop(0, i * ratio, functools.partial(tile, diag=None), (jnp.zeros((tq, LANES), F32),) * 2)
        for d in range(ratio):
            accs = tile(i * ratio + d, accs, d)
        dq_ref[...] = (jnp.where(_half((tq, LANES), 0), accs[0], pltpu.roll(accs[1], ATT_HEAD, 1)) * scale).astype(BF16)
        lane = lax.broadcasted_iota(jnp.int32, (tq, LANES), 1)
        rs_ref[...] = jnp.where(lane == 0, accs[0][:, AUG_C:AUG_C + 1],
                                jnp.where(lane == 1, accs[1][:, AUG_C:AUG_C + 1], 0.0))

    wide = pl.BlockSpec((tq, 2 * LANES), lambda p, i: (i, p))
    resident = pl.BlockSpec((t, 2 * LANES), lambda p, i: (0, p))
    pair = pl.BlockSpec((tq, LANES), lambda p, i: (i, p))
    return pl.pallas_call(
        body, name=name, grid=(npair, nt),
        in_specs=[wide, resident, resident, wide],
        out_specs=[pair, pair],
        out_shape=[jax.ShapeDtypeStruct((t, hd), BF16), jax.ShapeDtypeStruct((t, npair * LANES), F32)],
        compiler_params=_params((_PAR, _PAR)),
    )(qb, ka, va, da)


def fa_dkv(qb, ka, va, da, nh, name):
    t = qb.shape[0]
    hd = nh * ATT_HEAD
    npair = nh // 2
    tk, tq = _fox_tiles(t)
    nt, ratio = t // tk, tk // tq

    def body(q_ref, k_ref, v_ref, d_ref, dk_ref, dv_ref, cs_ref):
        j = pl.program_id(1)
        heads = [slice(a * LANES, (a + 1) * LANES) for a in range(2)]
        ks = [k_ref[:, cols] for cols in heads]
        vs = [v_ref[:, cols] for cols in heads]

        def tile(i, carry, diag):
            rows = pl.ds(pl.multiple_of(i * tq, tq), tq)
            out = []
            for (dk, dv), k, v, cols in zip(carry, ks, vs, heads):
                qi, di = q_ref[rows, cols], d_ref[rows, cols]
                st = _dot_nt(k, qi)
                if diag is not None:
                    st = jnp.where(_diag_mask(tk, tq, diag, True), st, NEG)
                pt = jnp.exp(st)
                dst = pt * _dot_nt(v, di)
                out.append((dk + _dot_nn(dst.astype(BF16), qi), dv + _dot_nn(pt.astype(BF16), di)))
            return tuple(out)

        zero = jnp.zeros((tk, LANES), F32)
        carry = ((zero, zero),) * 2
        for d in range(ratio):
            carry = tile(j * ratio + d, carry, d)
        carry = lax.fori_loop((j + 1) * ratio, t // tq, functools.partial(tile, diag=None), carry)
        dks, dvs = [c[0] for c in carry], [c[1] for c in carry]
        first = _half((tk, LANES), 0)
        dk_ref[...] = jnp.where(first, dks[0], pltpu.roll(dks[1], ATT_HEAD, 1)).astype(BF16)
        dv_ref[...] = jnp.where(first, dvs[0], pltpu.roll(dvs[1], ATT_HEAD, 1)).astype(BF16)
        lane = lax.broadcasted_iota(jnp.int32, (tk, LANES), 1)
        cs_ref[...] = jnp.where(lane == 0, dks[0][:, AUG_ONE:AUG_ONE + 1],
                                jnp.where(lane == 1, dks[1][:, AUG_ONE:AUG_ONE + 1], 0.0))

    wide = pl.BlockSpec((tk, 2 * LANES), lambda p, j: (j, p))
    resident = pl.BlockSpec((t, 2 * LANES), lambda p, j: (0, p))
    pair = pl.BlockSpec((tk, LANES), lambda p, j: (j, p))
    return pl.pallas_call(
        body, name=name, grid=(npair, nt),
        in_specs=[resident, wide, wide, resident],
        out_specs=[pair, pair, pair],
        out_shape=[jax.ShapeDtypeStruct((t, hd), BF16), jax.ShapeDtypeStruct((t, hd), BF16),
                   jax.ShapeDtypeStruct((t, npair * LANES), F32)],
        compiler_params=_params((_PAR, _PAR)),
    )(qb, ka, va, da)


def fox_pad_w_in(w_in, nh):
    hd = nh * ATT_HEAD
    pad = jnp.zeros((w_in.shape[0], FOX_FPAD - nh), w_in.dtype)
    return jnp.concatenate([w_in[:, :3 * hd + nh], pad, w_in[:, 3 * hd + nh:]], axis=1)


def fox_unpad_dw(dw, nh):
    hd = nh * ATT_HEAD
    return jnp.concatenate([dw[:, :3 * hd + nh], dw[:, 3 * hd + FOX_FPAD:]], axis=1)


def _pad_lanes(v):
    return jnp.pad(v.reshape(1, -1).astype(F32), ((0, 0), (0, LANES - v.size)))


def fox_mixer_fwd(hn, w_pad, w_out, b_f, name):
    nh = b_f.shape[0]
    hd = nh * ATT_HEAD
    proj = mm(hn, w_pad, "nn", BF16, name + "_in")
    fl = mm(hn, w_pad[:, 3 * hd:3 * hd + LANES], "nn", F32, name + "_fl")
    c = fox_cumsum(fl, _pad_lanes(b_f), name + "_cum")
    qa, ka, va = fa_prep(proj, c, nh, name + "_prep")
    o, og, lse = fa_fwd(qa, ka, va, proj, nh, name + "_att")
    y = mm(og, w_out, "nn", F32, name + "_out")
    return y, (proj, fl, qa, ka, va, o, og, lse)


def fox_mixer_bwd(dy, hn, w_pad, w_out, b_f, saved, name):
    proj, fl, qa, ka, va, o, og, lse = saved
    nh = b_f.shape[0]
    t = hn.shape[0]
    dwo = wgrad(og, dy, name + "_dwo")
    dog = mm(dy, w_out, "nt", BF16, name + "_dog")
    dg, qb, da = fa_prep_bwd(dog, o, proj, qa, lse, nh, name + "_prepb")
    dq, rsum = fa_dq(qb, ka, va, da, nh, name + "_dq")
    dk, dv, csum = fa_dkv(qb, ka, va, da, nh, name + "_dkv")
    dc = (rsum - csum).reshape(t, nh // 2, LANES)[:, :, :2].reshape(t, nh)
    dc = jnp.pad(dc, ((0, 0), (0, LANES - nh)))
    dfl, db = fox_cumsum_bwd(dc, fl, _pad_lanes(b_f), name + "_cumb")
    dfl = jnp.pad(dfl, ((0, 0), (0, FOX_FPAD - LANES)))
    dproj = jnp.concatenate([dq, dk, dv, dfl, dg], axis=1)
    dwi = wgrad(hn, dproj, name + "_dwi")
    dhn = mm(dproj, w_pad, "nt", BF16, name + "_dhn")
    return dhn, dwi, dwo, db[0, :nh]


HGRN_ROWS = 256
HGRN_TOGETHER = 8


def lb_table_row(lb_param, idx, name):
    nrow, w = lb_param.shape

    def body(p_ref, o_ref):
        rows = [p_ref[r:r + 1, :] for r in range(nrow)]
        mx = functools.reduce(jnp.maximum, rows)
        es = [jnp.exp(r - mx) for r in rows]
        o_ref[...] = sum(es[:idx + 1]) / sum(es)

    return pl.pallas_call(
        body, name=name, in_specs=[_full_spec((nrow, w))], out_specs=_full_spec((1, w)), grid=(1,),
        out_shape=jax.ShapeDtypeStruct((1, w), F32),
    )(lb_param)


def lb_table_row_bwd(lb_param, dlb, idx, name):
    nrow, w = lb_param.shape

    def body(p_ref, d_ref, o_ref):
        rows = [p_ref[r:r + 1, :] for r in range(nrow)]
        mx = functools.reduce(jnp.maximum, rows)
        es = [jnp.exp(r - mx) for r in rows]
        tot = sum(es)
        ps = [e / tot for e in es]
        dv = d_ref[...]
        inner = sum(ps[:idx + 1]) * dv
        for r in range(nrow):
            o_ref[r:r + 1, :] = ps[r] * ((dv if r <= idx else 0.0) - inner)

    return pl.pallas_call(
        body, name=name, in_specs=[_full_spec((nrow, w)), _full_spec((1, w))], out_specs=_full_spec((nrow, w)),
        grid=(1,), out_shape=jax.ShapeDtypeStruct((nrow, w), F32),
    )(lb_param, dlb)


def _hgrn_gates(qraw, fraw, lb):
    sq = _sigmoid(qraw)
    sf = _sigmoid(fraw)
    f = lb + (1.0 - lb) * sf
    return qraw * sq, sq, sf, f, 1.0 - f


def _hgrn_chunk(q, k, f):
    c = HGRN_CHUNK
    b = _dot_exact(_tri(c, False), jnp.log(f))
    bl = b[c - 1:c, :]
    bm = b[c // 2 - 1:c // 2, :]
    eq, ek = jnp.exp(b - bm), jnp.exp(bm - b)
    eb, el = jnp.exp(b), jnp.exp(bl - b)
    qt, kt = (q * eq).astype(BF16), (k * ek).astype(BF16)
    causal = _tri(c, False) > 0.5
    amat = jnp.where(causal, _dot_nt(qt, kt), 0.0).astype(BF16)
    return amat, qt, kt, (q * eb).astype(BF16), (k * el).astype(BF16), eq, ek, eb, el, jnp.exp(bl), causal


def hgrn_fwd(proj, lb, norm_g, nh, name):
    t = proj.shape[0]
    w = nh * HGRN_HEAD
    c = HGRN_CHUNK
    rows = _tile(t, (HGRN_ROWS, 128, 64, 32))
    nr, nc = t // rows, rows // c
    hp = _tile(nh, (HGRN_TOGETHER, 2, 1))
    wide = hp * HGRN_HEAD

    def body(q_ref, f_ref, i_ref, g_ref, lb_ref, ng_ref, og_ref, o_ref, st_ref, state):
        @pl.when(pl.program_id(1) == 0)
        def _():
            state[...] = jnp.zeros_like(state)

        def step(cc, carry):
            sl = pl.ds(pl.multiple_of(cc * c, c), c)
            for a in range(hp):
                hc = slice(a * HGRN_HEAD, (a + 1) * HGRN_HEAD)
                q, _, _, f, k = _hgrn_gates(q_ref[sl, hc].astype(F32), f_ref[sl, hc].astype(F32), lb_ref[:, hc])
                v = i_ref[sl, hc]
                amat, _, _, qd, kd, _, _, _, _, ebl, _ = _hgrn_chunk(q, k, f)
                st = state[a]
                st_ref[a, cc] = st.astype(BF16)
                o = _dot_nt(qd, st.astype(BF16)) + _dot_nn(amat, v)
                state[a] = st * ebl + _dot_tn(v, kd)
                o_ref[sl, hc] = o
                graw = g_ref[sl, hc].astype(F32)
                og_ref[sl, hc] = (o * _rstd(o) * ng_ref[...] * (graw * _sigmoid(graw))).astype(BF16)
            return carry

        lax.fori_loop(0, nc, step, 0)

    ng = nh // hp
    col = lambda off: pl.BlockSpec((rows, wide), lambda h, r: (r, off + h))
    return pl.pallas_call(
        body, name=name, grid=(ng, nr),
        in_specs=[col(0), col(ng), col(2 * ng), col(3 * ng),
                  pl.BlockSpec((1, wide), lambda h, r: (0, h)), _full_spec((1, HGRN_HEAD))],
        out_specs=[col(0), col(0), pl.BlockSpec((hp, nc, HGRN_HEAD, HGRN_HEAD), lambda h, r: (h, r, 0, 0))],
        out_shape=[jax.ShapeDtypeStruct((t, w), BF16), jax.ShapeDtypeStruct((t, w), F32),
                   jax.ShapeDtypeStruct((nh, t // c, HGRN_HEAD, HGRN_HEAD), BF16)],
        scratch_shapes=[pltpu.VMEM((hp, HGRN_HEAD, HGRN_HEAD), F32)],
        compiler_params=_params((_PAR, _ARB)),
    )(proj, proj, proj, proj, lb, norm_g)


def hgrn_bwd(proj, lb, norm_g, o, states, dog, nh, name):
    t = proj.shape[0]
    w = nh * HGRN_HEAD
    c = HGRN_CHUNK
    rows = _tile(t, (HGRN_ROWS, 128, 64, 32))
    nr, nc = t // rows, rows // c
    hp = _tile(nh, (HGRN_TOGETHER, 2, 1))
    wide = hp * HGRN_HEAD

    def body(q_ref, f_ref, i_ref, g_ref, lb_ref, ng_ref, o_ref, st_ref, dog_ref,
             dq_ref, df_ref, di_ref, dg_ref, dlb_ref, dng_ref, dstate):
        @pl.when(pl.program_id(1) == 0)
        def _():
            dstate[...] = jnp.zeros_like(dstate)
            dlb_ref[...] = jnp.zeros_like(dlb_ref)
            dng_ref[...] = jnp.zeros_like(dng_ref)

        ng = ng_ref[...]

        def step(idx, carry):
            cc = nc - 1 - idx
            sl = pl.ds(pl.multiple_of(cc * c, c), c)
            for a in range(hp):
                hc = slice(a * HGRN_HEAD, (a + 1) * HGRN_HEAD)
                lb = lb_ref[:, hc]
                qraw, fraw = q_ref[sl, hc].astype(F32), f_ref[sl, hc].astype(F32)
                q, sq, sf, f, k = _hgrn_gates(qraw, fraw, lb)
                v = i_ref[sl, hc]
                amat, qt, kt, qd, kd, eq, ek, eb, el, ebl, causal = _hgrn_chunk(q, k, f)
                ov = o_ref[sl, hc]
                graw = g_ref[sl, hc].astype(F32)
                dogv = dog_ref[sl, hc].astype(F32)
                sil, dsil = _silu_and_grad(graw)
                r = _rstd(ov)
                oh = ov * r
                don = dogv * sil
                dg_ref[sl, hc] = (dogv * oh * ng * dsil).astype(BF16)
                dng_ref[a] += jnp.sum(don * oh, axis=0, keepdims=True)
                doh = don * ng
                do = (r * (doh - oh * jnp.mean(doh * oh, axis=1, keepdims=True))).astype(BF16)
                dst = dstate[a]
                dstb = dst.astype(BF16)
                da = jnp.where(causal, _dot_nt(do, v), 0.0).astype(BF16)
                dv = _dot_tn(amat, do) + _dot_nt(kd, dstb)
                st0 = st_ref[a, cc]
                dq = _dot_nn(da, kt) * eq + _dot_nn(do, st0) * eb
                dk_inter = _dot_nn(v, dstb) * el
                dk = _dot_tn(da, qt) * ek + dk_inter
                dstate[a] = dst * ebl + _dot_tn(do, qd)
                through = jnp.sum(dst * st0.astype(F32), axis=0, keepdims=True) * ebl
                later = jnp.sum(k * dk_inter, axis=0, keepdims=True) + through
                dlf = _dot_exact(_tri(c, True), q * dq - k * dk) + later
                df = dlf / f - dk
                dq_ref[sl, hc] = (dq * (sq * (1.0 + qraw * (1.0 - sq)))).astype(BF16)
                df_ref[sl, hc] = (df * (1.0 - lb) * sf * (1.0 - sf)).astype(BF16)
                di_ref[sl, hc] = dv.astype(BF16)
                dlb_ref[:, hc] += jnp.sum(df * (1.0 - sf), axis=0, keepdims=True)
            return carry

        lax.fori_loop(0, nc, step, 0)

    ngr = nh // hp
    col = lambda off: pl.BlockSpec((rows, wide), lambda h, r: (nr - 1 - r, off + h))
    out = col(0)
    return pl.pallas_call(
        body, name=name, grid=(ngr, nr),
        in_specs=[col(0), col(ngr), col(2 * ngr), col(3 * ngr),
                  pl.BlockSpec((1, wide), lambda h, r: (0, h)), _full_spec((1, HGRN_HEAD)),
                  out, pl.BlockSpec((hp, nc, HGRN_HEAD, HGRN_HEAD), lambda h, r: (h, nr - 1 - r, 0, 0)), out],
        out_specs=[out, out, out, out, pl.BlockSpec((1, wide), lambda h, r: (0, h)),
                   pl.BlockSpec((hp, 1, HGRN_HEAD), lambda h, r: (h, 0, 0))],
        out_shape=[jax.ShapeDtypeStruct((t, w), BF16)] * 4 + [jax.ShapeDtypeStruct((1, w), F32),
                                                             jax.ShapeDtypeStruct((nh, 1, HGRN_HEAD), F32)],
        scratch_shapes=[pltpu.VMEM((hp, HGRN_HEAD, HGRN_HEAD), F32)],
        compiler_params=_params((_PAR, _ARB)),
    )(proj, proj, proj, proj, lb, norm_g, o, states, dog)


def hgrn_mixer_fwd(hn, w_in, w_out, norm_g, lb_param, idx, name):
    nh = w_out.shape[0] // HGRN_HEAD
    lb = lb_table_row(lb_param, idx, name + "_lb")
    proj = mm(hn, w_in, "nn", BF16, name + "_in")
    og, o, states = hgrn_fwd(proj, lb, norm_g, nh, name + "_rec")
    y = mm(og, w_out, "nn", F32, name + "_out")
    return y, (proj, lb, og, o, states)


def hgrn_mixer_bwd(dy, hn, w_in, w_out, norm_g, lb_param, idx, saved, name):
    proj, lb, og, o, states = saved
    nh = w_out.shape[0] // HGRN_HEAD
    dwo = wgrad(og, dy, name + "_dwo")
    dog = mm(dy, w_out, "nt", BF16, name + "_dog")
    dq, df, di, dg, dlb, dng = hgrn_bwd(proj, lb, norm_g, o, states, dog, nh, name + "_recb")
    dproj = jnp.concatenate([dq, df, di, dg], axis=1)
    dwi = wgrad(hn, dproj, name + "_dwi")
    dhn = mm(dproj, w_in, "nt", BF16, name + "_dhn")
    dlbp = lb_table_row_bwd(lb_param, dlb, idx, name + "_lbb")
    return dhn, dwi, dwo, jnp.sum(dng, axis=0), dlbp


_ANY = pl.BlockSpec(memory_space=pl.ANY)
_MESH = pl.DeviceIdType.MESH


def _place():
    x, y, c = lax.axis_index("x"), lax.axis_index("y"), lax.axis_index("c")
    chips = [(1 - x, y), (x, 1 - y), (1 - x, 1 - y)]
    return x, y, c, N_CHIPS // 2 * x + y, chips


def _chip_index(chip):
    return N_CHIPS // 2 * chip[0] + chip[1]


def _window(ref, axis, start, size):
    idx = [slice(None)] * len(ref.shape)
    idx[axis] = pl.ds(start, size)
    return ref.at[tuple(idx)]


_HBM = pl.BlockSpec(memory_space=pltpu.HBM)
_SEMS = pl.BlockSpec(memory_space=pltpu.SEMAPHORE)
_DATAFLOW = pltpu.SideEffectType.DATAFLOW_SIDE_EFFECTING


def _exchange_copy(src, land, axis, sems, k, j, chip, c, sender_side):
    x, y, _, me, _ = _place()
    peer = _chip_index(chip)
    if axis is None:
        src_part = src.at[peer]
        land_part = land.at[me if sender_side else peer]
    else:
        size = src.shape[axis]
        src_part = src
        land_part = _window(land, axis, (me if sender_side else peer) * size, size)
    which = k * (N_CHIPS - 1) + j
    return pltpu.make_async_remote_copy(src_ref=src_part, dst_ref=land_part, send_sem=sems[0].at[which],
                                        recv_sem=sems[1].at[which], device_id=(chip[0], chip[1], c),
                                        device_id_type=_MESH)


def place_own(shard, axis):
    _, _, _, me, _ = _place()
    shape = list(shard.shape)
    shape[axis] *= N_CHIPS
    return lax.dynamic_update_slice_in_dim(lax.empty(tuple(shape), shard.dtype), shard, me * shard.shape[axis], axis)


def cast_and_place(shards, layer, axis, name):
    _, rows, cols = shards.shape
    tr = _row_tile(rows, cols, 1 << 19)
    nrb = rows // tr
    _, _, _, me, _ = _place()
    full = (rows * N_CHIPS, cols) if axis == 0 else (rows, cols * N_CHIPS)

    def body(me_ref, s_ref, b_ref, land_ref):
        val = s_ref[...].astype(BF16)
        b_ref[...] = val
        land_ref[...] = val

    window = (lambda i, m: (m[0] * nrb + i, 0)) if axis == 0 else (lambda i, m: (i, m[0]))
    grid_spec = pltpu.PrefetchScalarGridSpec(
        num_scalar_prefetch=1, grid=(nrb,),
        in_specs=[pl.BlockSpec((None, tr, cols), lambda i, m: (layer, i, 0))],
        out_specs=[pl.BlockSpec((tr, cols), lambda i, m: (i, 0)), pl.BlockSpec((tr, cols), window)])
    return pl.pallas_call(
        body, name=name, grid_spec=grid_spec,
        out_shape=[jax.ShapeDtypeStruct((rows, cols), BF16), jax.ShapeDtypeStruct(full, BF16)],
        compiler_params=_params((_PAR,)),
    )(jnp.reshape(me, (1,)).astype(jnp.int32), shards)


def place_own_slot(p):
    _, _, _, me, _ = _place()
    mine = lax.dynamic_index_in_dim(p, me, 0, keepdims=True)
    return lax.dynamic_update_slice_in_dim(lax.empty(p.shape, p.dtype), mine, me, 0)


def exchange_start(srcs, lands, axes, name):
    n = len(srcs)

    def body(*refs):
        ins, lnd, sems = refs[:n], refs[n:2 * n], refs[2 * n:2 * n + 2]
        _, _, c, _, chips = _place()
        for k in range(n):
            for j, chip in enumerate(chips):
                _exchange_copy(ins[k], lnd[k], axes[k], sems, k, j, chip, c, True).start()
        refs[-1][...] = jnp.zeros_like(refs[-1])

    sem = pltpu.SemaphoreType.DMA((n * (N_CHIPS - 1),))
    arrays = list(srcs) + list(lands)
    out = pl.pallas_call(
        body, name=name,
        in_specs=[_HBM] * (2 * n),
        out_specs=(_SEMS, _SEMS) + (_HBM,) * (2 * n) + (pl.BlockSpec(memory_space=pltpu.VMEM),),
        out_shape=(sem, sem) + tuple(pltpu.HBM(a.shape, a.dtype) for a in arrays)
        + (jax.ShapeDtypeStruct((8, LANES), F32),),
        input_output_aliases={i: 2 + i for i in range(2 * n)},
        compiler_params=pltpu.CompilerParams(has_side_effects=_DATAFLOW),
    )(*[pltpu.with_memory_space_constraint(a, pltpu.HBM) for a in arrays])
    return (out[0], out[1], list(out[2:2 + n]), list(out[2 + n:2 + 2 * n]), list(axes)), out[-1]


def exchange_wait(started, ks, after, name):
    send, recv, srcs, lands, axes = started
    m = len(ks)

    def body(*refs):
        ins, lnd, sems = refs[:m], refs[m:2 * m], refs[2 * m:2 * m + 2]
        _, _, c, _, chips = _place()
        for q, k in enumerate(ks):
            for j, chip in enumerate(chips):
                cp = _exchange_copy(ins[q], lnd[q], axes[k], sems, k, j, chip, c, False)
                cp.wait_send()
                cp.wait_recv()

    arrays = [srcs[k] for k in ks] + [lands[k] for k in ks]
    out = pl.pallas_call(
        body, name=name,
        in_specs=[_HBM] * (2 * m) + [_SEMS, _SEMS, _ANY],
        out_specs=(_HBM,) * (2 * m),
        out_shape=tuple(pltpu.HBM(a.shape, a.dtype) for a in arrays),
        input_output_aliases={i: i for i in range(2 * m)},
        compiler_params=pltpu.CompilerParams(has_side_effects=_DATAFLOW),
    )(*arrays, send, recv, after)
    return list(out[m:])


def allreduce_small(buf, name):
    rows = buf.shape[0]
    n_dev = 2 * N_CHIPS

    def body(in_ref, out_ref, slots, send, recv):
        x, y, c, me, chips = _place()
        my_id = 2 * me + c
        slots[my_id] = in_ref[...]
        for j in range(1, n_dev):
            fx, fy, fc = (j >> 2) & 1, (j >> 1) & 1, j & 1
            peer = ((1 - x) if fx else x, (1 - y) if fy else y, (1 - c) if fc else c)
            pltpu.make_async_remote_copy(
                src_ref=in_ref, dst_ref=slots.at[my_id], send_sem=send.at[j], recv_sem=recv.at[j],
                device_id=peer, device_id_type=_MESH).start()
        for j in range(1, n_dev):
            fx, fy, fc = (j >> 2) & 1, (j >> 1) & 1, j & 1
            peer = ((1 - x) if fx else x, (1 - y) if fy else y, (1 - c) if fc else c)
            peer_id = 2 * _chip_index(peer) + peer[2]
            landed = pltpu.make_async_remote_copy(
                src_ref=in_ref, dst_ref=slots.at[peer_id], send_sem=send.at[j], recv_sem=recv.at[j],
                device_id=peer, device_id_type=_MESH)
            landed.wait_recv()
            landed.wait_send()
        tot = slots[0]
        for d in range(1, n_dev):
            tot = tot + slots[d]
        out_ref[...] = tot

    return pl.pallas_call(
        body, name=name,
        in_specs=[pl.BlockSpec(memory_space=pltpu.VMEM)], out_specs=pl.BlockSpec(memory_space=pltpu.VMEM),
        out_shape=jax.ShapeDtypeStruct(buf.shape, F32),
        scratch_shapes=[pltpu.VMEM((n_dev, rows, LANES), F32), pltpu.SemaphoreType.DMA((n_dev,)),
                        pltpu.SemaphoreType.DMA((n_dev,))],
        compiler_params=pltpu.CompilerParams(vmem_limit_bytes=VMEM_LIMIT),
    )(buf)


def _sibling_step(src_ref, slots, send, recv, credit, step, n_steps):
    x, y, c = lax.axis_index("x"), lax.axis_index("y"), lax.axis_index("c")
    slot = step % 2

    @pl.when(step >= 2)
    def _():
        pl.semaphore_wait(credit, 1)

    cp = pltpu.make_async_remote_copy(src_ref=src_ref, dst_ref=slots.at[slot], send_sem=send.at[slot],
                                      recv_sem=recv.at[slot], device_id=(x, y, 1 - c), device_id_type=_MESH)
    cp.start()
    cp.wait_recv()
    return cp, slot


def _sibling_done(cp, credit, step, n_steps):
    x, y, c = lax.axis_index("x"), lax.axis_index("y"), lax.axis_index("c")
    cp.wait_send()

    @pl.when(step < n_steps - 2)
    def _():
        pl.semaphore_signal(credit, 1, device_id=(x, y, 1 - c), device_id_type=_MESH)


def pair_sum(g, name):
    by_cols = g.ndim == 2
    s = N_CHIPS if by_cols else g.shape[0]
    r = g.shape[-2]
    cols = g.shape[-1] // s if by_cols else g.shape[-1]
    half = r // 2
    tr = _row_tile(half, cols, 1 << 19)
    nt = half // tr
    n_steps = s * nt

    def body(g_ref, o_ref, slots, send, recv, credit):
        c = lax.axis_index("c")
        step = pl.program_id(0) * nt + pl.program_id(1)
        cp, slot = _sibling_step(g_ref.at[0, 1 - c], slots, send, recv, credit, step, n_steps)
        o_ref[0] = (g_ref[0, c].astype(F32) + slots[slot].astype(F32)).astype(BF16)
        _sibling_done(cp, credit, step, n_steps)

    if by_cols:
        in_spec = pl.BlockSpec((1, 2, tr, cols), lambda k, i: (0, 0, i, k))
        g4 = g.reshape(1, 2, half, s * cols)
    else:
        in_spec = pl.BlockSpec((1, 2, tr, cols), lambda k, i: (k, 0, i, 0))
        g4 = g.reshape(s, 2, half, cols)
    return pl.pallas_call(
        body, name=name, grid=(s, nt),
        in_specs=[in_spec],
        out_specs=pl.BlockSpec((1, tr, cols), lambda k, i: (k, i, 0)),
        out_shape=jax.ShapeDtypeStruct((s, half, cols), BF16),
        scratch_shapes=[pltpu.VMEM((2, tr, cols), BF16), pltpu.SemaphoreType.DMA((2,)), pltpu.SemaphoreType.DMA((2,)),
                        pltpu.SemaphoreType.REGULAR],
        compiler_params=_params((_ARB, _ARB)),
    )(g4)


def chip_sum_share(q, acc, layer, name):
    s, r2, cols = q.shape
    tr = _row_tile(r2, cols, 1 << 18)
    nt = r2 // tr

    def body(q_ref, acc_ref, o_ref, slots, send, recv, credit):
        c = lax.axis_index("c")
        step = pl.program_id(0)
        tot = q_ref[0].astype(F32)
        for k in range(1, s):
            tot = tot + q_ref[k].astype(F32)
        o_ref[0, c] = tot
        cp, slot = _sibling_step(o_ref.at[0, c], slots, send, recv, credit, step, nt)
        o_ref[0, 1 - c] = slots[slot]
        _sibling_done(cp, credit, step, nt)

    return pl.pallas_call(
        body, name=name, grid=(nt,),
        in_specs=[pl.BlockSpec((s, tr, cols), lambda i: (0, i, 0)), _ANY],
        out_specs=pl.BlockSpec((1, 2, tr, cols), lambda i: (layer, 0, i, 0)),
        out_shape=jax.ShapeDtypeStruct(acc.shape, F32),
        input_output_aliases={1: 0},
        scratch_shapes=[pltpu.VMEM((2, tr, cols), F32), pltpu.SemaphoreType.DMA((2,)), pltpu.SemaphoreType.DMA((2,)),
                        pltpu.SemaphoreType.REGULAR],
        compiler_params=_params((_ARB,)),
    )(q, acc)


def _row_tile(rows, cols, budget):
    for tr in (1024, 512, 256, 128, 64, 32, 16, 8):
        if rows % tr == 0 and tr * cols <= budget:
            return tr
    return rows


def adamw(w, g, m, v, name):
    rows, cols = w.shape
    tr = _row_tile(rows, cols, 1 << 18)
    c1 = 1.0 - ADAM_B1 ** ADAM_STEP
    c2 = 1.0 - ADAM_B2 ** ADAM_STEP

    def body(w_ref, g_ref, m_ref, v_ref, d_ref, nm_ref, nv_ref):
        gv = g_ref[...]
        nm = ADAM_B1 * m_ref[...] + (1.0 - ADAM_B1) * gv
        nv = ADAM_B2 * v_ref[...] + (1.0 - ADAM_B2) * (gv * gv)
        nm_ref[...] = nm
        nv_ref[...] = nv
        d_ref[...] = -ADAM_LR * ((nm / c1) / (jnp.sqrt(nv / c2) + ADAM_EPS) + ADAM_WD * w_ref[...])

    spec = pl.BlockSpec((tr, cols), lambda i: (i, 0))
    return pl.pallas_call(
        body, name=name, grid=(rows // tr,),
        in_specs=[spec] * 4, out_specs=[spec] * 3,
        out_shape=[jax.ShapeDtypeStruct((rows, cols), F32)] * 3,
        compiler_params=_params((_PAR,)),
    )(w, g, m, v)


WEIGHTS = ("mix_pre_g", "mix_post_g", "ffn_pre_g", "ffn_post_g", "hgrn_w_in", "hgrn_w_out", "hgrn_norm_g",
           "hgrn_lb_param", "swa_w_in", "swa_w_out", "swa_sinks", "sc_w_in", "sc_conv_w", "sc_w_out", "fox_w_in",
           "fox_b_f", "fox_w_out", "ffn_w_up", "ffn_conv_w", "ffn_conv_b", "ffn_w_down")
N_MIXERS = 4


def _pack_small(parts):
    flat = jnp.concatenate([p.reshape(-1).astype(F32) for p in parts])
    rows = -(-flat.shape[0] // (8 * LANES)) * 8
    return jnp.pad(flat, (0, rows * LANES - flat.shape[0])).reshape(rows, LANES)


def _unpack_small(buf, shapes):
    flat, out, off = buf.reshape(-1), [], 0
    for s in shapes:
        n = math.prod(s)
        out.append(flat[off:off + n].reshape(s))
        off += n
    return out


def _stack_rows(dw):
    return dw.reshape(N_CHIPS, dw.shape[0] // N_CHIPS, dw.shape[1])


def kernel(x, positions, mix_pre_g, mix_post_g, ffn_pre_g, ffn_post_g, hgrn_w_in, hgrn_w_out, hgrn_norm_g, hgrn_lb_param, swa_w_in, swa_w_out, swa_sinks, sc_w_in, sc_conv_w, sc_w_out, fox_w_in, fox_b_f, fox_w_out, ffn_w_up, ffn_conv_w, ffn_conv_b, ffn_w_down, loss_target, m_mix_pre_g, m_mix_post_g, m_ffn_pre_g, m_ffn_post_g, m_hgrn_w_in, m_hgrn_w_out, m_hgrn_norm_g, m_hgrn_lb_param, m_swa_w_in, m_swa_w_out, m_swa_sinks, m_sc_w_in, m_sc_conv_w, m_sc_w_out, m_fox_w_in, m_fox_b_f, m_fox_w_out, m_ffn_w_up, m_ffn_conv_w, m_ffn_conv_b, m_ffn_w_down, v_mix_pre_g, v_mix_post_g, v_ffn_pre_g, v_ffn_post_g, v_hgrn_w_in, v_hgrn_w_out, v_hgrn_norm_g, v_hgrn_lb_param, v_swa_w_in, v_swa_w_out, v_swa_sinks, v_sc_w_in, v_sc_conv_w, v_sc_w_out, v_fox_w_in, v_fox_b_f, v_fox_w_out, v_ffn_w_up, v_ffn_conv_w, v_ffn_conv_b, v_ffn_w_down):
    given = dict(locals())
    depth = mix_pre_g.shape[0]
    assert depth == N_MIXERS and x.shape[0] == 1, "one batch element per device, one layer of each mixer"
    xi, target = x[0], loss_target[0]
    chip = N_CHIPS // 2 * lax.axis_index("x") + lax.axis_index("y")
    nh_fox = fox_b_f.shape[1]
    row = lambda a, i: a[i:i + 1]

    units = {"hg_in": (hgrn_w_in, 0, 1), "hg_out": (hgrn_w_out, 0, 0), "sw_in": (swa_w_in, 0, 1),
             "sw_out": (swa_w_out, 0, 0), "sc_in": (sc_w_in, 0, 1), "sc_out": (sc_w_out, 0, 0),
             "fx_in": (fox_w_in, 0, 0), "fx_out": (fox_w_out, 0, 0)}
    mix_units = (("hg_in", "hg_out"), ("sw_in", "sw_out"), ("sc_in", "sc_out", "sc_cw"), ("fx_in", "fx_out"))
    ffn_units = []
    for i in range(depth):
        units[f"up{i}"], units[f"down{i}"] = (ffn_w_up, i, 1), (ffn_w_down, i, 0)
        ffn_units.append((f"up{i}", f"down{i}") + (("f_cw",) if i == 0 else ()))
    order = [n for i in range(depth) for n in mix_units[i] + ffn_units[i]]
    placed = {n: cast_and_place(*units[n], "place_" + n) + (units[n][2],) for n in units}
    placed["f_cw"] = (ffn_conv_w, place_own(ffn_conv_w, 2), 2)
    placed["sc_cw"] = (sc_conv_w[0], place_own(sc_conv_w[0], 1), 1)
    gather, _ = exchange_start([placed[n][0] for n in order], [placed[n][1] for n in order],
                               [placed[n][2] for n in order], "gather_start")
    wt = {}

    def arrive(names, after, name):
        wt.update(zip(names, exchange_wait(gather, [order.index(n) for n in names], after, name)))

    saved = []
    xs = xi
    hn = rms_fwd(xs, row(mix_pre_g, 0), "pre_norm0")
    dx = loss = None
    for i in range(depth):
        nm = f"l{i}"
        arrive(mix_units[i], hn, nm + "_w_mix")
        if i == 0:
            y, sv = hgrn_mixer_fwd(hn, wt["hg_in"], wt["hg_out"], hgrn_norm_g, hgrn_lb_param, i, nm + "_hgrn")
        elif i == 1:
            y, sv = swa_mixer_fwd(hn, wt["sw_in"], wt["sw_out"], swa_sinks[0], positions, nm + "_swa")
        elif i == 2:
            proj = mm(hn, wt["sc_in"], "nn", BF16, nm + "_sc_in")
            yb = sconv_fwd(proj, wt["sc_cw"], nm + "_sc_conv")
            y, sv = mm(yb, wt["sc_out"], "nn", F32, nm + "_sc_out"), (proj, yb)
        else:
            fx4 = wt["fx_in"].reshape(N_CHIPS, -1, wt["fx_in"].shape[1])
            wt["fx_pad"] = fox_pad_w_in(jnp.concatenate([fx4[s] for s in range(N_CHIPS)], axis=1), nh_fox)
            y, sv = fox_mixer_fwd(hn, wt["fx_pad"], wt["fx_out"], fox_b_f[0], nm + "_fox")
        x1, hn2 = resid_norm(xs, y, row(mix_post_g, i), row(ffn_pre_g, i), nm + "_mix_resid")
        arrive(ffn_units[i], hn2, nm + "_w_ffn")
        z = mm(hn2, wt[f"up{i}"], "nn", BF16, nm + "_ffn_up")
        a = ffn_act(z, wt["f_cw"][i], row(ffn_conv_b, i), nm + "_ffn_act")
        y2 = mm(a, wt[f"down{i}"], "nn", F32, nm + "_ffn_down")
        saved.append((xs, hn, y, sv, x1, hn2, z, a, y2))
        if i < depth - 1:
            xs, hn = resid_norm(x1, y2, row(ffn_post_g, i), row(mix_pre_g, i + 1), nm + "_ffn_resid")
        else:
            dx, loss = resid_loss(x1, y2, row(ffn_post_g, i), target, nm + "_loss")

    grads = {}

    def start_reduce(tag, named):
        ps = [pair_sum(g, f"{tag}_pair_{n}") for n, _, g in named]
        started, token = exchange_start(ps, [place_own_slot(p) for p in ps], [None] * len(ps), tag + "_chips_start")
        return (named, started), token

    def finish_reduce(tag, pending, after):
        named, started = pending
        qs = exchange_wait(started, list(range(len(named))), after, tag + "_chips_wait")
        for (n, l, _), q in zip(named, qs):
            if n not in grads:
                grads[n] = lax.empty((given[n].shape[0], 2) + q.shape[1:], F32)
            grads[n] = chip_sum_share(q, grads[n], l, f"{tag}_share_{n}")

    d_pre, d_post, d_fpre, d_fpost = [None] * depth, [None] * depth, [None] * depth, [None] * depth
    d_fcw, d_fcb = [None] * depth, [None] * depth
    small = {}
    pending = token = None
    for i in reversed(range(depth)):
        nm = f"l{i}b"
        xs, hn, y, sv, x1, hn2, z, a, y2 = saved[i]
        f_cw = wt["f_cw"][i]
        dy2, d_fpost[i] = norm_bwd(y2, row(ffn_post_g, i), dx, None, BF16, nm + "_ffn_post", after=token)
        d_down = _stack_rows(wgrad(a, dy2, nm + "_dw_down"))
        da = mm(dy2, wt[f"down{i}"], "nt", BF16, nm + "_da")
        du, acc = ffn_act_bwd(z, da, f_cw, row(ffn_conv_b, i), nm + "_ffn_actb")
        d_fcw[i], d_fcb[i] = acc[0:CONV_WIDTH], acc[CONV_WIDTH]
        dz = conv_transpose(du, f_cw, nm + "_ffn_convT")
        d_up = wgrad(hn2, dz, nm + "_dw_up")
        dhn2 = mm(dz, wt[f"up{i}"], "nt", BF16, nm + "_dhn2")
        dx1, d_fpre[i] = norm_bwd(x1, row(ffn_pre_g, i), dhn2, dx, F32, nm + "_ffn_pre")
        dy, d_post[i] = norm_bwd(y, row(mix_post_g, i), dx1, None, BF16, nm + "_mix_post")
        if i == 0:
            dhn, dwi, dwo, small["hgrn_norm_g"], small["hgrn_lb_param"] = hgrn_mixer_bwd(
                dy, hn, wt["hg_in"], wt["hg_out"], hgrn_norm_g, hgrn_lb_param, i, sv, nm + "_hgrn")
            w_in, w_out = "hgrn_w_in", "hgrn_w_out"
        elif i == 1:
            dhn, dwi, dwo, small["swa_sinks"] = swa_mixer_bwd(dy, hn, wt["sw_in"], wt["sw_out"], swa_sinks[0],
                                                             positions, sv, nm + "_swa")
            w_in, w_out = "swa_w_in", "swa_w_out"
        elif i == 2:
            proj, yb = sv
            dwo = wgrad(yb, dy, nm + "_sc_dwo")
            dyb = mm(dy, wt["sc_out"], "nt", BF16, nm + "_sc_dyb")
            dproj, acc = sconv_bwd(proj, dyb, wt["sc_cw"], nm + "_sc_convb")
            dwi = wgrad(hn, dproj, nm + "_sc_dwi")
            dhn = mm(dproj, wt["sc_in"], "nt", BF16, nm + "_sc_dhn")
            small["sc_conv_w"] = acc[0:CONV_WIDTH]
            w_in, w_out = "sc_w_in", "sc_w_out"
        else:
            dhn, dwi, dwo, small["fox_b_f"] = fox_mixer_bwd(dy, hn, wt["fx_pad"], wt["fx_out"], fox_b_f[0], sv,
                                                            nm + "_fox")
            dwi = fox_unpad_dw(dwi, nh_fox)
            cols = dwi.shape[1] // N_CHIPS
            dwi = jnp.stack([dwi[:, s * cols:(s + 1) * cols] for s in range(N_CHIPS)])
            w_in, w_out = "fox_w_in", "fox_w_out"
        dx, d_pre[i] = norm_bwd(xs, row(mix_pre_g, i), dhn, dx1, F32, nm + "_mix_pre")
        if pending is not None:
            finish_reduce(f"l{i + 1}b", pending, dx)
        pending, token = start_reduce(nm, [(w_in, 0, dwi), (w_out, 0, _stack_rows(dwo)), ("ffn_w_up", i, d_up),
                                           ("ffn_w_down", i, d_down)])
    small.update(mix_pre_g=jnp.concatenate(d_pre), mix_post_g=jnp.concatenate(d_post),
                 ffn_pre_g=jnp.concatenate(d_fpre), ffn_post_g=jnp.concatenate(d_fpost),
                 ffn_conv_w=jnp.stack(d_fcw), ffn_conv_b=jnp.stack(d_fcb))

    small_names = [n for n in WEIGHTS if n in small]
    full_shape = {n: tuple(given[n].shape) for n in small_names}
    full_shape["sc_conv_w"] = (1, CONV_WIDTH, wt["sc_cw"].shape[1])
    full_shape["ffn_conv_w"] = tuple(wt["f_cw"].shape)
    small_sum = allreduce_small(_pack_small([small[n] for n in small_names] + [loss]), "small_sum")
    finish_reduce("l0b", pending, small_sum)
    summed = _unpack_small(small_sum, [full_shape[n] for n in small_names] + [()])
    loss = summed[-1]
    for n, g in zip(small_names, summed):
        if g.shape != given[n].shape:
            width = given[n].shape[-1]
            g = lax.dynamic_slice_in_dim(g, chip * width, width, axis=g.ndim - 1)
        grads[n] = g

    deltas, new_m, new_v = {}, {}, {}
    for n in WEIGHTS:
        w = given[n]
        flat = lambda a: a.reshape(-1, w.shape[-1])
        dl, nm_, nv_ = adamw(flat(w), flat(grads[n]), flat(given["m_" + n]), flat(given["v_" + n]), "adamw_" + n)
        deltas[n], new_m[n], new_v[n] = dl.reshape(w.shape), nm_.reshape(w.shape), nv_.reshape(w.shape)
    return (loss, dx[None], *[grads[n].reshape(given[n].shape) for n in WEIGHTS], *[deltas[n] for n in WEIGHTS],
            *[new_m[n] for n in WEIGHTS], *[new_v[n] for n in WEIGHTS])
```

```python
import functools
import math

import numpy as np
import jax
import jax.numpy as jnp
from jax import lax
from jax.experimental import pallas as pl
from jax.experimental.pallas import tpu as pltpu

F32 = jnp.float32
BF16 = jnp.bfloat16

RMS_EPS = 1e-6
HGRN_HEAD = 128
HGRN_CHUNK = 32
ATT_HEAD = 64
SWA_WINDOW = 128
SWA_GROUP = 8
ROT_DIM = 16
ROPE_THETA = 500000.0
CONV_WIDTH = 3
ADAM_LR = 0.001
ADAM_B1 = 0.9
ADAM_B2 = 0.999
ADAM_EPS = 1e-08
ADAM_WD = 0.01
ADAM_STEP = 10
N_CHIPS = 4
LANES = 128
BF16_ROWS = 16
VMEM_LIMIT = 48 * 1024 * 1024

_ARB = "arbitrary"
_PAR = "parallel"


def _params(sem, **kw):
    return pltpu.CompilerParams(dimension_semantics=sem, vmem_limit_bytes=VMEM_LIMIT, **kw)


def _tile(n, prefs):
    for p in prefs:
        if n % p == 0:
            return p
    return n


def _sigmoid(x):
    return 1.0 / (1.0 + jnp.exp(-x))


def _dot(a, b, dims):
    return lax.dot_general(a, b, (dims, ((), ())), preferred_element_type=F32)


def _dot_nn(a, b):
    return _dot(a, b, ((1,), (0,)))


def _dot_nt(a, b):
    return _dot(a, b, ((1,), (1,)))


def _dot_tn(a, b):
    return _dot(a, b, ((0,), (0,)))


MM_VMEM_BUDGET = 36 * 1024 * 1024
MM_HBM_RATE = 3.0e12
MM_MXU_RATE = 6.5e14
MM_STEP_S = 0.35e-6
MM_ACC_RATE = 3.0e12


def _mm_tiles(m, n, k, out_bytes):
    best = None
    for tm in (2048, 1024, 512, 256, 128):
        for tn in (2048, 1024, 512, 256, 128):
            for tk in sorted({k, 4096, 2816, 2048, 1408, 1024, 512, 256, 128}, reverse=True):
                if m % tm or n % tn or tk > k or k % tk:
                    continue
                nk = k // tk
                vmem = 4 * (tm * tk + tk * tn) + (4 * tm * tn if nk > 1 else 0) + 2 * tm * tn * out_bytes
                if vmem > MM_VMEM_BUDGET:
                    continue
                steps = (m // tm) * (n // tn) * nk
                traffic = 2 * m * k * (1 if nk == 1 else n // tn) + 2 * k * n * (m // tm) + m * n * out_bytes
                cost = max(traffic / MM_HBM_RATE, 2 * m * n * k / MM_MXU_RATE) + steps * MM_STEP_S
                if nk > 1:
                    cost += steps * 8 * tm * tn / MM_ACC_RATE
                if best is None or cost < best[0]:
                    best = (cost, tm, tn, tk)
    assert best is not None, (m, n, k)
    return best[1:]


def mm(a, b, mode, out_dtype, name="mm"):
    if mode == "nn":
        (m, k), (k2, n) = a.shape, b.shape
    elif mode == "nt":
        (m, k), (n, k2) = a.shape, b.shape
    else:
        (k, m), (k2, n) = a.shape, b.shape
    assert k == k2, (a.shape, b.shape, mode)
    tm, tn, tk = _mm_tiles(m, n, k, jnp.dtype(out_dtype).itemsize)
    nk = k // tk

    def product(a_ref, b_ref):
        av = a_ref[...].astype(BF16)
        bv = b_ref[...].astype(BF16)
        return {"nn": _dot_nn, "nt": _dot_nt, "tn": _dot_tn}[mode](av, bv)

    def body_one(a_ref, b_ref, o_ref):
        o_ref[...] = product(a_ref, b_ref).astype(out_dtype)

    def body_acc(a_ref, b_ref, o_ref, acc_ref):
        kk = pl.program_id(2)

        @pl.when(kk == 0)
        def _():
            acc_ref[...] = jnp.zeros_like(acc_ref)

        acc_ref[...] += product(a_ref, b_ref)

        @pl.when(kk == nk - 1)
        def _():
            o_ref[...] = acc_ref[...].astype(out_dtype)

    if mode == "nn":
        a_spec = pl.BlockSpec((tm, tk), lambda i, j, kk: (i, kk))
        b_spec = pl.BlockSpec((tk, tn), lambda i, j, kk: (kk, j))
    elif mode == "nt":
        a_spec = pl.BlockSpec((tm, tk), lambda i, j, kk: (i, kk))
        b_spec = pl.BlockSpec((tn, tk), lambda i, j, kk: (j, kk))
    else:
        a_spec = pl.BlockSpec((tk, tm), lambda i, j, kk: (kk, i))
        b_spec = pl.BlockSpec((tk, tn), lambda i, j, kk: (kk, j))
    return pl.pallas_call(
        body_one if nk == 1 else body_acc,
        name=name,
        grid=(m // tm, n // tn, nk),
        in_specs=[a_spec, b_spec],
        out_specs=pl.BlockSpec((tm, tn), lambda i, j, kk: (i, j)),
        out_shape=jax.ShapeDtypeStruct((m, n), out_dtype),
        scratch_shapes=[] if nk == 1 else [pltpu.VMEM((tm, tn), F32)],
        compiler_params=_params((_PAR, _PAR, _ARB)),
    )(a, b)


def wgrad(a, b, name):
    return mm(a, b, "tn", BF16, name)


def _rstd(xv):
    return lax.rsqrt(jnp.mean(xv * xv, axis=1, keepdims=True) + RMS_EPS)


def _row_spec(tr, w):
    return pl.BlockSpec((tr, w), lambda i: (i, 0))


def _full_spec(shape):
    nd = len(shape)
    return pl.BlockSpec(shape, lambda *_: (0,) * nd)


def rms_fwd(x, g, name):
    t, d = x.shape
    tr = _tile(t, (256, 128, 64, 32, 16))

    def body(x_ref, g_ref, o_ref):
        xv = x_ref[...]
        o_ref[...] = (xv * _rstd(xv) * g_ref[...]).astype(BF16)

    return pl.pallas_call(
        body, name=name, grid=(t // tr,),
        in_specs=[_row_spec(tr, d), _full_spec((1, d))],
        out_specs=_row_spec(tr, d),
        out_shape=jax.ShapeDtypeStruct((t, d), BF16),
        compiler_params=_params((_PAR,)),
    )(x, g)


def resid_norm(x, y, g_post, g_next, name):
    t, d = x.shape
    tr = _tile(t, (256, 128, 64, 32, 16))

    def body(x_ref, y_ref, gp_ref, gn_ref, x1_ref, hn_ref):
        yv = y_ref[...]
        x1 = x_ref[...] + yv * _rstd(yv) * gp_ref[...]
        x1_ref[...] = x1
        hn_ref[...] = (x1 * _rstd(x1) * gn_ref[...]).astype(BF16)

    return pl.pallas_call(
        body, name=name, grid=(t // tr,),
        in_specs=[_row_spec(tr, d), _row_spec(tr, d), _full_spec((1, d)), _full_spec((1, d))],
        out_specs=[_row_spec(tr, d), _row_spec(tr, d)],
        out_shape=[jax.ShapeDtypeStruct((t, d), F32), jax.ShapeDtypeStruct((t, d), BF16)],
        compiler_params=_params((_PAR,)),
    )(x, y, g_post, g_next)


def resid_loss(x, y, g_post, target, name):
    t, d = x.shape
    tr = _tile(t, (256, 128, 64, 32, 16))

    def body(x_ref, y_ref, gp_ref, t_ref, dx_ref, loss_ref):
        @pl.when(pl.program_id(0) == 0)
        def _():
            loss_ref[...] = jnp.zeros_like(loss_ref)

        yv = y_ref[...]
        err = x_ref[...] + yv * _rstd(yv) * gp_ref[...] - t_ref[...]
        dx_ref[...] = err * (1.0 / d)
        loss_ref[...] += 0.5 * jnp.sum(jnp.mean(err * err, axis=1, keepdims=True), axis=0, keepdims=True)

    dx, loss = pl.pallas_call(
        body, name=name, grid=(t // tr,),
        in_specs=[_row_spec(tr, d), _row_spec(tr, d), _full_spec((1, d)), _row_spec(tr, d)],
        out_specs=[_row_spec(tr, d), _full_spec((8, LANES))],
        out_shape=[jax.ShapeDtypeStruct((t, d), F32), jax.ShapeDtypeStruct((8, LANES), F32)],
        compiler_params=_params((_ARB,)),
    )(x, y, g_post, target)
    return dx, loss[0:1, 0:1]


def norm_bwd(yin, g, dout, res, out_dtype, name, after=None):
    t, d = yin.shape
    tr = _tile(t, (256, 128, 64, 32, 16))
    has_res = res is not None

    def body(*refs):
        refs = refs[:3 + has_res] + refs[-2:]
        if has_res:
            y_ref, g_ref, d_ref, r_ref, o_ref, dg_ref = refs
        else:
            y_ref, g_ref, d_ref, o_ref, dg_ref = refs

        @pl.when(pl.program_id(0) == 0)
        def _():
            dg_ref[...] = jnp.zeros_like(dg_ref)

        yv = y_ref[...]
        dv = d_ref[...].astype(F32)
        r = _rstd(yv)
        yh = yv * r
        dyh = dv * g_ref[...]
        dy = r * (dyh - yh * jnp.mean(dyh * yh, axis=1, keepdims=True))
        if has_res:
            dy = dy + r_ref[...]
        o_ref[...] = dy.astype(out_dtype)
        dg_ref[...] += jnp.sum(dv * yh, axis=0, keepdims=True)

    ins = [yin, g, dout] + ([res] if has_res else []) + ([] if after is None else [after])
    in_specs = ([_row_spec(tr, d), _full_spec((1, d)), _row_spec(tr, d)] + ([_row_spec(tr, d)] if has_res else [])
                + ([] if after is None else [pl.BlockSpec(memory_space=pl.ANY)]))
    return pl.pallas_call(
        body, name=name, grid=(t // tr,),
        in_specs=in_specs,
        out_specs=[_row_spec(tr, d), _full_spec((1, d))],
        out_shape=[jax.ShapeDtypeStruct((t, d), out_dtype), jax.ShapeDtypeStruct((1, d), F32)],
        compiler_params=_params((_ARB,)),
    )(*ins)


def _shift_down(x, halo):
    tr = x.shape[0]
    row = lax.broadcasted_iota(jnp.int32, x.shape, 0)
    h1 = halo[BF16_ROWS - 1:BF16_ROWS, :]
    h2 = halo[BF16_ROWS - 2:BF16_ROWS - 1, :]
    x1 = jnp.where(row == 0, h1, pltpu.roll(x, 1, 0))
    x2 = jnp.where(row == 0, h2, jnp.where(row == 1, h1, pltpu.roll(x, 2, 0)))
    return x1, x2


def _shift_up(x, halo):
    tr = x.shape[0]
    row = lax.broadcasted_iota(jnp.int32, x.shape, 0)
    h0 = halo[0:1, :]
    h1 = halo[1:2, :]
    x1 = jnp.where(row == tr - 1, h0, pltpu.roll(x, tr - 1, 0))
    x2 = jnp.where(row == tr - 1, h1, jnp.where(row == tr - 2, h0, pltpu.roll(x, tr - 2, 0)))
    return x1, x2


def _prev_halo_spec(tr, w, nt):
    return pl.BlockSpec((BF16_ROWS, w), lambda i: (jnp.maximum(i * (tr // BF16_ROWS) - 1, 0), 0))


def _next_halo_spec(tr, w, nt):
    last = nt * (tr // BF16_ROWS) - 1
    return pl.BlockSpec((BF16_ROWS, w), lambda i: (jnp.minimum((i + 1) * (tr // BF16_ROWS), last), 0))


def _silu_and_grad(u):
    s = _sigmoid(u)
    return u * s, s * (1.0 + u * (1.0 - s))


def ffn_act(z, conv_w, conv_b, name):
    t, f2 = z.shape
    f = f2 // 2
    tr = _tile(t, (128, 64, 32, 16))
    nt = t // tr
    cw = _tile(f, (512, 256, 128))

    def body(z_ref, zp_ref, w_ref, b_ref, a_ref):
        first = pl.program_id(0) == 0
        for j in range(f // cw):
            us = []
            for off in (j * cw, f + j * cw):
                cols = slice(off, off + cw)
                zc = z_ref[:, cols].astype(F32)
                hp = jnp.where(first, 0.0, zp_ref[:, cols].astype(F32))
                z1, z2 = _shift_down(zc, hp)
                us.append(w_ref[2:3, cols] * zc + w_ref[1:2, cols] * z1 + w_ref[0:1, cols] * z2 + b_ref[:, cols])
            sil, _ = _silu_and_grad(us[0])
            a_ref[:, j * cw:(j + 1) * cw] = (sil * us[1]).astype(BF16)

    return pl.pallas_call(
        body, name=name, grid=(nt,),
        in_specs=[_row_spec(tr, f2), _prev_halo_spec(tr, f2, nt), _full_spec((CONV_WIDTH, f2)), _full_spec((1, f2))],
        out_specs=_row_spec(tr, f),
        out_shape=jax.ShapeDtypeStruct((t, f), BF16),
        compiler_params=_params((_PAR,)),
    )(z, z, conv_w, conv_b)


def ffn_act_bwd(z, da, conv_w, conv_b, name):
    t, f2 = z.shape
    f = f2 // 2
    tr = _tile(t, (128, 64, 32, 16))
    nt = t // tr
    cw = _tile(f, (512, 256, 128))

    def body(z_ref, zp_ref, da_ref, w_ref, b_ref, du_ref, acc_ref):
        first = pl.program_id(0) == 0

        @pl.when(first)
        def _():
            acc_ref[...] = jnp.zeros_like(acc_ref)

        for j in range(f // cw):
            us, zs = [], []
            for off in (j * cw, f + j * cw):
                cols = slice(off, off + cw)
                zc = z_ref[:, cols].astype(F32)
                hp = jnp.where(first, 0.0, zp_ref[:, cols].astype(F32))
                z1, z2 = _shift_down(zc, hp)
                zs.append((z2, z1, zc))
                us.append(w_ref[2:3, cols] * zc + w_ref[1:2, cols] * z1 + w_ref[0:1, cols] * z2 + b_ref[:, cols])
            dav = da_ref[:, j * cw:(j + 1) * cw].astype(F32)
            sil, dsil = _silu_and_grad(us[0])
            dus = (dav * us[1] * dsil, dav * sil)
            for off, du, zsh in zip((j * cw, f + j * cw), dus, zs):
                cols = slice(off, off + cw)
                du_ref[:, cols] = du.astype(BF16)
                for k in range(CONV_WIDTH):
                    acc_ref[k:k + 1, cols] += jnp.sum(du * zsh[k], axis=0, keepdims=True)
                acc_ref[3:4, cols] += jnp.sum(du, axis=0, keepdims=True)

    return pl.pallas_call(
        body, name=name, grid=(nt,),
        in_specs=[_row_spec(tr, f2), _prev_halo_spec(tr, f2, nt), _row_spec(tr, f),
                  _full_spec((CONV_WIDTH, f2)), _full_spec((1, f2))],
        out_specs=[_row_spec(tr, f2), _full_spec((8, f2))],
        out_shape=[jax.ShapeDtypeStruct((t, f2), BF16), jax.ShapeDtypeStruct((8, f2), F32)],
        compiler_params=_params((_ARB,)),
    )(z, z, da, conv_w, conv_b)


def conv_transpose(du, conv_w, name):
    t, w = du.shape
    tr = _tile(t, (128, 64, 32, 16))
    nt = t // tr
    cw = _tile(w, (512, 256, 128))

    def body(d_ref, dn_ref, w_ref, o_ref):
        last = pl.program_id(0) == nt - 1
        for j in range(w // cw):
            cols = slice(j * cw, (j + 1) * cw)
            dc = d_ref[:, cols].astype(F32)
            hn = jnp.where(last, 0.0, dn_ref[:, cols].astype(F32))
            d1, d2 = _shift_up(dc, hn)
            o_ref[:, cols] = (w_ref[2:3, cols] * dc + w_ref[1:2, cols] * d1 + w_ref[0:1, cols] * d2).astype(BF16)

    return pl.pallas_call(
        body, name=name, grid=(nt,),
        in_specs=[_row_spec(tr, w), _next_halo_spec(tr, w, nt), _full_spec((CONV_WIDTH, w))],
        out_specs=_row_spec(tr, w),
        out_shape=jax.ShapeDtypeStruct((t, w), BF16),
        compiler_params=_params((_PAR,)),
    )(du, du, conv_w)


def sconv_fwd(proj, conv_w, name):
    t, w3 = proj.shape
    d = w3 // 3
    tr = _tile(t, (128, 64, 32, 16))
    nt = t // tr
    cw = _tile(d, (512, 256, 128))

    def body(p_ref, pp_ref, w_ref, o_ref):
        first = pl.program_id(0) == 0
        for j in range(d // cw):
            cb, cc, cx = (slice(k * d + j * cw, k * d + (j + 1) * cw) for k in range(3))
            zc = p_ref[:, cc].astype(F32) * p_ref[:, cx].astype(F32)
            hp = jnp.where(first, 0.0, pp_ref[:, cc].astype(F32) * pp_ref[:, cx].astype(F32))
            z1, z2 = _shift_down(zc, hp)
            wc = slice(j * cw, (j + 1) * cw)
            cz = w_ref[2:3, wc] * zc + w_ref[1:2, wc] * z1 + w_ref[0:1, wc] * z2
            o_ref[:, wc] = (p_ref[:, cb].astype(F32) * cz).astype(BF16)

    return pl.pallas_call(
        body, name=name, grid=(nt,),
        in_specs=[_row_spec(tr, w3), _prev_halo_spec(tr, w3, nt), _full_spec((CONV_WIDTH, d))],
        out_specs=_row_spec(tr, d),
        out_shape=jax.ShapeDtypeStruct((t, d), BF16),
        compiler_params=_params((_PAR,)),
    )(proj, proj, conv_w)


def sconv_bwd(proj, dyb, conv_w, name):
    t, w3 = proj.shape
    d = w3 // 3
    tr = _tile(t, (128, 64, 32, 16))
    nt = t // tr
    cw = _tile(d, (512, 256, 128))

    def body(p_ref, pp_ref, pn_ref, dy_ref, dyn_ref, w_ref, o_ref, acc_ref):
        first = pl.program_id(0) == 0
        last = pl.program_id(0) == nt - 1

        @pl.when(first)
        def _():
            acc_ref[...] = jnp.zeros_like(acc_ref)

        for j in range(d // cw):
            cb, cc, cx = (slice(k * d + j * cw, k * d + (j + 1) * cw) for k in range(3))
            wc = slice(j * cw, (j + 1) * cw)
            bv, cv, xv = p_ref[:, cb].astype(F32), p_ref[:, cc].astype(F32), p_ref[:, cx].astype(F32)
            zc = cv * xv
            hp = jnp.where(first, 0.0, pp_ref[:, cc].astype(F32) * pp_ref[:, cx].astype(F32))
            z1, z2 = _shift_down(zc, hp)
            w0, w1, w2 = w_ref[0:1, wc], w_ref[1:2, wc], w_ref[2:3, wc]
            cz = w2 * zc + w1 * z1 + w0 * z2
            dyv = dy_ref[:, wc].astype(F32)
            dcz = dyv * bv
            hn = jnp.where(last, 0.0, dyn_ref[:, wc].astype(F32) * pn_ref[:, cb].astype(F32))
            n1, n2 = _shift_up(dcz, hn)
            dz = w2 * dcz + w1 * n1 + w0 * n2
            o_ref[:, cb] = (dyv * cz).astype(BF16)
            o_ref[:, cc] = (dz * xv).astype(BF16)
            o_ref[:, cx] = (dz * cv).astype(BF16)
            for k, zsh in enumerate((z2, z1, zc)):
                acc_ref[k:k + 1, wc] += jnp.sum(dcz * zsh, axis=0, keepdims=True)

    return pl.pallas_call(
        body, name=name, grid=(nt,),
        in_specs=[_row_spec(tr, w3), _prev_halo_spec(tr, w3, nt), _next_halo_spec(tr, w3, nt),
                  _row_spec(tr, d), _next_halo_spec(tr, d, nt), _full_spec((CONV_WIDTH, d))],
        out_specs=[_row_spec(tr, w3), _full_spec((8, d))],
        out_shape=[jax.ShapeDtypeStruct((t, w3), BF16), jax.ShapeDtypeStruct((8, d), F32)],
        compiler_params=_params((_ARB,)),
    )(proj, proj, proj, dyb, dyb, conv_w)


def rope_tables(positions):
    half = ROT_DIM // 2
    inv_freq = ROPE_THETA ** (-jnp.arange(half, dtype=F32) / half)
    ang = positions.astype(F32)[:, None] * inv_freq[None, :]
    cos, sin = jnp.cos(ang), jnp.sin(ang)
    ones = jnp.ones((positions.shape[0], ATT_HEAD - ROT_DIM), F32)
    c64 = jnp.concatenate([cos, cos, ones], axis=1)
    s64 = jnp.concatenate([-sin, sin, 0.0 * ones], axis=1)
    perm = np.zeros((LANES, LANES), np.float32)
    for lane in range(LANES):
        dim = lane % ATT_HEAD
        if dim < half:
            perm[lane + half, lane] = 1.0
        elif dim < ROT_DIM:
            perm[lane - half, lane] = 1.0
    return jnp.tile(c64, (1, 2)), jnp.tile(s64, (1, 2)), jnp.asarray(perm, BF16)


def rope(xin, ctab, stab, perm, n_rot, sign, name):
    t, w = xin.shape
    tr = _tile(t, (256, 128, 64, 32, 16))

    def body(x_ref, c_ref, s_ref, p_ref, o_ref):
        cv, sv = c_ref[...], s_ref[...] * sign
        for j in range(n_rot // LANES):
            cols = slice(j * LANES, (j + 1) * LANES)
            xb = x_ref[:, cols]
            o_ref[:, cols] = (xb.astype(F32) * cv + _dot_nn(xb, p_ref[...]) * sv).astype(BF16)
        if n_rot < w:
            o_ref[:, n_rot:] = x_ref[:, n_rot:]

    return pl.pallas_call(
        body, name=name, grid=(t // tr,),
        in_specs=[_row_spec(tr, w), _row_spec(tr, LANES), _row_spec(tr, LANES), _full_spec((LANES, LANES))],
        out_specs=_row_spec(tr, w),
        out_shape=jax.ShapeDtypeStruct((t, w), BF16),
        compiler_params=_params((_PAR,)),
    )(xin, ctab, stab, perm)


NEG = -1e30


def _half(shape, h):
    return (lax.broadcasted_iota(jnp.int32, shape, 1) // ATT_HEAD) == h


def _dup_head(xb, kvh):
    xf = jnp.where(_half(xb.shape, kvh), xb.astype(F32), 0.0)
    return (xf + pltpu.roll(xf, ATT_HEAD, 1)).astype(BF16)


def _swa_mask(n, rows, cur_only):
    w = SWA_WINDOW
    shape = (w, w) if cur_only else (w, 2 * w)
    qi = lax.broadcasted_iota(jnp.int32, shape, 0)
    kj = lax.broadcasted_iota(jnp.int32, shape, 1) + (w if cur_only else 0)
    diff = qi + w - kj
    ok = (diff >= 0) & (diff < w)
    return ok & ((kj >= w) | (n > 0))


def swa_fwd(qkv, sinks, hq, name):
    t = qkv.shape[0]
    w = SWA_WINDOW
    nb = t // w
    hkv = hq // SWA_GROUP
    npair = hkv // 2
    qw = 2 * SWA_GROUP * ATT_HEAD
    kcol = hq * ATT_HEAD // LANES
    vcol = kcol + npair
    scale = ATT_HEAD ** -0.5

    def body(sink_ref, q_ref, kp_ref, kc_ref, vp_ref, vc_ref, o_ref, lse_ref):
        m, n = pl.program_id(0), pl.program_id(1)
        kb = jnp.concatenate([kp_ref[...], kc_ref[...]], axis=0)
        vb = jnp.concatenate([vp_ref[...], vc_ref[...]], axis=0)
        ok = _swa_mask(n, w, False)
        for kvh in range(2):
            kd, vd = _dup_head(kb, kvh), _dup_head(vb, kvh)
            for jj in range(SWA_GROUP // 2):
                jp = kvh * (SWA_GROUP // 2) + jj
                q2 = q_ref[:, jp * LANES:(jp + 1) * LANES]
                outs = []
                for a in range(2):
                    qa = jnp.where(_half(q2.shape, a), q2, jnp.zeros_like(q2))
                    s = jnp.where(ok, _dot_nt(qa, kd) * scale, NEG)
                    sink = sink_ref[m * 2 * SWA_GROUP + jp * 2 + a]
                    mx = jnp.maximum(jnp.max(s, axis=1, keepdims=True), sink)
                    e = jnp.exp(s - mx)
                    den = jnp.sum(e, axis=1, keepdims=True) + jnp.exp(sink - mx)
                    p = (e / den).astype(BF16)
                    outs.append(_dot_nn(p, vd))
                    lse_ref[jp * 2 + a] = jnp.broadcast_to(mx + jnp.log(den), (w, LANES))
                o_ref[:, jp * LANES:(jp + 1) * LANES] = jnp.where(_half(outs[0].shape, 0), outs[0], outs[1]).astype(BF16)

    prev = lambda m, n: jnp.maximum(n - 1, 0)
    grid_spec = pltpu.PrefetchScalarGridSpec(
        num_scalar_prefetch=1, grid=(npair, nb),
        in_specs=[
            pl.BlockSpec((w, qw), lambda m, n, s: (n, m)),
            pl.BlockSpec((w, LANES), lambda m, n, s: (prev(m, n), kcol + m)),
            pl.BlockSpec((w, LANES), lambda m, n, s: (n, kcol + m)),
            pl.BlockSpec((w, LANES), lambda m, n, s: (prev(m, n), vcol + m)),
            pl.BlockSpec((w, LANES), lambda m, n, s: (n, vcol + m)),
        ],
        out_specs=[
            pl.BlockSpec((w, qw), lambda m, n, s: (n, m)),
            pl.BlockSpec((2 * SWA_GROUP, w, LANES), lambda m, n, s: (m, n, 0)),
        ],
    )
    return pl.pallas_call(
        body, name=name, grid_spec=grid_spec,
        out_shape=[jax.ShapeDtypeStruct((t, hq * ATT_HEAD), BF16), jax.ShapeDtypeStruct((hq, t, LANES), F32)],
        compiler_params=_params((_PAR, _PAR)),
    )(sinks, qkv, qkv, qkv, qkv, qkv)


def swa_bwd(qkv, o, lse, do, sinks, hq, name):
    t = qkv.shape[0]
    w = SWA_WINDOW
    nb = t // w
    hkv = hq // SWA_GROUP
    npair = hkv // 2
    qw = 2 * SWA_GROUP * ATT_HEAD
    kcol = hq * ATT_HEAD // LANES
    vcol = kcol + npair
    scale = ATT_HEAD ** -0.5
    gh = 2 * SWA_GROUP

    def body(sink_ref, qc_ref, qn_ref, kp_ref, kc_ref, vp_ref, vc_ref, oc_ref, on_ref, dc_ref, dn_ref,
             lc_ref, ln_ref, dq_ref, dk_ref, dv_ref, ds_ref):
        m, n = pl.program_id(0), pl.program_id(1)
        kb = jnp.concatenate([kp_ref[...], kc_ref[...]], axis=0)
        vb = jnp.concatenate([vp_ref[...], vc_ref[...]], axis=0)
        ok_band = _swa_mask(n, w, False)
        ok_cur = _swa_mask(n, w, True)
        qi = lax.broadcasted_iota(jnp.int32, (w, w), 0)
        kj = lax.broadcasted_iota(jnp.int32, (w, w), 1)
        ok_next = (kj > qi) & (n < nb - 1)
        row16 = lax.broadcasted_iota(jnp.int32, (gh, LANES), 0)
        dsink = jnp.zeros((gh, LANES), F32)
        dk_tot = jnp.zeros((w, LANES), F32)
        dv_tot = jnp.zeros((w, LANES), F32)
        for kvh in range(2):
            kd, vd = _dup_head(kb, kvh), _dup_head(vb, kvh)
            kdc, vdc = kd[w:, :], vd[w:, :]
            acc_k = [jnp.zeros((w, LANES), F32), jnp.zeros((w, LANES), F32)]
            acc_v = [jnp.zeros((w, LANES), F32), jnp.zeros((w, LANES), F32)]
            for jj in range(SWA_GROUP // 2):
                jp = kvh * (SWA_GROUP // 2) + jj
                cols = slice(jp * LANES, (jp + 1) * LANES)
                dqs = []
                for a in range(2):
                    hd = jp * 2 + a
                    sink = sink_ref[m * gh + hd]
                    half = _half((w, LANES), a)
                    q2 = jnp.where(half, qc_ref[:, cols], jnp.zeros((w, LANES), BF16))
                    d2 = jnp.where(half, dc_ref[:, cols], jnp.zeros((w, LANES), BF16))
                    delta = jnp.sum(d2.astype(F32) * oc_ref[:, cols].astype(F32), axis=1, keepdims=True)
                    lse_c = lc_ref[hd][:, 0:1]
                    p = jnp.exp(jnp.where(ok_band, _dot_nt(q2, kd) * scale, NEG) - lse_c)
                    dsv = p * (_dot_nt(d2, vd) - delta)
                    dqs.append(_dot_nn(dsv.astype(BF16), kd) * scale)
                    psink = jnp.exp(sink - lse_c)
                    dsink = jnp.where(row16 == hd, dsink - jnp.sum(psink * delta, axis=0, keepdims=True), dsink)
                    for q_ref, d_ref, o_ref, l_ref, okm in ((qc_ref, dc_ref, oc_ref, lc_ref, ok_cur),
                                                           (qn_ref, dn_ref, on_ref, ln_ref, ok_next)):
                        q2 = jnp.where(half, q_ref[:, cols], jnp.zeros((w, LANES), BF16))
                        d2 = jnp.where(half, d_ref[:, cols], jnp.zeros((w, LANES), BF16))
                        delta = jnp.sum(d2.astype(F32) * o_ref[:, cols].astype(F32), axis=1, keepdims=True)
                        p = jnp.exp(jnp.where(okm, _dot_nt(q2, kdc) * scale, NEG) - l_ref[hd][:, 0:1])
                        dsv = p * (_dot_nt(d2, vdc) - delta)
                        acc_v[a] = acc_v[a] + _dot_tn(p.astype(BF16), d2)
                        acc_k[a] = acc_k[a] + _dot_tn(dsv.astype(BF16), q2) * scale
                dq_ref[:, cols] = jnp.where(_half((w, LANES), 0), dqs[0], dqs[1]).astype(BF16)
            dk_tot = dk_tot + acc_k[kvh] + pltpu.roll(acc_k[1 - kvh], ATT_HEAD, 1)
            dv_tot = dv_tot + acc_v[kvh] + pltpu.roll(acc_v[1 - kvh], ATT_HEAD, 1)
        dk_ref[...] = dk_tot.astype(BF16)
        dv_ref[...] = dv_tot.astype(BF16)
        ds_ref[0, 0] = dsink

    prev = lambda n: jnp.maximum(n - 1, 0)
    nxt = lambda n: jnp.minimum(n + 1, nb - 1)
    qspec = lambda f: pl.BlockSpec((w, qw), lambda m, n, s: (f(n), m))
    lspec = lambda f: pl.BlockSpec((gh, w, LANES), lambda m, n, s: (m, f(n), 0))
    same = lambda n: n
    grid_spec = pltpu.PrefetchScalarGridSpec(
        num_scalar_prefetch=1, grid=(npair, nb),
        in_specs=[
            qspec(same), qspec(nxt),
            pl.BlockSpec((w, LANES), lambda m, n, s: (prev(n), kcol + m)),
            pl.BlockSpec((w, LANES), lambda m, n, s: (n, kcol + m)),
            pl.BlockSpec((w, LANES), lambda m, n, s: (prev(n), vcol + m)),
            pl.BlockSpec((w, LANES), lambda m, n, s: (n, vcol + m)),
            qspec(same), qspec(nxt), qspec(same), qspec(nxt),
            lspec(same), lspec(nxt),
        ],
        out_specs=[
            pl.BlockSpec((w, qw), lambda m, n, s: (n, m)),
            pl.BlockSpec((w, LANES), lambda m, n, s: (n, m)),
            pl.BlockSpec((w, LANES), lambda m, n, s: (n, m)),
            pl.BlockSpec((1, 1, gh, LANES), lambda m, n, s: (m, n, 0, 0)),
        ],
    )
    return pl.pallas_call(
        body, name=name, grid_spec=grid_spec,
        out_shape=[jax.ShapeDtypeStruct((t, hq * ATT_HEAD), BF16),
                   jax.ShapeDtypeStruct((t, hkv * ATT_HEAD), BF16),
                   jax.ShapeDtypeStruct((t, hkv * ATT_HEAD), BF16),
                   jax.ShapeDtypeStruct((npair, nb, gh, LANES), F32)],
        compiler_params=_params((_PAR, _PAR)),
    )(sinks, qkv, qkv, qkv, qkv, qkv, qkv, o, o, do, do, lse, lse)


def swa_mixer_fwd(hn, w_in, w_out, sinks, positions, name):
    hq = sinks.shape[0]
    n_rot = (hq + hq // SWA_GROUP) * ATT_HEAD
    tabs = rope_tables(positions)
    proj = mm(hn, w_in, "nn", BF16, name + "_in")
    qkv = rope(proj, *tabs, n_rot, 1.0, name + "_rope")
    o, lse = swa_fwd(qkv, sinks, hq, name + "_att")
    y = mm(o, w_out, "nn", F32, name + "_out")
    return y, (qkv, o, lse)


def swa_mixer_bwd(dy, hn, w_in, w_out, sinks, positions, saved, name):
    qkv, o, lse = saved
    hq = sinks.shape[0]
    n_rot = (hq + hq // SWA_GROUP) * ATT_HEAD
    tabs = rope_tables(positions)
    dwo = wgrad(o, dy, name + "_dwo")
    do = mm(dy, w_out, "nt", BF16, name + "_do")
    dq, dk, dv, dsp = swa_bwd(qkv, o, lse, do, sinks, hq, name + "_attb")
    dproj = rope(jnp.concatenate([dq, dk, dv], axis=1), *tabs, n_rot, -1.0, name + "_ropeb")
    dwi = wgrad(hn, dproj, name + "_dwi")
    dhn = mm(dproj, w_in, "nt", BF16, name + "_dhn")
    dsinks = jnp.sum(dsp[:, :, :, 0], axis=1).reshape(hq)
    return dhn, dwi, dwo, dsinks


FOX_FPAD = 512


def _log_sigmoid(x):
    return jnp.minimum(x, 0.0) - jnp.log(1.0 + jnp.exp(-jnp.abs(x)))


def _tri(n, upper):
    r = lax.broadcasted_iota(jnp.int32, (n, n), 0)
    c = lax.broadcasted_iota(jnp.int32, (n, n), 1)
    return jnp.where((c >= r) if upper else (c <= r), 1.0, 0.0).astype(F32)


def _dot_exact(a, b):
    return jnp.dot(a, b, precision=lax.Precision.HIGHEST, preferred_element_type=F32)


def fox_cumsum(fl, b_pad, name):
    t = fl.shape[0]
    tr = _tile(t, (256, 128, 64, 32, 16, 8))

    def body(f_ref, b_ref, c_ref, carry_ref):
        @pl.when(pl.program_id(0) == 0)
        def _():
            carry_ref[...] = jnp.zeros_like(carry_ref)

        c = _dot_exact(_tri(tr, False), _log_sigmoid(f_ref[...] + b_ref[...])) + carry_ref[...]
        c_ref[...] = c
        carry_ref[...] = c[tr - 1:tr, :]

    return pl.pallas_call(
        body, name=name, grid=(t // tr,),
        in_specs=[_row_spec(tr, LANES), _full_spec((1, LANES))],
        out_specs=_row_spec(tr, LANES),
        out_shape=jax.ShapeDtypeStruct((t, LANES), F32),
        scratch_shapes=[pltpu.VMEM((1, LANES), F32)],
        compiler_params=_params((_ARB,)),
    )(fl, b_pad)


def fox_cumsum_bwd(dc, fl, b_pad, name):
    t = fl.shape[0]
    tr = _tile(t, (256, 128, 64, 32, 16, 8))
    nt = t // tr

    def body(d_ref, f_ref, b_ref, o_ref, db_ref, carry_ref):
        @pl.when(pl.program_id(0) == 0)
        def _():
            carry_ref[...] = jnp.zeros_like(carry_ref)
            db_ref[...] = jnp.zeros_like(db_ref)

        dlf = _dot_exact(_tri(tr, True), d_ref[...]) + carry_ref[...]
        carry_ref[...] = dlf[0:1, :]
        dfl = dlf * _sigmoid(-(f_ref[...] + b_ref[...]))
        o_ref[...] = dfl.astype(BF16)
        db_ref[...] += jnp.sum(dfl, axis=0, keepdims=True)

    rev = pl.BlockSpec((tr, LANES), lambda i: (nt - 1 - i, 0))
    return pl.pallas_call(
        body, name=name, grid=(nt,),
        in_specs=[rev, rev, _full_spec((1, LANES))],
        out_specs=[rev, _full_spec((1, LANES))],
        out_shape=[jax.ShapeDtypeStruct((t, LANES), BF16), jax.ShapeDtypeStruct((1, LANES), F32)],
        scratch_shapes=[pltpu.VMEM((1, LANES), F32)],
        compiler_params=_params((_ARB,)),
    )(dc, fl, b_pad)


AUG_C, AUG_ONE, AUG_LSE = ATT_HEAD, ATT_HEAD + 3, ATT_HEAD + 6


def _split3(x):
    hi = x.astype(BF16).astype(F32)
    mid = (x - hi).astype(BF16).astype(F32)
    return hi, mid, (x - hi - mid).astype(BF16).astype(F32)


def _aug(base, lane, entries):
    out = jnp.where(lane < ATT_HEAD, base, 0.0)
    for first, parts in entries:
        if parts is None:
            out = jnp.where((lane >= first) & (lane < first + 3), 1.0, out)
        else:
            for k, part in enumerate(parts):
                out = jnp.where(lane == first + k, part, out)
    return out.astype(BF16)


def _head_of_pair(x2, a):
    xf = x2.astype(F32)
    return xf if a == 0 else pltpu.roll(xf, ATT_HEAD, 1)


def fa_prep(proj, c, nh, name):
    t = proj.shape[0]
    npair = nh // 2
    tr = _tile(t, (256, 128))
    scale = ATT_HEAD ** -0.5

    def body(q_ref, k_ref, v_ref, c_ref, qa_ref, ka_ref, va_ref):
        lane = lax.broadcasted_iota(jnp.int32, (tr, LANES), 1)
        for p in range(npair):
            pc = slice(p * LANES, (p + 1) * LANES)
            for a in range(2):
                h = 2 * p + a
                hc = slice(h * LANES, (h + 1) * LANES)
                ch = c_ref[:, h:h + 1]
                qa_ref[:, hc] = _aug(_head_of_pair(q_ref[:, pc], a) * scale, lane,
                                     [(AUG_C, _split3(ch)), (AUG_ONE, None)])
                ka_ref[:, hc] = _aug(_head_of_pair(k_ref[:, pc], a), lane,
                                     [(AUG_C, None), (AUG_ONE, _split3(-ch)), (AUG_LSE, None)])
                va_ref[:, hc] = _aug(_head_of_pair(v_ref[:, pc], a), lane, [(AUG_C, None)])

    hd = nh * ATT_HEAD
    part = lambda k: pl.BlockSpec((tr, hd), lambda i: (i, k))
    out = pl.BlockSpec((tr, nh * LANES), lambda i: (i, 0))
    return pl.pallas_call(
        body, name=name, grid=(t // tr,),
        in_specs=[part(0), part(1), part(2), _row_spec(tr, LANES)],
        out_specs=[out, out, out],
        out_shape=[jax.ShapeDtypeStruct((t, nh * LANES), BF16)] * 3,
        compiler_params=_params((_PAR,)),
    )(proj, proj, proj, c)


def _fox_tiles(t, most):
    outer = _tile(t, tuple(s for s in (2048, 1024, 512, 256, 128) if s <= most))
    return outer, min(outer, 256)


def _diag_mask(outer, inner, d, transposed=False):
    r = lax.broadcasted_iota(jnp.int32, (outer, inner), 0)
    c = lax.broadcasted_iota(jnp.int32, (outer, inner), 1) + d * inner
    return (r <= c) if transposed else (c <= r)


def fa_fwd(qa, ka, va, proj, nh, name):
    t = qa.shape[0]
    hd = nh * ATT_HEAD
    npair = nh // 2
    tq, tk = _fox_tiles(t, 2048)
    nt, ratio = t // tq, tq // tk
    gcol = (3 * hd + FOX_FPAD) // LANES

    def body(q_ref, k_ref, v_ref, g_ref, o_ref, og_ref, lse_ref):
        i = pl.program_id(1)
        heads = [slice(a * LANES, (a + 1) * LANES) for a in range(2)]
        qs = [q_ref[:, cols] for cols in heads]

        def tile(j, carry, diag):
            rows = pl.ds(pl.multiple_of(j * tk, tk), tk)
            out = []
            for (mx, acc), q, cols in zip(carry, qs, heads):
                s = _dot_nt(q, k_ref[rows, cols])
                if diag is not None:
                    s = jnp.where(_diag_mask(tq, tk, diag), s, NEG)
                mnew = jnp.maximum(mx, jnp.max(s, axis=1, keepdims=True))
                p = jnp.exp(s - mnew).astype(BF16)
                out.append((mnew, jnp.exp(mx - mnew) * acc + _dot_nn(p, v_ref[rows, cols])))
            return tuple(out)

        carry = ((jnp.full((tq, 1), NEG, F32), jnp.zeros((tq, LANES), F32)),) * 2
        carry = lax.fori_loop(0, i * ratio, functools.partial(tile, diag=None), carry)
        for d in range(ratio):
            carry = tile(i * ratio + d, carry, d)
        outs = []
        for a, (mx, acc) in enumerate(carry):
            l = acc[:, AUG_C:AUG_C + 1]
            outs.append(acc / l)
            lse_ref[a] = jnp.broadcast_to(mx + jnp.log(l), (tq, LANES))
        o = jnp.where(_half((tq, LANES), 0), outs[0], pltpu.roll(outs[1], ATT_HEAD, 1))
        o_ref[...] = o.astype(BF16)
        og_ref[...] = (o * _sigmoid(g_ref[...].astype(F32))).astype(BF16)

    pair = pl.BlockSpec((tq, LANES), lambda p, i: (i, p))
    return pl.pallas_call(
        body, name=name, grid=(npair, nt),
        in_specs=[pl.BlockSpec((tq, 2 * LANES), lambda p, i: (i, p)),
                  pl.BlockSpec((t, 2 * LANES), lambda p, i: (0, p)),
                  pl.BlockSpec((t, 2 * LANES), lambda p, i: (0, p)),
                  pl.BlockSpec((tq, LANES), lambda p, i: (i, gcol + p))],
        out_specs=[pair, pair, pl.BlockSpec((2, tq, LANES), lambda p, i: (p, i, 0))],
        out_shape=[jax.ShapeDtypeStruct((t, hd), BF16), jax.ShapeDtypeStruct((t, hd), BF16),
                   jax.ShapeDtypeStruct((nh, t, LANES), F32)],
        compiler_params=_params((_PAR, _PAR)),
    )(qa, ka, va, proj)


def fa_prep_bwd(dog, o, proj, qa, lse, nh, name):
    t, hd = o.shape
    npair = nh // 2
    tr = _tile(t, (256, 128))
    gcol = (3 * hd + FOX_FPAD) // LANES

    def body(d_ref, o_ref, g_ref, q_ref, l_ref, dg_ref, qb_ref, da_ref):
        lane = lax.broadcasted_iota(jnp.int32, (tr, LANES), 1)
        dv, ov = d_ref[...].astype(F32), o_ref[...].astype(F32)
        sg = _sigmoid(g_ref[...].astype(F32))
        do = (dv * sg).astype(BF16).astype(F32)
        dg_ref[...] = (dv * ov * sg * (1.0 - sg)).astype(BF16)
        prod = do * ov
        for a in range(2):
            cols = slice(a * LANES, (a + 1) * LANES)
            delta = jnp.sum(jnp.where(_half(prod.shape, a), prod, 0.0), axis=1, keepdims=True)
            da_ref[:, cols] = _aug(_head_of_pair(do, a), lane, [(AUG_C, _split3(-delta))])
            nl = _split3(-l_ref[a][:, 0:1])
            qb = q_ref[:, cols].astype(F32)
            for k in range(3):
                qb = jnp.where(lane == AUG_LSE + k, nl[k], qb)
            qb_ref[:, cols] = qb.astype(BF16)

    pair = pl.BlockSpec((tr, LANES), lambda p, i: (i, p))
    wide = pl.BlockSpec((tr, 2 * LANES), lambda p, i: (i, p))
    return pl.pallas_call(
        body, name=name, grid=(npair, t // tr),
        in_specs=[pair, pair, pl.BlockSpec((tr, LANES), lambda p, i: (i, gcol + p)), wide,
                  pl.BlockSpec((2, tr, LANES), lambda p, i: (p, i, 0))],
        out_specs=[pair, wide, wide],
        out_shape=[jax.ShapeDtypeStruct((t, hd), BF16), jax.ShapeDtypeStruct((t, nh * LANES), BF16),
                   jax.ShapeDtypeStruct((t, nh * LANES), BF16)],
        compiler_params=_params((_PAR, _PAR)),
    )(dog, o, proj, qa, lse)


def fa_dq(qb, ka, va, da, nh, name):
    t = qb.shape[0]
    hd = nh * ATT_HEAD
    npair = nh // 2
    tq, tk = _fox_tiles(t, 1024)
    nt, ratio = t // tq, tq // tk
    scale = ATT_HEAD ** -0.5

    def body(q_ref, k_ref, v_ref, d_ref, dq_ref, rs_ref):
        i = pl.program_id(1)
        heads = [slice(a * LANES, (a + 1) * LANES) for a in range(2)]
        qs = [q_ref[:, cols] for cols in heads]
        ds = [d_ref[:, cols] for cols in heads]

        def tile(j, carry, diag):
            rows = pl.ds(pl.multiple_of(j * tk, tk), tk)
            out = []
            for acc, q, d, cols in zip(carry, qs, ds, heads):
                kj = k_ref[rows, cols]
                s = _dot_nt(q, kj)
                if diag is not None:
                    s = jnp.where(_diag_mask(tq, tk, diag), s, NEG)
                dsv = jnp.exp(s) * _dot_nt(d, v_ref[rows, cols])
                out.append(acc + _dot_nn(dsv.astype(BF16), kj))
            return tuple(out)

        accs = lax.fori_loop(0, i * ratio, functools.partial(tile, diag=None), (jnp.zeros((tq, LANES), F32),) * 2)
        for d in range(ratio):
            accs = tile(i * ratio + d, accs, d)
        dq_ref[...] = (jnp.where(_half((tq, LANES), 0), accs[0], pltpu.roll(accs[1], ATT_HEAD, 1)) * scale).astype(BF16)
        lane = lax.broadcasted_iota(jnp.int32, (tq, LANES), 1)
        rs_ref[...] = jnp.where(lane == 0, accs[0][:, AUG_C:AUG_C + 1],
                                jnp.where(lane == 1, accs[1][:, AUG_C:AUG_C + 1], 0.0))

    wide = pl.BlockSpec((tq, 2 * LANES), lambda p, i: (i, p))
    resident = pl.BlockSpec((t, 2 * LANES), lambda p, i: (0, p))
    pair = pl.BlockSpec((tq, LANES), lambda p, i: (i, p))
    return pl.pallas_call(
        body, name=name, grid=(npair, nt),
        in_specs=[wide, resident, resident, wide],
        out_specs=[pair, pair],
        out_shape=[jax.ShapeDtypeStruct((t, hd), BF16), jax.ShapeDtypeStruct((t, npair * LANES), F32)],
        compiler_params=_params((_PAR, _PAR)),
    )(qb, ka, va, da)


def fa_dkv(qb, ka, va, da, nh, name):
    t = qb.shape[0]
    hd = nh * ATT_HEAD
    npair = nh // 2
    tk, tq = _fox_tiles(t, 1024)
    nt, ratio = t // tk, tk // tq

    def body(q_ref, k_ref, v_ref, d_ref, dk_ref, dv_ref, cs_ref):
        j = pl.program_id(1)
        heads = [slice(a * LANES, (a + 1) * LANES) for a in range(2)]
        ks = [k_ref[:, cols] for cols in heads]
        vs = [v_ref[:, cols] for cols in heads]

        def tile(i, carry, diag):
            rows = pl.ds(pl.multiple_of(i * tq, tq), tq)
            out = []
            for (dk, dv), k, v, cols in zip(carry, ks, vs, heads):
                qi, di = q_ref[rows, cols], d_ref[rows, cols]
                st = _dot_nt(k, qi)
                if diag is not None:
                    st = jnp.where(_diag_mask(tk, tq, diag, True), st, NEG)
                pt = jnp.exp(st)
                dst = pt * _dot_nt(v, di)
                out.append((dk + _dot_nn(dst.astype(BF16), qi), dv + _dot_nn(pt.astype(BF16), di)))
            return tuple(out)

        zero = jnp.zeros((tk, LANES), F32)
        carry = ((zero, zero),) * 2
        for d in range(ratio):
            carry = tile(j * ratio + d, carry, d)
        carry = lax.fori_loop((j + 1) * ratio, t // tq, functools.partial(tile, diag=None), carry)
        dks, dvs = [c[0] for c in carry], [c[1] for c in carry]
        first = _half((tk, LANES), 0)
        dk_ref[...] = jnp.where(first, dks[0], pltpu.roll(dks[1], ATT_HEAD, 1)).astype(BF16)
        dv_ref[...] = jnp.where(first, dvs[0], pltpu.roll(dvs[1], ATT_HEAD, 1)).astype(BF16)
        lane = lax.broadcasted_iota(jnp.int32, (tk, LANES), 1)
        cs_ref[...] = jnp.where(lane == 0, dks[0][:, AUG_ONE:AUG_ONE + 1],
                                jnp.where(lane == 1, dks[1][:, AUG_ONE:AUG_ONE + 1], 0.0))

    wide = pl.BlockSpec((tk, 2 * LANES), lambda p, j: (j, p))
    resident = pl.BlockSpec((t, 2 * LANES), lambda p, j: (0, p))
    pair = pl.BlockSpec((tk, LANES), lambda p, j: (j, p))
    return pl.pallas_call(
        body, name=name, grid=(npair, nt),
        in_specs=[resident, wide, wide, resident],
        out_specs=[pair, pair, pair],
        out_shape=[jax.ShapeDtypeStruct((t, hd), BF16), jax.ShapeDtypeStruct((t, hd), BF16),
                   jax.ShapeDtypeStruct((t, npair * LANES), F32)],
        compiler_params=_params((_PAR, _PAR)),
    )(qb, ka, va, da)


def fox_pad_w_in(w_in, nh):
    hd = nh * ATT_HEAD
    pad = jnp.zeros((w_in.shape[0], FOX_FPAD - nh), w_in.dtype)
    return jnp.concatenate([w_in[:, :3 * hd + nh], pad, w_in[:, 3 * hd + nh:]], axis=1)


def fox_unpad_dw(dw, nh):
    hd = nh * ATT_HEAD
    return jnp.concatenate([dw[:, :3 * hd + nh], dw[:, 3 * hd + FOX_FPAD:]], axis=1)


def _pad_lanes(v):
    return jnp.pad(v.reshape(1, -1).astype(F32), ((0, 0), (0, LANES - v.size)))


def fox_mixer_fwd(hn, w_pad, w_out, b_f, name):
    nh = b_f.shape[0]
    hd = nh * ATT_HEAD
    proj = mm(hn, w_pad, "nn", BF16, name + "_in")
    fl = mm(hn, w_pad[:, 3 * hd:3 * hd + LANES], "nn", F32, name + "_fl")
    c = fox_cumsum(fl, _pad_lanes(b_f), name + "_cum")
    qa, ka, va = fa_prep(proj, c, nh, name + "_prep")
    o, og, lse = fa_fwd(qa, ka, va, proj, nh, name + "_att")
    y = mm(og, w_out, "nn", F32, name + "_out")
    return y, (proj, fl, qa, ka, va, o, og, lse)


def fox_mixer_bwd(dy, hn, w_pad, w_out, b_f, saved, name):
    proj, fl, qa, ka, va, o, og, lse = saved
    nh = b_f.shape[0]
    t = hn.shape[0]
    dwo = wgrad(og, dy, name + "_dwo")
    dog = mm(dy, w_out, "nt", BF16, name + "_dog")
    dg, qb, da = fa_prep_bwd(dog, o, proj, qa, lse, nh, name + "_prepb")
    dq, rsum = fa_dq(qb, ka, va, da, nh, name + "_dq")
    dk, dv, csum = fa_dkv(qb, ka, va, da, nh, name + "_dkv")
    dc = (rsum - csum).reshape(t, nh // 2, LANES)[:, :, :2].reshape(t, nh)
    dc = jnp.pad(dc, ((0, 0), (0, LANES - nh)))
    dfl, db = fox_cumsum_bwd(dc, fl, _pad_lanes(b_f), name + "_cumb")
    dfl = jnp.pad(dfl, ((0, 0), (0, FOX_FPAD - LANES)))
    dproj = jnp.concatenate([dq, dk, dv, dfl, dg], axis=1)
    dwi = wgrad(hn, dproj, name + "_dwi")
    dhn = mm(dproj, w_pad, "nt", BF16, name + "_dhn")
    return dhn, dwi, dwo, db[0, :nh]


HGRN_ROWS = 256
HGRN_TOGETHER = 8


def lb_table_row(lb_param, idx, name):
    nrow, w = lb_param.shape

    def body(p_ref, o_ref):
        rows = [p_ref[r:r + 1, :] for r in range(nrow)]
        mx = functools.reduce(jnp.maximum, rows)
        es = [jnp.exp(r - mx) for r in rows]
        o_ref[...] = sum(es[:idx + 1]) / sum(es)

    return pl.pallas_call(
        body, name=name, in_specs=[_full_spec((nrow, w))], out_specs=_full_spec((1, w)), grid=(1,),
        out_shape=jax.ShapeDtypeStruct((1, w), F32),
    )(lb_param)


def lb_table_row_bwd(lb_param, dlb, idx, name):
    nrow, w = lb_param.shape

    def body(p_ref, d_ref, o_ref):
        rows = [p_ref[r:r + 1, :] for r in range(nrow)]
        mx = functools.reduce(jnp.maximum, rows)
        es = [jnp.exp(r - mx) for r in rows]
        tot = sum(es)
        ps = [e / tot for e in es]
        dv = d_ref[...]
        inner = sum(ps[:idx + 1]) * dv
        for r in range(nrow):
            o_ref[r:r + 1, :] = ps[r] * ((dv if r <= idx else 0.0) - inner)

    return pl.pallas_call(
        body, name=name, in_specs=[_full_spec((nrow, w)), _full_spec((1, w))], out_specs=_full_spec((nrow, w)),
        grid=(1,), out_shape=jax.ShapeDtypeStruct((nrow, w), F32),
    )(lb_param, dlb)


def _hgrn_gates(qraw, fraw, lb):
    sq = _sigmoid(qraw)
    sf = _sigmoid(fraw)
    f = lb + (1.0 - lb) * sf
    return qraw * sq, sq, sf, f, 1.0 - f


def _hgrn_chunk(q, k, f):
    c = HGRN_CHUNK
    b = _dot_exact(_tri(c, False), jnp.log(f))
    bl = b[c - 1:c, :]
    bm = b[c // 2 - 1:c // 2, :]
    eq, ek = jnp.exp(b - bm), jnp.exp(bm - b)
    eb, el = jnp.exp(b), jnp.exp(bl - b)
    qt, kt = (q * eq).astype(BF16), (k * ek).astype(BF16)
    causal = _tri(c, False) > 0.5
    amat = jnp.where(causal, _dot_nt(qt, kt), 0.0).astype(BF16)
    return amat, qt, kt, (q * eb).astype(BF16), (k * el).astype(BF16), eq, ek, eb, el, jnp.exp(bl), causal


def hgrn_fwd(proj, lb, norm_g, nh, name):
    t = proj.shape[0]
    w = nh * HGRN_HEAD
    c = HGRN_CHUNK
    rows = _tile(t, (HGRN_ROWS, 128, 64, 32))
    nr, nc = t // rows, rows // c
    hp = _tile(nh, (HGRN_TOGETHER, 2, 1))
    wide = hp * HGRN_HEAD

    def body(q_ref, f_ref, i_ref, g_ref, lb_ref, ng_ref, og_ref, o_ref, st_ref, state):
        @pl.when(pl.program_id(1) == 0)
        def _():
            state[...] = jnp.zeros_like(state)

        def step(cc, carry):
            sl = pl.ds(pl.multiple_of(cc * c, c), c)
            for a in range(hp):
                hc = slice(a * HGRN_HEAD, (a + 1) * HGRN_HEAD)
                q, _, _, f, k = _hgrn_gates(q_ref[sl, hc].astype(F32), f_ref[sl, hc].astype(F32), lb_ref[:, hc])
                v = i_ref[sl, hc]
                amat, _, _, qd, kd, _, _, _, _, ebl, _ = _hgrn_chunk(q, k, f)
                st = state[a]
                st_ref[a, cc] = st.astype(BF16)
                o = _dot_nt(qd, st.astype(BF16)) + _dot_nn(amat, v)
                state[a] = st * ebl + _dot_tn(v, kd)
                o_ref[sl, hc] = o
                graw = g_ref[sl, hc].astype(F32)
                og_ref[sl, hc] = (o * _rstd(o) * ng_ref[...] * (graw * _sigmoid(graw))).astype(BF16)
            return carry

        lax.fori_loop(0, nc, step, 0)

    ng = nh // hp
    col = lambda off: pl.BlockSpec((rows, wide), lambda h, r: (r, off + h))
    return pl.pallas_call(
        body, name=name, grid=(ng, nr),
        in_specs=[col(0), col(ng), col(2 * ng), col(3 * ng),
                  pl.BlockSpec((1, wide), lambda h, r: (0, h)), _full_spec((1, HGRN_HEAD))],
        out_specs=[col(0), col(0), pl.BlockSpec((hp, nc, HGRN_HEAD, HGRN_HEAD), lambda h, r: (h, r, 0, 0))],
        out_shape=[jax.ShapeDtypeStruct((t, w), BF16), jax.ShapeDtypeStruct((t, w), F32),
                   jax.ShapeDtypeStruct((nh, t // c, HGRN_HEAD, HGRN_HEAD), BF16)],
        scratch_shapes=[pltpu.VMEM((hp, HGRN_HEAD, HGRN_HEAD), F32)],
        compiler_params=_params((_PAR, _ARB)),
    )(proj, proj, proj, proj, lb, norm_g)


def hgrn_bwd(proj, lb, norm_g, o, states, dog, nh, name):
    t = proj.shape[0]
    w = nh * HGRN_HEAD
    c = HGRN_CHUNK
    rows = _tile(t, (HGRN_ROWS, 128, 64, 32))
    nr, nc = t // rows, rows // c
    hp = _tile(nh, (HGRN_TOGETHER, 2, 1))
    wide = hp * HGRN_HEAD

    def body(q_ref, f_ref, i_ref, g_ref, lb_ref, ng_ref, o_ref, st_ref, dog_ref,
             dq_ref, df_ref, di_ref, dg_ref, dlb_ref, dng_ref, dstate):
        @pl.when(pl.program_id(1) == 0)
        def _():
            dstate[...] = jnp.zeros_like(dstate)
            dlb_ref[...] = jnp.zeros_like(dlb_ref)
            dng_ref[...] = jnp.zeros_like(dng_ref)

        ng = ng_ref[...]

        def step(idx, carry):
            cc = nc - 1 - idx
            sl = pl.ds(pl.multiple_of(cc * c, c), c)
            for a in range(hp):
                hc = slice(a * HGRN_HEAD, (a + 1) * HGRN_HEAD)
                lb = lb_ref[:, hc]
                qraw, fraw = q_ref[sl, hc].astype(F32), f_ref[sl, hc].astype(F32)
                q, sq, sf, f, k = _hgrn_gates(qraw, fraw, lb)
                v = i_ref[sl, hc]
                amat, qt, kt, qd, kd, eq, ek, eb, el, ebl, causal = _hgrn_chunk(q, k, f)
                ov = o_ref[sl, hc]
                graw = g_ref[sl, hc].astype(F32)
                dogv = dog_ref[sl, hc].astype(F32)
                sil, dsil = _silu_and_grad(graw)
                r = _rstd(ov)
                oh = ov * r
                don = dogv * sil
                dg_ref[sl, hc] = (dogv * oh * ng * dsil).astype(BF16)
                dng_ref[a] += jnp.sum(don * oh, axis=0, keepdims=True)
                doh = don * ng
                do = (r * (doh - oh * jnp.mean(doh * oh, axis=1, keepdims=True))).astype(BF16)
                dst = dstate[a]
                dstb = dst.astype(BF16)
                da = jnp.where(causal, _dot_nt(do, v), 0.0).astype(BF16)
                dv = _dot_tn(amat, do) + _dot_nt(kd, dstb)
                st0 = st_ref[a, cc]
                dq = _dot_nn(da, kt) * eq + _dot_nn(do, st0) * eb
                dk_inter = _dot_nn(v, dstb) * el
                dk = _dot_tn(da, qt) * ek + dk_inter
                dstate[a] = dst * ebl + _dot_tn(do, qd)
                through = jnp.sum(dst * st0.astype(F32), axis=0, keepdims=True) * ebl
                later = jnp.sum(k * dk_inter, axis=0, keepdims=True) + through
                dlf = _dot_exact(_tri(c, True), q * dq - k * dk) + later
                df = dlf / f - dk
                dq_ref[sl, hc] = (dq * (sq * (1.0 + qraw * (1.0 - sq)))).astype(BF16)
                df_ref[sl, hc] = (df * (1.0 - lb) * sf * (1.0 - sf)).astype(BF16)
                di_ref[sl, hc] = dv.astype(BF16)
                dlb_ref[:, hc] += jnp.sum(df * (1.0 - sf), axis=0, keepdims=True)
            return carry

        lax.fori_loop(0, nc, step, 0)

    ngr = nh // hp
    col = lambda off: pl.BlockSpec((rows, wide), lambda h, r: (nr - 1 - r, off + h))
    out = col(0)
    return pl.pallas_call(
        body, name=name, grid=(ngr, nr),
        in_specs=[col(0), col(ngr), col(2 * ngr), col(3 * ngr),
                  pl.BlockSpec((1, wide), lambda h, r: (0, h)), _full_spec((1, HGRN_HEAD)),
                  out, pl.BlockSpec((hp, nc, HGRN_HEAD, HGRN_HEAD), lambda h, r: (h, nr - 1 - r, 0, 0)), out],
        out_specs=[out, out, out, out, pl.BlockSpec((1, wide), lambda h, r: (0, h)),
                   pl.BlockSpec((hp, 1, HGRN_HEAD), lambda h, r: (h, 0, 0))],
        out_shape=[jax.ShapeDtypeStruct((t, w), BF16)] * 4 + [jax.ShapeDtypeStruct((1, w), F32),
                                                             jax.ShapeDtypeStruct((nh, 1, HGRN_HEAD), F32)],
        scratch_shapes=[pltpu.VMEM((hp, HGRN_HEAD, HGRN_HEAD), F32)],
        compiler_params=_params((_PAR, _ARB)),
    )(proj, proj, proj, proj, lb, norm_g, o, states, dog)


def hgrn_mixer_fwd(hn, w_in, w_out, norm_g, lb_param, idx, name):
    nh = w_out.shape[0] // HGRN_HEAD
    lb = lb_table_row(lb_param, idx, name + "_lb")
    proj = mm(hn, w_in, "nn", BF16, name + "_in")
    og, o, states = hgrn_fwd(proj, lb, norm_g, nh, name + "_rec")
    y = mm(og, w_out, "nn", F32, name + "_out")
    return y, (proj, lb, og, o, states)


def hgrn_mixer_bwd(dy, hn, w_in, w_out, norm_g, lb_param, idx, saved, name):
    proj, lb, og, o, states = saved
    nh = w_out.shape[0] // HGRN_HEAD
    dwo = wgrad(og, dy, name + "_dwo")
    dog = mm(dy, w_out, "nt", BF16, name + "_dog")
    dq, df, di, dg, dlb, dng = hgrn_bwd(proj, lb, norm_g, o, states, dog, nh, name + "_recb")
    dproj = jnp.concatenate([dq, df, di, dg], axis=1)
    dwi = wgrad(hn, dproj, name + "_dwi")
    dhn = mm(dproj, w_in, "nt", BF16, name + "_dhn")
    dlbp = lb_table_row_bwd(lb_param, dlb, idx, name + "_lbb")
    return dhn, dwi, dwo, jnp.sum(dng, axis=0), dlbp


_ANY = pl.BlockSpec(memory_space=pl.ANY)
_MESH = pl.DeviceIdType.MESH


def _place():
    x, y, c = lax.axis_index("x"), lax.axis_index("y"), lax.axis_index("c")
    chips = [(1 - x, y), (x, 1 - y), (1 - x, 1 - y)]
    return x, y, c, N_CHIPS // 2 * x + y, chips


def _chip_index(chip):
    return N_CHIPS // 2 * chip[0] + chip[1]


def _window(ref, axis, start, size):
    idx = [slice(None)] * len(ref.shape)
    idx[axis] = pl.ds(start, size)
    return ref.at[tuple(idx)]


_HBM = pl.BlockSpec(memory_space=pltpu.HBM)
_SEMS = pl.BlockSpec(memory_space=pltpu.SEMAPHORE)
_DATAFLOW = pltpu.SideEffectType.DATAFLOW_SIDE_EFFECTING


def _exchange_copy(src, land, axis, sems, k, j, chip, c, sender_side):
    x, y, _, me, _ = _place()
    peer = _chip_index(chip)
    if axis is None:
        src_part = src.at[peer]
        land_part = land.at[me if sender_side else peer]
    else:
        size = src.shape[axis]
        src_part = src
        land_part = _window(land, axis, (me if sender_side else peer) * size, size)
    which = k * (N_CHIPS - 1) + j
    return pltpu.make_async_remote_copy(src_ref=src_part, dst_ref=land_part, send_sem=sems[0].at[which],
                                        recv_sem=sems[1].at[which], device_id=(chip[0], chip[1], c),
                                        device_id_type=_MESH)


def place_own(shard, axis):
    _, _, _, me, _ = _place()
    shape = list(shard.shape)
    shape[axis] *= N_CHIPS
    return lax.dynamic_update_slice_in_dim(lax.empty(tuple(shape), shard.dtype), shard, me * shard.shape[axis], axis)


def cast_and_place(shards, layer, axis, name):
    _, rows, cols = shards.shape
    tr = _row_tile(rows, cols, 1 << 19)
    nrb = rows // tr
    _, _, _, me, _ = _place()
    full = (rows * N_CHIPS, cols) if axis == 0 else (rows, cols * N_CHIPS)

    def body(me_ref, s_ref, b_ref, land_ref):
        val = s_ref[...].astype(BF16)
        b_ref[...] = val
        land_ref[...] = val

    window = (lambda i, m: (m[0] * nrb + i, 0)) if axis == 0 else (lambda i, m: (i, m[0]))
    grid_spec = pltpu.PrefetchScalarGridSpec(
        num_scalar_prefetch=1, grid=(nrb,),
        in_specs=[pl.BlockSpec((None, tr, cols), lambda i, m: (layer, i, 0))],
        out_specs=[pl.BlockSpec((tr, cols), lambda i, m: (i, 0)), pl.BlockSpec((tr, cols), window)])
    return pl.pallas_call(
        body, name=name, grid_spec=grid_spec,
        out_shape=[jax.ShapeDtypeStruct((rows, cols), BF16), jax.ShapeDtypeStruct(full, BF16)],
        compiler_params=_params((_PAR,)),
    )(jnp.reshape(me, (1,)).astype(jnp.int32), shards)


def place_own_slot(p):
    _, _, _, me, _ = _place()
    mine = lax.dynamic_index_in_dim(p, me, 0, keepdims=True)
    return lax.dynamic_update_slice_in_dim(lax.empty(p.shape, p.dtype), mine, me, 0)


def exchange_start(srcs, lands, axes, name):
    n = len(srcs)

    def body(*refs):
        ins, lnd, sems = refs[:n], refs[n:2 * n], refs[2 * n:2 * n + 2]
        _, _, c, _, chips = _place()
        for k in range(n):
            for j, chip in enumerate(chips):
                _exchange_copy(ins[k], lnd[k], axes[k], sems, k, j, chip, c, True).start()
        refs[-1][...] = jnp.zeros_like(refs[-1])

    sem = pltpu.SemaphoreType.DMA((n * (N_CHIPS - 1),))
    arrays = list(srcs) + list(lands)
    out = pl.pallas_call(
        body, name=name,
        in_specs=[_HBM] * (2 * n),
        out_specs=(_SEMS, _SEMS) + (_HBM,) * (2 * n) + (pl.BlockSpec(memory_space=pltpu.VMEM),),
        out_shape=(sem, sem) + tuple(pltpu.HBM(a.shape, a.dtype) for a in arrays)
        + (jax.ShapeDtypeStruct((8, LANES), F32),),
        input_output_aliases={i: 2 + i for i in range(2 * n)},
        compiler_params=pltpu.CompilerParams(has_side_effects=_DATAFLOW),
    )(*[pltpu.with_memory_space_constraint(a, pltpu.HBM) for a in arrays])
    return (out[0], out[1], list(out[2:2 + n]), list(out[2 + n:2 + 2 * n]), list(axes)), out[-1]


def exchange_wait(started, ks, after, name):
    send, recv, srcs, lands, axes = started
    m = len(ks)
    after = list(after) if isinstance(after, (list, tuple)) else [after]

    def body(*refs):
        ins, lnd, sems = refs[:m], refs[m:2 * m], refs[2 * m:2 * m + 2]
        _, _, c, _, chips = _place()
        for q, k in enumerate(ks):
            for j, chip in enumerate(chips):
                cp = _exchange_copy(ins[q], lnd[q], axes[k], sems, k, j, chip, c, False)
                cp.wait_send()
                cp.wait_recv()

    arrays = [srcs[k] for k in ks] + [lands[k] for k in ks]
    out = pl.pallas_call(
        body, name=name,
        in_specs=[_HBM] * (2 * m) + [_SEMS, _SEMS] + [_ANY] * len(after),
        out_specs=(_HBM,) * (2 * m),
        out_shape=tuple(pltpu.HBM(a.shape, a.dtype) for a in arrays),
        input_output_aliases={i: i for i in range(2 * m)},
        compiler_params=pltpu.CompilerParams(has_side_effects=_DATAFLOW),
    )(*arrays, send, recv, *after)
    return list(out[m:])


def allreduce_small(buf, name):
    rows = buf.shape[0]
    n_dev = 2 * N_CHIPS

    def body(in_ref, out_ref, slots, send, recv):
        x, y, c, me, chips = _place()
        my_id = 2 * me + c
        slots[my_id] = in_ref[...]
        for j in range(1, n_dev):
            fx, fy, fc = (j >> 2) & 1, (j >> 1) & 1, j & 1
            peer = ((1 - x) if fx else x, (1 - y) if fy else y, (1 - c) if fc else c)
            pltpu.make_async_remote_copy(
                src_ref=in_ref, dst_ref=slots.at[my_id], send_sem=send.at[j], recv_sem=recv.at[j],
                device_id=peer, device_id_type=_MESH).start()
        for j in range(1, n_dev):
            fx, fy, fc = (j >> 2) & 1, (j >> 1) & 1, j & 1
            peer = ((1 - x) if fx else x, (1 - y) if fy else y, (1 - c) if fc else c)
            peer_id = 2 * _chip_index(peer) + peer[2]
            landed = pltpu.make_async_remote_copy(
                src_ref=in_ref, dst_ref=slots.at[peer_id], send_sem=send.at[j], recv_sem=recv.at[j],
                device_id=peer, device_id_type=_MESH)
            landed.wait_recv()
            landed.wait_send()
        tot = slots[0]
        for d in range(1, n_dev):
            tot = tot + slots[d]
        out_ref[...] = tot

    return pl.pallas_call(
        body, name=name,
        in_specs=[pl.BlockSpec(memory_space=pltpu.VMEM)], out_specs=pl.BlockSpec(memory_space=pltpu.VMEM),
        out_shape=jax.ShapeDtypeStruct(buf.shape, F32),
        scratch_shapes=[pltpu.VMEM((n_dev, rows, LANES), F32), pltpu.SemaphoreType.DMA((n_dev,)),
                        pltpu.SemaphoreType.DMA((n_dev,))],
        compiler_params=pltpu.CompilerParams(vmem_limit_bytes=VMEM_LIMIT),
    )(buf)


def _sibling_step(src_ref, slots, send, recv, credit, step, n_steps):
    x, y, c = lax.axis_index("x"), lax.axis_index("y"), lax.axis_index("c")
    slot = step % 2

    @pl.when(step >= 2)
    def _():
        pl.semaphore_wait(credit, 1)

    cp = pltpu.make_async_remote_copy(src_ref=src_ref, dst_ref=slots.at[slot], send_sem=send.at[slot],
                                      recv_sem=recv.at[slot], device_id=(x, y, 1 - c), device_id_type=_MESH)
    cp.start()
    cp.wait_recv()
    return cp, slot


def _sibling_done(cp, credit, step, n_steps):
    x, y, c = lax.axis_index("x"), lax.axis_index("y"), lax.axis_index("c")
    cp.wait_send()

    @pl.when(step < n_steps - 2)
    def _():
        pl.semaphore_signal(credit, 1, device_id=(x, y, 1 - c), device_id_type=_MESH)


def pair_sum(g, name):
    by_cols = g.ndim == 2
    s = N_CHIPS if by_cols else g.shape[0]
    r = g.shape[-2]
    cols = g.shape[-1] // s if by_cols else g.shape[-1]
    half = r // 2
    tr = _row_tile(half, cols, 1 << 19)
    nt = half // tr
    n_steps = s * nt

    def body(g_ref, o_ref, slots, send, recv, credit):
        c = lax.axis_index("c")
        step = pl.program_id(0) * nt + pl.program_id(1)
        cp, slot = _sibling_step(g_ref.at[0, 1 - c], slots, send, recv, credit, step, n_steps)
        o_ref[0] = (g_ref[0, c].astype(F32) + slots[slot].astype(F32)).astype(BF16)
        _sibling_done(cp, credit, step, n_steps)

    if by_cols:
        in_spec = pl.BlockSpec((1, 2, tr, cols), lambda k, i: (0, 0, i, k))
        g4 = g.reshape(1, 2, half, s * cols)
    else:
        in_spec = pl.BlockSpec((1, 2, tr, cols), lambda k, i: (k, 0, i, 0))
        g4 = g.reshape(s, 2, half, cols)
    return pl.pallas_call(
        body, name=name, grid=(s, nt),
        in_specs=[in_spec],
        out_specs=pl.BlockSpec((1, tr, cols), lambda k, i: (k, i, 0)),
        out_shape=jax.ShapeDtypeStruct((s, half, cols), BF16),
        scratch_shapes=[pltpu.VMEM((2, tr, cols), BF16), pltpu.SemaphoreType.DMA((2,)), pltpu.SemaphoreType.DMA((2,)),
                        pltpu.SemaphoreType.REGULAR],
        compiler_params=_params((_ARB, _ARB)),
    )(g4)


def chip_sum_share(q, acc, layer, name, after):
    s, r2, cols = q.shape
    tr = _row_tile(r2, cols, 1 << 18)
    nt = r2 // tr

    def body(q_ref, acc_ref, after_ref, o_ref, slots, send, recv, credit):
        c = lax.axis_index("c")
        step = pl.program_id(0)
        tot = q_ref[0].astype(F32)
        for k in range(1, s):
            tot = tot + q_ref[k].astype(F32)
        o_ref[0, c] = tot
        cp, slot = _sibling_step(o_ref.at[0, c], slots, send, recv, credit, step, nt)
        o_ref[0, 1 - c] = slots[slot]
        _sibling_done(cp, credit, step, nt)

    return pl.pallas_call(
        body, name=name, grid=(nt,),
        in_specs=[pl.BlockSpec((s, tr, cols), lambda i: (0, i, 0)), _ANY, _ANY],
        out_specs=pl.BlockSpec((1, 2, tr, cols), lambda i: (layer, 0, i, 0)),
        out_shape=jax.ShapeDtypeStruct(acc.shape, F32),
        input_output_aliases={1: 0},
        scratch_shapes=[pltpu.VMEM((2, tr, cols), F32), pltpu.SemaphoreType.DMA((2,)), pltpu.SemaphoreType.DMA((2,)),
                        pltpu.SemaphoreType.REGULAR],
        compiler_params=_params((_ARB,)),
    )(q, acc, after)


def _row_tile(rows, cols, budget):
    for tr in (1024, 512, 256, 128, 64, 32, 16, 8):
        if rows % tr == 0 and tr * cols <= budget:
            return tr
    return rows


def adamw(w, g, m, v, name, after):
    rows, cols = w.shape
    tr = _row_tile(rows, cols, 1 << 18)
    c1 = 1.0 - ADAM_B1 ** ADAM_STEP
    c2 = 1.0 - ADAM_B2 ** ADAM_STEP

    def body(w_ref, g_ref, m_ref, v_ref, after_ref, d_ref, nm_ref, nv_ref):
        gv = g_ref[...]
        nm = ADAM_B1 * m_ref[...] + (1.0 - ADAM_B1) * gv
        nv = ADAM_B2 * v_ref[...] + (1.0 - ADAM_B2) * (gv * gv)
        nm_ref[...] = nm
        nv_ref[...] = nv
        d_ref[...] = -ADAM_LR * ((nm / c1) / (jnp.sqrt(nv / c2) + ADAM_EPS) + ADAM_WD * w_ref[...])

    spec = pl.BlockSpec((tr, cols), lambda i: (i, 0))
    return pl.pallas_call(
        body, name=name, grid=(rows // tr,),
        in_specs=[spec] * 4 + [_ANY], out_specs=[spec] * 3,
        out_shape=[jax.ShapeDtypeStruct((rows, cols), F32)] * 3,
        compiler_params=_params((_PAR,)),
    )(w, g, m, v, after)


WEIGHTS = ("mix_pre_g", "mix_post_g", "ffn_pre_g", "ffn_post_g", "hgrn_w_in", "hgrn_w_out", "hgrn_norm_g",
           "hgrn_lb_param", "swa_w_in", "swa_w_out", "swa_sinks", "sc_w_in", "sc_conv_w", "sc_w_out", "fox_w_in",
           "fox_b_f", "fox_w_out", "ffn_w_up", "ffn_conv_w", "ffn_conv_b", "ffn_w_down")
N_MIXERS = 4


def _pack_small(parts):
    flat = jnp.concatenate([p.reshape(-1).astype(F32) for p in parts])
    rows = -(-flat.shape[0] // (8 * LANES)) * 8
    return jnp.pad(flat, (0, rows * LANES - flat.shape[0])).reshape(rows, LANES)


def _unpack_small(buf, shapes):
    flat, out, off = buf.reshape(-1), [], 0
    for s in shapes:
        n = math.prod(s)
        out.append(flat[off:off + n].reshape(s))
        off += n
    return out


def _stack_rows(dw):
    return dw.reshape(N_CHIPS, dw.shape[0] // N_CHIPS, dw.shape[1])


def kernel(x, positions, mix_pre_g, mix_post_g, ffn_pre_g, ffn_post_g, hgrn_w_in, hgrn_w_out, hgrn_norm_g, hgrn_lb_param, swa_w_in, swa_w_out, swa_sinks, sc_w_in, sc_conv_w, sc_w_out, fox_w_in, fox_b_f, fox_w_out, ffn_w_up, ffn_conv_w, ffn_conv_b, ffn_w_down, loss_target, m_mix_pre_g, m_mix_post_g, m_ffn_pre_g, m_ffn_post_g, m_hgrn_w_in, m_hgrn_w_out, m_hgrn_norm_g, m_hgrn_lb_param, m_swa_w_in, m_swa_w_out, m_swa_sinks, m_sc_w_in, m_sc_conv_w, m_sc_w_out, m_fox_w_in, m_fox_b_f, m_fox_w_out, m_ffn_w_up, m_ffn_conv_w, m_ffn_conv_b, m_ffn_w_down, v_mix_pre_g, v_mix_post_g, v_ffn_pre_g, v_ffn_post_g, v_hgrn_w_in, v_hgrn_w_out, v_hgrn_norm_g, v_hgrn_lb_param, v_swa_w_in, v_swa_w_out, v_swa_sinks, v_sc_w_in, v_sc_conv_w, v_sc_w_out, v_fox_w_in, v_fox_b_f, v_fox_w_out, v_ffn_w_up, v_ffn_conv_w, v_ffn_conv_b, v_ffn_w_down):
    given = dict(locals())
    depth = mix_pre_g.shape[0]
    assert depth == N_MIXERS and x.shape[0] == 1, "one batch element per device, one layer of each mixer"
    xi, target = x[0], loss_target[0]
    chip = N_CHIPS // 2 * lax.axis_index("x") + lax.axis_index("y")
    nh_fox = fox_b_f.shape[1]
    row = lambda a, i: a[i:i + 1]

    units = {"hg_in": (hgrn_w_in, 0, 1), "hg_out": (hgrn_w_out, 0, 0), "sw_in": (swa_w_in, 0, 1),
             "sw_out": (swa_w_out, 0, 0), "sc_in": (sc_w_in, 0, 1), "sc_out": (sc_w_out, 0, 0),
             "fx_in": (fox_w_in, 0, 0), "fx_out": (fox_w_out, 0, 0)}
    mix_units = (("hg_in", "hg_out"), ("sw_in", "sw_out"), ("sc_in", "sc_out", "sc_cw"), ("fx_in", "fx_out"))
    ffn_units = []
    for i in range(depth):
        units[f"up{i}"], units[f"down{i}"] = (ffn_w_up, i, 1), (ffn_w_down, i, 0)
        ffn_units.append((f"up{i}", f"down{i}") + (("f_cw",) if i == 0 else ()))
    order = [n for i in range(depth) for n in mix_units[i] + ffn_units[i]]
    placed = {n: cast_and_place(*units[n], "place_" + n) + (units[n][2],) for n in units}
    placed["f_cw"] = (ffn_conv_w, place_own(ffn_conv_w, 2), 2)
    placed["sc_cw"] = (sc_conv_w[0], place_own(sc_conv_w[0], 1), 1)
    gather, _ = exchange_start([placed[n][0] for n in order], [placed[n][1] for n in order],
                               [placed[n][2] for n in order], "gather_start")
    wt = {}

    def arrive(names, after, name):
        wt.update(zip(names, exchange_wait(gather, [order.index(n) for n in names], after, name)))

    saved = []
    xs = xi
    hn = rms_fwd(xs, row(mix_pre_g, 0), "pre_norm0")
    dx = loss = None
    for i in range(depth):
        nm = f"l{i}"
        arrive(mix_units[i], hn, nm + "_w_mix")
        if i == 0:
            y, sv = hgrn_mixer_fwd(hn, wt["hg_in"], wt["hg_out"], hgrn_norm_g, hgrn_lb_param, i, nm + "_hgrn")
        elif i == 1:
            y, sv = swa_mixer_fwd(hn, wt["sw_in"], wt["sw_out"], swa_sinks[0], positions, nm + "_swa")
        elif i == 2:
            proj = mm(hn, wt["sc_in"], "nn", BF16, nm + "_sc_in")
            yb = sconv_fwd(proj, wt["sc_cw"], nm + "_sc_conv")
            y, sv = mm(yb, wt["sc_out"], "nn", F32, nm + "_sc_out"), (proj, yb)
        else:
            fx4 = wt["fx_in"].reshape(N_CHIPS, -1, wt["fx_in"].shape[1])
            wt["fx_pad"] = fox_pad_w_in(jnp.concatenate([fx4[s] for s in range(N_CHIPS)], axis=1), nh_fox)
            y, sv = fox_mixer_fwd(hn, wt["fx_pad"], wt["fx_out"], fox_b_f[0], nm + "_fox")
        x1, hn2 = resid_norm(xs, y, row(mix_post_g, i), row(ffn_pre_g, i), nm + "_mix_resid")
        arrive(ffn_units[i], hn2, nm + "_w_ffn")
        z = mm(hn2, wt[f"up{i}"], "nn", BF16, nm + "_ffn_up")
        a = ffn_act(z, wt["f_cw"][i], row(ffn_conv_b, i), nm + "_ffn_act")
        y2 = mm(a, wt[f"down{i}"], "nn", F32, nm + "_ffn_down")
        saved.append((xs, hn, y, sv, x1, hn2, z, a, y2))
        if i < depth - 1:
            xs, hn = resid_norm(x1, y2, row(ffn_post_g, i), row(mix_pre_g, i + 1), nm + "_ffn_resid")
        else:
            dx, loss = resid_loss(x1, y2, row(ffn_post_g, i), target, nm + "_loss")

    grads = {}

    def start_reduce(tag, named):
        ps = [pair_sum(g, f"{tag}_pair_{n}") for n, _, g in named]
        started, token = exchange_start(ps, [place_own_slot(p) for p in ps], [None] * len(ps), tag + "_chips_start")
        return (named, started), token

    def finish_reduce(tag, pending, after):
        named, started = pending
        qs = exchange_wait(started, list(range(len(named))), after, tag + "_chips_wait")
        for (n, l, _), q in zip(named, qs):
            if n not in grads:
                grads[n] = lax.empty((given[n].shape[0], 2) + q.shape[1:], F32)
            grads[n] = chip_sum_share(q, grads[n], l, f"{tag}_share_{n}", after[0])

    def update(n, after):
        w = given[n]
        flat = lambda a: a.reshape(-1, w.shape[-1])
        dl, nm_, nv_ = adamw(flat(w), flat(grads[n]), flat(given["m_" + n]), flat(given["v_" + n]), "adamw_" + n,
                             after)
        deltas[n], new_m[n], new_v[n] = dl.reshape(w.shape), nm_.reshape(w.shape), nv_.reshape(w.shape)

    deltas, new_m, new_v = {}, {}, {}

    d_pre, d_post, d_fpre, d_fpost = [None] * depth, [None] * depth, [None] * depth, [None] * depth
    d_fcw, d_fcb = [None] * depth, [None] * depth
    small = {}
    pending = token = None
    for i in reversed(range(depth)):
        nm = f"l{i}b"
        xs, hn, y, sv, x1, hn2, z, a, y2 = saved[i]
        f_cw = wt["f_cw"][i]
        dy2, d_fpost[i] = norm_bwd(y2, row(ffn_post_g, i), dx, None, BF16, nm + "_ffn_post", after=token)
        d_down = _stack_rows(wgrad(a, dy2, nm + "_dw_down"))
        da = mm(dy2, wt[f"down{i}"], "nt", BF16, nm + "_da")
        du, acc = ffn_act_bwd(z, da, f_cw, row(ffn_conv_b, i), nm + "_ffn_actb")
        d_fcw[i], d_fcb[i] = acc[0:CONV_WIDTH], acc[CONV_WIDTH]
        dz = conv_transpose(du, f_cw, nm + "_ffn_convT")
        d_up = wgrad(hn2, dz, nm + "_dw_up")
        dhn2 = mm(dz, wt[f"up{i}"], "nt", BF16, nm + "_dhn2")
        dx1, d_fpre[i] = norm_bwd(x1, row(ffn_pre_g, i), dhn2, dx, F32, nm + "_ffn_pre")
        dy, d_post[i] = norm_bwd(y, row(mix_post_g, i), dx1, None, BF16, nm + "_mix_post")
        if i == 0:
            dhn, dwi, dwo, small["hgrn_norm_g"], small["hgrn_lb_param"] = hgrn_mixer_bwd(
                dy, hn, wt["hg_in"], wt["hg_out"], hgrn_norm_g, hgrn_lb_param, i, sv, nm + "_hgrn")
            w_in, w_out = "hgrn_w_in", "hgrn_w_out"
        elif i == 1:
            dhn, dwi, dwo, small["swa_sinks"] = swa_mixer_bwd(dy, hn, wt["sw_in"], wt["sw_out"], swa_sinks[0],
                                                             positions, sv, nm + "_swa")
            w_in, w_out = "swa_w_in", "swa_w_out"
        elif i == 2:
            proj, yb = sv
            dwo = wgrad(yb, dy, nm + "_sc_dwo")
            dyb = mm(dy, wt["sc_out"], "nt", BF16, nm + "_sc_dyb")
            dproj, acc = sconv_bwd(proj, dyb, wt["sc_cw"], nm + "_sc_convb")
            dwi = wgrad(hn, dproj, nm + "_sc_dwi")
            dhn = mm(dproj, wt["sc_in"], "nt", BF16, nm + "_sc_dhn")
            small["sc_conv_w"] = acc[0:CONV_WIDTH]
            w_in, w_out = "sc_w_in", "sc_w_out"
        else:
            dhn, dwi, dwo, small["fox_b_f"] = fox_mixer_bwd(dy, hn, wt["fx_pad"], wt["fx_out"], fox_b_f[0], sv,
                                                            nm + "_fox")
            dwi = fox_unpad_dw(dwi, nh_fox)
            cols = dwi.shape[1] // N_CHIPS
            dwi = jnp.stack([dwi[:, s * cols:(s + 1) * cols] for s in range(N_CHIPS)])
            w_in, w_out = "fox_w_in", "fox_w_out"
        dx, d_pre[i] = norm_bwd(xs, row(mix_pre_g, i), dhn, dx1, F32, nm + "_mix_pre")
        earlier = pending
        pending, token = start_reduce(nm, [(w_in, 0, dwi), (w_out, 0, _stack_rows(dwo)), ("ffn_w_up", i, d_up),
                                           ("ffn_w_down", i, d_down)])
        if earlier is not None:
            finish_reduce(f"l{i + 1}b", earlier, [token, dx])
    small.update(mix_pre_g=jnp.concatenate(d_pre), mix_post_g=jnp.concatenate(d_post),
                 ffn_pre_g=jnp.concatenate(d_fpre), ffn_post_g=jnp.concatenate(d_fpost),
                 ffn_conv_w=jnp.stack(d_fcw), ffn_conv_b=jnp.stack(d_fcb))

    small_names = [n for n in WEIGHTS if n in small]
    full_shape = {n: tuple(given[n].shape) for n in small_names}
    full_shape["sc_conv_w"] = (1, CONV_WIDTH, wt["sc_cw"].shape[1])
    full_shape["ffn_conv_w"] = tuple(wt["f_cw"].shape)
    small_sum = allreduce_small(_pack_small([small[n] for n in small_names] + [loss]), "small_sum")
    summed = _unpack_small(small_sum, [full_shape[n] for n in small_names] + [()])
    loss = summed[-1]
    for n, g in zip(small_names, summed):
        if g.shape != given[n].shape:
            width = given[n].shape[-1]
            g = lax.dynamic_slice_in_dim(g, chip * width, width, axis=g.ndim - 1)
        grads[n] = g

    last = [n for n, _, _ in pending[0]]
    ready = [n for n in WEIGHTS if n not in last]
    for n in ready:
        update(n, token)
    finish_reduce("l0b", pending, [token, small_sum] + [deltas[n] for n in ready])
    for n in last:
        update(n, token)
    return (loss, dx[None], *[grads[n].reshape(given[n].shape) for n in WEIGHTS], *[deltas[n] for n in WEIGHTS],
            *[new_m[n] for n in WEIGHTS], *[new_v[n] for n in WEIGHTS])
```

```python
import functools
import math

import numpy as np
import jax
import jax.numpy as jnp
from jax import lax
from jax.experimental import pallas as pl
from jax.experimental.pallas import tpu as pltpu

F32 = jnp.float32
BF16 = jnp.bfloat16

RMS_EPS = 1e-6
HGRN_HEAD = 128
HGRN_CHUNK = 32
ATT_HEAD = 64
SWA_WINDOW = 128
SWA_GROUP = 8
ROT_DIM = 16
ROPE_THETA = 500000.0
CONV_WIDTH = 3
ADAM_LR = 0.001
ADAM_B1 = 0.9
ADAM_B2 = 0.999
ADAM_EPS = 1e-08
ADAM_WD = 0.01
ADAM_STEP = 10
N_CHIPS = 4
LANES = 128
BF16_ROWS = 16
VMEM_LIMIT = 48 * 1024 * 1024

_ARB = "arbitrary"
_PAR = "parallel"


def _params(sem, **kw):
    return pltpu.CompilerParams(dimension_semantics=sem, vmem_limit_bytes=VMEM_LIMIT, **kw)


def _tile(n, prefs):
    for p in prefs:
        if n % p == 0:
            return p
    return n


def _sigmoid(x):
    return 1.0 / (1.0 + jnp.exp(-x))


def _dot(a, b, dims):
    return lax.dot_general(a, b, (dims, ((), ())), preferred_element_type=F32)


def _dot_nn(a, b):
    return _dot(a, b, ((1,), (0,)))


def _dot_nt(a, b):
    return _dot(a, b, ((1,), (1,)))


def _dot_tn(a, b):
    return _dot(a, b, ((0,), (0,)))


MM_VMEM_BUDGET = 36 * 1024 * 1024
MM_HBM_RATE = 3.0e12
MM_MXU_RATE = 6.5e14
MM_STEP_S = 0.35e-6
MM_ACC_RATE = 3.0e12


def _mm_tiles(m, n, k, out_bytes):
    best = None
    for tm in (2048, 1024, 512, 256, 128):
        for tn in (2048, 1024, 512, 256, 128):
            for tk in sorted({k, 4096, 2816, 2048, 1408, 1024, 512, 256, 128}, reverse=True):
                if m % tm or n % tn or tk > k or k % tk:
                    continue
                nk = k // tk
                vmem = 4 * (tm * tk + tk * tn) + (4 * tm * tn if nk > 1 else 0) + 2 * tm * tn * out_bytes
                if vmem > MM_VMEM_BUDGET:
                    continue
                steps = (m // tm) * (n // tn) * nk
                traffic = 2 * m * k * (1 if nk == 1 else n // tn) + 2 * k * n * (m // tm) + m * n * out_bytes
                cost = max(traffic / MM_HBM_RATE, 2 * m * n * k / MM_MXU_RATE) + steps * MM_STEP_S
                if nk > 1:
                    cost += steps * 8 * tm * tn / MM_ACC_RATE
                if best is None or cost < best[0]:
                    best = (cost, tm, tn, tk)
    assert best is not None, (m, n, k)
    return best[1:]


def mm(a, b, mode, out_dtype, name="mm"):
    if mode == "nn":
        (m, k), (k2, n) = a.shape, b.shape
    elif mode == "nt":
        (m, k), (n, k2) = a.shape, b.shape
    else:
        (k, m), (k2, n) = a.shape, b.shape
    assert k == k2, (a.shape, b.shape, mode)
    tm, tn, tk = _mm_tiles(m, n, k, jnp.dtype(out_dtype).itemsize)
    nk = k // tk

    def product(a_ref, b_ref):
        av = a_ref[...].astype(BF16)
        bv = b_ref[...].astype(BF16)
        return {"nn": _dot_nn, "nt": _dot_nt, "tn": _dot_tn}[mode](av, bv)

    def body_one(a_ref, b_ref, o_ref):
        o_ref[...] = product(a_ref, b_ref).astype(out_dtype)

    def body_acc(a_ref, b_ref, o_ref, acc_ref):
        kk = pl.program_id(2)

        @pl.when(kk == 0)
        def _():
            acc_ref[...] = jnp.zeros_like(acc_ref)

        acc_ref[...] += product(a_ref, b_ref)

        @pl.when(kk == nk - 1)
        def _():
            o_ref[...] = acc_ref[...].astype(out_dtype)

    if mode == "nn":
        a_spec = pl.BlockSpec((tm, tk), lambda i, j, kk: (i, kk))
        b_spec = pl.BlockSpec((tk, tn), lambda i, j, kk: (kk, j))
    elif mode == "nt":
        a_spec = pl.BlockSpec((tm, tk), lambda i, j, kk: (i, kk))
        b_spec = pl.BlockSpec((tn, tk), lambda i, j, kk: (j, kk))
    else:
        a_spec = pl.BlockSpec((tk, tm), lambda i, j, kk: (kk, i))
        b_spec = pl.BlockSpec((tk, tn), lambda i, j, kk: (kk, j))
    return pl.pallas_call(
        body_one if nk == 1 else body_acc,
        name=name,
        grid=(m // tm, n // tn, nk),
        in_specs=[a_spec, b_spec],
        out_specs=pl.BlockSpec((tm, tn), lambda i, j, kk: (i, j)),
        out_shape=jax.ShapeDtypeStruct((m, n), out_dtype),
        scratch_shapes=[] if nk == 1 else [pltpu.VMEM((tm, tn), F32)],
        compiler_params=_params((_PAR, _PAR, _ARB)),
    )(a, b)


def wgrad(a, b, name):
    return mm(a, b, "tn", BF16, name)


def _rstd(xv):
    return lax.rsqrt(jnp.mean(xv * xv, axis=1, keepdims=True) + RMS_EPS)


def _row_spec(tr, w):
    return pl.BlockSpec((tr, w), lambda i: (i, 0))


def _full_spec(shape):
    nd = len(shape)
    return pl.BlockSpec(shape, lambda *_: (0,) * nd)


def rms_fwd(x, g, name):
    t, d = x.shape
    tr = _tile(t, (256, 128, 64, 32, 16))

    def body(x_ref, g_ref, o_ref):
        xv = x_ref[...]
        o_ref[...] = (xv * _rstd(xv) * g_ref[...]).astype(BF16)

    return pl.pallas_call(
        body, name=name, grid=(t // tr,),
        in_specs=[_row_spec(tr, d), _full_spec((1, d))],
        out_specs=_row_spec(tr, d),
        out_shape=jax.ShapeDtypeStruct((t, d), BF16),
        compiler_params=_params((_PAR,)),
    )(x, g)


def resid_norm(x, y, g_post, g_next, name):
    t, d = x.shape
    tr = _tile(t, (256, 128, 64, 32, 16))

    def body(x_ref, y_ref, gp_ref, gn_ref, x1_ref, hn_ref):
        yv = y_ref[...]
        x1 = x_ref[...] + yv * _rstd(yv) * gp_ref[...]
        x1_ref[...] = x1
        hn_ref[...] = (x1 * _rstd(x1) * gn_ref[...]).astype(BF16)

    return pl.pallas_call(
        body, name=name, grid=(t // tr,),
        in_specs=[_row_spec(tr, d), _row_spec(tr, d), _full_spec((1, d)), _full_spec((1, d))],
        out_specs=[_row_spec(tr, d), _row_spec(tr, d)],
        out_shape=[jax.ShapeDtypeStruct((t, d), F32), jax.ShapeDtypeStruct((t, d), BF16)],
        compiler_params=_params((_PAR,)),
    )(x, y, g_post, g_next)


def resid_loss(x, y, g_post, target, name):
    t, d = x.shape
    tr = _tile(t, (256, 128, 64, 32, 16))

    def body(x_ref, y_ref, gp_ref, t_ref, dx_ref, loss_ref):
        @pl.when(pl.program_id(0) == 0)
        def _():
            loss_ref[...] = jnp.zeros_like(loss_ref)

        yv = y_ref[...]
        err = x_ref[...] + yv * _rstd(yv) * gp_ref[...] - t_ref[...]
        dx_ref[...] = err * (1.0 / d)
        loss_ref[...] += 0.5 * jnp.sum(jnp.mean(err * err, axis=1, keepdims=True), axis=0, keepdims=True)

    dx, loss = pl.pallas_call(
        body, name=name, grid=(t // tr,),
        in_specs=[_row_spec(tr, d), _row_spec(tr, d), _full_spec((1, d)), _row_spec(tr, d)],
        out_specs=[_row_spec(tr, d), _full_spec((8, LANES))],
        out_shape=[jax.ShapeDtypeStruct((t, d), F32), jax.ShapeDtypeStruct((8, LANES), F32)],
        compiler_params=_params((_ARB,)),
    )(x, y, g_post, target)
    return dx, loss[0:1, 0:1]


def norm_bwd(yin, g, dout, res, out_dtype, name, after=None):
    t, d = yin.shape
    tr = _tile(t, (256, 128, 64, 32, 16))
    has_res = res is not None

    def body(*refs):
        refs = refs[:3 + has_res] + refs[-2:]
        if has_res:
            y_ref, g_ref, d_ref, r_ref, o_ref, dg_ref = refs
        else:
            y_ref, g_ref, d_ref, o_ref, dg_ref = refs

        @pl.when(pl.program_id(0) == 0)
        def _():
            dg_ref[...] = jnp.zeros_like(dg_ref)

        yv = y_ref[...]
        dv = d_ref[...].astype(F32)
        r = _rstd(yv)
        yh = yv * r
        dyh = dv * g_ref[...]
        dy = r * (dyh - yh * jnp.mean(dyh * yh, axis=1, keepdims=True))
        if has_res:
            dy = dy + r_ref[...]
        o_ref[...] = dy.astype(out_dtype)
        dg_ref[...] += jnp.sum(dv * yh, axis=0, keepdims=True)

    ins = [yin, g, dout] + ([res] if has_res else []) + ([] if after is None else [after])
    in_specs = ([_row_spec(tr, d), _full_spec((1, d)), _row_spec(tr, d)] + ([_row_spec(tr, d)] if has_res else [])
                + ([] if after is None else [pl.BlockSpec(memory_space=pl.ANY)]))
    return pl.pallas_call(
        body, name=name, grid=(t // tr,),
        in_specs=in_specs,
        out_specs=[_row_spec(tr, d), _full_spec((1, d))],
        out_shape=[jax.ShapeDtypeStruct((t, d), out_dtype), jax.ShapeDtypeStruct((1, d), F32)],
        compiler_params=_params((_ARB,)),
    )(*ins)


def _shift_down(x, halo):
    tr = x.shape[0]
    row = lax.broadcasted_iota(jnp.int32, x.shape, 0)
    h1 = halo[BF16_ROWS - 1:BF16_ROWS, :]
    h2 = halo[BF16_ROWS - 2:BF16_ROWS - 1, :]
    x1 = jnp.where(row == 0, h1, pltpu.roll(x, 1, 0))
    x2 = jnp.where(row == 0, h2, jnp.where(row == 1, h1, pltpu.roll(x, 2, 0)))
    return x1, x2


def _shift_up(x, halo):
    tr = x.shape[0]
    row = lax.broadcasted_iota(jnp.int32, x.shape, 0)
    h0 = halo[0:1, :]
    h1 = halo[1:2, :]
    x1 = jnp.where(row == tr - 1, h0, pltpu.roll(x, tr - 1, 0))
    x2 = jnp.where(row == tr - 1, h1, jnp.where(row == tr - 2, h0, pltpu.roll(x, tr - 2, 0)))
    return x1, x2


def _prev_halo_spec(tr, w, nt):
    return pl.BlockSpec((BF16_ROWS, w), lambda i: (jnp.maximum(i * (tr // BF16_ROWS) - 1, 0), 0))


def _next_halo_spec(tr, w, nt):
    last = nt * (tr // BF16_ROWS) - 1
    return pl.BlockSpec((BF16_ROWS, w), lambda i: (jnp.minimum((i + 1) * (tr // BF16_ROWS), last), 0))


def _silu_and_grad(u):
    s = _sigmoid(u)
    return u * s, s * (1.0 + u * (1.0 - s))


def ffn_act(z, conv_w, conv_b, name):
    t, f2 = z.shape
    f = f2 // 2
    tr = _tile(t, (128, 64, 32, 16))
    nt = t // tr
    cw = _tile(f, (512, 256, 128))

    def body(z_ref, zp_ref, w_ref, b_ref, a_ref):
        first = pl.program_id(0) == 0
        for j in range(f // cw):
            us = []
            for off in (j * cw, f + j * cw):
                cols = slice(off, off + cw)
                zc = z_ref[:, cols].astype(F32)
                hp = jnp.where(first, 0.0, zp_ref[:, cols].astype(F32))
                z1, z2 = _shift_down(zc, hp)
                us.append(w_ref[2:3, cols] * zc + w_ref[1:2, cols] * z1 + w_ref[0:1, cols] * z2 + b_ref[:, cols])
            sil, _ = _silu_and_grad(us[0])
            a_ref[:, j * cw:(j + 1) * cw] = (sil * us[1]).astype(BF16)

    return pl.pallas_call(
        body, name=name, grid=(nt,),
        in_specs=[_row_spec(tr, f2), _prev_halo_spec(tr, f2, nt), _full_spec((CONV_WIDTH, f2)), _full_spec((1, f2))],
        out_specs=_row_spec(tr, f),
        out_shape=jax.ShapeDtypeStruct((t, f), BF16),
        compiler_params=_params((_PAR,)),
    )(z, z, conv_w, conv_b)


def ffn_act_bwd(z, da, conv_w, conv_b, name):
    t, f2 = z.shape
    f = f2 // 2
    tr = _tile(t, (128, 64, 32, 16))
    nt = t // tr
    cw = _tile(f, (512, 256, 128))

    def body(z_ref, zp_ref, da_ref, w_ref, b_ref, du_ref, acc_ref):
        first = pl.program_id(0) == 0

        @pl.when(first)
        def _():
            acc_ref[...] = jnp.zeros_like(acc_ref)

        for j in range(f // cw):
            us, zs = [], []
            for off in (j * cw, f + j * cw):
                cols = slice(off, off + cw)
                zc = z_ref[:, cols].astype(F32)
                hp = jnp.where(first, 0.0, zp_ref[:, cols].astype(F32))
                z1, z2 = _shift_down(zc, hp)
                zs.append((z2, z1, zc))
                us.append(w_ref[2:3, cols] * zc + w_ref[1:2, cols] * z1 + w_ref[0:1, cols] * z2 + b_ref[:, cols])
            dav = da_ref[:, j * cw:(j + 1) * cw].astype(F32)
            sil, dsil = _silu_and_grad(us[0])
            dus = (dav * us[1] * dsil, dav * sil)
            for off, du, zsh in zip((j * cw, f + j * cw), dus, zs):
                cols = slice(off, off + cw)
                du_ref[:, cols] = du.astype(BF16)
                for k in range(CONV_WIDTH):
                    acc_ref[k:k + 1, cols] += jnp.sum(du * zsh[k], axis=0, keepdims=True)
                acc_ref[3:4, cols] += jnp.sum(du, axis=0, keepdims=True)

    return pl.pallas_call(
        body, name=name, grid=(nt,),
        in_specs=[_row_spec(tr, f2), _prev_halo_spec(tr, f2, nt), _row_spec(tr, f),
                  _full_spec((CONV_WIDTH, f2)), _full_spec((1, f2))],
        out_specs=[_row_spec(tr, f2), _full_spec((8, f2))],
        out_shape=[jax.ShapeDtypeStruct((t, f2), BF16), jax.ShapeDtypeStruct((8, f2), F32)],
        compiler_params=_params((_ARB,)),
    )(z, z, da, conv_w, conv_b)


def conv_transpose(du, conv_w, name):
    t, w = du.shape
    tr = _tile(t, (128, 64, 32, 16))
    nt = t // tr
    cw = _tile(w, (512, 256, 128))

    def body(d_ref, dn_ref, w_ref, o_ref):
        last = pl.program_id(0) == nt - 1
        for j in range(w // cw):
            cols = slice(j * cw, (j + 1) * cw)
            dc = d_ref[:, cols].astype(F32)
            hn = jnp.where(last, 0.0, dn_ref[:, cols].astype(F32))
            d1, d2 = _shift_up(dc, hn)
            o_ref[:, cols] = (w_ref[2:3, cols] * dc + w_ref[1:2, cols] * d1 + w_ref[0:1, cols] * d2).astype(BF16)

    return pl.pallas_call(
        body, name=name, grid=(nt,),
        in_specs=[_row_spec(tr, w), _next_halo_spec(tr, w, nt), _full_spec((CONV_WIDTH, w))],
        out_specs=_row_spec(tr, w),
        out_shape=jax.ShapeDtypeStruct((t, w), BF16),
        compiler_params=_params((_PAR,)),
    )(du, du, conv_w)


def sconv_fwd(proj, conv_w, name):
    t, w3 = proj.shape
    d = w3 // 3
    tr = _tile(t, (128, 64, 32, 16))
    nt = t // tr
    cw = _tile(d, (512, 256, 128))

    def body(p_ref, pp_ref, w_ref, o_ref):
        first = pl.program_id(0) == 0
        for j in range(d // cw):
            cb, cc, cx = (slice(k * d + j * cw, k * d + (j + 1) * cw) for k in range(3))
            zc = p_ref[:, cc].astype(F32) * p_ref[:, cx].astype(F32)
            hp = jnp.where(first, 0.0, pp_ref[:, cc].astype(F32) * pp_ref[:, cx].astype(F32))
            z1, z2 = _shift_down(zc, hp)
            wc = slice(j * cw, (j + 1) * cw)
            cz = w_ref[2:3, wc] * zc + w_ref[1:2, wc] * z1 + w_ref[0:1, wc] * z2
            o_ref[:, wc] = (p_ref[:, cb].astype(F32) * cz).astype(BF16)

    return pl.pallas_call(
        body, name=name, grid=(nt,),
        in_specs=[_row_spec(tr, w3), _prev_halo_spec(tr, w3, nt), _full_spec((CONV_WIDTH, d))],
        out_specs=_row_spec(tr, d),
        out_shape=jax.ShapeDtypeStruct((t, d), BF16),
        compiler_params=_params((_PAR,)),
    )(proj, proj, conv_w)


def sconv_bwd(proj, dyb, conv_w, name):
    t, w3 = proj.shape
    d = w3 // 3
    tr = _tile(t, (128, 64, 32, 16))
    nt = t // tr
    cw = _tile(d, (512, 256, 128))

    def body(p_ref, pp_ref, pn_ref, dy_ref, dyn_ref, w_ref, o_ref, acc_ref):
        first = pl.program_id(0) == 0
        last = pl.program_id(0) == nt - 1

        @pl.when(first)
        def _():
            acc_ref[...] = jnp.zeros_like(acc_ref)

        for j in range(d // cw):
            cb, cc, cx = (slice(k * d + j * cw, k * d + (j + 1) * cw) for k in range(3))
            wc = slice(j * cw, (j + 1) * cw)
            bv, cv, xv = p_ref[:, cb].astype(F32), p_ref[:, cc].astype(F32), p_ref[:, cx].astype(F32)
            zc = cv * xv
            hp = jnp.where(first, 0.0, pp_ref[:, cc].astype(F32) * pp_ref[:, cx].astype(F32))
            z1, z2 = _shift_down(zc, hp)
            w0, w1, w2 = w_ref[0:1, wc], w_ref[1:2, wc], w_ref[2:3, wc]
            cz = w2 * zc + w1 * z1 + w0 * z2
            dyv = dy_ref[:, wc].astype(F32)
            dcz = dyv * bv
            hn = jnp.where(last, 0.0, dyn_ref[:, wc].astype(F32) * pn_ref[:, cb].astype(F32))
            n1, n2 = _shift_up(dcz, hn)
            dz = w2 * dcz + w1 * n1 + w0 * n2
            o_ref[:, cb] = (dyv * cz).astype(BF16)
            o_ref[:, cc] = (dz * xv).astype(BF16)
            o_ref[:, cx] = (dz * cv).astype(BF16)
            for k, zsh in enumerate((z2, z1, zc)):
                acc_ref[k:k + 1, wc] += jnp.sum(dcz * zsh, axis=0, keepdims=True)

    return pl.pallas_call(
        body, name=name, grid=(nt,),
        in_specs=[_row_spec(tr, w3), _prev_halo_spec(tr, w3, nt), _next_halo_spec(tr, w3, nt),
                  _row_spec(tr, d), _next_halo_spec(tr, d, nt), _full_spec((CONV_WIDTH, d))],
        out_specs=[_row_spec(tr, w3), _full_spec((8, d))],
        out_shape=[jax.ShapeDtypeStruct((t, w3), BF16), jax.ShapeDtypeStruct((8, d), F32)],
        compiler_params=_params((_ARB,)),
    )(proj, proj, proj, dyb, dyb, conv_w)


def rope_tables(positions):
    half = ROT_DIM // 2
    inv_freq = ROPE_THETA ** (-jnp.arange(half, dtype=F32) / half)
    ang = positions.astype(F32)[:, None] * inv_freq[None, :]
    cos, sin = jnp.cos(ang), jnp.sin(ang)
    ones = jnp.ones((positions.shape[0], ATT_HEAD - ROT_DIM), F32)
    c64 = jnp.concatenate([cos, cos, ones], axis=1)
    s64 = jnp.concatenate([-sin, sin, 0.0 * ones], axis=1)
    perm = np.zeros((LANES, LANES), np.float32)
    for lane in range(LANES):
        dim = lane % ATT_HEAD
        if dim < half:
            perm[lane + half, lane] = 1.0
        elif dim < ROT_DIM:
            perm[lane - half, lane] = 1.0
    return jnp.tile(c64, (1, 2)), jnp.tile(s64, (1, 2)), jnp.asarray(perm, BF16)


def rope(xin, ctab, stab, perm, n_rot, sign, name):
    t, w = xin.shape
    tr = _tile(t, (256, 128, 64, 32, 16))

    def body(x_ref, c_ref, s_ref, p_ref, o_ref):
        cv, sv = c_ref[...], s_ref[...] * sign
        for j in range(n_rot // LANES):
            cols = slice(j * LANES, (j + 1) * LANES)
            xb = x_ref[:, cols]
            o_ref[:, cols] = (xb.astype(F32) * cv + _dot_nn(xb, p_ref[...]) * sv).astype(BF16)
        if n_rot < w:
            o_ref[:, n_rot:] = x_ref[:, n_rot:]

    return pl.pallas_call(
        body, name=name, grid=(t // tr,),
        in_specs=[_row_spec(tr, w), _row_spec(tr, LANES), _row_spec(tr, LANES), _full_spec((LANES, LANES))],
        out_specs=_row_spec(tr, w),
        out_shape=jax.ShapeDtypeStruct((t, w), BF16),
        compiler_params=_params((_PAR,)),
    )(xin, ctab, stab, perm)


NEG = -1e30


def _half(shape, h):
    return (lax.broadcasted_iota(jnp.int32, shape, 1) // ATT_HEAD) == h


def _dup_head(xb, kvh):
    xf = jnp.where(_half(xb.shape, kvh), xb.astype(F32), 0.0)
    return (xf + pltpu.roll(xf, ATT_HEAD, 1)).astype(BF16)


def _swa_mask(n, rows, cur_only):
    w = SWA_WINDOW
    shape = (w, w) if cur_only else (w, 2 * w)
    qi = lax.broadcasted_iota(jnp.int32, shape, 0)
    kj = lax.broadcasted_iota(jnp.int32, shape, 1) + (w if cur_only else 0)
    diff = qi + w - kj
    ok = (diff >= 0) & (diff < w)
    return ok & ((kj >= w) | (n > 0))


def swa_fwd(qkv, sinks, hq, name):
    t = qkv.shape[0]
    w = SWA_WINDOW
    nb = t // w
    hkv = hq // SWA_GROUP
    npair = hkv // 2
    qw = 2 * SWA_GROUP * ATT_HEAD
    kcol = hq * ATT_HEAD // LANES
    vcol = kcol + npair
    scale = ATT_HEAD ** -0.5

    def body(sink_ref, q_ref, kp_ref, kc_ref, vp_ref, vc_ref, o_ref, lse_ref):
        m, n = pl.program_id(0), pl.program_id(1)
        kb = jnp.concatenate([kp_ref[...], kc_ref[...]], axis=0)
        vb = jnp.concatenate([vp_ref[...], vc_ref[...]], axis=0)
        ok = _swa_mask(n, w, False)
        for kvh in range(2):
            kd, vd = _dup_head(kb, kvh), _dup_head(vb, kvh)
            for jj in range(SWA_GROUP // 2):
                jp = kvh * (SWA_GROUP // 2) + jj
                q2 = q_ref[:, jp * LANES:(jp + 1) * LANES]
                outs = []
                for a in range(2):
                    qa = jnp.where(_half(q2.shape, a), q2, jnp.zeros_like(q2))
                    s = jnp.where(ok, _dot_nt(qa, kd) * scale, NEG)
                    sink = sink_ref[m * 2 * SWA_GROUP + jp * 2 + a]
                    mx = jnp.maximum(jnp.max(s, axis=1, keepdims=True), sink)
                    e = jnp.exp(s - mx)
                    den = jnp.sum(e, axis=1, keepdims=True) + jnp.exp(sink - mx)
                    p = (e / den).astype(BF16)
                    outs.append(_dot_nn(p, vd))
                    lse_ref[jp * 2 + a] = jnp.broadcast_to(mx + jnp.log(den), (w, LANES))
                o_ref[:, jp * LANES:(jp + 1) * LANES] = jnp.where(_half(outs[0].shape, 0), outs[0], outs[1]).astype(BF16)

    prev = lambda m, n: jnp.maximum(n - 1, 0)
    grid_spec = pltpu.PrefetchScalarGridSpec(
        num_scalar_prefetch=1, grid=(npair, nb),
        in_specs=[
            pl.BlockSpec((w, qw), lambda m, n, s: (n, m)),
            pl.BlockSpec((w, LANES), lambda m, n, s: (prev(m, n), kcol + m)),
            pl.BlockSpec((w, LANES), lambda m, n, s: (n, kcol + m)),
            pl.BlockSpec((w, LANES), lambda m, n, s: (prev(m, n), vcol + m)),
            pl.BlockSpec((w, LANES), lambda m, n, s: (n, vcol + m)),
        ],
        out_specs=[
            pl.BlockSpec((w, qw), lambda m, n, s: (n, m)),
            pl.BlockSpec((2 * SWA_GROUP, w, LANES), lambda m, n, s: (m, n, 0)),
        ],
    )
    return pl.pallas_call(
        body, name=name, grid_spec=grid_spec,
        out_shape=[jax.ShapeDtypeStruct((t, hq * ATT_HEAD), BF16), jax.ShapeDtypeStruct((hq, t, LANES), F32)],
        compiler_params=_params((_PAR, _PAR)),
    )(sinks, qkv, qkv, qkv, qkv, qkv)


def swa_bwd(qkv, o, lse, do, sinks, hq, name):
    t = qkv.shape[0]
    w = SWA_WINDOW
    nb = t // w
    hkv = hq // SWA_GROUP
    npair = hkv // 2
    qw = 2 * SWA_GROUP * ATT_HEAD
    kcol = hq * ATT_HEAD // LANES
    vcol = kcol + npair
    scale = ATT_HEAD ** -0.5
    gh = 2 * SWA_GROUP

    def body(sink_ref, qc_ref, qn_ref, kp_ref, kc_ref, vp_ref, vc_ref, oc_ref, on_ref, dc_ref, dn_ref,
             lc_ref, ln_ref, dq_ref, dk_ref, dv_ref, ds_ref):
        m, n = pl.program_id(0), pl.program_id(1)
        kb = jnp.concatenate([kp_ref[...], kc_ref[...]], axis=0)
        vb = jnp.concatenate([vp_ref[...], vc_ref[...]], axis=0)
        ok_band = _swa_mask(n, w, False)
        ok_cur = _swa_mask(n, w, True)
        qi = lax.broadcasted_iota(jnp.int32, (w, w), 0)
        kj = lax.broadcasted_iota(jnp.int32, (w, w), 1)
        ok_next = (kj > qi) & (n < nb - 1)
        row16 = lax.broadcasted_iota(jnp.int32, (gh, LANES), 0)
        dsink = jnp.zeros((gh, LANES), F32)
        dk_tot = jnp.zeros((w, LANES), F32)
        dv_tot = jnp.zeros((w, LANES), F32)
        for kvh in range(2):
            kd, vd = _dup_head(kb, kvh), _dup_head(vb, kvh)
            kdc, vdc = kd[w:, :], vd[w:, :]
            acc_k = [jnp.zeros((w, LANES), F32), jnp.zeros((w, LANES), F32)]
            acc_v = [jnp.zeros((w, LANES), F32), jnp.zeros((w, LANES), F32)]
            for jj in range(SWA_GROUP // 2):
                jp = kvh * (SWA_GROUP // 2) + jj
                cols = slice(jp * LANES, (jp + 1) * LANES)
                dqs = []
                for a in range(2):
                    hd = jp * 2 + a
                    sink = sink_ref[m * gh + hd]
                    half = _half((w, LANES), a)
                    q2 = jnp.where(half, qc_ref[:, cols], jnp.zeros((w, LANES), BF16))
                    d2 = jnp.where(half, dc_ref[:, cols], jnp.zeros((w, LANES), BF16))
                    delta = jnp.sum(d2.astype(F32) * oc_ref[:, cols].astype(F32), axis=1, keepdims=True)
                    lse_c = lc_ref[hd][:, 0:1]
                    p = jnp.exp(jnp.where(ok_band, _dot_nt(q2, kd) * scale, NEG) - lse_c)
                    dsv = p * (_dot_nt(d2, vd) - delta)
                    dqs.append(_dot_nn(dsv.astype(BF16), kd) * scale)
                    psink = jnp.exp(sink - lse_c)
                    dsink = jnp.where(row16 == hd, dsink - jnp.sum(psink * delta, axis=0, keepdims=True), dsink)
                    for q_ref, d_ref, o_ref, l_ref, okm in ((qc_ref, dc_ref, oc_ref, lc_ref, ok_cur),
                                                           (qn_ref, dn_ref, on_ref, ln_ref, ok_next)):
                        q2 = jnp.where(half, q_ref[:, cols], jnp.zeros((w, LANES), BF16))
                        d2 = jnp.where(half, d_ref[:, cols], jnp.zeros((w, LANES), BF16))
                        delta = jnp.sum(d2.astype(F32) * o_ref[:, cols].astype(F32), axis=1, keepdims=True)
                        p = jnp.exp(jnp.where(okm, _dot_nt(q2, kdc) * scale, NEG) - l_ref[hd][:, 0:1])
                        dsv = p * (_dot_nt(d2, vdc) - delta)
                        acc_v[a] = acc_v[a] + _dot_tn(p.astype(BF16), d2)
                        acc_k[a] = acc_k[a] + _dot_tn(dsv.astype(BF16), q2) * scale
                dq_ref[:, cols] = jnp.where(_half((w, LANES), 0), dqs[0], dqs[1]).astype(BF16)
            dk_tot = dk_tot + acc_k[kvh] + pltpu.roll(acc_k[1 - kvh], ATT_HEAD, 1)
            dv_tot = dv_tot + acc_v[kvh] + pltpu.roll(acc_v[1 - kvh], ATT_HEAD, 1)
        dk_ref[...] = dk_tot.astype(BF16)
        dv_ref[...] = dv_tot.astype(BF16)
        ds_ref[0, 0] = dsink

    prev = lambda n: jnp.maximum(n - 1, 0)
    nxt = lambda n: jnp.minimum(n + 1, nb - 1)
    qspec = lambda f: pl.BlockSpec((w, qw), lambda m, n, s: (f(n), m))
    lspec = lambda f: pl.BlockSpec((gh, w, LANES), lambda m, n, s: (m, f(n), 0))
    same = lambda n: n
    grid_spec = pltpu.PrefetchScalarGridSpec(
        num_scalar_prefetch=1, grid=(npair, nb),
        in_specs=[
            qspec(same), qspec(nxt),
            pl.BlockSpec((w, LANES), lambda m, n, s: (prev(n), kcol + m)),
            pl.BlockSpec((w, LANES), lambda m, n, s: (n, kcol + m)),
            pl.BlockSpec((w, LANES), lambda m, n, s: (prev(n), vcol + m)),
            pl.BlockSpec((w, LANES), lambda m, n, s: (n, vcol + m)),
            qspec(same), qspec(nxt), qspec(same), qspec(nxt),
            lspec(same), lspec(nxt),
        ],
        out_specs=[
            pl.BlockSpec((w, qw), lambda m, n, s: (n, m)),
            pl.BlockSpec((w, LANES), lambda m, n, s: (n, m)),
            pl.BlockSpec((w, LANES), lambda m, n, s: (n, m)),
            pl.BlockSpec((1, 1, gh, LANES), lambda m, n, s: (m, n, 0, 0)),
        ],
    )
    return pl.pallas_call(
        body, name=name, grid_spec=grid_spec,
        out_shape=[jax.ShapeDtypeStruct((t, hq * ATT_HEAD), BF16),
                   jax.ShapeDtypeStruct((t, hkv * ATT_HEAD), BF16),
                   jax.ShapeDtypeStruct((t, hkv * ATT_HEAD), BF16),
                   jax.ShapeDtypeStruct((npair, nb, gh, LANES), F32)],
        compiler_params=_params((_PAR, _PAR)),
    )(sinks, qkv, qkv, qkv, qkv, qkv, qkv, o, o, do, do, lse, lse)


def swa_mixer_fwd(hn, w_in, w_out, sinks, positions, name):
    hq = sinks.shape[0]
    n_rot = (hq + hq // SWA_GROUP) * ATT_HEAD
    tabs = rope_tables(positions)
    proj = mm(hn, w_in, "nn", BF16, name + "_in")
    qkv = rope(proj, *tabs, n_rot, 1.0, name + "_rope")
    o, lse = swa_fwd(qkv, sinks, hq, name + "_att")
    y = mm(o, w_out, "nn", F32, name + "_out")
    return y, (qkv, o, lse)


def swa_mixer_bwd(dy, hn, w_in, w_out, sinks, positions, saved, name):
    qkv, o, lse = saved
    hq = sinks.shape[0]
    n_rot = (hq + hq // SWA_GROUP) * ATT_HEAD
    tabs = rope_tables(positions)
    dwo = wgrad(o, dy, name + "_dwo")
    do = mm(dy, w_out, "nt", BF16, name + "_do")
    dq, dk, dv, dsp = swa_bwd(qkv, o, lse, do, sinks, hq, name + "_attb")
    dproj = rope(jnp.concatenate([dq, dk, dv], axis=1), *tabs, n_rot, -1.0, name + "_ropeb")
    dwi = wgrad(hn, dproj, name + "_dwi")
    dhn = mm(dproj, w_in, "nt", BF16, name + "_dhn")
    dsinks = jnp.sum(dsp[:, :, :, 0], axis=1).reshape(hq)
    return dhn, dwi, dwo, dsinks


FOX_FPAD = 512


def _log_sigmoid(x):
    return jnp.minimum(x, 0.0) - jnp.log(1.0 + jnp.exp(-jnp.abs(x)))


def _tri(n, upper):
    r = lax.broadcasted_iota(jnp.int32, (n, n), 0)
    c = lax.broadcasted_iota(jnp.int32, (n, n), 1)
    return jnp.where((c >= r) if upper else (c <= r), 1.0, 0.0).astype(F32)


def _dot_exact(a, b):
    return jnp.dot(a, b, precision=lax.Precision.HIGHEST, preferred_element_type=F32)


def fox_cumsum(fl, b_pad, name):
    t = fl.shape[0]
    tr = _tile(t, (256, 128, 64, 32, 16, 8))

    def body(f_ref, b_ref, c_ref, carry_ref):
        @pl.when(pl.program_id(0) == 0)
        def _():
            carry_ref[...] = jnp.zeros_like(carry_ref)

        c = _dot_exact(_tri(tr, False), _log_sigmoid(f_ref[...] + b_ref[...])) + carry_ref[...]
        c_ref[...] = c
        carry_ref[...] = c[tr - 1:tr, :]

    return pl.pallas_call(
        body, name=name, grid=(t // tr,),
        in_specs=[_row_spec(tr, LANES), _full_spec((1, LANES))],
        out_specs=_row_spec(tr, LANES),
        out_shape=jax.ShapeDtypeStruct((t, LANES), F32),
        scratch_shapes=[pltpu.VMEM((1, LANES), F32)],
        compiler_params=_params((_ARB,)),
    )(fl, b_pad)


def fox_cumsum_bwd(dc, fl, b_pad, name):
    t = fl.shape[0]
    tr = _tile(t, (256, 128, 64, 32, 16, 8))
    nt = t // tr

    def body(d_ref, f_ref, b_ref, o_ref, db_ref, carry_ref):
        @pl.when(pl.program_id(0) == 0)
        def _():
            carry_ref[...] = jnp.zeros_like(carry_ref)
            db_ref[...] = jnp.zeros_like(db_ref)

        dlf = _dot_exact(_tri(tr, True), d_ref[...]) + carry_ref[...]
        carry_ref[...] = dlf[0:1, :]
        dfl = dlf * _sigmoid(-(f_ref[...] + b_ref[...]))
        o_ref[...] = dfl.astype(BF16)
        db_ref[...] += jnp.sum(dfl, axis=0, keepdims=True)

    rev = pl.BlockSpec((tr, LANES), lambda i: (nt - 1 - i, 0))
    return pl.pallas_call(
        body, name=name, grid=(nt,),
        in_specs=[rev, rev, _full_spec((1, LANES))],
        out_specs=[rev, _full_spec((1, LANES))],
        out_shape=[jax.ShapeDtypeStruct((t, LANES), BF16), jax.ShapeDtypeStruct((1, LANES), F32)],
        scratch_shapes=[pltpu.VMEM((1, LANES), F32)],
        compiler_params=_params((_ARB,)),
    )(dc, fl, b_pad)


AUG_C, AUG_ONE, AUG_LSE = ATT_HEAD, ATT_HEAD + 3, ATT_HEAD + 6


def _split3(x):
    hi = x.astype(BF16).astype(F32)
    mid = (x - hi).astype(BF16).astype(F32)
    return hi, mid, (x - hi - mid).astype(BF16).astype(F32)


def _aug(base, lane, entries):
    out = jnp.where(lane < ATT_HEAD, base, 0.0)
    for first, parts in entries:
        if parts is None:
            out = jnp.where((lane >= first) & (lane < first + 3), 1.0, out)
        else:
            for k, part in enumerate(parts):
                out = jnp.where(lane == first + k, part, out)
    return out.astype(BF16)


def _head_of_pair(x2, a):
    xf = x2.astype(F32)
    return xf if a == 0 else pltpu.roll(xf, ATT_HEAD, 1)


def fa_prep(proj, c, nh, name):
    t = proj.shape[0]
    npair = nh // 2
    tr = _tile(t, (256, 128))
    scale = ATT_HEAD ** -0.5

    def body(q_ref, k_ref, v_ref, c_ref, qa_ref, ka_ref, va_ref):
        lane = lax.broadcasted_iota(jnp.int32, (tr, LANES), 1)
        for p in range(npair):
            pc = slice(p * LANES, (p + 1) * LANES)
            for a in range(2):
                h = 2 * p + a
                hc = slice(h * LANES, (h + 1) * LANES)
                ch = c_ref[:, h:h + 1]
                qa_ref[:, hc] = _aug(_head_of_pair(q_ref[:, pc], a) * scale, lane,
                                     [(AUG_C, _split3(ch)), (AUG_ONE, None)])
                ka_ref[:, hc] = _aug(_head_of_pair(k_ref[:, pc], a), lane,
                                     [(AUG_C, None), (AUG_ONE, _split3(-ch)), (AUG_LSE, None)])
                va_ref[:, hc] = _aug(_head_of_pair(v_ref[:, pc], a), lane, [(AUG_C, None)])

    hd = nh * ATT_HEAD
    part = lambda k: pl.BlockSpec((tr, hd), lambda i: (i, k))
    out = pl.BlockSpec((tr, nh * LANES), lambda i: (i, 0))
    return pl.pallas_call(
        body, name=name, grid=(t // tr,),
        in_specs=[part(0), part(1), part(2), _row_spec(tr, LANES)],
        out_specs=[out, out, out],
        out_shape=[jax.ShapeDtypeStruct((t, nh * LANES), BF16)] * 3,
        compiler_params=_params((_PAR,)),
    )(proj, proj, proj, c)


def _fox_tiles(t, most):
    outer = _tile(t, tuple(s for s in (2048, 1024, 512, 256, 128) if s <= most))
    return outer, min(outer, 256)


def _diag_mask(outer, inner, d, transposed=False):
    r = lax.broadcasted_iota(jnp.int32, (outer, inner), 0)
    c = lax.broadcasted_iota(jnp.int32, (outer, inner), 1) + d * inner
    return (r <= c) if transposed else (c <= r)


def fa_fwd(qa, ka, va, proj, nh, name):
    t = qa.shape[0]
    hd = nh * ATT_HEAD
    npair = nh // 2
    tq, tk = _fox_tiles(t, 2048)
    nt, ratio = t // tq, tq // tk
    gcol = (3 * hd + FOX_FPAD) // LANES

    def body(q_ref, k_ref, v_ref, g_ref, o_ref, og_ref, lse_ref):
        i = pl.program_id(1)
        heads = [slice(a * LANES, (a + 1) * LANES) for a in range(2)]
        qs = [q_ref[:, cols] for cols in heads]

        def tile(j, carry, diag):
            rows = pl.ds(pl.multiple_of(j * tk, tk), tk)
            out = []
            for (mx, acc), q, cols in zip(carry, qs, heads):
                s = _dot_nt(q, k_ref[rows, cols])
                if diag is not None:
                    s = jnp.where(_diag_mask(tq, tk, diag), s, NEG)
                mnew = jnp.maximum(mx, jnp.max(s, axis=1, keepdims=True))
                p = jnp.exp(s - mnew).astype(BF16)
                out.append((mnew, jnp.exp(mx - mnew) * acc + _dot_nn(p, v_ref[rows, cols])))
            return tuple(out)

        carry = ((jnp.full((tq, 1), NEG, F32), jnp.zeros((tq, LANES), F32)),) * 2
        carry = lax.fori_loop(0, i * ratio, functools.partial(tile, diag=None), carry)
        for d in range(ratio):
            carry = tile(i * ratio + d, carry, d)
        outs = []
        for a, (mx, acc) in enumerate(carry):
            l = acc[:, AUG_C:AUG_C + 1]
            outs.append(acc / l)
            lse_ref[a] = jnp.broadcast_to(mx + jnp.log(l), (tq, LANES))
        o = jnp.where(_half((tq, LANES), 0), outs[0], pltpu.roll(outs[1], ATT_HEAD, 1))
        o_ref[...] = o.astype(BF16)
        og_ref[...] = (o * _sigmoid(g_ref[...].astype(F32))).astype(BF16)

    pair = pl.BlockSpec((tq, LANES), lambda p, i: (i, p))
    return pl.pallas_call(
        body, name=name, grid=(npair, nt),
        in_specs=[pl.BlockSpec((tq, 2 * LANES), lambda p, i: (i, p)),
                  pl.BlockSpec((t, 2 * LANES), lambda p, i: (0, p)),
                  pl.BlockSpec((t, 2 * LANES), lambda p, i: (0, p)),
                  pl.BlockSpec((tq, LANES), lambda p, i: (i, gcol + p))],
        out_specs=[pair, pair, pl.BlockSpec((2, tq, LANES), lambda p, i: (p, i, 0))],
        out_shape=[jax.ShapeDtypeStruct((t, hd), BF16), jax.ShapeDtypeStruct((t, hd), BF16),
                   jax.ShapeDtypeStruct((nh, t, LANES), F32)],
        compiler_params=_params((_PAR, _PAR)),
    )(qa, ka, va, proj)


def fa_prep_bwd(dog, o, proj, qa, lse, nh, name):
    t, hd = o.shape
    npair = nh // 2
    tr = _tile(t, (256, 128))
    gcol = (3 * hd + FOX_FPAD) // LANES

    def body(d_ref, o_ref, g_ref, q_ref, l_ref, dg_ref, qb_ref, da_ref):
        lane = lax.broadcasted_iota(jnp.int32, (tr, LANES), 1)
        dv, ov = d_ref[...].astype(F32), o_ref[...].astype(F32)
        sg = _sigmoid(g_ref[...].astype(F32))
        do = (dv * sg).astype(BF16).astype(F32)
        dg_ref[...] = (dv * ov * sg * (1.0 - sg)).astype(BF16)
        prod = do * ov
        for a in range(2):
            cols = slice(a * LANES, (a + 1) * LANES)
            delta = jnp.sum(jnp.where(_half(prod.shape, a), prod, 0.0), axis=1, keepdims=True)
            da_ref[:, cols] = _aug(_head_of_pair(do, a), lane, [(AUG_C, _split3(-delta))])
            nl = _split3(-l_ref[a][:, 0:1])
            qb = q_ref[:, cols].astype(F32)
            for k in range(3):
                qb = jnp.where(lane == AUG_LSE + k, nl[k], qb)
            qb_ref[:, cols] = qb.astype(BF16)

    pair = pl.BlockSpec((tr, LANES), lambda p, i: (i, p))
    wide = pl.BlockSpec((tr, 2 * LANES), lambda p, i: (i, p))
    return pl.pallas_call(
        body, name=name, grid=(npair, t // tr),
        in_specs=[pair, pair, pl.BlockSpec((tr, LANES), lambda p, i: (i, gcol + p)), wide,
                  pl.BlockSpec((2, tr, LANES), lambda p, i: (p, i, 0))],
        out_specs=[pair, wide, wide],
        out_shape=[jax.ShapeDtypeStruct((t, hd), BF16), jax.ShapeDtypeStruct((t, nh * LANES), BF16),
                   jax.ShapeDtypeStruct((t, nh * LANES), BF16)],
        compiler_params=_params((_PAR, _PAR)),
    )(dog, o, proj, qa, lse)


def fa_dq(qb, ka, va, da, nh, name):
    t = qb.shape[0]
    hd = nh * ATT_HEAD
    npair = nh // 2
    tq, tk = _fox_tiles(t, 1024)
    nt, ratio = t // tq, tq // tk
    scale = ATT_HEAD ** -0.5

    def body(q_ref, k_ref, v_ref, d_ref, dq_ref, rs_ref):
        i = pl.program_id(1)
        heads = [slice(a * LANES, (a + 1) * LANES) for a in range(2)]
        qs = [q_ref[:, cols] for cols in heads]
        ds = [d_ref[:, cols] for cols in heads]

        def tile(j, carry, diag):
            rows = pl.ds(pl.multiple_of(j * tk, tk), tk)
            out = []
            for acc, q, d, cols in zip(carry, qs, ds, heads):
                kj = k_ref[rows, cols]
                s = _dot_nt(q, kj)
                if diag is not None:
                    s = jnp.where(_diag_mask(tq, tk, diag), s, NEG)
                dsv = jnp.exp(s) * _dot_nt(d, v_ref[rows, cols])
                out.append(acc + _dot_nn(dsv.astype(BF16), kj))
            return tuple(out)

        accs = lax.fori_loop(0, i * ratio, functools.partial(tile, diag=None), (jnp.zeros((tq, LANES), F32),) * 2)
        for d in range(ratio):
            accs = tile(i * ratio + d, accs, d)
        dq_ref[...] = (jnp.where(_half((tq, LANES), 0), accs[0], pltpu.roll(accs[1], ATT_HEAD, 1)) * scale).astype(BF16)
        lane = lax.broadcasted_iota(jnp.int32, (tq, LANES), 1)
        rs_ref[...] = jnp.where(lane == 0, accs[0][:, AUG_C:AUG_C + 1],
                                jnp.where(lane == 1, accs[1][:, AUG_C:AUG_C + 1], 0.0))

    wide = pl.BlockSpec((tq, 2 * LANES), lambda p, i: (i, p))
    resident = pl.BlockSpec((t, 2 * LANES), lambda p, i: (0, p))
    pair = pl.BlockSpec((tq, LANES), lambda p, i: (i, p))
    return pl.pallas_call(
        body, name=name, grid=(npair, nt),
        in_specs=[wide, resident, resident, wide],
        out_specs=[pair, pair],
        out_shape=[jax.ShapeDtypeStruct((t, hd), BF16), jax.ShapeDtypeStruct((t, npair * LANES), F32)],
        compiler_params=_params((_PAR, _PAR)),
    )(qb, ka, va, da)


def fa_dkv(qb, ka, va, da, nh, name):
    t = qb.shape[0]
    hd = nh * ATT_HEAD
    npair = nh // 2
    tk, tq = _fox_tiles(t, 1024)
    nt, ratio = t // tk, tk // tq

    def body(q_ref, k_ref, v_ref, d_ref, dk_ref, dv_ref, cs_ref):
        j = pl.program_id(1)
        heads = [slice(a * LANES, (a + 1) * LANES) for a in range(2)]
        ks = [k_ref[:, cols] for cols in heads]
        vs = [v_ref[:, cols] for cols in heads]

        def tile(i, carry, diag):
            rows = pl.ds(pl.multiple_of(i * tq, tq), tq)
            out = []
            for (dk, dv), k, v, cols in zip(carry, ks, vs, heads):
                qi, di = q_ref[rows, cols], d_ref[rows, cols]
                st = _dot_nt(k, qi)
                if diag is not None:
                    st = jnp.where(_diag_mask(tk, tq, diag, True), st, NEG)
                pt = jnp.exp(st)
                dst = pt * _dot_nt(v, di)
                out.append((dk + _dot_nn(dst.astype(BF16), qi), dv + _dot_nn(pt.astype(BF16), di)))
            return tuple(out)

        zero = jnp.zeros((tk, LANES), F32)
        carry = ((zero, zero),) * 2
        for d in range(ratio):
            carry = tile(j * ratio + d, carry, d)
        carry = lax.fori_loop((j + 1) * ratio, t // tq, functools.partial(tile, diag=None), carry)
        dks, dvs = [c[0] for c in carry], [c[1] for c in carry]
        first = _half((tk, LANES), 0)
        dk_ref[...] = jnp.where(first, dks[0], pltpu.roll(dks[1], ATT_HEAD, 1)).astype(BF16)
        dv_ref[...] = jnp.where(first, dvs[0], pltpu.roll(dvs[1], ATT_HEAD, 1)).astype(BF16)
        lane = lax.broadcasted_iota(jnp.int32, (tk, LANES), 1)
        cs_ref[...] = jnp.where(lane == 0, dks[0][:, AUG_ONE:AUG_ONE + 1],
                                jnp.where(lane == 1, dks[1][:, AUG_ONE:AUG_ONE + 1], 0.0))

    wide = pl.BlockSpec((tk, 2 * LANES), lambda p, j: (j, p))
    resident = pl.BlockSpec((t, 2 * LANES), lambda p, j: (0, p))
    pair = pl.BlockSpec((tk, LANES), lambda p, j: (j, p))
    return pl.pallas_call(
        body, name=name, grid=(npair, nt),
        in_specs=[resident, wide, wide, resident],
        out_specs=[pair, pair, pair],
        out_shape=[jax.ShapeDtypeStruct((t, hd), BF16), jax.ShapeDtypeStruct((t, hd), BF16),
                   jax.ShapeDtypeStruct((t, npair * LANES), F32)],
        compiler_params=_params((_PAR, _PAR)),
    )(qb, ka, va, da)


def fox_pad_w_in(w_in, nh):
    hd = nh * ATT_HEAD
    pad = jnp.zeros((w_in.shape[0], FOX_FPAD - nh), w_in.dtype)
    return jnp.concatenate([w_in[:, :3 * hd + nh], pad, w_in[:, 3 * hd + nh:]], axis=1)


def fox_unpad_dw(dw, nh):
    hd = nh * ATT_HEAD
    return jnp.concatenate([dw[:, :3 * hd + nh], dw[:, 3 * hd + FOX_FPAD:]], axis=1)


def _pad_lanes(v):
    return jnp.pad(v.reshape(1, -1).astype(F32), ((0, 0), (0, LANES - v.size)))


def fox_mixer_fwd(hn, w_pad, w_out, b_f, name):
    nh = b_f.shape[0]
    hd = nh * ATT_HEAD
    proj = mm(hn, w_pad, "nn", BF16, name + "_in")
    fl = mm(hn, w_pad[:, 3 * hd:3 * hd + LANES], "nn", F32, name + "_fl")
    c = fox_cumsum(fl, _pad_lanes(b_f), name + "_cum")
    qa, ka, va = fa_prep(proj, c, nh, name + "_prep")
    o, og, lse = fa_fwd(qa, ka, va, proj, nh, name + "_att")
    y = mm(og, w_out, "nn", F32, name + "_out")
    return y, (proj, fl, qa, ka, va, o, og, lse)


def fox_mixer_bwd(dy, hn, w_pad, w_out, b_f, saved, name):
    proj, fl, qa, ka, va, o, og, lse = saved
    nh = b_f.shape[0]
    t = hn.shape[0]
    dwo = wgrad(og, dy, name + "_dwo")
    dog = mm(dy, w_out, "nt", BF16, name + "_dog")
    dg, qb, da = fa_prep_bwd(dog, o, proj, qa, lse, nh, name + "_prepb")
    dq, rsum = fa_dq(qb, ka, va, da, nh, name + "_dq")
    dk, dv, csum = fa_dkv(qb, ka, va, da, nh, name + "_dkv")
    dc = (rsum - csum).reshape(t, nh // 2, LANES)[:, :, :2].reshape(t, nh)
    dc = jnp.pad(dc, ((0, 0), (0, LANES - nh)))
    dfl, db = fox_cumsum_bwd(dc, fl, _pad_lanes(b_f), name + "_cumb")
    dfl = jnp.pad(dfl, ((0, 0), (0, FOX_FPAD - LANES)))
    dproj = jnp.concatenate([dq, dk, dv, dfl, dg], axis=1)
    dwi = wgrad(hn, dproj, name + "_dwi")
    dhn = mm(dproj, w_pad, "nt", BF16, name + "_dhn")
    return dhn, dwi, dwo, db[0, :nh]


HGRN_ROWS = 256
HGRN_TOGETHER = 8


def lb_table_row(lb_param, idx, name):
    nrow, w = lb_param.shape

    def body(p_ref, o_ref):
        rows = [p_ref[r:r + 1, :] for r in range(nrow)]
        mx = functools.reduce(jnp.maximum, rows)
        es = [jnp.exp(r - mx) for r in rows]
        o_ref[...] = sum(es[:idx + 1]) / sum(es)

    return pl.pallas_call(
        body, name=name, in_specs=[_full_spec((nrow, w))], out_specs=_full_spec((1, w)), grid=(1,),
        out_shape=jax.ShapeDtypeStruct((1, w), F32),
    )(lb_param)


def lb_table_row_bwd(lb_param, dlb, idx, name):
    nrow, w = lb_param.shape

    def body(p_ref, d_ref, o_ref):
        rows = [p_ref[r:r + 1, :] for r in range(nrow)]
        mx = functools.reduce(jnp.maximum, rows)
        es = [jnp.exp(r - mx) for r in rows]
        tot = sum(es)
        ps = [e / tot for e in es]
        dv = d_ref[...]
        inner = sum(ps[:idx + 1]) * dv
        for r in range(nrow):
            o_ref[r:r + 1, :] = ps[r] * ((dv if r <= idx else 0.0) - inner)

    return pl.pallas_call(
        body, name=name, in_specs=[_full_spec((nrow, w)), _full_spec((1, w))], out_specs=_full_spec((nrow, w)),
        grid=(1,), out_shape=jax.ShapeDtypeStruct((nrow, w), F32),
    )(lb_param, dlb)


def _hgrn_gates(qraw, fraw, lb):
    sq = _sigmoid(qraw)
    sf = _sigmoid(fraw)
    f = lb + (1.0 - lb) * sf
    return qraw * sq, sq, sf, f, 1.0 - f


def _hgrn_chunk(q, k, f):
    c = HGRN_CHUNK
    b = _dot_exact(_tri(c, False), jnp.log(f))
    bl = b[c - 1:c, :]
    bm = b[c // 2 - 1:c // 2, :]
    eq, ek = jnp.exp(b - bm), jnp.exp(bm - b)
    eb, el = jnp.exp(b), jnp.exp(bl - b)
    qt, kt = (q * eq).astype(BF16), (k * ek).astype(BF16)
    causal = _tri(c, False) > 0.5
    amat = jnp.where(causal, _dot_nt(qt, kt), 0.0).astype(BF16)
    return amat, qt, kt, (q * eb).astype(BF16), (k * el).astype(BF16), eq, ek, eb, el, jnp.exp(bl), causal


def hgrn_fwd(proj, lb, norm_g, nh, name):
    t = proj.shape[0]
    w = nh * HGRN_HEAD
    c = HGRN_CHUNK
    rows = _tile(t, (HGRN_ROWS, 128, 64, 32))
    nr, nc = t // rows, rows // c
    hp = _tile(nh, (HGRN_TOGETHER, 2, 1))
    wide = hp * HGRN_HEAD

    def body(q_ref, f_ref, i_ref, g_ref, lb_ref, ng_ref, og_ref, o_ref, st_ref, state):
        @pl.when(pl.program_id(1) == 0)
        def _():
            state[...] = jnp.zeros_like(state)

        def step(cc, carry):
            sl = pl.ds(pl.multiple_of(cc * c, c), c)
            for a in range(hp):
                hc = slice(a * HGRN_HEAD, (a + 1) * HGRN_HEAD)
                q, _, _, f, k = _hgrn_gates(q_ref[sl, hc].astype(F32), f_ref[sl, hc].astype(F32), lb_ref[:, hc])
                v = i_ref[sl, hc]
                amat, _, _, qd, kd, _, _, _, _, ebl, _ = _hgrn_chunk(q, k, f)
                st = state[a]
                st_ref[a, cc] = st.astype(BF16)
                o = _dot_nt(qd, st.astype(BF16)) + _dot_nn(amat, v)
                state[a] = st * ebl + _dot_tn(v, kd)
                o_ref[sl, hc] = o
                graw = g_ref[sl, hc].astype(F32)
                og_ref[sl, hc] = (o * _rstd(o) * ng_ref[...] * (graw * _sigmoid(graw))).astype(BF16)
            return carry

        lax.fori_loop(0, nc, step, 0)

    ng = nh // hp
    col = lambda off: pl.BlockSpec((rows, wide), lambda h, r: (r, off + h))
    return pl.pallas_call(
        body, name=name, grid=(ng, nr),
        in_specs=[col(0), col(ng), col(2 * ng), col(3 * ng),
                  pl.BlockSpec((1, wide), lambda h, r: (0, h)), _full_spec((1, HGRN_HEAD))],
        out_specs=[col(0), col(0), pl.BlockSpec((hp, nc, HGRN_HEAD, HGRN_HEAD), lambda h, r: (h, r, 0, 0))],
        out_shape=[jax.ShapeDtypeStruct((t, w), BF16), jax.ShapeDtypeStruct((t, w), F32),
                   jax.ShapeDtypeStruct((nh, t // c, HGRN_HEAD, HGRN_HEAD), BF16)],
        scratch_shapes=[pltpu.VMEM((hp, HGRN_HEAD, HGRN_HEAD), F32)],
        compiler_params=_params((_PAR, _ARB)),
    )(proj, proj, proj, proj, lb, norm_g)


def hgrn_bwd(proj, lb, norm_g, o, states, dog, nh, name):
    t = proj.shape[0]
    w = nh * HGRN_HEAD
    c = HGRN_CHUNK
    rows = _tile(t, (HGRN_ROWS, 128, 64, 32))
    nr, nc = t // rows, rows // c
    hp = _tile(nh, (HGRN_TOGETHER, 2, 1))
    wide = hp * HGRN_HEAD

    def body(q_ref, f_ref, i_ref, g_ref, lb_ref, ng_ref, o_ref, st_ref, dog_ref,
             dq_ref, df_ref, di_ref, dg_ref, dlb_ref, dng_ref, dstate):
        @pl.when(pl.program_id(1) == 0)
        def _():
            dstate[...] = jnp.zeros_like(dstate)
            dlb_ref[...] = jnp.zeros_like(dlb_ref)
            dng_ref[...] = jnp.zeros_like(dng_ref)

        ng = ng_ref[...]

        def step(idx, carry):
            cc = nc - 1 - idx
            sl = pl.ds(pl.multiple_of(cc * c, c), c)
            for a in range(hp):
                hc = slice(a * HGRN_HEAD, (a + 1) * HGRN_HEAD)
                lb = lb_ref[:, hc]
                qraw, fraw = q_ref[sl, hc].astype(F32), f_ref[sl, hc].astype(F32)
                q, sq, sf, f, k = _hgrn_gates(qraw, fraw, lb)
                v = i_ref[sl, hc]
                amat, qt, kt, qd, kd, eq, ek, eb, el, ebl, causal = _hgrn_chunk(q, k, f)
                ov = o_ref[sl, hc]
                graw = g_ref[sl, hc].astype(F32)
                dogv = dog_ref[sl, hc].astype(F32)
                sil, dsil = _silu_and_grad(graw)
                r = _rstd(ov)
                oh = ov * r
                don = dogv * sil
                dg_ref[sl, hc] = (dogv * oh * ng * dsil).astype(BF16)
                dng_ref[a] += jnp.sum(don * oh, axis=0, keepdims=True)
                doh = don * ng
                do = (r * (doh - oh * jnp.mean(doh * oh, axis=1, keepdims=True))).astype(BF16)
                dst = dstate[a]
                dstb = dst.astype(BF16)
                da = jnp.where(causal, _dot_nt(do, v), 0.0).astype(BF16)
                dv = _dot_tn(amat, do) + _dot_nt(kd, dstb)
                st0 = st_ref[a, cc]
                dq = _dot_nn(da, kt) * eq + _dot_nn(do, st0) * eb
                dk_inter = _dot_nn(v, dstb) * el
                dk = _dot_tn(da, qt) * ek + dk_inter
                dstate[a] = dst * ebl + _dot_tn(do, qd)
                through = jnp.sum(dst * st0.astype(F32), axis=0, keepdims=True) * ebl
                later = jnp.sum(k * dk_inter, axis=0, keepdims=True) + through
                dlf = _dot_exact(_tri(c, True), q * dq - k * dk) + later
                df = dlf / f - dk
                dq_ref[sl, hc] = (dq * (sq * (1.0 + qraw * (1.0 - sq)))).astype(BF16)
                df_ref[sl, hc] = (df * (1.0 - lb) * sf * (1.0 - sf)).astype(BF16)
                di_ref[sl, hc] = dv.astype(BF16)
                dlb_ref[:, hc] += jnp.sum(df * (1.0 - sf), axis=0, keepdims=True)
            return carry

        lax.fori_loop(0, nc, step, 0)

    ngr = nh // hp
    col = lambda off: pl.BlockSpec((rows, wide), lambda h, r: (nr - 1 - r, off + h))
    out = col(0)
    return pl.pallas_call(
        body, name=name, grid=(ngr, nr),
        in_specs=[col(0), col(ngr), col(2 * ngr), col(3 * ngr),
                  pl.BlockSpec((1, wide), lambda h, r: (0, h)), _full_spec((1, HGRN_HEAD)),
                  out, pl.BlockSpec((hp, nc, HGRN_HEAD, HGRN_HEAD), lambda h, r: (h, nr - 1 - r, 0, 0)), out],
        out_specs=[out, out, out, out, pl.BlockSpec((1, wide), lambda h, r: (0, h)),
                   pl.BlockSpec((hp, 1, HGRN_HEAD), lambda h, r: (h, 0, 0))],
        out_shape=[jax.ShapeDtypeStruct((t, w), BF16)] * 4 + [jax.ShapeDtypeStruct((1, w), F32),
                                                             jax.ShapeDtypeStruct((nh, 1, HGRN_HEAD), F32)],
        scratch_shapes=[pltpu.VMEM((hp, HGRN_HEAD, HGRN_HEAD), F32)],
        compiler_params=_params((_PAR, _ARB)),
    )(proj, proj, proj, proj, lb, norm_g, o, states, dog)


def hgrn_mixer_fwd(hn, w_in, w_out, norm_g, lb_param, idx, name):
    nh = w_out.shape[0] // HGRN_HEAD
    lb = lb_table_row(lb_param, idx, name + "_lb")
    proj = mm(hn, w_in, "nn", BF16, name + "_in")
    og, o, states = hgrn_fwd(proj, lb, norm_g, nh, name + "_rec")
    y = mm(og, w_out, "nn", F32, name + "_out")
    return y, (proj, lb, og, o, states)


def hgrn_mixer_bwd(dy, hn, w_in, w_out, norm_g, lb_param, idx, saved, name):
    proj, lb, og, o, states = saved
    nh = w_out.shape[0] // HGRN_HEAD
    dwo = wgrad(og, dy, name + "_dwo")
    dog = mm(dy, w_out, "nt", BF16, name + "_dog")
    dq, df, di, dg, dlb, dng = hgrn_bwd(proj, lb, norm_g, o, states, dog, nh, name + "_recb")
    dproj = jnp.concatenate([dq, df, di, dg], axis=1)
    dwi = wgrad(hn, dproj, name + "_dwi")
    dhn = mm(dproj, w_in, "nt", BF16, name + "_dhn")
    dlbp = lb_table_row_bwd(lb_param, dlb, idx, name + "_lbb")
    return dhn, dwi, dwo, jnp.sum(dng, axis=0), dlbp


_ANY = pl.BlockSpec(memory_space=pl.ANY)
_MESH = pl.DeviceIdType.MESH


def _place():
    x, y, c = lax.axis_index("x"), lax.axis_index("y"), lax.axis_index("c")
    chips = [(1 - x, y), (x, 1 - y), (1 - x, 1 - y)]
    return x, y, c, N_CHIPS // 2 * x + y, chips


def _chip_index(chip):
    return N_CHIPS // 2 * chip[0] + chip[1]


def _window(ref, axis, start, size):
    idx = [slice(None)] * len(ref.shape)
    idx[axis] = pl.ds(start, size)
    return ref.at[tuple(idx)]


_HBM = pl.BlockSpec(memory_space=pltpu.HBM)
_SEMS = pl.BlockSpec(memory_space=pltpu.SEMAPHORE)
_DATAFLOW = pltpu.SideEffectType.DATAFLOW_SIDE_EFFECTING


def _exchange_copy(src, land, axis, sems, k, j, chip, c, sender_side):
    x, y, _, me, _ = _place()
    peer = _chip_index(chip)
    if axis is None:
        src_part = src.at[peer]
        land_part = land.at[me if sender_side else peer]
    else:
        size = src.shape[axis]
        src_part = src
        land_part = _window(land, axis, (me if sender_side else peer) * size, size)
    which = k * (N_CHIPS - 1) + j
    return pltpu.make_async_remote_copy(src_ref=src_part, dst_ref=land_part, send_sem=sems[0].at[which],
                                        recv_sem=sems[1].at[which], device_id=(chip[0], chip[1], c),
                                        device_id_type=_MESH)


def place_own(shard, axis):
    _, _, _, me, _ = _place()
    shape = list(shard.shape)
    shape[axis] *= N_CHIPS
    return lax.dynamic_update_slice_in_dim(lax.empty(tuple(shape), shard.dtype), shard, me * shard.shape[axis], axis)


def cast_and_place(shards, layer, axis, name):
    _, rows, cols = shards.shape
    tr = _row_tile(rows, cols, 1 << 19)
    nrb = rows // tr
    _, _, _, me, _ = _place()
    full = (rows * N_CHIPS, cols) if axis == 0 else (rows, cols * N_CHIPS)

    def body(me_ref, s_ref, b_ref, land_ref):
        val = s_ref[...].astype(BF16)
        b_ref[...] = val
        land_ref[...] = val

    window = (lambda i, m: (m[0] * nrb + i, 0)) if axis == 0 else (lambda i, m: (i, m[0]))
    grid_spec = pltpu.PrefetchScalarGridSpec(
        num_scalar_prefetch=1, grid=(nrb,),
        in_specs=[pl.BlockSpec((None, tr, cols), lambda i, m: (layer, i, 0))],
        out_specs=[pl.BlockSpec((tr, cols), lambda i, m: (i, 0)), pl.BlockSpec((tr, cols), window)])
    return pl.pallas_call(
        body, name=name, grid_spec=grid_spec,
        out_shape=[jax.ShapeDtypeStruct((rows, cols), BF16), jax.ShapeDtypeStruct(full, BF16)],
        compiler_params=_params((_PAR,)),
    )(jnp.reshape(me, (1,)).astype(jnp.int32), shards)


def place_own_slot(p):
    _, _, _, me, _ = _place()
    mine = lax.dynamic_index_in_dim(p, me, 0, keepdims=True)
    return lax.dynamic_update_slice_in_dim(lax.empty(p.shape, p.dtype), mine, me, 0)


def exchange_start(srcs, lands, axes, name):
    n = len(srcs)

    def body(*refs):
        ins, lnd, sems = refs[:n], refs[n:2 * n], refs[2 * n:2 * n + 2]
        _, _, c, _, chips = _place()
        for k in range(n):
            for j, chip in enumerate(chips):
                _exchange_copy(ins[k], lnd[k], axes[k], sems, k, j, chip, c, True).start()
        refs[-1][...] = jnp.zeros_like(refs[-1])

    sem = pltpu.SemaphoreType.DMA((n * (N_CHIPS - 1),))
    arrays = list(srcs) + list(lands)
    out = pl.pallas_call(
        body, name=name,
        in_specs=[_HBM] * (2 * n),
        out_specs=(_SEMS, _SEMS) + (_HBM,) * (2 * n) + (pl.BlockSpec(memory_space=pltpu.VMEM),),
        out_shape=(sem, sem) + tuple(pltpu.HBM(a.shape, a.dtype) for a in arrays)
        + (jax.ShapeDtypeStruct((8, LANES), F32),),
        input_output_aliases={i: 2 + i for i in range(2 * n)},
        compiler_params=pltpu.CompilerParams(has_side_effects=_DATAFLOW),
    )(*[pltpu.with_memory_space_constraint(a, pltpu.HBM) for a in arrays])
    return (out[0], out[1], list(out[2:2 + n]), list(out[2 + n:2 + 2 * n]), list(axes)), out[-1]


def exchange_wait(started, ks, after, name):
    send, recv, srcs, lands, axes = started
    m = len(ks)
    after = list(after) if isinstance(after, (list, tuple)) else [after]

    def body(*refs):
        ins, lnd, sems = refs[:m], refs[m:2 * m], refs[2 * m:2 * m + 2]
        _, _, c, _, chips = _place()
        for q, k in enumerate(ks):
            for j, chip in enumerate(chips):
                cp = _exchange_copy(ins[q], lnd[q], axes[k], sems, k, j, chip, c, False)
                cp.wait_send()
                cp.wait_recv()

    arrays = [srcs[k] for k in ks] + [lands[k] for k in ks]
    out = pl.pallas_call(
        body, name=name,
        in_specs=[_HBM] * (2 * m) + [_SEMS, _SEMS] + [_ANY] * len(after),
        out_specs=(_HBM,) * (2 * m),
        out_shape=tuple(pltpu.HBM(a.shape, a.dtype) for a in arrays),
        input_output_aliases={i: i for i in range(2 * m)},
        compiler_params=pltpu.CompilerParams(has_side_effects=_DATAFLOW),
    )(*arrays, send, recv, *after)
    return list(out[m:])


def allreduce_small(buf, name):
    rows = buf.shape[0]
    n_dev = 2 * N_CHIPS

    def body(in_ref, out_ref, slots, send, recv):
        x, y, c, me, chips = _place()
        my_id = 2 * me + c
        slots[my_id] = in_ref[...]
        for j in range(1, n_dev):
            fx, fy, fc = (j >> 2) & 1, (j >> 1) & 1, j & 1
            peer = ((1 - x) if fx else x, (1 - y) if fy else y, (1 - c) if fc else c)
            pltpu.make_async_remote_copy(
                src_ref=in_ref, dst_ref=slots.at[my_id], send_sem=send.at[j], recv_sem=recv.at[j],
                device_id=peer, device_id_type=_MESH).start()
        for j in range(1, n_dev):
            fx, fy, fc = (j >> 2) & 1, (j >> 1) & 1, j & 1
            peer = ((1 - x) if fx else x, (1 - y) if fy else y, (1 - c) if fc else c)
            peer_id = 2 * _chip_index(peer) + peer[2]
            landed = pltpu.make_async_remote_copy(
                src_ref=in_ref, dst_ref=slots.at[peer_id], send_sem=send.at[j], recv_sem=recv.at[j],
                device_id=peer, device_id_type=_MESH)
            landed.wait_recv()
            landed.wait_send()
        tot = slots[0]
        for d in range(1, n_dev):
            tot = tot + slots[d]
        out_ref[...] = tot

    return pl.pallas_call(
        body, name=name,
        in_specs=[pl.BlockSpec(memory_space=pltpu.VMEM)], out_specs=pl.BlockSpec(memory_space=pltpu.VMEM),
        out_shape=jax.ShapeDtypeStruct(buf.shape, F32),
        scratch_shapes=[pltpu.VMEM((n_dev, rows, LANES), F32), pltpu.SemaphoreType.DMA((n_dev,)),
                        pltpu.SemaphoreType.DMA((n_dev,))],
        compiler_params=pltpu.CompilerParams(vmem_limit_bytes=VMEM_LIMIT),
    )(buf)


def _sibling_step(src_ref, slots, send, recv, credit, step, n_steps):
    x, y, c = lax.axis_index("x"), lax.axis_index("y"), lax.axis_index("c")
    slot = step % 2

    @pl.when(step >= 2)
    def _():
        pl.semaphore_wait(credit, 1)

    cp = pltpu.make_async_remote_copy(src_ref=src_ref, dst_ref=slots.at[slot], send_sem=send.at[slot],
                                      recv_sem=recv.at[slot], device_id=(x, y, 1 - c), device_id_type=_MESH)
    cp.start()
    cp.wait_recv()
    return cp, slot


def _sibling_done(cp, credit, step, n_steps):
    x, y, c = lax.axis_index("x"), lax.axis_index("y"), lax.axis_index("c")
    cp.wait_send()

    @pl.when(step < n_steps - 2)
    def _():
        pl.semaphore_signal(credit, 1, device_id=(x, y, 1 - c), device_id_type=_MESH)


def pair_sum(g, name):
    by_cols = g.ndim == 2
    s = N_CHIPS if by_cols else g.shape[0]
    r = g.shape[-2]
    cols = g.shape[-1] // s if by_cols else g.shape[-1]
    half = r // 2
    tr = _row_tile(half, cols, 1 << 20)
    nt = half // tr
    n_steps = s * nt

    def body(g_ref, o_ref, slots, send, recv, credit):
        c = lax.axis_index("c")
        step = pl.program_id(0) * nt + pl.program_id(1)
        cp, slot = _sibling_step(g_ref.at[0, 1 - c], slots, send, recv, credit, step, n_steps)
        o_ref[0] = (g_ref[0, c].astype(F32) + slots[slot].astype(F32)).astype(BF16)
        _sibling_done(cp, credit, step, n_steps)

    if by_cols:
        in_spec = pl.BlockSpec((1, 2, tr, cols), lambda k, i: (0, 0, i, k))
        g4 = g.reshape(1, 2, half, s * cols)
    else:
        in_spec = pl.BlockSpec((1, 2, tr, cols), lambda k, i: (k, 0, i, 0))
        g4 = g.reshape(s, 2, half, cols)
    return pl.pallas_call(
        body, name=name, grid=(s, nt),
        in_specs=[in_spec],
        out_specs=pl.BlockSpec((1, tr, cols), lambda k, i: (k, i, 0)),
        out_shape=jax.ShapeDtypeStruct((s, half, cols), BF16),
        scratch_shapes=[pltpu.VMEM((2, tr, cols), BF16), pltpu.SemaphoreType.DMA((2,)), pltpu.SemaphoreType.DMA((2,)),
                        pltpu.SemaphoreType.REGULAR],
        compiler_params=_params((_ARB, _ARB)),
    )(g4)


def chip_sum_share(q, acc, layer, name, after):
    s, r2, cols = q.shape
    tr = _row_tile(r2, cols, 1 << 19)
    nt = r2 // tr

    def body(q_ref, acc_ref, after_ref, o_ref, slots, send, recv, credit):
        c = lax.axis_index("c")
        step = pl.program_id(0)
        tot = q_ref[0].astype(F32)
        for k in range(1, s):
            tot = tot + q_ref[k].astype(F32)
        o_ref[0, c] = tot
        cp, slot = _sibling_step(o_ref.at[0, c], slots, send, recv, credit, step, nt)
        o_ref[0, 1 - c] = slots[slot]
        _sibling_done(cp, credit, step, nt)

    return pl.pallas_call(
        body, name=name, grid=(nt,),
        in_specs=[pl.BlockSpec((s, tr, cols), lambda i: (0, i, 0)), _ANY, _ANY],
        out_specs=pl.BlockSpec((1, 2, tr, cols), lambda i: (layer, 0, i, 0)),
        out_shape=jax.ShapeDtypeStruct(acc.shape, F32),
        input_output_aliases={1: 0},
        scratch_shapes=[pltpu.VMEM((2, tr, cols), F32), pltpu.SemaphoreType.DMA((2,)), pltpu.SemaphoreType.DMA((2,)),
                        pltpu.SemaphoreType.REGULAR],
        compiler_params=_params((_ARB,)),
    )(q, acc, after)


def _row_tile(rows, cols, budget):
    for tr in (1024, 512, 256, 128, 64, 32, 16, 8):
        if rows % tr == 0 and tr * cols <= budget:
            return tr
    return rows


def adamw(w, g, m, v, name, after):
    rows, cols = w.shape
    tr = _row_tile(rows, cols, 1 << 18)
    c1 = 1.0 - ADAM_B1 ** ADAM_STEP
    c2 = 1.0 - ADAM_B2 ** ADAM_STEP

    def body(w_ref, g_ref, m_ref, v_ref, after_ref, d_ref, nm_ref, nv_ref):
        gv = g_ref[...]
        nm = ADAM_B1 * m_ref[...] + (1.0 - ADAM_B1) * gv
        nv = ADAM_B2 * v_ref[...] + (1.0 - ADAM_B2) * (gv * gv)
        nm_ref[...] = nm
        nv_ref[...] = nv
        d_ref[...] = -ADAM_LR * ((nm / c1) / (jnp.sqrt(nv / c2) + ADAM_EPS) + ADAM_WD * w_ref[...])

    spec = pl.BlockSpec((tr, cols), lambda i: (i, 0))
    return pl.pallas_call(
        body, name=name, grid=(rows // tr,),
        in_specs=[spec] * 4 + [_ANY], out_specs=[spec] * 3,
        out_shape=[jax.ShapeDtypeStruct((rows, cols), F32)] * 3,
        compiler_params=_params((_PAR,)),
    )(w, g, m, v, after)


WEIGHTS = ("mix_pre_g", "mix_post_g", "ffn_pre_g", "ffn_post_g", "hgrn_w_in", "hgrn_w_out", "hgrn_norm_g",
           "hgrn_lb_param", "swa_w_in", "swa_w_out", "swa_sinks", "sc_w_in", "sc_conv_w", "sc_w_out", "fox_w_in",
           "fox_b_f", "fox_w_out", "ffn_w_up", "ffn_conv_w", "ffn_conv_b", "ffn_w_down")
N_MIXERS = 4


def _pack_small(parts):
    flat = jnp.concatenate([p.reshape(-1).astype(F32) for p in parts])
    rows = -(-flat.shape[0] // (8 * LANES)) * 8
    return jnp.pad(flat, (0, rows * LANES - flat.shape[0])).reshape(rows, LANES)


def _unpack_small(buf, shapes):
    flat, out, off = buf.reshape(-1), [], 0
    for s in shapes:
        n = math.prod(s)
        out.append(flat[off:off + n].reshape(s))
        off += n
    return out


def _stack_rows(dw):
    return dw.reshape(N_CHIPS, dw.shape[0] // N_CHIPS, dw.shape[1])


def kernel(x, positions, mix_pre_g, mix_post_g, ffn_pre_g, ffn_post_g, hgrn_w_in, hgrn_w_out, hgrn_norm_g, hgrn_lb_param, swa_w_in, swa_w_out, swa_sinks, sc_w_in, sc_conv_w, sc_w_out, fox_w_in, fox_b_f, fox_w_out, ffn_w_up, ffn_conv_w, ffn_conv_b, ffn_w_down, loss_target, m_mix_pre_g, m_mix_post_g, m_ffn_pre_g, m_ffn_post_g, m_hgrn_w_in, m_hgrn_w_out, m_hgrn_norm_g, m_hgrn_lb_param, m_swa_w_in, m_swa_w_out, m_swa_sinks, m_sc_w_in, m_sc_conv_w, m_sc_w_out, m_fox_w_in, m_fox_b_f, m_fox_w_out, m_ffn_w_up, m_ffn_conv_w, m_ffn_conv_b, m_ffn_w_down, v_mix_pre_g, v_mix_post_g, v_ffn_pre_g, v_ffn_post_g, v_hgrn_w_in, v_hgrn_w_out, v_hgrn_norm_g, v_hgrn_lb_param, v_swa_w_in, v_swa_w_out, v_swa_sinks, v_sc_w_in, v_sc_conv_w, v_sc_w_out, v_fox_w_in, v_fox_b_f, v_fox_w_out, v_ffn_w_up, v_ffn_conv_w, v_ffn_conv_b, v_ffn_w_down):
    given = dict(locals())
    depth = mix_pre_g.shape[0]
    assert depth == N_MIXERS and x.shape[0] == 1, "one batch element per device, one layer of each mixer"
    xi, target = x[0], loss_target[0]
    chip = N_CHIPS // 2 * lax.axis_index("x") + lax.axis_index("y")
    nh_fox = fox_b_f.shape[1]
    row = lambda a, i: a[i:i + 1]

    units = {"hg_in": (hgrn_w_in, 0, 1), "hg_out": (hgrn_w_out, 0, 0), "sw_in": (swa_w_in, 0, 1),
             "sw_out": (swa_w_out, 0, 0), "sc_in": (sc_w_in, 0, 1), "sc_out": (sc_w_out, 0, 0),
             "fx_in": (fox_w_in, 0, 0), "fx_out": (fox_w_out, 0, 0)}
    mix_units = (("hg_in", "hg_out"), ("sw_in", "sw_out"), ("sc_in", "sc_out", "sc_cw"), ("fx_in", "fx_out"))
    ffn_units = []
    for i in range(depth):
        units[f"up{i}"], units[f"down{i}"] = (ffn_w_up, i, 1), (ffn_w_down, i, 0)
        ffn_units.append((f"up{i}", f"down{i}") + (("f_cw",) if i == 0 else ()))
    order = [n for i in range(depth) for n in mix_units[i] + ffn_units[i]]
    placed = {n: cast_and_place(*units[n], "place_" + n) + (units[n][2],) for n in units}
    placed["f_cw"] = (ffn_conv_w, place_own(ffn_conv_w, 2), 2)
    placed["sc_cw"] = (sc_conv_w[0], place_own(sc_conv_w[0], 1), 1)
    gather, _ = exchange_start([placed[n][0] for n in order], [placed[n][1] for n in order],
                               [placed[n][2] for n in order], "gather_start")
    wt = {}

    def arrive(names, after, name):
        wt.update(zip(names, exchange_wait(gather, [order.index(n) for n in names], after, name)))

    saved = []
    xs = xi
    hn = rms_fwd(xs, row(mix_pre_g, 0), "pre_norm0")
    dx = loss = None
    for i in range(depth):
        nm = f"l{i}"
        arrive(mix_units[i], hn, nm + "_w_mix")
        if i == 0:
            y, sv = hgrn_mixer_fwd(hn, wt["hg_in"], wt["hg_out"], hgrn_norm_g, hgrn_lb_param, i, nm + "_hgrn")
        elif i == 1:
            y, sv = swa_mixer_fwd(hn, wt["sw_in"], wt["sw_out"], swa_sinks[0], positions, nm + "_swa")
        elif i == 2:
            proj = mm(hn, wt["sc_in"], "nn", BF16, nm + "_sc_in")
            yb = sconv_fwd(proj, wt["sc_cw"], nm + "_sc_conv")
            y, sv = mm(yb, wt["sc_out"], "nn", F32, nm + "_sc_out"), (proj, yb)
        else:
            fx4 = wt["fx_in"].reshape(N_CHIPS, -1, wt["fx_in"].shape[1])
            wt["fx_pad"] = fox_pad_w_in(jnp.concatenate([fx4[s] for s in range(N_CHIPS)], axis=1), nh_fox)
            y, sv = fox_mixer_fwd(hn, wt["fx_pad"], wt["fx_out"], fox_b_f[0], nm + "_fox")
        x1, hn2 = resid_norm(xs, y, row(mix_post_g, i), row(ffn_pre_g, i), nm + "_mix_resid")
        arrive(ffn_units[i], hn2, nm + "_w_ffn")
        z = mm(hn2, wt[f"up{i}"], "nn", BF16, nm + "_ffn_up")
        a = ffn_act(z, wt["f_cw"][i], row(ffn_conv_b, i), nm + "_ffn_act")
        y2 = mm(a, wt[f"down{i}"], "nn", F32, nm + "_ffn_down")
        saved.append((xs, hn, y, sv, x1, hn2, z, a, y2))
        if i < depth - 1:
            xs, hn = resid_norm(x1, y2, row(ffn_post_g, i), row(mix_pre_g, i + 1), nm + "_ffn_resid")
        else:
            dx, loss = resid_loss(x1, y2, row(ffn_post_g, i), target, nm + "_loss")

    grads = {}

    def start_reduce(tag, named):
        ps = [pair_sum(g, f"{tag}_pair_{n}") for n, _, g in named]
        started, token = exchange_start(ps, [place_own_slot(p) for p in ps], [None] * len(ps), tag + "_chips_start")
        return (named, started), token

    def finish_reduce(tag, pending, after):
        named, started = pending
        qs = exchange_wait(started, list(range(len(named))), after, tag + "_chips_wait")
        for (n, l, _), q in zip(named, qs):
            if n not in grads:
                grads[n] = lax.empty((given[n].shape[0], 2) + q.shape[1:], F32)
            grads[n] = chip_sum_share(q, grads[n], l, f"{tag}_share_{n}", after[0])

    def update(n, after):
        w = given[n]
        flat = lambda a: a.reshape(-1, w.shape[-1])
        dl, nm_, nv_ = adamw(flat(w), flat(grads[n]), flat(given["m_" + n]), flat(given["v_" + n]), "adamw_" + n,
                             after)
        deltas[n], new_m[n], new_v[n] = dl.reshape(w.shape), nm_.reshape(w.shape), nv_.reshape(w.shape)

    deltas, new_m, new_v = {}, {}, {}

    d_pre, d_post, d_fpre, d_fpost = [None] * depth, [None] * depth, [None] * depth, [None] * depth
    d_fcw, d_fcb = [None] * depth, [None] * depth
    small = {}
    pending = token = None
    for i in reversed(range(depth)):
        nm = f"l{i}b"
        xs, hn, y, sv, x1, hn2, z, a, y2 = saved[i]
        f_cw = wt["f_cw"][i]
        dy2, d_fpost[i] = norm_bwd(y2, row(ffn_post_g, i), dx, None, BF16, nm + "_ffn_post", after=token)
        d_down = _stack_rows(wgrad(a, dy2, nm + "_dw_down"))
        da = mm(dy2, wt[f"down{i}"], "nt", BF16, nm + "_da")
        du, acc = ffn_act_bwd(z, da, f_cw, row(ffn_conv_b, i), nm + "_ffn_actb")
        d_fcw[i], d_fcb[i] = acc[0:CONV_WIDTH], acc[CONV_WIDTH]
        dz = conv_transpose(du, f_cw, nm + "_ffn_convT")
        d_up = wgrad(hn2, dz, nm + "_dw_up")
        dhn2 = mm(dz, wt[f"up{i}"], "nt", BF16, nm + "_dhn2")
        dx1, d_fpre[i] = norm_bwd(x1, row(ffn_pre_g, i), dhn2, dx, F32, nm + "_ffn_pre")
        dy, d_post[i] = norm_bwd(y, row(mix_post_g, i), dx1, None, BF16, nm + "_mix_post")
        if i == 0:
            dhn, dwi, dwo, small["hgrn_norm_g"], small["hgrn_lb_param"] = hgrn_mixer_bwd(
                dy, hn, wt["hg_in"], wt["hg_out"], hgrn_norm_g, hgrn_lb_param, i, sv, nm + "_hgrn")
            w_in, w_out = "hgrn_w_in", "hgrn_w_out"
        elif i == 1:
            dhn, dwi, dwo, small["swa_sinks"] = swa_mixer_bwd(dy, hn, wt["sw_in"], wt["sw_out"], swa_sinks[0],
                                                             positions, sv, nm + "_swa")
            w_in, w_out = "swa_w_in", "swa_w_out"
        elif i == 2:
            proj, yb = sv
            dwo = wgrad(yb, dy, nm + "_sc_dwo")
            dyb = mm(dy, wt["sc_out"], "nt", BF16, nm + "_sc_dyb")
            dproj, acc = sconv_bwd(proj, dyb, wt["sc_cw"], nm + "_sc_convb")
            dwi = wgrad(hn, dproj, nm + "_sc_dwi")
            dhn = mm(dproj, wt["sc_in"], "nt", BF16, nm + "_sc_dhn")
            small["sc_conv_w"] = acc[0:CONV_WIDTH]
            w_in, w_out = "sc_w_in", "sc_w_out"
        else:
            dhn, dwi, dwo, small["fox_b_f"] = fox_mixer_bwd(dy, hn, wt["fx_pad"], wt["fx_out"], fox_b_f[0], sv,
                                                            nm + "_fox")
            dwi = fox_unpad_dw(dwi, nh_fox)
            cols = dwi.shape[1] // N_CHIPS
            dwi = jnp.stack([dwi[:, s * cols:(s + 1) * cols] for s in range(N_CHIPS)])
            w_in, w_out = "fox_w_in", "fox_w_out"
        dx, d_pre[i] = norm_bwd(xs, row(mix_pre_g, i), dhn, dx1, F32, nm + "_mix_pre")
        earlier = pending
        pending, token = start_reduce(nm, [(w_in, 0, dwi), (w_out, 0, _stack_rows(dwo)), ("ffn_w_up", i, d_up),
                                           ("ffn_w_down", i, d_down)])
        if earlier is not None:
            finish_reduce(f"l{i + 1}b", earlier, [token, dx])
    small.update(mix_pre_g=jnp.concatenate(d_pre), mix_post_g=jnp.concatenate(d_post),
                 ffn_pre_g=jnp.concatenate(d_fpre), ffn_post_g=jnp.concatenate(d_fpost),
                 ffn_conv_w=jnp.stack(d_fcw), ffn_conv_b=jnp.stack(d_fcb))

    small_names = [n for n in WEIGHTS if n in small]
    full_shape = {n: tuple(given[n].shape) for n in small_names}
    full_shape["sc_conv_w"] = (1, CONV_WIDTH, wt["sc_cw"].shape[1])
    full_shape["ffn_conv_w"] = tuple(wt["f_cw"].shape)
    small_sum = allreduce_small(_pack_small([small[n] for n in small_names] + [loss]), "small_sum")
    summed = _unpack_small(small_sum, [full_shape[n] for n in small_names] + [()])
    loss = summed[-1]
    for n, g in zip(small_names, summed):
        if g.shape != given[n].shape:
            width = given[n].shape[-1]
            g = lax.dynamic_slice_in_dim(g, chip * width, width, axis=g.ndim - 1)
        grads[n] = g

    last = [n for n, _, _ in pending[0]]
    ready = [n for n in WEIGHTS if n not in last]
    for n in ready:
        update(n, token)
    finish_reduce("l0b", pending, [token, small_sum] + [deltas[n] for n in ready])
    for n in last:
        update(n, token)
    return (loss, dx[None], *[grads[n].reshape(given[n].shape) for n in WEIGHTS], *[deltas[n] for n in WEIGHTS],
            *[new_m[n] for n in WEIGHTS], *[new_v[n] for n in WEIGHTS])
```

```python
import functools
import math

import numpy as np
import jax
import jax.numpy as jnp
from jax import lax
from jax.experimental import pallas as pl
from jax.experimental.pallas import tpu as pltpu

F32 = jnp.float32
BF16 = jnp.bfloat16

RMS_EPS = 1e-6
HGRN_HEAD = 128
HGRN_CHUNK = 32
ATT_HEAD = 64
SWA_WINDOW = 128
SWA_GROUP = 8
ROT_DIM = 16
ROPE_THETA = 500000.0
CONV_WIDTH = 3
ADAM_LR = 0.001
ADAM_B1 = 0.9
ADAM_B2 = 0.999
ADAM_EPS = 1e-08
ADAM_WD = 0.01
ADAM_STEP = 10
N_CHIPS = 4
LANES = 128
BF16_ROWS = 16
VMEM_LIMIT = 48 * 1024 * 1024

_ARB = "arbitrary"
_PAR = "parallel"


def _params(sem, **kw):
    return pltpu.CompilerParams(dimension_semantics=sem, vmem_limit_bytes=VMEM_LIMIT, **kw)


def _tile(n, prefs):
    for p in prefs:
        if n % p == 0:
            return p
    return n


def _sigmoid(x):
    return 1.0 / (1.0 + jnp.exp(-x))


def _dot(a, b, dims):
    return lax.dot_general(a, b, (dims, ((), ())), preferred_element_type=F32)


def _dot_nn(a, b):
    return _dot(a, b, ((1,), (0,)))


def _dot_nt(a, b):
    return _dot(a, b, ((1,), (1,)))


def _dot_tn(a, b):
    return _dot(a, b, ((0,), (0,)))


MM_VMEM_BUDGET = 36 * 1024 * 1024
MM_HBM_RATE = 3.0e12
MM_MXU_RATE = 6.5e14
MM_STEP_S = 0.35e-6
MM_ACC_RATE = 3.0e12


def _mm_tiles(m, n, k, out_bytes):
    best = None
    for tm in (2048, 1024, 512, 256, 128):
        for tn in (2048, 1024, 512, 256, 128):
            for tk in sorted({k, 4096, 2816, 2048, 1408, 1024, 512, 256, 128}, reverse=True):
                if m % tm or n % tn or tk > k or k % tk:
                    continue
                nk = k // tk
                vmem = 4 * (tm * tk + tk * tn) + (4 * tm * tn if nk > 1 else 0) + 2 * tm * tn * out_bytes
                if vmem > MM_VMEM_BUDGET:
                    continue
                steps = (m // tm) * (n // tn) * nk
                traffic = 2 * m * k * (1 if nk == 1 else n // tn) + 2 * k * n * (m // tm) + m * n * out_bytes
                cost = max(traffic / MM_HBM_RATE, 2 * m * n * k / MM_MXU_RATE) + steps * MM_STEP_S
                if nk > 1:
                    cost += steps * 8 * tm * tn / MM_ACC_RATE
                if best is None or cost < best[0]:
                    best = (cost, tm, tn, tk)
    assert best is not None, (m, n, k)
    return best[1:]


def mm(a, b, mode, out_dtype, name="mm"):
    if mode == "nn":
        (m, k), (k2, n) = a.shape, b.shape
    elif mode == "nt":
        (m, k), (n, k2) = a.shape, b.shape
    else:
        (k, m), (k2, n) = a.shape, b.shape
    assert k == k2, (a.shape, b.shape, mode)
    tm, tn, tk = _mm_tiles(m, n, k, jnp.dtype(out_dtype).itemsize)
    nk = k // tk

    def product(a_ref, b_ref):
        av = a_ref[...].astype(BF16)
        bv = b_ref[...].astype(BF16)
        return {"nn": _dot_nn, "nt": _dot_nt, "tn": _dot_tn}[mode](av, bv)

    def body_one(a_ref, b_ref, o_ref):
        o_ref[...] = product(a_ref, b_ref).astype(out_dtype)

    def body_acc(a_ref, b_ref, o_ref, acc_ref):
        kk = pl.program_id(2)

        @pl.when(kk == 0)
        def _():
            acc_ref[...] = jnp.zeros_like(acc_ref)

        acc_ref[...] += product(a_ref, b_ref)

        @pl.when(kk == nk - 1)
        def _():
            o_ref[...] = acc_ref[...].astype(out_dtype)

    if mode == "nn":
        a_spec = pl.BlockSpec((tm, tk), lambda i, j, kk: (i, kk))
        b_spec = pl.BlockSpec((tk, tn), lambda i, j, kk: (kk, j))
    elif mode == "nt":
        a_spec = pl.BlockSpec((tm, tk), lambda i, j, kk: (i, kk))
        b_spec = pl.BlockSpec((tn, tk), lambda i, j, kk: (j, kk))
    else:
        a_spec = pl.BlockSpec((tk, tm), lambda i, j, kk: (kk, i))
        b_spec = pl.BlockSpec((tk, tn), lambda i, j, kk: (kk, j))
    return pl.pallas_call(
        body_one if nk == 1 else body_acc,
        name=name,
        grid=(m // tm, n // tn, nk),
        in_specs=[a_spec, b_spec],
        out_specs=pl.BlockSpec((tm, tn), lambda i, j, kk: (i, j)),
        out_shape=jax.ShapeDtypeStruct((m, n), out_dtype),
        scratch_shapes=[] if nk == 1 else [pltpu.VMEM((tm, tn), F32)],
        compiler_params=_params((_PAR, _PAR, _ARB)),
    )(a, b)


def wgrad(a, b, name):
    return mm(a, b, "tn", BF16, name)


def _rstd(xv):
    return lax.rsqrt(jnp.mean(xv * xv, axis=1, keepdims=True) + RMS_EPS)


def _row_spec(tr, w):
    return pl.BlockSpec((tr, w), lambda i: (i, 0))


def _full_spec(shape):
    nd = len(shape)
    return pl.BlockSpec(shape, lambda *_: (0,) * nd)


def rms_fwd(x, g, name):
    t, d = x.shape
    tr = _tile(t, (256, 128, 64, 32, 16))

    def body(x_ref, g_ref, o_ref):
        xv = x_ref[...]
        o_ref[...] = (xv * _rstd(xv) * g_ref[...]).astype(BF16)

    return pl.pallas_call(
        body, name=name, grid=(t // tr,),
        in_specs=[_row_spec(tr, d), _full_spec((1, d))],
        out_specs=_row_spec(tr, d),
        out_shape=jax.ShapeDtypeStruct((t, d), BF16),
        compiler_params=_params((_PAR,)),
    )(x, g)


def resid_norm(x, y, g_post, g_next, name):
    t, d = x.shape
    tr = _tile(t, (256, 128, 64, 32, 16))

    def body(x_ref, y_ref, gp_ref, gn_ref, x1_ref, hn_ref):
        yv = y_ref[...]
        x1 = x_ref[...] + yv * _rstd(yv) * gp_ref[...]
        x1_ref[...] = x1
        hn_ref[...] = (x1 * _rstd(x1) * gn_ref[...]).astype(BF16)

    return pl.pallas_call(
        body, name=name, grid=(t // tr,),
        in_specs=[_row_spec(tr, d), _row_spec(tr, d), _full_spec((1, d)), _full_spec((1, d))],
        out_specs=[_row_spec(tr, d), _row_spec(tr, d)],
        out_shape=[jax.ShapeDtypeStruct((t, d), F32), jax.ShapeDtypeStruct((t, d), BF16)],
        compiler_params=_params((_PAR,)),
    )(x, y, g_post, g_next)


def resid_loss(x, y, g_post, target, name):
    t, d = x.shape
    tr = _tile(t, (256, 128, 64, 32, 16))

    def body(x_ref, y_ref, gp_ref, t_ref, dx_ref, loss_ref):
        @pl.when(pl.program_id(0) == 0)
        def _():
            loss_ref[...] = jnp.zeros_like(loss_ref)

        yv = y_ref[...]
        err = x_ref[...] + yv * _rstd(yv) * gp_ref[...] - t_ref[...]
        dx_ref[...] = err * (1.0 / d)
        loss_ref[...] += 0.5 * jnp.sum(jnp.mean(err * err, axis=1, keepdims=True), axis=0, keepdims=True)

    dx, loss = pl.pallas_call(
        body, name=name, grid=(t // tr,),
        in_specs=[_row_spec(tr, d), _row_spec(tr, d), _full_spec((1, d)), _row_spec(tr, d)],
        out_specs=[_row_spec(tr, d), _full_spec((8, LANES))],
        out_shape=[jax.ShapeDtypeStruct((t, d), F32), jax.ShapeDtypeStruct((8, LANES), F32)],
        compiler_params=_params((_ARB,)),
    )(x, y, g_post, target)
    return dx, loss[0:1, 0:1]


def norm_bwd(yin, g, dout, res, out_dtype, name, after=None):
    t, d = yin.shape
    tr = _tile(t, (256, 128, 64, 32, 16))
    has_res = res is not None

    def body(*refs):
        refs = refs[:3 + has_res] + refs[-2:]
        if has_res:
            y_ref, g_ref, d_ref, r_ref, o_ref, dg_ref = refs
        else:
            y_ref, g_ref, d_ref, o_ref, dg_ref = refs

        @pl.when(pl.program_id(0) == 0)
        def _():
            dg_ref[...] = jnp.zeros_like(dg_ref)

        yv = y_ref[...]
        dv = d_ref[...].astype(F32)
        r = _rstd(yv)
        yh = yv * r
        dyh = dv * g_ref[...]
        dy = r * (dyh - yh * jnp.mean(dyh * yh, axis=1, keepdims=True))
        if has_res:
            dy = dy + r_ref[...]
        o_ref[...] = dy.astype(out_dtype)
        dg_ref[...] += jnp.sum(dv * yh, axis=0, keepdims=True)

    ins = [yin, g, dout] + ([res] if has_res else []) + ([] if after is None else [after])
    in_specs = ([_row_spec(tr, d), _full_spec((1, d)), _row_spec(tr, d)] + ([_row_spec(tr, d)] if has_res else [])
                + ([] if after is None else [pl.BlockSpec(memory_space=pl.ANY)]))
    return pl.pallas_call(
        body, name=name, grid=(t // tr,),
        in_specs=in_specs,
        out_specs=[_row_spec(tr, d), _full_spec((1, d))],
        out_shape=[jax.ShapeDtypeStruct((t, d), out_dtype), jax.ShapeDtypeStruct((1, d), F32)],
        compiler_params=_params((_ARB,)),
    )(*ins)


def _shift_down(x, halo):
    tr = x.shape[0]
    row = lax.broadcasted_iota(jnp.int32, x.shape, 0)
    h1 = halo[BF16_ROWS - 1:BF16_ROWS, :]
    h2 = halo[BF16_ROWS - 2:BF16_ROWS - 1, :]
    x1 = jnp.where(row == 0, h1, pltpu.roll(x, 1, 0))
    x2 = jnp.where(row == 0, h2, jnp.where(row == 1, h1, pltpu.roll(x, 2, 0)))
    return x1, x2


def _shift_up(x, halo):
    tr = x.shape[0]
    row = lax.broadcasted_iota(jnp.int32, x.shape, 0)
    h0 = halo[0:1, :]
    h1 = halo[1:2, :]
    x1 = jnp.where(row == tr - 1, h0, pltpu.roll(x, tr - 1, 0))
    x2 = jnp.where(row == tr - 1, h1, jnp.where(row == tr - 2, h0, pltpu.roll(x, tr - 2, 0)))
    return x1, x2


def _prev_halo_spec(tr, w, nt):
    return pl.BlockSpec((BF16_ROWS, w), lambda i: (jnp.maximum(i * (tr // BF16_ROWS) - 1, 0), 0))


def _next_halo_spec(tr, w, nt):
    last = nt * (tr // BF16_ROWS) - 1
    return pl.BlockSpec((BF16_ROWS, w), lambda i: (jnp.minimum((i + 1) * (tr // BF16_ROWS), last), 0))


def _silu_and_grad(u):
    s = _sigmoid(u)
    return u * s, s * (1.0 + u * (1.0 - s))


def ffn_act(z, conv_w, conv_b, name):
    t, f2 = z.shape
    f = f2 // 2
    tr = _tile(t, (128, 64, 32, 16))
    nt = t // tr
    cw = _tile(f, (512, 256, 128))

    def body(z_ref, zp_ref, w_ref, b_ref, a_ref):
        first = pl.program_id(0) == 0
        for j in range(f // cw):
            us = []
            for off in (j * cw, f + j * cw):
                cols = slice(off, off + cw)
                zc = z_ref[:, cols].astype(F32)
                hp = jnp.where(first, 0.0, zp_ref[:, cols].astype(F32))
                z1, z2 = _shift_down(zc, hp)
                us.append(w_ref[2:3, cols] * zc + w_ref[1:2, cols] * z1 + w_ref[0:1, cols] * z2 + b_ref[:, cols])
            sil, _ = _silu_and_grad(us[0])
            a_ref[:, j * cw:(j + 1) * cw] = (sil * us[1]).astype(BF16)

    return pl.pallas_call(
        body, name=name, grid=(nt,),
        in_specs=[_row_spec(tr, f2), _prev_halo_spec(tr, f2, nt), _full_spec((CONV_WIDTH, f2)), _full_spec((1, f2))],
        out_specs=_row_spec(tr, f),
        out_shape=jax.ShapeDtypeStruct((t, f), BF16),
        compiler_params=_params((_PAR,)),
    )(z, z, conv_w, conv_b)


def ffn_act_bwd(z, da, conv_w, conv_b, name):
    t, f2 = z.shape
    f = f2 // 2
    tr = _tile(t, (128, 64, 32, 16))
    nt = t // tr
    cw = _tile(f, (512, 256, 128))

    def body(z_ref, zp_ref, da_ref, w_ref, b_ref, du_ref, acc_ref):
        first = pl.program_id(0) == 0

        @pl.when(first)
        def _():
            acc_ref[...] = jnp.zeros_like(acc_ref)

        for j in range(f // cw):
            us, zs = [], []
            for off in (j * cw, f + j * cw):
                cols = slice(off, off + cw)
                zc = z_ref[:, cols].astype(F32)
                hp = jnp.where(first, 0.0, zp_ref[:, cols].astype(F32))
                z1, z2 = _shift_down(zc, hp)
                zs.append((z2, z1, zc))
                us.append(w_ref[2:3, cols] * zc + w_ref[1:2, cols] * z1 + w_ref[0:1, cols] * z2 + b_ref[:, cols])
            dav = da_ref[:, j * cw:(j + 1) * cw].astype(F32)
            sil, dsil = _silu_and_grad(us[0])
            dus = (dav * us[1] * dsil, dav * sil)
            for off, du, zsh in zip((j * cw, f + j * cw), dus, zs):
                cols = slice(off, off + cw)
                du_ref[:, cols] = du.astype(BF16)
                for k in range(CONV_WIDTH):
                    acc_ref[k:k + 1, cols] += jnp.sum(du * zsh[k], axis=0, keepdims=True)
                acc_ref[3:4, cols] += jnp.sum(du, axis=0, keepdims=True)

    return pl.pallas_call(
        body, name=name, grid=(nt,),
        in_specs=[_row_spec(tr, f2), _prev_halo_spec(tr, f2, nt), _row_spec(tr, f),
                  _full_spec((CONV_WIDTH, f2)), _full_spec((1, f2))],
        out_specs=[_row_spec(tr, f2), _full_spec((8, f2))],
        out_shape=[jax.ShapeDtypeStruct((t, f2), BF16), jax.ShapeDtypeStruct((8, f2), F32)],
        compiler_params=_params((_ARB,)),
    )(z, z, da, conv_w, conv_b)


def conv_transpose(du, conv_w, name):
    t, w = du.shape
    tr = _tile(t, (128, 64, 32, 16))
    nt = t // tr
    cw = _tile(w, (512, 256, 128))

    def body(d_ref, dn_ref, w_ref, o_ref):
        last = pl.program_id(0) == nt - 1
        for j in range(w // cw):
            cols = slice(j * cw, (j + 1) * cw)
            dc = d_ref[:, cols].astype(F32)
            hn = jnp.where(last, 0.0, dn_ref[:, cols].astype(F32))
            d1, d2 = _shift_up(dc, hn)
            o_ref[:, cols] = (w_ref[2:3, cols] * dc + w_ref[1:2, cols] * d1 + w_ref[0:1, cols] * d2).astype(BF16)

    return pl.pallas_call(
        body, name=name, grid=(nt,),
        in_specs=[_row_spec(tr, w), _next_halo_spec(tr, w, nt), _full_spec((CONV_WIDTH, w))],
        out_specs=_row_spec(tr, w),
        out_shape=jax.ShapeDtypeStruct((t, w), BF16),
        compiler_params=_params((_PAR,)),
    )(du, du, conv_w)


def sconv_fwd(proj, conv_w, name):
    t, w3 = proj.shape
    d = w3 // 3
    tr = _tile(t, (128, 64, 32, 16))
    nt = t // tr
    cw = _tile(d, (512, 256, 128))

    def body(p_ref, pp_ref, w_ref, o_ref):
        first = pl.program_id(0) == 0
        for j in range(d // cw):
            cb, cc, cx = (slice(k * d + j * cw, k * d + (j + 1) * cw) for k in range(3))
            zc = p_ref[:, cc].astype(F32) * p_ref[:, cx].astype(F32)
            hp = jnp.where(first, 0.0, pp_ref[:, cc].astype(F32) * pp_ref[:, cx].astype(F32))
            z1, z2 = _shift_down(zc, hp)
            wc = slice(j * cw, (j + 1) * cw)
            cz = w_ref[2:3, wc] * zc + w_ref[1:2, wc] * z1 + w_ref[0:1, wc] * z2
            o_ref[:, wc] = (p_ref[:, cb].astype(F32) * cz).astype(BF16)

    return pl.pallas_call(
        body, name=name, grid=(nt,),
        in_specs=[_row_spec(tr, w3), _prev_halo_spec(tr, w3, nt), _full_spec((CONV_WIDTH, d))],
        out_specs=_row_spec(tr, d),
        out_shape=jax.ShapeDtypeStruct((t, d), BF16),
        compiler_params=_params((_PAR,)),
    )(proj, proj, conv_w)


def sconv_bwd(proj, dyb, conv_w, name):
    t, w3 = proj.shape
    d = w3 // 3
    tr = _tile(t, (128, 64, 32, 16))
    nt = t // tr
    cw = _tile(d, (512, 256, 128))

    def body(p_ref, pp_ref, pn_ref, dy_ref, dyn_ref, w_ref, o_ref, acc_ref):
        first = pl.program_id(0) == 0
        last = pl.program_id(0) == nt - 1

        @pl.when(first)
        def _():
            acc_ref[...] = jnp.zeros_like(acc_ref)

        for j in range(d // cw):
            cb, cc, cx = (slice(k * d + j * cw, k * d + (j + 1) * cw) for k in range(3))
            wc = slice(j * cw, (j + 1) * cw)
            bv, cv, xv = p_ref[:, cb].astype(F32), p_ref[:, cc].astype(F32), p_ref[:, cx].astype(F32)
            zc = cv * xv
            hp = jnp.where(first, 0.0, pp_ref[:, cc].astype(F32) * pp_ref[:, cx].astype(F32))
            z1, z2 = _shift_down(zc, hp)
            w0, w1, w2 = w_ref[0:1, wc], w_ref[1:2, wc], w_ref[2:3, wc]
            cz = w2 * zc + w1 * z1 + w0 * z2
            dyv = dy_ref[:, wc].astype(F32)
            dcz = dyv * bv
            hn = jnp.where(last, 0.0, dyn_ref[:, wc].astype(F32) * pn_ref[:, cb].astype(F32))
            n1, n2 = _shift_up(dcz, hn)
            dz = w2 * dcz + w1 * n1 + w0 * n2
            o_ref[:, cb] = (dyv * cz).astype(BF16)
            o_ref[:, cc] = (dz * xv).astype(BF16)
            o_ref[:, cx] = (dz * cv).astype(BF16)
            for k, zsh in enumerate((z2, z1, zc)):
                acc_ref[k:k + 1, wc] += jnp.sum(dcz * zsh, axis=0, keepdims=True)

    return pl.pallas_call(
        body, name=name, grid=(nt,),
        in_specs=[_row_spec(tr, w3), _prev_halo_spec(tr, w3, nt), _next_halo_spec(tr, w3, nt),
                  _row_spec(tr, d), _next_halo_spec(tr, d, nt), _full_spec((CONV_WIDTH, d))],
        out_specs=[_row_spec(tr, w3), _full_spec((8, d))],
        out_shape=[jax.ShapeDtypeStruct((t, w3), BF16), jax.ShapeDtypeStruct((8, d), F32)],
        compiler_params=_params((_ARB,)),
    )(proj, proj, proj, dyb, dyb, conv_w)


def rope_tables(positions):
    half = ROT_DIM // 2
    inv_freq = ROPE_THETA ** (-jnp.arange(half, dtype=F32) / half)
    ang = positions.astype(F32)[:, None] * inv_freq[None, :]
    cos, sin = jnp.cos(ang), jnp.sin(ang)
    ones = jnp.ones((positions.shape[0], ATT_HEAD - ROT_DIM), F32)
    c64 = jnp.concatenate([cos, cos, ones], axis=1)
    s64 = jnp.concatenate([-sin, sin, 0.0 * ones], axis=1)
    perm = np.zeros((LANES, LANES), np.float32)
    for lane in range(LANES):
        dim = lane % ATT_HEAD
        if dim < half:
            perm[lane + half, lane] = 1.0
        elif dim < ROT_DIM:
            perm[lane - half, lane] = 1.0
    return jnp.tile(c64, (1, 2)), jnp.tile(s64, (1, 2)), jnp.asarray(perm, BF16)


def rope(xin, ctab, stab, perm, n_rot, sign, name):
    t, w = xin.shape
    tr = _tile(t, (256, 128, 64, 32, 16))

    def body(x_ref, c_ref, s_ref, p_ref, o_ref):
        cv, sv = c_ref[...], s_ref[...] * sign
        for j in range(n_rot // LANES):
            cols = slice(j * LANES, (j + 1) * LANES)
            xb = x_ref[:, cols]
            o_ref[:, cols] = (xb.astype(F32) * cv + _dot_nn(xb, p_ref[...]) * sv).astype(BF16)
        if n_rot < w:
            o_ref[:, n_rot:] = x_ref[:, n_rot:]

    return pl.pallas_call(
        body, name=name, grid=(t // tr,),
        in_specs=[_row_spec(tr, w), _row_spec(tr, LANES), _row_spec(tr, LANES), _full_spec((LANES, LANES))],
        out_specs=_row_spec(tr, w),
        out_shape=jax.ShapeDtypeStruct((t, w), BF16),
        compiler_params=_params((_PAR,)),
    )(xin, ctab, stab, perm)


NEG = -1e30


def _half(shape, h):
    return (lax.broadcasted_iota(jnp.int32, shape, 1) // ATT_HEAD) == h


def _dup_head(xb, kvh):
    xf = jnp.where(_half(xb.shape, kvh), xb.astype(F32), 0.0)
    return (xf + pltpu.roll(xf, ATT_HEAD, 1)).astype(BF16)


def _swa_mask(n, rows, cur_only):
    w = SWA_WINDOW
    shape = (w, w) if cur_only else (w, 2 * w)
    qi = lax.broadcasted_iota(jnp.int32, shape, 0)
    kj = lax.broadcasted_iota(jnp.int32, shape, 1) + (w if cur_only else 0)
    diff = qi + w - kj
    ok = (diff >= 0) & (diff < w)
    return ok & ((kj >= w) | (n > 0))


def swa_fwd(qkv, sinks, hq, name):
    t = qkv.shape[0]
    w = SWA_WINDOW
    nb = t // w
    hkv = hq // SWA_GROUP
    npair = hkv // 2
    qw = 2 * SWA_GROUP * ATT_HEAD
    kcol = hq * ATT_HEAD // LANES
    vcol = kcol + npair
    scale = ATT_HEAD ** -0.5

    def body(sink_ref, q_ref, kp_ref, kc_ref, vp_ref, vc_ref, o_ref, lse_ref):
        m, n = pl.program_id(0), pl.program_id(1)
        kb = jnp.concatenate([kp_ref[...], kc_ref[...]], axis=0)
        vb = jnp.concatenate([vp_ref[...], vc_ref[...]], axis=0)
        ok = _swa_mask(n, w, False)
        for kvh in range(2):
            kd, vd = _dup_head(kb, kvh), _dup_head(vb, kvh)
            for jj in range(SWA_GROUP // 2):
                jp = kvh * (SWA_GROUP // 2) + jj
                q2 = q_ref[:, jp * LANES:(jp + 1) * LANES]
                outs = []
                for a in range(2):
                    qa = jnp.where(_half(q2.shape, a), q2, jnp.zeros_like(q2))
                    s = jnp.where(ok, _dot_nt(qa, kd) * scale, NEG)
                    sink = sink_ref[m * 2 * SWA_GROUP + jp * 2 + a]
                    mx = jnp.maximum(jnp.max(s, axis=1, keepdims=True), sink)
                    e = jnp.exp(s - mx)
                    den = jnp.sum(e, axis=1, keepdims=True) + jnp.exp(sink - mx)
                    p = (e / den).astype(BF16)
                    outs.append(_dot_nn(p, vd))
                    lse_ref[jp * 2 + a] = jnp.broadcast_to(mx + jnp.log(den), (w, LANES))
                o_ref[:, jp * LANES:(jp + 1) * LANES] = jnp.where(_half(outs[0].shape, 0), outs[0], outs[1]).astype(BF16)

    prev = lambda m, n: jnp.maximum(n - 1, 0)
    grid_spec = pltpu.PrefetchScalarGridSpec(
        num_scalar_prefetch=1, grid=(npair, nb),
        in_specs=[
            pl.BlockSpec((w, qw), lambda m, n, s: (n, m)),
            pl.BlockSpec((w, LANES), lambda m, n, s: (prev(m, n), kcol + m)),
            pl.BlockSpec((w, LANES), lambda m, n, s: (n, kcol + m)),
            pl.BlockSpec((w, LANES), lambda m, n, s: (prev(m, n), vcol + m)),
            pl.BlockSpec((w, LANES), lambda m, n, s: (n, vcol + m)),
        ],
        out_specs=[
            pl.BlockSpec((w, qw), lambda m, n, s: (n, m)),
            pl.BlockSpec((2 * SWA_GROUP, w, LANES), lambda m, n, s: (m, n, 0)),
        ],
    )
    return pl.pallas_call(
        body, name=name, grid_spec=grid_spec,
        out_shape=[jax.ShapeDtypeStruct((t, hq * ATT_HEAD), BF16), jax.ShapeDtypeStruct((hq, t, LANES), F32)],
        compiler_params=_params((_PAR, _PAR)),
    )(sinks, qkv, qkv, qkv, qkv, qkv)


def swa_bwd(qkv, o, lse, do, sinks, hq, name):
    t = qkv.shape[0]
    w = SWA_WINDOW
    nb = t // w
    hkv = hq // SWA_GROUP
    npair = hkv // 2
    qw = 2 * SWA_GROUP * ATT_HEAD
    kcol = hq * ATT_HEAD // LANES
    vcol = kcol + npair
    scale = ATT_HEAD ** -0.5
    gh = 2 * SWA_GROUP

    def body(sink_ref, qc_ref, qn_ref, kp_ref, kc_ref, vp_ref, vc_ref, oc_ref, on_ref, dc_ref, dn_ref,
             lc_ref, ln_ref, dq_ref, dk_ref, dv_ref, ds_ref):
        m, n = pl.program_id(0), pl.program_id(1)
        kb = jnp.concatenate([kp_ref[...], kc_ref[...]], axis=0)
        vb = jnp.concatenate([vp_ref[...], vc_ref[...]], axis=0)
        ok_band = _swa_mask(n, w, False)
        ok_cur = _swa_mask(n, w, True)
        qi = lax.broadcasted_iota(jnp.int32, (w, w), 0)
        kj = lax.broadcasted_iota(jnp.int32, (w, w), 1)
        ok_next = (kj > qi) & (n < nb - 1)
        row16 = lax.broadcasted_iota(jnp.int32, (gh, LANES), 0)
        dsink = jnp.zeros((gh, LANES), F32)
        dk_tot = jnp.zeros((w, LANES), F32)
        dv_tot = jnp.zeros((w, LANES), F32)
        for kvh in range(2):
            kd, vd = _dup_head(kb, kvh), _dup_head(vb, kvh)
            kdc, vdc = kd[w:, :], vd[w:, :]
            acc_k = [jnp.zeros((w, LANES), F32), jnp.zeros((w, LANES), F32)]
            acc_v = [jnp.zeros((w, LANES), F32), jnp.zeros((w, LANES), F32)]
            for jj in range(SWA_GROUP // 2):
                jp = kvh * (SWA_GROUP // 2) + jj
                cols = slice(jp * LANES, (jp + 1) * LANES)
                dqs = []
                for a in range(2):
                    hd = jp * 2 + a
                    sink = sink_ref[m * gh + hd]
                    half = _half((w, LANES), a)
                    q2 = jnp.where(half, qc_ref[:, cols], jnp.zeros((w, LANES), BF16))
                    d2 = jnp.where(half, dc_ref[:, cols], jnp.zeros((w, LANES), BF16))
                    delta = jnp.sum(d2.astype(F32) * oc_ref[:, cols].astype(F32), axis=1, keepdims=True)
                    lse_c = lc_ref[hd][:, 0:1]
                    p = jnp.exp(jnp.where(ok_band, _dot_nt(q2, kd) * scale, NEG) - lse_c)
                    dsv = p * (_dot_nt(d2, vd) - delta)
                    dqs.append(_dot_nn(dsv.astype(BF16), kd) * scale)
                    psink = jnp.exp(sink - lse_c)
                    dsink = jnp.where(row16 == hd, dsink - jnp.sum(psink * delta, axis=0, keepdims=True), dsink)
                    for q_ref, d_ref, o_ref, l_ref, okm in ((qc_ref, dc_ref, oc_ref, lc_ref, ok_cur),
                                                           (qn_ref, dn_ref, on_ref, ln_ref, ok_next)):
                        q2 = jnp.where(half, q_ref[:, cols], jnp.zeros((w, LANES), BF16))
                        d2 = jnp.where(half, d_ref[:, cols], jnp.zeros((w, LANES), BF16))
                        delta = jnp.sum(d2.astype(F32) * o_ref[:, cols].astype(F32), axis=1, keepdims=True)
                        p = jnp.exp(jnp.where(okm, _dot_nt(q2, kdc) * scale, NEG) - l_ref[hd][:, 0:1])
                        dsv = p * (_dot_nt(d2, vdc) - delta)
                        acc_v[a] = acc_v[a] + _dot_tn(p.astype(BF16), d2)
                        acc_k[a] = acc_k[a] + _dot_tn(dsv.astype(BF16), q2) * scale
                dq_ref[:, cols] = jnp.where(_half((w, LANES), 0), dqs[0], dqs[1]).astype(BF16)
            dk_tot = dk_tot + acc_k[kvh] + pltpu.roll(acc_k[1 - kvh], ATT_HEAD, 1)
            dv_tot = dv_tot + acc_v[kvh] + pltpu.roll(acc_v[1 - kvh], ATT_HEAD, 1)
        dk_ref[...] = dk_tot.astype(BF16)
        dv_ref[...] = dv_tot.astype(BF16)
        ds_ref[0, 0] = dsink

    prev = lambda n: jnp.maximum(n - 1, 0)
    nxt = lambda n: jnp.minimum(n + 1, nb - 1)
    qspec = lambda f: pl.BlockSpec((w, qw), lambda m, n, s: (f(n), m))
    lspec = lambda f: pl.BlockSpec((gh, w, LANES), lambda m, n, s: (m, f(n), 0))
    same = lambda n: n
    grid_spec = pltpu.PrefetchScalarGridSpec(
        num_scalar_prefetch=1, grid=(npair, nb),
        in_specs=[
            qspec(same), qspec(nxt),
            pl.BlockSpec((w, LANES), lambda m, n, s: (prev(n), kcol + m)),
            pl.BlockSpec((w, LANES), lambda m, n, s: (n, kcol + m)),
            pl.BlockSpec((w, LANES), lambda m, n, s: (prev(n), vcol + m)),
            pl.BlockSpec((w, LANES), lambda m, n, s: (n, vcol + m)),
            qspec(same), qspec(nxt), qspec(same), qspec(nxt),
            lspec(same), lspec(nxt),
        ],
        out_specs=[
            pl.BlockSpec((w, qw), lambda m, n, s: (n, m)),
            pl.BlockSpec((w, LANES), lambda m, n, s: (n, m)),
            pl.BlockSpec((w, LANES), lambda m, n, s: (n, m)),
            pl.BlockSpec((1, 1, gh, LANES), lambda m, n, s: (m, n, 0, 0)),
        ],
    )
    return pl.pallas_call(
        body, name=name, grid_spec=grid_spec,
        out_shape=[jax.ShapeDtypeStruct((t, hq * ATT_HEAD), BF16),
                   jax.ShapeDtypeStruct((t, hkv * ATT_HEAD), BF16),
                   jax.ShapeDtypeStruct((t, hkv * ATT_HEAD), BF16),
                   jax.ShapeDtypeStruct((npair, nb, gh, LANES), F32)],
        compiler_params=_params((_PAR, _PAR)),
    )(sinks, qkv, qkv, qkv, qkv, qkv, qkv, o, o, do, do, lse, lse)


def swa_mixer_fwd(hn, w_in, w_out, sinks, positions, name):
    hq = sinks.shape[0]
    n_rot = (hq + hq // SWA_GROUP) * ATT_HEAD
    tabs = rope_tables(positions)
    proj = mm(hn, w_in, "nn", BF16, name + "_in")
    qkv = rope(proj, *tabs, n_rot, 1.0, name + "_rope")
    o, lse = swa_fwd(qkv, sinks, hq, name + "_att")
    y = mm(o, w_out, "nn", F32, name + "_out")
    return y, (qkv, o, lse)


def swa_mixer_bwd(dy, hn, w_in, w_out, sinks, positions, saved, name):
    qkv, o, lse = saved
    hq = sinks.shape[0]
    n_rot = (hq + hq // SWA_GROUP) * ATT_HEAD
    tabs = rope_tables(positions)
    dwo = wgrad(o, dy, name + "_dwo")
    do = mm(dy, w_out, "nt", BF16, name + "_do")
    dq, dk, dv, dsp = swa_bwd(qkv, o, lse, do, sinks, hq, name + "_attb")
    dproj = rope(jnp.concatenate([dq, dk, dv], axis=1), *tabs, n_rot, -1.0, name + "_ropeb")
    dwi = wgrad(hn, dproj, name + "_dwi")
    dhn = mm(dproj, w_in, "nt", BF16, name + "_dhn")
    dsinks = jnp.sum(dsp[:, :, :, 0], axis=1).reshape(hq)
    return dhn, dwi, dwo, dsinks


FOX_FPAD = 512


def _log_sigmoid(x):
    return jnp.minimum(x, 0.0) - jnp.log(1.0 + jnp.exp(-jnp.abs(x)))


def _tri(n, upper):
    r = lax.broadcasted_iota(jnp.int32, (n, n), 0)
    c = lax.broadcasted_iota(jnp.int32, (n, n), 1)
    return jnp.where((c >= r) if upper else (c <= r), 1.0, 0.0).astype(F32)


def _dot_exact(a, b):
    return jnp.dot(a, b, precision=lax.Precision.HIGHEST, preferred_element_type=F32)


def fox_cumsum(fl, b_pad, name):
    t = fl.shape[0]
    tr = _tile(t, (256, 128, 64, 32, 16, 8))

    def body(f_ref, b_ref, c_ref, carry_ref):
        @pl.when(pl.program_id(0) == 0)
        def _():
            carry_ref[...] = jnp.zeros_like(carry_ref)

        c = _dot_exact(_tri(tr, False), _log_sigmoid(f_ref[...] + b_ref[...])) + carry_ref[...]
        c_ref[...] = c
        carry_ref[...] = c[tr - 1:tr, :]

    return pl.pallas_call(
        body, name=name, grid=(t // tr,),
        in_specs=[_row_spec(tr, LANES), _full_spec((1, LANES))],
        out_specs=_row_spec(tr, LANES),
        out_shape=jax.ShapeDtypeStruct((t, LANES), F32),
        scratch_shapes=[pltpu.VMEM((1, LANES), F32)],
        compiler_params=_params((_ARB,)),
    )(fl, b_pad)


def fox_cumsum_bwd(dc, fl, b_pad, name):
    t = fl.shape[0]
    tr = _tile(t, (256, 128, 64, 32, 16, 8))
    nt = t // tr

    def body(d_ref, f_ref, b_ref, o_ref, db_ref, carry_ref):
        @pl.when(pl.program_id(0) == 0)
        def _():
            carry_ref[...] = jnp.zeros_like(carry_ref)
            db_ref[...] = jnp.zeros_like(db_ref)

        dlf = _dot_exact(_tri(tr, True), d_ref[...]) + carry_ref[...]
        carry_ref[...] = dlf[0:1, :]
        dfl = dlf * _sigmoid(-(f_ref[...] + b_ref[...]))
        o_ref[...] = dfl.astype(BF16)
        db_ref[...] += jnp.sum(dfl, axis=0, keepdims=True)

    rev = pl.BlockSpec((tr, LANES), lambda i: (nt - 1 - i, 0))
    return pl.pallas_call(
        body, name=name, grid=(nt,),
        in_specs=[rev, rev, _full_spec((1, LANES))],
        out_specs=[rev, _full_spec((1, LANES))],
        out_shape=[jax.ShapeDtypeStruct((t, LANES), BF16), jax.ShapeDtypeStruct((1, LANES), F32)],
        scratch_shapes=[pltpu.VMEM((1, LANES), F32)],
        compiler_params=_params((_ARB,)),
    )(dc, fl, b_pad)


AUG_C, AUG_ONE, AUG_LSE = ATT_HEAD, ATT_HEAD + 3, ATT_HEAD + 6


def _split3(x):
    hi = x.astype(BF16).astype(F32)
    mid = (x - hi).astype(BF16).astype(F32)
    return hi, mid, (x - hi - mid).astype(BF16).astype(F32)


def _aug(base, lane, entries):
    out = jnp.where(lane < ATT_HEAD, base, 0.0)
    for first, parts in entries:
        if parts is None:
            out = jnp.where((lane >= first) & (lane < first + 3), 1.0, out)
        else:
            for k, part in enumerate(parts):
                out = jnp.where(lane == first + k, part, out)
    return out.astype(BF16)


def _head_of_pair(x2, a):
    xf = x2.astype(F32)
    return xf if a == 0 else pltpu.roll(xf, ATT_HEAD, 1)


def fa_prep(proj, c, nh, name):
    t = proj.shape[0]
    npair = nh // 2
    tr = _tile(t, (256, 128))
    scale = ATT_HEAD ** -0.5

    def body(q_ref, k_ref, v_ref, c_ref, qa_ref, ka_ref, va_ref):
        lane = lax.broadcasted_iota(jnp.int32, (tr, LANES), 1)
        for p in range(npair):
            pc = slice(p * LANES, (p + 1) * LANES)
            for a in range(2):
                h = 2 * p + a
                hc = slice(h * LANES, (h + 1) * LANES)
                ch = c_ref[:, h:h + 1]
                qa_ref[:, hc] = _aug(_head_of_pair(q_ref[:, pc], a) * scale, lane,
                                     [(AUG_C, _split3(ch)), (AUG_ONE, None)])
                ka_ref[:, hc] = _aug(_head_of_pair(k_ref[:, pc], a), lane,
                                     [(AUG_C, None), (AUG_ONE, _split3(-ch)), (AUG_LSE, None)])
                va_ref[:, hc] = _aug(_head_of_pair(v_ref[:, pc], a), lane, [(AUG_C, None)])

    hd = nh * ATT_HEAD
    part = lambda k: pl.BlockSpec((tr, hd), lambda i: (i, k))
    out = pl.BlockSpec((tr, nh * LANES), lambda i: (i, 0))
    return pl.pallas_call(
        body, name=name, grid=(t // tr,),
        in_specs=[part(0), part(1), part(2), _row_spec(tr, LANES)],
        out_specs=[out, out, out],
        out_shape=[jax.ShapeDtypeStruct((t, nh * LANES), BF16)] * 3,
        compiler_params=_params((_PAR,)),
    )(proj, proj, proj, c)


def _fox_tiles(t, most):
    outer = _tile(t, tuple(s for s in (2048, 1024, 512, 256, 128) if s <= most))
    return outer, min(outer, 512)


def _diag_mask(outer, inner, d, transposed=False):
    r = lax.broadcasted_iota(jnp.int32, (outer, inner), 0)
    c = lax.broadcasted_iota(jnp.int32, (outer, inner), 1) + d * inner
    return (r <= c) if transposed else (c <= r)


def fa_fwd(qa, ka, va, proj, nh, name):
    t = qa.shape[0]
    hd = nh * ATT_HEAD
    npair = nh // 2
    tq, tk = _fox_tiles(t, 2048)
    nt, ratio = t // tq, tq // tk
    gcol = (3 * hd + FOX_FPAD) // LANES

    def body(q_ref, k_ref, v_ref, g_ref, o_ref, og_ref, lse_ref):
        i = pl.program_id(1)
        heads = [slice(a * LANES, (a + 1) * LANES) for a in range(2)]
        qs = [q_ref[:, cols] for cols in heads]

        def tile(j, carry, diag):
            rows = pl.ds(pl.multiple_of(j * tk, tk), tk)
            out = []
            for (mx, acc), q, cols in zip(carry, qs, heads):
                s = _dot_nt(q, k_ref[rows, cols])
                if diag is not None:
                    s = jnp.where(_diag_mask(tq, tk, diag), s, NEG)
                mnew = jnp.maximum(mx, jnp.max(s, axis=1, keepdims=True))
                p = jnp.exp(s - mnew).astype(BF16)
                out.append((mnew, jnp.exp(mx - mnew) * acc + _dot_nn(p, v_ref[rows, cols])))
            return tuple(out)

        carry = ((jnp.full((tq, 1), NEG, F32), jnp.zeros((tq, LANES), F32)),) * 2
        carry = lax.fori_loop(0, i * ratio, functools.partial(tile, diag=None), carry)
        for d in range(ratio):
            carry = tile(i * ratio + d, carry, d)
        outs = []
        for a, (mx, acc) in enumerate(carry):
            l = acc[:, AUG_C:AUG_C + 1]
            outs.append(acc / l)
            lse_ref[a] = jnp.broadcast_to(mx + jnp.log(l), (tq, LANES))
        o = jnp.where(_half((tq, LANES), 0), outs[0], pltpu.roll(outs[1], ATT_HEAD, 1))
        o_ref[...] = o.astype(BF16)
        og_ref[...] = (o * _sigmoid(g_ref[...].astype(F32))).astype(BF16)

    pair = pl.BlockSpec((tq, LANES), lambda p, i: (i, p))
    return pl.pallas_call(
        body, name=name, grid=(npair, nt),
        in_specs=[pl.BlockSpec((tq, 2 * LANES), lambda p, i: (i, p)),
                  pl.BlockSpec((t, 2 * LANES), lambda p, i: (0, p)),
                  pl.BlockSpec((t, 2 * LANES), lambda p, i: (0, p)),
                  pl.BlockSpec((tq, LANES), lambda p, i: (i, gcol + p))],
        out_specs=[pair, pair, pl.BlockSpec((2, tq, LANES), lambda p, i: (p, i, 0))],
        out_shape=[jax.ShapeDtypeStruct((t, hd), BF16), jax.ShapeDtypeStruct((t, hd), BF16),
                   jax.ShapeDtypeStruct((nh, t, LANES), F32)],
        compiler_params=_params((_PAR, _PAR)),
    )(qa, ka, va, proj)


def fa_prep_bwd(dog, o, proj, qa, lse, nh, name):
    t, hd = o.shape
    npair = nh // 2
    tr = _tile(t, (256, 128))
    gcol = (3 * hd + FOX_FPAD) // LANES

    def body(d_ref, o_ref, g_ref, q_ref, l_ref, dg_ref, qb_ref, da_ref):
        lane = lax.broadcasted_iota(jnp.int32, (tr, LANES), 1)
        dv, ov = d_ref[...].astype(F32), o_ref[...].astype(F32)
        sg = _sigmoid(g_ref[...].astype(F32))
        do = (dv * sg).astype(BF16).astype(F32)
        dg_ref[...] = (dv * ov * sg * (1.0 - sg)).astype(BF16)
        prod = do * ov
        for a in range(2):
            cols = slice(a * LANES, (a + 1) * LANES)
            delta = jnp.sum(jnp.where(_half(prod.shape, a), prod, 0.0), axis=1, keepdims=True)
            da_ref[:, cols] = _aug(_head_of_pair(do, a), lane, [(AUG_C, _split3(-delta))])
            nl = _split3(-l_ref[a][:, 0:1])
            qb = q_ref[:, cols].astype(F32)
            for k in range(3):
                qb = jnp.where(lane == AUG_LSE + k, nl[k], qb)
            qb_ref[:, cols] = qb.astype(BF16)

    pair = pl.BlockSpec((tr, LANES), lambda p, i: (i, p))
    wide = pl.BlockSpec((tr, 2 * LANES), lambda p, i: (i, p))
    return pl.pallas_call(
        body, name=name, grid=(npair, t // tr),
        in_specs=[pair, pair, pl.BlockSpec((tr, LANES), lambda p, i: (i, gcol + p)), wide,
                  pl.BlockSpec((2, tr, LANES), lambda p, i: (p, i, 0))],
        out_specs=[pair, wide, wide],
        out_shape=[jax.ShapeDtypeStruct((t, hd), BF16), jax.ShapeDtypeStruct((t, nh * LANES), BF16),
                   jax.ShapeDtypeStruct((t, nh * LANES), BF16)],
        compiler_params=_params((_PAR, _PAR)),
    )(dog, o, proj, qa, lse)


def fa_dq(qb, ka, va, da, nh, name):
    t = qb.shape[0]
    hd = nh * ATT_HEAD
    npair = nh // 2
    tq, tk = _fox_tiles(t, 1024)
    nt, ratio = t // tq, tq // tk
    scale = ATT_HEAD ** -0.5

    def body(q_ref, k_ref, v_ref, d_ref, dq_ref, rs_ref):
        i = pl.program_id(1)
        heads = [slice(a * LANES, (a + 1) * LANES) for a in range(2)]
        qs = [q_ref[:, cols] for cols in heads]
        ds = [d_ref[:, cols] for cols in heads]

        def tile(j, carry, diag):
            rows = pl.ds(pl.multiple_of(j * tk, tk), tk)
            out = []
            for acc, q, d, cols in zip(carry, qs, ds, heads):
                kj = k_ref[rows, cols]
                s = _dot_nt(q, kj)
                if diag is not None:
                    s = jnp.where(_diag_mask(tq, tk, diag), s, NEG)
                dsv = jnp.exp(s) * _dot_nt(d, v_ref[rows, cols])
                out.append(acc + _dot_nn(dsv.astype(BF16), kj))
            return tuple(out)

        accs = lax.fori_loop(0, i * ratio, functools.partial(tile, diag=None), (jnp.zeros((tq, LANES), F32),) * 2)
        for d in range(ratio):
            accs = tile(i * ratio + d, accs, d)
        dq_ref[...] = (jnp.where(_half((tq, LANES), 0), accs[0], pltpu.roll(accs[1], ATT_HEAD, 1)) * scale).astype(BF16)
        lane = lax.broadcasted_iota(jnp.int32, (tq, LANES), 1)
        rs_ref[...] = jnp.where(lane == 0, accs[0][:, AUG_C:AUG_C + 1],
                                jnp.where(lane == 1, accs[1][:, AUG_C:AUG_C + 1], 0.0))

    wide = pl.BlockSpec((tq, 2 * LANES), lambda p, i: (i, p))
    resident = pl.BlockSpec((t, 2 * LANES), lambda p, i: (0, p))
    pair = pl.BlockSpec((tq, LANES), lambda p, i: (i, p))
    return pl.pallas_call(
        body, name=name, grid=(npair, nt),
        in_specs=[wide, resident, resident, wide],
        out_specs=[pair, pair],
        out_shape=[jax.ShapeDtypeStruct((t, hd), BF16), jax.ShapeDtypeStruct((t, npair * LANES), F32)],
        compiler_params=_params((_PAR, _PAR)),
    )(qb, ka, va, da)


def fa_dkv(qb, ka, va, da, nh, name):
    t = qb.shape[0]
    hd = nh * ATT_HEAD
    npair = nh // 2
    tk, tq = _fox_tiles(t, 1024)
    nt, ratio = t // tk, tk // tq

    def body(q_ref, k_ref, v_ref, d_ref, dk_ref, dv_ref, cs_ref):
        j = pl.program_id(1)
        heads = [slice(a * LANES, (a + 1) * LANES) for a in range(2)]
        ks = [k_ref[:, cols] for cols in heads]
        vs = [v_ref[:, cols] for cols in heads]

        def tile(i, carry, diag):
            rows = pl.ds(pl.multiple_of(i * tq, tq), tq)
            out = []
            for (dk, dv), k, v, cols in zip(carry, ks, vs, heads):
                qi, di = q_ref[rows, cols], d_ref[rows, cols]
                st = _dot_nt(k, qi)
                if diag is not None:
                    st = jnp.where(_diag_mask(tk, tq, diag, True), st, NEG)
                pt = jnp.exp(st)
                dst = pt * _dot_nt(v, di)
                out.append((dk + _dot_nn(dst.astype(BF16), qi), dv + _dot_nn(pt.astype(BF16), di)))
            return tuple(out)

        zero = jnp.zeros((tk, LANES), F32)
        carry = ((zero, zero),) * 2
        for d in range(ratio):
            carry = tile(j * ratio + d, carry, d)
        carry = lax.fori_loop((j + 1) * ratio, t // tq, functools.partial(tile, diag=None), carry)
        dks, dvs = [c[0] for c in carry], [c[1] for c in carry]
        first = _half((tk, LANES), 0)
        dk_ref[...] = jnp.where(first, dks[0], pltpu.roll(dks[1], ATT_HEAD, 1)).astype(BF16)
        dv_ref[...] = jnp.where(first, dvs[0], pltpu.roll(dvs[1], ATT_HEAD, 1)).astype(BF16)
        lane = lax.broadcasted_iota(jnp.int32, (tk, LANES), 1)
        cs_ref[...] = jnp.where(lane == 0, dks[0][:, AUG_ONE:AUG_ONE + 1],
                                jnp.where(lane == 1, dks[1][:, AUG_ONE:AUG_ONE + 1], 0.0))

    wide = pl.BlockSpec((tk, 2 * LANES), lambda p, j: (j, p))
    resident = pl.BlockSpec((t, 2 * LANES), lambda p, j: (0, p))
    pair = pl.BlockSpec((tk, LANES), lambda p, j: (j, p))
    return pl.pallas_call(
        body, name=name, grid=(npair, nt),
        in_specs=[resident, wide, wide, resident],
        out_specs=[pair, pair, pair],
        out_shape=[jax.ShapeDtypeStruct((t, hd), BF16), jax.ShapeDtypeStruct((t, hd), BF16),
                   jax.ShapeDtypeStruct((t, npair * LANES), F32)],
        compiler_params=_params((_PAR, _PAR)),
    )(qb, ka, va, da)


def fox_pad_w_in(w_in, nh):
    hd = nh * ATT_HEAD
    pad = jnp.zeros((w_in.shape[0], FOX_FPAD - nh), w_in.dtype)
    return jnp.concatenate([w_in[:, :3 * hd + nh], pad, w_in[:, 3 * hd + nh:]], axis=1)


def fox_unpad_dw(dw, nh):
    hd = nh * ATT_HEAD
    return jnp.concatenate([dw[:, :3 * hd + nh], dw[:, 3 * hd + FOX_FPAD:]], axis=1)


def _pad_lanes(v):
    return jnp.pad(v.reshape(1, -1).astype(F32), ((0, 0), (0, LANES - v.size)))


def fox_mixer_fwd(hn, w_pad, w_out, b_f, name):
    nh = b_f.shape[0]
    hd = nh * ATT_HEAD
    proj = mm(hn, w_pad, "nn", BF16, name + "_in")
    fl = mm(hn, w_pad[:, 3 * hd:3 * hd + LANES], "nn", F32, name + "_fl")
    c = fox_cumsum(fl, _pad_lanes(b_f), name + "_cum")
    qa, ka, va = fa_prep(proj, c, nh, name + "_prep")
    o, og, lse = fa_fwd(qa, ka, va, proj, nh, name + "_att")
    y = mm(og, w_out, "nn", F32, name + "_out")
    return y, (proj, fl, qa, ka, va, o, og, lse)


def fox_mixer_bwd(dy, hn, w_pad, w_out, b_f, saved, name):
    proj, fl, qa, ka, va, o, og, lse = saved
    nh = b_f.shape[0]
    t = hn.shape[0]
    dwo = wgrad(og, dy, name + "_dwo")
    dog = mm(dy, w_out, "nt", BF16, name + "_dog")
    dg, qb, da = fa_prep_bwd(dog, o, proj, qa, lse, nh, name + "_prepb")
    dq, rsum = fa_dq(qb, ka, va, da, nh, name + "_dq")
    dk, dv, csum = fa_dkv(qb, ka, va, da, nh, name + "_dkv")
    dc = (rsum - csum).reshape(t, nh // 2, LANES)[:, :, :2].reshape(t, nh)
    dc = jnp.pad(dc, ((0, 0), (0, LANES - nh)))
    dfl, db = fox_cumsum_bwd(dc, fl, _pad_lanes(b_f), name + "_cumb")
    dfl = jnp.pad(dfl, ((0, 0), (0, FOX_FPAD - LANES)))
    dproj = jnp.concatenate([dq, dk, dv, dfl, dg], axis=1)
    dwi = wgrad(hn, dproj, name + "_dwi")
    dhn = mm(dproj, w_pad, "nt", BF16, name + "_dhn")
    return dhn, dwi, dwo, db[0, :nh]


HGRN_ROWS = 256
HGRN_TOGETHER = 8


def lb_table_row(lb_param, idx, name):
    nrow, w = lb_param.shape

    def body(p_ref, o_ref):
        rows = [p_ref[r:r + 1, :] for r in range(nrow)]
        mx = functools.reduce(jnp.maximum, rows)
        es = [jnp.exp(r - mx) for r in rows]
        o_ref[...] = sum(es[:idx + 1]) / sum(es)

    return pl.pallas_call(
        body, name=name, in_specs=[_full_spec((nrow, w))], out_specs=_full_spec((1, w)), grid=(1,),
        out_shape=jax.ShapeDtypeStruct((1, w), F32),
    )(lb_param)


def lb_table_row_bwd(lb_param, dlb, idx, name):
    nrow, w = lb_param.shape

    def body(p_ref, d_ref, o_ref):
        rows = [p_ref[r:r + 1, :] for r in range(nrow)]
        mx = functools.reduce(jnp.maximum, rows)
        es = [jnp.exp(r - mx) for r in rows]
        tot = sum(es)
        ps = [e / tot for e in es]
        dv = d_ref[...]
        inner = sum(ps[:idx + 1]) * dv
        for r in range(nrow):
            o_ref[r:r + 1, :] = ps[r] * ((dv if r <= idx else 0.0) - inner)

    return pl.pallas_call(
        body, name=name, in_specs=[_full_spec((nrow, w)), _full_spec((1, w))], out_specs=_full_spec((nrow, w)),
        grid=(1,), out_shape=jax.ShapeDtypeStruct((nrow, w), F32),
    )(lb_param, dlb)


def _hgrn_gates(qraw, fraw, lb):
    sq = _sigmoid(qraw)
    sf = _sigmoid(fraw)
    f = lb + (1.0 - lb) * sf
    return qraw * sq, sq, sf, f, 1.0 - f


def _hgrn_chunk(q, k, f):
    c = HGRN_CHUNK
    b = _dot_exact(_tri(c, False), jnp.log(f))
    bl = b[c - 1:c, :]
    bm = b[c // 2 - 1:c // 2, :]
    eq, ek = jnp.exp(b - bm), jnp.exp(bm - b)
    eb, el = jnp.exp(b), jnp.exp(bl - b)
    qt, kt = (q * eq).astype(BF16), (k * ek).astype(BF16)
    causal = _tri(c, False) > 0.5
    amat = jnp.where(causal, _dot_nt(qt, kt), 0.0).astype(BF16)
    return amat, qt, kt, (q * eb).astype(BF16), (k * el).astype(BF16), eq, ek, eb, el, jnp.exp(bl), causal


def hgrn_fwd(proj, lb, norm_g, nh, name):
    t = proj.shape[0]
    w = nh * HGRN_HEAD
    c = HGRN_CHUNK
    rows = _tile(t, (HGRN_ROWS, 128, 64, 32))
    nr, nc = t // rows, rows // c
    hp = _tile(nh, (HGRN_TOGETHER, 2, 1))
    wide = hp * HGRN_HEAD

    def body(q_ref, f_ref, i_ref, g_ref, lb_ref, ng_ref, og_ref, o_ref, st_ref, state):
        @pl.when(pl.program_id(1) == 0)
        def _():
            state[...] = jnp.zeros_like(state)

        def step(cc, carry):
            sl = pl.ds(pl.multiple_of(cc * c, c), c)
            for a in range(hp):
                hc = slice(a * HGRN_HEAD, (a + 1) * HGRN_HEAD)
                q, _, _, f, k = _hgrn_gates(q_ref[sl, hc].astype(F32), f_ref[sl, hc].astype(F32), lb_ref[:, hc])
                v = i_ref[sl, hc]
                amat, _, _, qd, kd, _, _, _, _, ebl, _ = _hgrn_chunk(q, k, f)
                st = state[a]
                st_ref[a, cc] = st.astype(BF16)
                o = _dot_nt(qd, st.astype(BF16)) + _dot_nn(amat, v)
                state[a] = st * ebl + _dot_tn(v, kd)
                o_ref[sl, hc] = o
                graw = g_ref[sl, hc].astype(F32)
                og_ref[sl, hc] = (o * _rstd(o) * ng_ref[...] * (graw * _sigmoid(graw))).astype(BF16)
            return carry

        lax.fori_loop(0, nc, step, 0)

    ng = nh // hp
    col = lambda off: pl.BlockSpec((rows, wide), lambda h, r: (r, off + h))
    return pl.pallas_call(
        body, name=name, grid=(ng, nr),
        in_specs=[col(0), col(ng), col(2 * ng), col(3 * ng),
                  pl.BlockSpec((1, wide), lambda h, r: (0, h)), _full_spec((1, HGRN_HEAD))],
        out_specs=[col(0), col(0), pl.BlockSpec((hp, nc, HGRN_HEAD, HGRN_HEAD), lambda h, r: (h, r, 0, 0))],
        out_shape=[jax.ShapeDtypeStruct((t, w), BF16), jax.ShapeDtypeStruct((t, w), F32),
                   jax.ShapeDtypeStruct((nh, t // c, HGRN_HEAD, HGRN_HEAD), BF16)],
        scratch_shapes=[pltpu.VMEM((hp, HGRN_HEAD, HGRN_HEAD), F32)],
        compiler_params=_params((_PAR, _ARB)),
    )(proj, proj, proj, proj, lb, norm_g)


def hgrn_bwd(proj, lb, norm_g, o, states, dog, nh, name):
    t = proj.shape[0]
    w = nh * HGRN_HEAD
    c = HGRN_CHUNK
    rows = _tile(t, (HGRN_ROWS, 128, 64, 32))
    nr, nc = t // rows, rows // c
    hp = _tile(nh, (HGRN_TOGETHER, 2, 1))
    wide = hp * HGRN_HEAD

    def body(q_ref, f_ref, i_ref, g_ref, lb_ref, ng_ref, o_ref, st_ref, dog_ref,
             dq_ref, df_ref, di_ref, dg_ref, dlb_ref, dng_ref, dstate):
        @pl.when(pl.program_id(1) == 0)
        def _():
            dstate[...] = jnp.zeros_like(dstate)
            dlb_ref[...] = jnp.zeros_like(dlb_ref)
            dng_ref[...] = jnp.zeros_like(dng_ref)

        ng = ng_ref[...]

        def step(idx, carry):
            cc = nc - 1 - idx
            sl = pl.ds(pl.multiple_of(cc * c, c), c)
            for a in range(hp):
                hc = slice(a * HGRN_HEAD, (a + 1) * HGRN_HEAD)
                lb = lb_ref[:, hc]
                qraw, fraw = q_ref[sl, hc].astype(F32), f_ref[sl, hc].astype(F32)
                q, sq, sf, f, k = _hgrn_gates(qraw, fraw, lb)
                v = i_ref[sl, hc]
                amat, qt, kt, qd, kd, eq, ek, eb, el, ebl, causal = _hgrn_chunk(q, k, f)
                ov = o_ref[sl, hc]
                graw = g_ref[sl, hc].astype(F32)
                dogv = dog_ref[sl, hc].astype(F32)
                sil, dsil = _silu_and_grad(graw)
                r = _rstd(ov)
                oh = ov * r
                don = dogv * sil
                dg_ref[sl, hc] = (dogv * oh * ng * dsil).astype(BF16)
                dng_ref[a] += jnp.sum(don * oh, axis=0, keepdims=True)
                doh = don * ng
                do = (r * (doh - oh * jnp.mean(doh * oh, axis=1, keepdims=True))).astype(BF16)
                dst = dstate[a]
                dstb = dst.astype(BF16)
                da = jnp.where(causal, _dot_nt(do, v), 0.0).astype(BF16)
                dv = _dot_tn(amat, do) + _dot_nt(kd, dstb)
                st0 = st_ref[a, cc]
                dq = _dot_nn(da, kt) * eq + _dot_nn(do, st0) * eb
                dk_inter = _dot_nn(v, dstb) * el
                dk = _dot_tn(da, qt) * ek + dk_inter
                dstate[a] = dst * ebl + _dot_tn(do, qd)
                through = jnp.sum(dst * st0.astype(F32), axis=0, keepdims=True) * ebl
                later = jnp.sum(k * dk_inter, axis=0, keepdims=True) + through
                dlf = _dot_exact(_tri(c, True), q * dq - k * dk) + later
                df = dlf / f - dk
                dq_ref[sl, hc] = (dq * (sq * (1.0 + qraw * (1.0 - sq)))).astype(BF16)
                df_ref[sl, hc] = (df * (1.0 - lb) * sf * (1.0 - sf)).astype(BF16)
                di_ref[sl, hc] = dv.astype(BF16)
                dlb_ref[:, hc] += jnp.sum(df * (1.0 - sf), axis=0, keepdims=True)
            return carry

        lax.fori_loop(0, nc, step, 0)

    ngr = nh // hp
    col = lambda off: pl.BlockSpec((rows, wide), lambda h, r: (nr - 1 - r, off + h))
    out = col(0)
    return pl.pallas_call(
        body, name=name, grid=(ngr, nr),
        in_specs=[col(0), col(ngr), col(2 * ngr), col(3 * ngr),
                  pl.BlockSpec((1, wide), lambda h, r: (0, h)), _full_spec((1, HGRN_HEAD)),
                  out, pl.BlockSpec((hp, nc, HGRN_HEAD, HGRN_HEAD), lambda h, r: (h, nr - 1 - r, 0, 0)), out],
        out_specs=[out, out, out, out, pl.BlockSpec((1, wide), lambda h, r: (0, h)),
                   pl.BlockSpec((hp, 1, HGRN_HEAD), lambda h, r: (h, 0, 0))],
        out_shape=[jax.ShapeDtypeStruct((t, w), BF16)] * 4 + [jax.ShapeDtypeStruct((1, w), F32),
                                                             jax.ShapeDtypeStruct((nh, 1, HGRN_HEAD), F32)],
        scratch_shapes=[pltpu.VMEM((hp, HGRN_HEAD, HGRN_HEAD), F32)],
        compiler_params=_params((_PAR, _ARB)),
    )(proj, proj, proj, proj, lb, norm_g, o, states, dog)


def hgrn_mixer_fwd(hn, w_in, w_out, norm_g, lb_param, idx, name):
    nh = w_out.shape[0] // HGRN_HEAD
    lb = lb_table_row(lb_param, idx, name + "_lb")
    proj = mm(hn, w_in, "nn", BF16, name + "_in")
    og, o, states = hgrn_fwd(proj, lb, norm_g, nh, name + "_rec")
    y = mm(og, w_out, "nn", F32, name + "_out")
    return y, (proj, lb, og, o, states)


def hgrn_mixer_bwd(dy, hn, w_in, w_out, norm_g, lb_param, idx, saved, name):
    proj, lb, og, o, states = saved
    nh = w_out.shape[0] // HGRN_HEAD
    dwo = wgrad(og, dy, name + "_dwo")
    dog = mm(dy, w_out, "nt", BF16, name + "_dog")
    dq, df, di, dg, dlb, dng = hgrn_bwd(proj, lb, norm_g, o, states, dog, nh, name + "_recb")
    dproj = jnp.concatenate([dq, df, di, dg], axis=1)
    dwi = wgrad(hn, dproj, name + "_dwi")
    dhn = mm(dproj, w_in, "nt", BF16, name + "_dhn")
    dlbp = lb_table_row_bwd(lb_param, dlb, idx, name + "_lbb")
    return dhn, dwi, dwo, jnp.sum(dng, axis=0), dlbp


_ANY = pl.BlockSpec(memory_space=pl.ANY)
_MESH = pl.DeviceIdType.MESH


def _place():
    x, y, c = lax.axis_index("x"), lax.axis_index("y"), lax.axis_index("c")
    chips = [(1 - x, y), (x, 1 - y), (1 - x, 1 - y)]
    return x, y, c, N_CHIPS // 2 * x + y, chips


def _chip_index(chip):
    return N_CHIPS // 2 * chip[0] + chip[1]


def _window(ref, axis, start, size):
    idx = [slice(None)] * len(ref.shape)
    idx[axis] = pl.ds(start, size)
    return ref.at[tuple(idx)]


_HBM = pl.BlockSpec(memory_space=pltpu.HBM)
_SEMS = pl.BlockSpec(memory_space=pltpu.SEMAPHORE)
_DATAFLOW = pltpu.SideEffectType.DATAFLOW_SIDE_EFFECTING


def _exchange_copy(src, land, axis, sems, k, j, chip, c, sender_side):
    x, y, _, me, _ = _place()
    peer = _chip_index(chip)
    if axis is None:
        src_part = src.at[peer]
        land_part = land.at[me if sender_side else peer]
    else:
        size = src.shape[axis]
        src_part = src
        land_part = _window(land, axis, (me if sender_side else peer) * size, size)
    which = k * (N_CHIPS - 1) + j
    return pltpu.make_async_remote_copy(src_ref=src_part, dst_ref=land_part, send_sem=sems[0].at[which],
                                        recv_sem=sems[1].at[which], device_id=(chip[0], chip[1], c),
                                        device_id_type=_MESH)


def place_own(shard, axis):
    _, _, _, me, _ = _place()
    shape = list(shard.shape)
    shape[axis] *= N_CHIPS
    return lax.dynamic_update_slice_in_dim(lax.empty(tuple(shape), shard.dtype), shard, me * shard.shape[axis], axis)


def cast_and_place(shards, layer, axis, name):
    _, rows, cols = shards.shape
    tr = _row_tile(rows, cols, 1 << 19)
    nrb = rows // tr
    _, _, _, me, _ = _place()
    full = (rows * N_CHIPS, cols) if axis == 0 else (rows, cols * N_CHIPS)

    def body(me_ref, s_ref, b_ref, land_ref):
        val = s_ref[...].astype(BF16)
        b_ref[...] = val
        land_ref[...] = val

    window = (lambda i, m: (m[0] * nrb + i, 0)) if axis == 0 else (lambda i, m: (i, m[0]))
    grid_spec = pltpu.PrefetchScalarGridSpec(
        num_scalar_prefetch=1, grid=(nrb,),
        in_specs=[pl.BlockSpec((None, tr, cols), lambda i, m: (layer, i, 0))],
        out_specs=[pl.BlockSpec((tr, cols), lambda i, m: (i, 0)), pl.BlockSpec((tr, cols), window)])
    return pl.pallas_call(
        body, name=name, grid_spec=grid_spec,
        out_shape=[jax.ShapeDtypeStruct((rows, cols), BF16), jax.ShapeDtypeStruct(full, BF16)],
        compiler_params=_params((_PAR,)),
    )(jnp.reshape(me, (1,)).astype(jnp.int32), shards)


def place_own_slot(p):
    _, _, _, me, _ = _place()
    mine = lax.dynamic_index_in_dim(p, me, 0, keepdims=True)
    return lax.dynamic_update_slice_in_dim(lax.empty(p.shape, p.dtype), mine, me, 0)


def exchange_start(srcs, lands, axes, name):
    n = len(srcs)

    def body(*refs):
        ins, lnd, sems = refs[:n], refs[n:2 * n], refs[2 * n:2 * n + 2]
        _, _, c, _, chips = _place()
        for k in range(n):
            for j, chip in enumerate(chips):
                _exchange_copy(ins[k], lnd[k], axes[k], sems, k, j, chip, c, True).start()
        refs[-1][...] = jnp.zeros_like(refs[-1])

    sem = pltpu.SemaphoreType.DMA((n * (N_CHIPS - 1),))
    arrays = list(srcs) + list(lands)
    out = pl.pallas_call(
        body, name=name,
        in_specs=[_HBM] * (2 * n),
        out_specs=(_SEMS, _SEMS) + (_HBM,) * (2 * n) + (pl.BlockSpec(memory_space=pltpu.VMEM),),
        out_shape=(sem, sem) + tuple(pltpu.HBM(a.shape, a.dtype) for a in arrays)
        + (jax.ShapeDtypeStruct((8, LANES), F32),),
        input_output_aliases={i: 2 + i for i in range(2 * n)},
        compiler_params=pltpu.CompilerParams(has_side_effects=_DATAFLOW),
    )(*[pltpu.with_memory_space_constraint(a, pltpu.HBM) for a in arrays])
    return (out[0], out[1], list(out[2:2 + n]), list(out[2 + n:2 + 2 * n]), list(axes)), out[-1]


def exchange_wait(started, ks, after, name):
    send, recv, srcs, lands, axes = started
    m = len(ks)
    after = list(after) if isinstance(after, (list, tuple)) else [after]

    def body(*refs):
        ins, lnd, sems = refs[:m], refs[m:2 * m], refs[2 * m:2 * m + 2]
        _, _, c, _, chips = _place()
        for q, k in enumerate(ks):
            for j, chip in enumerate(chips):
                cp = _exchange_copy(ins[q], lnd[q], axes[k], sems, k, j, chip, c, False)
                cp.wait_send()
                cp.wait_recv()

    arrays = [srcs[k] for k in ks] + [lands[k] for k in ks]
    out = pl.pallas_call(
        body, name=name,
        in_specs=[_HBM] * (2 * m) + [_SEMS, _SEMS] + [_ANY] * len(after),
        out_specs=(_HBM,) * (2 * m),
        out_shape=tuple(pltpu.HBM(a.shape, a.dtype) for a in arrays),
        input_output_aliases={i: i for i in range(2 * m)},
        compiler_params=pltpu.CompilerParams(has_side_effects=_DATAFLOW),
    )(*arrays, send, recv, *after)
    return list(out[m:])


def allreduce_small(buf, name):
    rows = buf.shape[0]
    n_dev = 2 * N_CHIPS

    def body(in_ref, out_ref, slots, send, recv):
        x, y, c, me, chips = _place()
        my_id = 2 * me + c
        slots[my_id] = in_ref[...]
        for j in range(1, n_dev):
            fx, fy, fc = (j >> 2) & 1, (j >> 1) & 1, j & 1
            peer = ((1 - x) if fx else x, (1 - y) if fy else y, (1 - c) if fc else c)
            pltpu.make_async_remote_copy(
                src_ref=in_ref, dst_ref=slots.at[my_id], send_sem=send.at[j], recv_sem=recv.at[j],
                device_id=peer, device_id_type=_MESH).start()
        for j in range(1, n_dev):
            fx, fy, fc = (j >> 2) & 1, (j >> 1) & 1, j & 1
            peer = ((1 - x) if fx else x, (1 - y) if fy else y, (1 - c) if fc else c)
            peer_id = 2 * _chip_index(peer) + peer[2]
            landed = pltpu.make_async_remote_copy(
                src_ref=in_ref, dst_ref=slots.at[peer_id], send_sem=send.at[j], recv_sem=recv.at[j],
                device_id=peer, device_id_type=_MESH)
            landed.wait_recv()
            landed.wait_send()
        tot = slots[0]
        for d in range(1, n_dev):
            tot = tot + slots[d]
        out_ref[...] = tot

    return pl.pallas_call(
        body, name=name,
        in_specs=[pl.BlockSpec(memory_space=pltpu.VMEM)], out_specs=pl.BlockSpec(memory_space=pltpu.VMEM),
        out_shape=jax.ShapeDtypeStruct(buf.shape, F32),
        scratch_shapes=[pltpu.VMEM((n_dev, rows, LANES), F32), pltpu.SemaphoreType.DMA((n_dev,)),
                        pltpu.SemaphoreType.DMA((n_dev,))],
        compiler_params=pltpu.CompilerParams(vmem_limit_bytes=VMEM_LIMIT),
    )(buf)


def _sibling_step(src_ref, slots, send, recv, credit, step, n_steps):
    x, y, c = lax.axis_index("x"), lax.axis_index("y"), lax.axis_index("c")
    slot = step % 2

    @pl.when(step >= 2)
    def _():
        pl.semaphore_wait(credit, 1)

    cp = pltpu.make_async_remote_copy(src_ref=src_ref, dst_ref=slots.at[slot], send_sem=send.at[slot],
                                      recv_sem=recv.at[slot], device_id=(x, y, 1 - c), device_id_type=_MESH)
    cp.start()
    cp.wait_recv()
    return cp, slot


def _sibling_done(cp, credit, step, n_steps):
    x, y, c = lax.axis_index("x"), lax.axis_index("y"), lax.axis_index("c")
    cp.wait_send()

    @pl.when(step < n_steps - 2)
    def _():
        pl.semaphore_signal(credit, 1, device_id=(x, y, 1 - c), device_id_type=_MESH)


def pair_sum(g, name):
    by_cols = g.ndim == 2
    s = N_CHIPS if by_cols else g.shape[0]
    r = g.shape[-2]
    cols = g.shape[-1] // s if by_cols else g.shape[-1]
    half = r // 2
    tr = _row_tile(half, cols, 1 << 20)
    nt = half // tr
    n_steps = s * nt

    def body(g_ref, o_ref, slots, send, recv, credit):
        c = lax.axis_index("c")
        step = pl.program_id(0) * nt + pl.program_id(1)
        cp, slot = _sibling_step(g_ref.at[0, 1 - c], slots, send, recv, credit, step, n_steps)
        o_ref[0] = (g_ref[0, c].astype(F32) + slots[slot].astype(F32)).astype(BF16)
        _sibling_done(cp, credit, step, n_steps)

    if by_cols:
        in_spec = pl.BlockSpec((1, 2, tr, cols), lambda k, i: (0, 0, i, k))
        g4 = g.reshape(1, 2, half, s * cols)
    else:
        in_spec = pl.BlockSpec((1, 2, tr, cols), lambda k, i: (k, 0, i, 0))
        g4 = g.reshape(s, 2, half, cols)
    return pl.pallas_call(
        body, name=name, grid=(s, nt),
        in_specs=[in_spec],
        out_specs=pl.BlockSpec((1, tr, cols), lambda k, i: (k, i, 0)),
        out_shape=jax.ShapeDtypeStruct((s, half, cols), BF16),
        scratch_shapes=[pltpu.VMEM((2, tr, cols), BF16), pltpu.SemaphoreType.DMA((2,)), pltpu.SemaphoreType.DMA((2,)),
                        pltpu.SemaphoreType.REGULAR],
        compiler_params=_params((_ARB, _ARB)),
    )(g4)


def chip_sum_share(q, acc, layer, name, after):
    s, r2, cols = q.shape
    tr = _row_tile(r2, cols, 1 << 19)
    nt = r2 // tr

    def body(q_ref, acc_ref, after_ref, o_ref, slots, send, recv, credit):
        c = lax.axis_index("c")
        step = pl.program_id(0)
        tot = q_ref[0].astype(F32)
        for k in range(1, s):
            tot = tot + q_ref[k].astype(F32)
        o_ref[0, c] = tot
        cp, slot = _sibling_step(o_ref.at[0, c], slots, send, recv, credit, step, nt)
        o_ref[0, 1 - c] = slots[slot]
        _sibling_done(cp, credit, step, nt)

    return pl.pallas_call(
        body, name=name, grid=(nt,),
        in_specs=[pl.BlockSpec((s, tr, cols), lambda i: (0, i, 0)), _ANY, _ANY],
        out_specs=pl.BlockSpec((1, 2, tr, cols), lambda i: (layer, 0, i, 0)),
        out_shape=jax.ShapeDtypeStruct(acc.shape, F32),
        input_output_aliases={1: 0},
        scratch_shapes=[pltpu.VMEM((2, tr, cols), F32), pltpu.SemaphoreType.DMA((2,)), pltpu.SemaphoreType.DMA((2,)),
                        pltpu.SemaphoreType.REGULAR],
        compiler_params=_params((_ARB,)),
    )(q, acc, after)


def _row_tile(rows, cols, budget):
    for tr in (1024, 512, 256, 128, 64, 32, 16, 8):
        if rows % tr == 0 and tr * cols <= budget:
            return tr
    return rows


def adamw(w, g, m, v, name, after):
    rows, cols = w.shape
    tr = _row_tile(rows, cols, 1 << 18)
    c1 = 1.0 - ADAM_B1 ** ADAM_STEP
    c2 = 1.0 - ADAM_B2 ** ADAM_STEP

    def body(w_ref, g_ref, m_ref, v_ref, after_ref, d_ref, nm_ref, nv_ref):
        gv = g_ref[...]
        nm = ADAM_B1 * m_ref[...] + (1.0 - ADAM_B1) * gv
        nv = ADAM_B2 * v_ref[...] + (1.0 - ADAM_B2) * (gv * gv)
        nm_ref[...] = nm
        nv_ref[...] = nv
        d_ref[...] = -ADAM_LR * ((nm / c1) / (jnp.sqrt(nv / c2) + ADAM_EPS) + ADAM_WD * w_ref[...])

    spec = pl.BlockSpec((tr, cols), lambda i: (i, 0))
    return pl.pallas_call(
        body, name=name, grid=(rows // tr,),
        in_specs=[spec] * 4 + [_ANY], out_specs=[spec] * 3,
        out_shape=[jax.ShapeDtypeStruct((rows, cols), F32)] * 3,
        compiler_params=_params((_PAR,)),
    )(w, g, m, v, after)


WEIGHTS = ("mix_pre_g", "mix_post_g", "ffn_pre_g", "ffn_post_g", "hgrn_w_in", "hgrn_w_out", "hgrn_norm_g",
           "hgrn_lb_param", "swa_w_in", "swa_w_out", "swa_sinks", "sc_w_in", "sc_conv_w", "sc_w_out", "fox_w_in",
           "fox_b_f", "fox_w_out", "ffn_w_up", "ffn_conv_w", "ffn_conv_b", "ffn_w_down")
N_MIXERS = 4


def _pack_small(parts):
    flat = jnp.concatenate([p.reshape(-1).astype(F32) for p in parts])
    rows = -(-flat.shape[0] // (8 * LANES)) * 8
    return jnp.pad(flat, (0, rows * LANES - flat.shape[0])).reshape(rows, LANES)


def _unpack_small(buf, shapes):
    flat, out, off = buf.reshape(-1), [], 0
    for s in shapes:
        n = math.prod(s)
        out.append(flat[off:off + n].reshape(s))
        off += n
    return out


def _stack_rows(dw):
    return dw.reshape(N_CHIPS, dw.shape[0] // N_CHIPS, dw.shape[1])


def kernel(x, positions, mix_pre_g, mix_post_g, ffn_pre_g, ffn_post_g, hgrn_w_in, hgrn_w_out, hgrn_norm_g, hgrn_lb_param, swa_w_in, swa_w_out, swa_sinks, sc_w_in, sc_conv_w, sc_w_out, fox_w_in, fox_b_f, fox_w_out, ffn_w_up, ffn_conv_w, ffn_conv_b, ffn_w_down, loss_target, m_mix_pre_g, m_mix_post_g, m_ffn_pre_g, m_ffn_post_g, m_hgrn_w_in, m_hgrn_w_out, m_hgrn_norm_g, m_hgrn_lb_param, m_swa_w_in, m_swa_w_out, m_swa_sinks, m_sc_w_in, m_sc_conv_w, m_sc_w_out, m_fox_w_in, m_fox_b_f, m_fox_w_out, m_ffn_w_up, m_ffn_conv_w, m_ffn_conv_b, m_ffn_w_down, v_mix_pre_g, v_mix_post_g, v_ffn_pre_g, v_ffn_post_g, v_hgrn_w_in, v_hgrn_w_out, v_hgrn_norm_g, v_hgrn_lb_param, v_swa_w_in, v_swa_w_out, v_swa_sinks, v_sc_w_in, v_sc_conv_w, v_sc_w_out, v_fox_w_in, v_fox_b_f, v_fox_w_out, v_ffn_w_up, v_ffn_conv_w, v_ffn_conv_b, v_ffn_w_down):
    given = dict(locals())
    depth = mix_pre_g.shape[0]
    assert depth == N_MIXERS and x.shape[0] == 1, "one batch element per device, one layer of each mixer"
    xi, target = x[0], loss_target[0]
    chip = N_CHIPS // 2 * lax.axis_index("x") + lax.axis_index("y")
    nh_fox = fox_b_f.shape[1]
    row = lambda a, i: a[i:i + 1]

    units = {"hg_in": (hgrn_w_in, 0, 1), "hg_out": (hgrn_w_out, 0, 0), "sw_in": (swa_w_in, 0, 1),
             "sw_out": (swa_w_out, 0, 0), "sc_in": (sc_w_in, 0, 1), "sc_out": (sc_w_out, 0, 0),
             "fx_in": (fox_w_in, 0, 0), "fx_out": (fox_w_out, 0, 0)}
    mix_units = (("hg_in", "hg_out"), ("sw_in", "sw_out"), ("sc_in", "sc_out", "sc_cw"), ("fx_in", "fx_out"))
    ffn_units = []
    for i in range(depth):
        units[f"up{i}"], units[f"down{i}"] = (ffn_w_up, i, 1), (ffn_w_down, i, 0)
        ffn_units.append((f"up{i}", f"down{i}") + (("f_cw",) if i == 0 else ()))
    order = [n for i in range(depth) for n in mix_units[i] + ffn_units[i]]
    placed = {n: cast_and_place(*units[n], "place_" + n) + (units[n][2],) for n in units}
    placed["f_cw"] = (ffn_conv_w, place_own(ffn_conv_w, 2), 2)
    placed["sc_cw"] = (sc_conv_w[0], place_own(sc_conv_w[0], 1), 1)
    gather, _ = exchange_start([placed[n][0] for n in order], [placed[n][1] for n in order],
                               [placed[n][2] for n in order], "gather_start")
    wt = {}

    def arrive(names, after, name):
        wt.update(zip(names, exchange_wait(gather, [order.index(n) for n in names], after, name)))

    saved = []
    xs = xi
    hn = rms_fwd(xs, row(mix_pre_g, 0), "pre_norm0")
    dx = loss = None
    for i in range(depth):
        nm = f"l{i}"
        arrive(mix_units[i], hn, nm + "_w_mix")
        if i == 0:
            y, sv = hgrn_mixer_fwd(hn, wt["hg_in"], wt["hg_out"], hgrn_norm_g, hgrn_lb_param, i, nm + "_hgrn")
        elif i == 1:
            y, sv = swa_mixer_fwd(hn, wt["sw_in"], wt["sw_out"], swa_sinks[0], positions, nm + "_swa")
        elif i == 2:
            proj = mm(hn, wt["sc_in"], "nn", BF16, nm + "_sc_in")
            yb = sconv_fwd(proj, wt["sc_cw"], nm + "_sc_conv")
            y, sv = mm(yb, wt["sc_out"], "nn", F32, nm + "_sc_out"), (proj, yb)
        else:
            fx4 = wt["fx_in"].reshape(N_CHIPS, -1, wt["fx_in"].shape[1])
            wt["fx_pad"] = fox_pad_w_in(jnp.concatenate([fx4[s] for s in range(N_CHIPS)], axis=1), nh_fox)
            y, sv = fox_mixer_fwd(hn, wt["fx_pad"], wt["fx_out"], fox_b_f[0], nm + "_fox")
        x1, hn2 = resid_norm(xs, y, row(mix_post_g, i), row(ffn_pre_g, i), nm + "_mix_resid")
        arrive(ffn_units[i], hn2, nm + "_w_ffn")
        z = mm(hn2, wt[f"up{i}"], "nn", BF16, nm + "_ffn_up")
        a = ffn_act(z, wt["f_cw"][i], row(ffn_conv_b, i), nm + "_ffn_act")
        y2 = mm(a, wt[f"down{i}"], "nn", F32, nm + "_ffn_down")
        saved.append((xs, hn, y, sv, x1, hn2, z, a, y2))
        if i < depth - 1:
            xs, hn = resid_norm(x1, y2, row(ffn_post_g, i), row(mix_pre_g, i + 1), nm + "_ffn_resid")
        else:
            dx, loss = resid_loss(x1, y2, row(ffn_post_g, i), target, nm + "_loss")

    grads = {}

    def start_reduce(tag, named):
        ps = [pair_sum(g, f"{tag}_pair_{n}") for n, _, g in named]
        started, token = exchange_start(ps, [place_own_slot(p) for p in ps], [None] * len(ps), tag + "_chips_start")
        return (named, started), token

    def finish_reduce(tag, pending, after):
        named, started = pending
        qs = exchange_wait(started, list(range(len(named))), after, tag + "_chips_wait")
        for (n, l, _), q in zip(named, qs):
            if n not in grads:
                grads[n] = lax.empty((given[n].shape[0], 2) + q.shape[1:], F32)
            grads[n] = chip_sum_share(q, grads[n], l, f"{tag}_share_{n}", after[0])

    def update(n, after):
        w = given[n]
        flat = lambda a: a.reshape(-1, w.shape[-1])
        dl, nm_, nv_ = adamw(flat(w), flat(grads[n]), flat(given["m_" + n]), flat(given["v_" + n]), "adamw_" + n,
                             after)
        deltas[n], new_m[n], new_v[n] = dl.reshape(w.shape), nm_.reshape(w.shape), nv_.reshape(w.shape)

    deltas, new_m, new_v = {}, {}, {}

    d_pre, d_post, d_fpre, d_fpost = [None] * depth, [None] * depth, [None] * depth, [None] * depth
    d_fcw, d_fcb = [None] * depth, [None] * depth
    small = {}
    pending = token = None
    for i in reversed(range(depth)):
        nm = f"l{i}b"
        xs, hn, y, sv, x1, hn2, z, a, y2 = saved[i]
        f_cw = wt["f_cw"][i]
        dy2, d_fpost[i] = norm_bwd(y2, row(ffn_post_g, i), dx, None, BF16, nm + "_ffn_post", after=token)
        d_down = _stack_rows(wgrad(a, dy2, nm + "_dw_down"))
        da = mm(dy2, wt[f"down{i}"], "nt", BF16, nm + "_da")
        du, acc = ffn_act_bwd(z, da, f_cw, row(ffn_conv_b, i), nm + "_ffn_actb")
        d_fcw[i], d_fcb[i] = acc[0:CONV_WIDTH], acc[CONV_WIDTH]
        dz = conv_transpose(du, f_cw, nm + "_ffn_convT")
        d_up = wgrad(hn2, dz, nm + "_dw_up")
        dhn2 = mm(dz, wt[f"up{i}"], "nt", BF16, nm + "_dhn2")
        dx1, d_fpre[i] = norm_bwd(x1, row(ffn_pre_g, i), dhn2, dx, F32, nm + "_ffn_pre")
        dy, d_post[i] = norm_bwd(y, row(mix_post_g, i), dx1, None, BF16, nm + "_mix_post")
        if i == 0:
            dhn, dwi, dwo, small["hgrn_norm_g"], small["hgrn_lb_param"] = hgrn_mixer_bwd(
                dy, hn, wt["hg_in"], wt["hg_out"], hgrn_norm_g, hgrn_lb_param, i, sv, nm + "_hgrn")
            w_in, w_out = "hgrn_w_in", "hgrn_w_out"
        elif i == 1:
            dhn, dwi, dwo, small["swa_sinks"] = swa_mixer_bwd(dy, hn, wt["sw_in"], wt["sw_out"], swa_sinks[0],
                                                             positions, sv, nm + "_swa")
            w_in, w_out = "swa_w_in", "swa_w_out"
        elif i == 2:
            proj, yb = sv
            dwo = wgrad(yb, dy, nm + "_sc_dwo")
            dyb = mm(dy, wt["sc_out"], "nt", BF16, nm + "_sc_dyb")
            dproj, acc = sconv_bwd(proj, dyb, wt["sc_cw"], nm + "_sc_convb")
            dwi = wgrad(hn, dproj, nm + "_sc_dwi")
            dhn = mm(dproj, wt["sc_in"], "nt", BF16, nm + "_sc_dhn")
            small["sc_conv_w"] = acc[0:CONV_WIDTH]
            w_in, w_out = "sc_w_in", "sc_w_out"
        else:
            dhn, dwi, dwo, small["fox_b_f"] = fox_mixer_bwd(dy, hn, wt["fx_pad"], wt["fx_out"], fox_b_f[0], sv,
                                                            nm + "_fox")
            dwi = fox_unpad_dw(dwi, nh_fox)
            cols = dwi.shape[1] // N_CHIPS
            dwi = jnp.stack([dwi[:, s * cols:(s + 1) * cols] for s in range(N_CHIPS)])
            w_in, w_out = "fox_w_in", "fox_w_out"
        dx, d_pre[i] = norm_bwd(xs, row(mix_pre_g, i), dhn, dx1, F32, nm + "_mix_pre")
        earlier = pending
        pending, token = start_reduce(nm, [(w_in, 0, dwi), (w_out, 0, _stack_rows(dwo)), ("ffn_w_up", i, d_up),
                                           ("ffn_w_down", i, d_down)])
        if earlier is not None:
            finish_reduce(f"l{i + 1}b", earlier, [token, dx])
    small.update(mix_pre_g=jnp.concatenate(d_pre), mix_post_g=jnp.concatenate(d_post),
                 ffn_pre_g=jnp.concatenate(d_fpre), ffn_post_g=jnp.concatenate(d_fpost),
                 ffn_conv_w=jnp.stack(d_fcw), ffn_conv_b=jnp.stack(d_fcb))

    small_names = [n for n in WEIGHTS if n in small]
    full_shape = {n: tuple(given[n].shape) for n in small_names}
    full_shape["sc_conv_w"] = (1, CONV_WIDTH, wt["sc_cw"].shape[1])
    full_shape["ffn_conv_w"] = tuple(wt["f_cw"].shape)
    small_sum = allreduce_small(_pack_small([small[n] for n in small_names] + [loss]), "small_sum")
    summed = _unpack_small(small_sum, [full_shape[n] for n in small_names] + [()])
    loss = summed[-1]
    for n, g in zip(small_names, summed):
        if g.shape != given[n].shape:
            width = given[n].shape[-1]
            g = lax.dynamic_slice_in_dim(g, chip * width, width, axis=g.ndim - 1)
        grads[n] = g

    last = [n for n, _, _ in pending[0]]
    ready = [n for n in WEIGHTS if n not in last]
    for n in ready:
        update(n, token)
    finish_reduce("l0b", pending, [token, small_sum] + [deltas[n] for n in ready])
    for n in last:
        update(n, token)
    return (loss, dx[None], *[grads[n].reshape(given[n].shape) for n in WEIGHTS], *[deltas[n] for n in WEIGHTS],
            *[new_m[n] for n in WEIGHTS], *[new_v[n] for n in WEIGHTS])
```
